```python
import jax, jax.numpy as jnp
from jax import lax
import numpy as np

D_MODEL = 2048
BATCH = 2
SEQ = 4096
DEPTH = 2
DEC_BATCH = 4
DEC_SEQ = 4096
PAST_LEN = 128

GRID_W = 64
HEAD_DIM = 128
N_Q_HEADS = 8
N_KV_HEADS = 2
Q_GROUP = N_Q_HEADS // N_KV_HEADS
ATTN_WIDTH = N_Q_HEADS * HEAD_DIM
KV_WIDTH = N_KV_HEADS * HEAD_DIM
Q_BLOCK = 128
ROPE_THETA = 10000.0
ROPE_AXIS_PAIRS = HEAD_DIM // 4
N_HGRN_HEADS = 8
HGRN_KDIM = 128
HGRN_VDIM = 128
HGRN_KWIDTH = N_HGRN_HEADS * HGRN_KDIM
HGRN_VWIDTH = N_HGRN_HEADS * HGRN_VDIM
HGRN_CHUNK = 64
N_EXPERTS = 16
EC_CAPACITY = 2
EXPERT_FF = 1024
NORM_EPS = 1e-6
IN_SPLITS = (ATTN_WIDTH, KV_WIDTH, KV_WIDTH, HGRN_KWIDTH, HGRN_KWIDTH, HGRN_KWIDTH,
             HGRN_VWIDTH, HGRN_VWIDTH, D_MODEL, D_MODEL)
IN_WIDTH = 10752

kernel_name = "hybrid_gqa_hgrn2_ec_encoder"

F32 = jnp.float32


def rms_norm(x, gain):
    xf = x.astype(F32)
    y = xf * lax.rsqrt(jnp.mean(xf * xf, axis=-1, keepdims=True) + NORM_EPS)
    return (y * gain.astype(F32)).astype(x.dtype)


def axial_rope_tables(seq_len):
    rows = seq_len // GRID_W
    row = jnp.repeat(jnp.arange(rows, dtype=F32), GRID_W)
    col = jnp.tile(jnp.arange(GRID_W, dtype=F32), rows)
    inv_freq = 1.0 / (ROPE_THETA ** (jnp.arange(ROPE_AXIS_PAIRS, dtype=F32) / ROPE_AXIS_PAIRS))
    ang = jnp.stack([row[:, None] * inv_freq, col[:, None] * inv_freq], axis=1)
    return jnp.cos(ang)[:, None], jnp.sin(ang)[:, None]


def apply_axial_rope(x, cos, sin):
    B, S, H, _ = x.shape
    xr = x.astype(F32).reshape(B, S, H, 2, 2, ROPE_AXIS_PAIRS)
    x1, x2 = xr[..., 0, :], xr[..., 1, :]
    out = jnp.stack([x1 * cos - x2 * sin, x1 * sin + x2 * cos], axis=-2)
    return out.reshape(B, S, H, HEAD_DIM).astype(x.dtype)


def block_attention(q, k, v):
    B, S, _, _ = q.shape
    nb = S // Q_BLOCK
    qb = q.reshape(B, nb, Q_BLOCK, N_KV_HEADS, Q_GROUP, HEAD_DIM).transpose(1, 0, 2, 3, 4, 5)
    scale = HEAD_DIM ** -0.5

    def one_block(qblk):
        s = jnp.einsum('bqhgd,bkhd->bhgqk', qblk, k, preferred_element_type=F32) * scale
        p = jax.nn.softmax(s, axis=-1)
        return jnp.einsum('bhgqk,bkhd->bqhgd', p.astype(v.dtype), v)

    o = lax.map(one_block, qb)
    return o.transpose(1, 0, 2, 3, 4, 5).reshape(B, S, ATTN_WIDTH)


def gla_chunk_scan(q, k, v, log_f):
    B, S, H, DK = q.shape
    DV = v.shape[-1]
    nc = S // HGRN_CHUNK

    def to_chunks(a):
        return a.reshape(B, nc, HGRN_CHUNK, H, a.shape[-1]).transpose(1, 0, 3, 2, 4)

    causal = jnp.tril(jnp.ones((HGRN_CHUNK, HGRN_CHUNK), dtype=bool))[:, :, None]

    def step(state, inp):
        qi, ki, vi, gi = inp
        b = jnp.cumsum(gi, axis=-2)
        diff = b[:, :, :, None, :] - b[:, :, None, :, :]
        decay = jnp.exp(jnp.where(causal, diff, -jnp.inf))
        scores = jnp.einsum('bhtd,bhsd,bhtsd->bhts', qi, ki, decay)
        o = jnp.einsum('bhts,bhse->bhte', scores, vi) + jnp.einsum('bhtd,bhde->bhte', qi * jnp.exp(b), state)
        b_last = b[:, :, -1:, :]
        new_state = state * jnp.exp(b_last)[:, :, 0, :, None] + jnp.einsum(
            'bhsd,bhse->bhde', ki * jnp.exp(b_last - b), vi)
        return new_state, o

    init = jnp.zeros((B, H, DK, DV), F32)
    _, o = lax.scan(step, init, (to_chunks(q), to_chunks(k), to_chunks(v), to_chunks(log_f)))
    return o.transpose(1, 0, 3, 2, 4).reshape(B, S, H, DV)


def hgrn2_gates(z, lb):
    z = z.astype(F32)
    log_f = jnp.logaddexp(jnp.log(lb), jnp.log1p(-lb) + jax.nn.log_sigmoid(z))
    k = (1.0 - lb) * jax.nn.sigmoid(-z)
    return log_f, k


def hgrn2_branch(q, z_fwd, z_bwd, i, og, lb_fwd, lb_bwd, out_gain):
    B, S, _ = q.shape
    hk = lambda a: a.reshape(B, S, N_HGRN_HEADS, -1)
    qh = hk(q).astype(F32)
    ih = hk(i).astype(F32)
    lf_f, k_f = hgrn2_gates(hk(z_fwd), lb_fwd.reshape(N_HGRN_HEADS, HGRN_KDIM))
    lf_b, k_b = hgrn2_gates(hk(z_bwd), lb_bwd.reshape(N_HGRN_HEADS, HGRN_KDIM))
    o_fwd = gla_chunk_scan(qh, k_f, ih, lf_f)
    flip = lambda a: jnp.flip(a, axis=1)
    o_bwd = flip(gla_chunk_scan(flip(qh), flip(k_b), flip(ih), flip(lf_b)))
    o = rms_norm(o_fwd + o_bwd, out_gain)
    o = o.reshape(B, S, HGRN_VWIDTH) * jax.nn.silu(og.astype(F32))
    return o.astype(q.dtype)


def expert_choice_ffn(h, w_router, w_gate, w_up, w_down):
    B, S, D = h.shape
    T = B * S
    cap = EC_CAPACITY * T // N_EXPERTS
    xt = h.reshape(T, D)
    logits = jnp.einsum('td,de->te', xt, w_router, preferred_element_type=F32)
    aff = jax.nn.softmax(logits, axis=-1)
    gate_vals, idx = lax.top_k(aff.T, cap)
    xe = xt[idx]
    g = jnp.einsum('ecd,edf->ecf', xe, w_gate)
    u = jnp.einsum('ecd,edf->ecf', xe, w_up)
    ye = jnp.einsum('ecf,efd->ecd', jax.nn.silu(g) * u, w_down)
    ye = ye * gate_vals[..., None].astype(ye.dtype)
    out = jnp.zeros_like(xt).at[idx.reshape(-1)].add(ye.reshape(-1, D))
    return out.reshape(B, S, D)


def trunk(x, w_in, q_norm, k_norm, lb_all, hgrn_norm, w_proj_attn, w_proj_hgrn, w_out,
          norm_mix, norm_ffn, w_router, w_gate, w_up, w_down, norm_final):
    B, S, _ = x.shape
    cos, sin = axial_rope_tables(S)
    offsets = [int(o) for o in np.cumsum(IN_SPLITS)[:-1]]
    for l in range(DEPTH):
        h = rms_norm(x, norm_mix[l])
        proj = h @ w_in[l]
        (aq, ak, av, hq, hf_f, hf_b, hi, hog, ga, gb) = jnp.split(proj, offsets, axis=-1)
        aq = apply_axial_rope(rms_norm(aq.reshape(B, S, N_Q_HEADS, HEAD_DIM), q_norm[l]), cos, sin)
        ak = apply_axial_rope(rms_norm(ak.reshape(B, S, N_KV_HEADS, HEAD_DIM), k_norm[l]), cos, sin)
        av = av.reshape(B, S, N_KV_HEADS, HEAD_DIM)
        y_attn = block_attention(aq, ak, av) @ w_proj_attn[l]
        y_hgrn = hgrn2_branch(hq, hf_f, hf_b, hi, hog, lb_all[0, l], lb_all[1, l], hgrn_norm[l]) @ w_proj_hgrn[l]
        merged = jax.nn.sigmoid(ga) * y_attn + jax.nn.sigmoid(gb) * y_hgrn
        x = x + merged @ w_out[l]
        x = x + expert_choice_ffn(rms_norm(x, norm_ffn[l]), w_router[l], w_gate[l], w_up[l], w_down[l])
    return rms_norm(x, norm_final)


def setup_inputs(seed: int = 0) -> dict:
    key = jax.random.key(seed)
    ks = jax.random.split(key, 20)
    n = lambda k, shape, scale: jax.random.normal(k, shape, F32) * scale
    return {
        "x_prompt": n(ks[0], (BATCH, SEQ, D_MODEL), 1.0),
        "x_sample": n(ks[1], (DEC_BATCH, DEC_SEQ, D_MODEL), 1.0),
        "w_in": n(ks[2], (DEPTH, D_MODEL, IN_WIDTH), D_MODEL ** -0.5),
        "q_norm": 1.0 + n(ks[3], (DEPTH, HEAD_DIM), 0.02),
        "k_norm": 1.0 + n(ks[4], (DEPTH, HEAD_DIM), 0.02),
        "lower_bounds": n(ks[5], (2, DEPTH, HGRN_KWIDTH), 0.5),
        "hgrn_norm": 1.0 + n(ks[6], (DEPTH, HGRN_VDIM), 0.02),
        "w_proj_attn": n(ks[7], (DEPTH, ATTN_WIDTH, D_MODEL), ATTN_WIDTH ** -0.5),
        "w_proj_hgrn": n(ks[8], (DEPTH, HGRN_VWIDTH, D_MODEL), HGRN_VWIDTH ** -0.5),
        "w_out": n(ks[9], (DEPTH, D_MODEL, D_MODEL), D_MODEL ** -0.5),
        "norm_mix": 1.0 + n(ks[10], (DEPTH, D_MODEL), 0.02),
        "norm_ffn": 1.0 + n(ks[11], (DEPTH, D_MODEL), 0.02),
        "w_router": n(ks[12], (DEPTH, D_MODEL, N_EXPERTS), D_MODEL ** -0.5),
        "w_gate": n(ks[13], (DEPTH, N_EXPERTS, D_MODEL, EXPERT_FF), D_MODEL ** -0.5),
        "w_up": n(ks[14], (DEPTH, N_EXPERTS, D_MODEL, EXPERT_FF), D_MODEL ** -0.5),
        "w_down": n(ks[15], (DEPTH, N_EXPERTS, EXPERT_FF, D_MODEL), EXPERT_FF ** -0.5),
        "norm_final": 1.0 + n(ks[16], (D_MODEL,), 0.02),
    }


def reference(x_prompt, x_sample, w_in, q_norm, k_norm, lower_bounds, hgrn_norm, w_proj_attn,
              w_proj_hgrn, w_out, norm_mix, norm_ffn, w_router, w_gate, w_up, w_down, norm_final):
    lb_all = jnp.cumsum(jax.nn.softmax(lower_bounds.astype(F32), axis=1), axis=1)
    lb_all = lb_all - lb_all[:, :1]
    y_prompt = trunk(x_prompt, w_in, q_norm, k_norm, lb_all, hgrn_norm, w_proj_attn, w_proj_hgrn, w_out,
                     norm_mix, norm_ffn, w_router, w_gate, w_up, w_down, norm_final)
    y_sample = trunk(x_sample, w_in, q_norm, k_norm, lb_all, hgrn_norm, w_proj_attn, w_proj_hgrn, w_out,
                     norm_mix, norm_ffn, w_router, w_gate, w_up, w_down, norm_final)
    return (y_prompt, y_sample)
```

```python
import functools

import numpy as np
import jax
import jax.numpy as jnp
from jax import lax
from jax.experimental import pallas as pl
from jax.experimental.pallas import tpu as pltpu

F32 = jnp.float32
BF16 = jnp.bfloat16

D_MODEL = 2048
SEQ = 4096
GRID_W = 64
HEAD_DIM = 128
N_Q_HEADS = 8
N_KV_HEADS = 2
Q_GROUP = N_Q_HEADS // N_KV_HEADS
ATTN_WIDTH = N_Q_HEADS * HEAD_DIM
KV_WIDTH = N_KV_HEADS * HEAD_DIM
ROPE_THETA = 10000.0
ROPE_AXIS_PAIRS = HEAD_DIM // 4
N_HGRN_HEADS = 8
HGRN_DIM = 128
HGRN_WIDTH = N_HGRN_HEADS * HGRN_DIM
N_EXPERTS = 16
EC_CAPACITY = 2
EXPERT_FF = 1024
NORM_EPS = 1e-6
IN_WIDTH = 10752

COL_GA = 0
COL_GB = 2048
COL_HQ = 4096
COL_ZF = 5120
COL_ZB = 6144
COL_HI = 7168
COL_HOG = 8192
COL_QKV = 9216
QKV_WIDTH = ATTN_WIDTH + 2 * KV_WIDTH

HGRN_CHUNK = 64
HGRN_SUB = 16
HGRN_HEADS_PER_STEP = 4
EXP_CLAMP = 80.0

V7X_VMEM_BYTES = 64 * 1024 * 1024


def _cparams(n_grid, vmem_mb):
    assert vmem_mb * 1024 * 1024 < V7X_VMEM_BYTES
    return pltpu.CompilerParams(
        dimension_semantics=("arbitrary",) * n_grid,
        vmem_limit_bytes=vmem_mb * 1024 * 1024,
    )


def _sigmoid(x):
    return 1.0 / (1.0 + jnp.exp(-x))


def _rms(x, gain):
    return x * lax.rsqrt(jnp.mean(x * x, axis=-1, keepdims=True) + NORM_EPS) * gain


def _dot_nt(a, b):
    return lax.dot_general(a, b, (((1,), (1,)), ((), ())), preferred_element_type=F32)


def _dot_tn(a, b):
    return lax.dot_general(a, b, (((0,), (0,)), ((), ())), preferred_element_type=F32)


def _norm_body(x_ref, g_ref, o_ref):
    o_ref[...] = _rms(x_ref[...], g_ref[...]).astype(o_ref.dtype)


def rms_norm_rows(x, gain, out_dtype, tm=512):
    T, D = x.shape
    return pl.pallas_call(
        _norm_body,
        grid=(T // tm,),
        in_specs=[pl.BlockSpec((tm, D), lambda i: (i, 0)), pl.BlockSpec((1, D), lambda i: (0, 0))],
        out_specs=pl.BlockSpec((tm, D), lambda i: (i, 0)),
        out_shape=jax.ShapeDtypeStruct((T, D), out_dtype),
        compiler_params=_cparams(1, 32),
        name="rms_norm_rows",
    )(x, gain.reshape(1, D))


def _mm_body(a_ref, w_ref, o_ref):
    o_ref[...] = jnp.dot(a_ref[...], w_ref[...], preferred_element_type=F32).astype(o_ref.dtype)


def matmul(a, w, out_dtype, tm=512, tn=1536):
    M, K = a.shape
    N = w.shape[1]
    return pl.pallas_call(
        _mm_body,
        grid=(N // tn, M // tm),
        in_specs=[pl.BlockSpec((tm, K), lambda j, i: (i, 0)), pl.BlockSpec((K, tn), lambda j, i: (0, j))],
        out_specs=pl.BlockSpec((tm, tn), lambda j, i: (i, j)),
        out_shape=jax.ShapeDtypeStruct((M, N), out_dtype),
        compiler_params=_cparams(2, 48),
        name="in_proj",
    )(a, w)


def _qkprep_body(p_ref, cos_ref, sin_ref, qg_ref, kg_ref, q_ref, k_ref, v_ref, *, scale):
    cos = cos_ref[...]
    sin = sin_ref[...]
    lane = lax.broadcasted_iota(jnp.int32, cos.shape, 1)
    first = (lane & ROPE_AXIS_PAIRS) == 0

    def prep(x, g, s):
        y = _rms(x, g)
        partner = jnp.where(first, pltpu.roll(y, HEAD_DIM - ROPE_AXIS_PAIRS, 1), pltpu.roll(y, ROPE_AXIS_PAIRS, 1))
        return (y * cos + partner * sin) * s

    for h in range(N_Q_HEADS):
        sl = slice(h * HEAD_DIM, (h + 1) * HEAD_DIM)
        q_ref[:, sl] = prep(p_ref[:, sl], qg_ref[...], scale).astype(q_ref.dtype)
    for h in range(N_KV_HEADS):
        sl = slice(h * HEAD_DIM, (h + 1) * HEAD_DIM)
        src = slice(ATTN_WIDTH + h * HEAD_DIM, ATTN_WIDTH + (h + 1) * HEAD_DIM)
        k_ref[:, sl] = prep(p_ref[:, src], kg_ref[...], 1.0).astype(k_ref.dtype)
    v_ref[...] = p_ref[:, ATTN_WIDTH + KV_WIDTH:].astype(v_ref.dtype)


def qk_prep(proj, cos, sin, q_gain, k_gain, seq, tm=256):
    T = proj.shape[0]
    nblk = seq // tm
    return pl.pallas_call(
        functools.partial(_qkprep_body, scale=HEAD_DIM ** -0.5),
        grid=(T // tm,),
        in_specs=[
            pl.BlockSpec((tm, QKV_WIDTH), lambda i: (i, COL_QKV // QKV_WIDTH)),
            pl.BlockSpec((tm, HEAD_DIM), lambda i: (i % nblk, 0)),
            pl.BlockSpec((tm, HEAD_DIM), lambda i: (i % nblk, 0)),
            pl.BlockSpec((1, HEAD_DIM), lambda i: (0, 0)),
            pl.BlockSpec((1, HEAD_DIM), lambda i: (0, 0)),
        ],
        out_specs=[
            pl.BlockSpec((tm, ATTN_WIDTH), lambda i: (i, 0)),
            pl.BlockSpec((tm, KV_WIDTH), lambda i: (i, 0)),
            pl.BlockSpec((tm, KV_WIDTH), lambda i: (i, 0)),
        ],
        out_shape=[
            jax.ShapeDtypeStruct((T, ATTN_WIDTH), BF16),
            jax.ShapeDtypeStruct((T, KV_WIDTH), BF16),
            jax.ShapeDtypeStruct((T, KV_WIDTH), BF16),
        ],
        compiler_params=_cparams(1, 32),
        name="qk_prep",
    )(proj, cos, sin, q_gain.reshape(1, HEAD_DIM), k_gain.reshape(1, HEAD_DIM))


def rope_tables(seq):
    rows = seq // GRID_W
    row = jnp.repeat(jnp.arange(rows, dtype=F32), GRID_W)
    col = jnp.tile(jnp.arange(GRID_W, dtype=F32), rows)
    inv_freq = 1.0 / (ROPE_THETA ** (jnp.arange(ROPE_AXIS_PAIRS, dtype=F32) / ROPE_AXIS_PAIRS))
    ang_r = row[:, None] * inv_freq
    ang_c = col[:, None] * inv_freq
    cos = jnp.concatenate([jnp.cos(ang_r), jnp.cos(ang_r), jnp.cos(ang_c), jnp.cos(ang_c)], axis=1)
    sin = jnp.concatenate([-jnp.sin(ang_r), jnp.sin(ang_r), -jnp.sin(ang_c), jnp.sin(ang_c)], axis=1)
    return cos, sin


def _attn_body(q_ref, k_ref, v_ref, o_ref):
    k = k_ref[...]
    v = v_ref[...]
    for g in range(Q_GROUP):
        sl = slice(g * HEAD_DIM, (g + 1) * HEAD_DIM)
        s = _dot_nt(q_ref[:, sl], k)
        m = jnp.max(s, axis=-1, keepdims=True)
        p = jnp.exp(s - m)
        l = jnp.sum(p, axis=-1, keepdims=True)
        o = jnp.dot(p.astype(BF16), v, preferred_element_type=F32)
        o_ref[:, sl] = (o / l).astype(o_ref.dtype)


def attention(q, k, v, seq, tq=128):
    T = q.shape[0]
    n_seq = T // seq
    nq = seq // tq
    gw = Q_GROUP * HEAD_DIM
    return pl.pallas_call(
        _attn_body,
        grid=(n_seq, N_KV_HEADS, nq),
        in_specs=[
            pl.BlockSpec((tq, gw), lambda b, h, i: (b * nq + i, h)),
            pl.BlockSpec((seq, HEAD_DIM), lambda b, h, i: (b, h)),
            pl.BlockSpec((seq, HEAD_DIM), lambda b, h, i: (b, h)),
        ],
        out_specs=pl.BlockSpec((tq, gw), lambda b, h, i: (b * nq + i, h)),
        out_shape=jax.ShapeDtypeStruct((T, ATTN_WIDTH), BF16),
        compiler_params=_cparams(3, 48),
        name="attention",
    )(q, k, v)


def _split3(x):
    hi = x.astype(BF16)
    r1 = x - hi.astype(F32)
    mid = r1.astype(BF16)
    lo = (r1 - mid.astype(F32)).astype(BF16)
    return hi, mid, lo


def _hgrn_direction(q, z, v, loglb, log1mlb, omlb, tri, st_ref, fwd):
    C, SB = HGRN_CHUNK, HGRN_SUB
    e = jnp.exp(-jnp.abs(z))
    r = 1.0 / (1.0 + e)
    kk = omlb * jnp.where(z >= 0, e * r, r)
    cc = log1mlb + (jnp.minimum(z, 0.0) - jnp.log1p(e))
    g = jnp.maximum(loglb, cc) + jnp.log1p(jnp.exp(-jnp.abs(loglb - cc)))
    hi, mid, lo = _split3(g)
    b = (jnp.dot(tri, hi, preferred_element_type=F32) + jnp.dot(tri, mid, preferred_element_type=F32)
         + jnp.dot(tri, lo, preferred_element_type=F32))
    tot = b[C - 1:C, :] if fwd else b[0:1, :]
    vb = v.astype(BF16)
    st = st_ref[...]
    o_inter = _dot_nt((q * jnp.exp(b)).astype(BF16), st.astype(BF16))
    k_st = (kk * jnp.exp(tot - b)).astype(BF16)
    st_ref[...] = st * jnp.exp(tot) + _dot_tn(vb, k_st)
    outs = []
    for i in range(C // SB):
        r0 = i * SB
        if fwd:
            ref = b[r0 - 1:r0, :] if i > 0 else jnp.zeros((1, HGRN_DIM), F32)
            k0, k1 = 0, r0 + SB
        else:
            ref = b[r0 + SB:r0 + SB + 1, :] if r0 + SB < C else jnp.zeros((1, HGRN_DIM), F32)
            k0, k1 = r0, C
        qs = (q[r0:r0 + SB] * jnp.exp(b[r0:r0 + SB] - ref)).astype(BF16)
        ks = (kk[k0:k1] * jnp.exp(jnp.minimum(ref - b[k0:k1], EXP_CLAMP))).astype(BF16)
        s = _dot_nt(qs, ks)
        t_idx = r0 + lax.broadcasted_iota(jnp.int32, s.shape, 0)
        s_idx = k0 + lax.broadcasted_iota(jnp.int32, s.shape, 1)
        keep = (s_idx <= t_idx) if fwd else (s_idx >= t_idx)
        s = jnp.where(keep, s, 0.0)
        outs.append(jnp.dot(s.astype(BF16), vb[k0:k1], preferred_element_type=F32))
    return o_inter + jnp.concatenate(outs, axis=0)


def _hgrn_body(qf_ref, zf_ref, vf_ref, qb_ref, zb_ref, vb_ref, par_ref, of_ref, ob_ref, stf_ref, stb_ref):
    C = HGRN_CHUNK

    @pl.when(pl.program_id(2) == 0)
    def _():
        stf_ref[...] = jnp.zeros(stf_ref.shape, F32)
        stb_ref[...] = jnp.zeros(stb_ref.shape, F32)

    row = lax.broadcasted_iota(jnp.int32, (C, C), 0)
    col = lax.broadcasted_iota(jnp.int32, (C, C), 1)
    tri_f = jnp.where(col <= row, 1.0, 0.0).astype(BF16)
    tri_b = jnp.where(col >= row, 1.0, 0.0).astype(BF16)
    for h in range(HGRN_HEADS_PER_STEP):
        sl = slice(h * HGRN_DIM, (h + 1) * HGRN_DIM)
        of_ref[:, sl] = _hgrn_direction(
            qf_ref[:, sl], zf_ref[:, sl], vf_ref[:, sl],
            par_ref[0:1, sl], par_ref[1:2, sl], par_ref[2:3, sl], tri_f, stf_ref.at[h], True)
        ob_ref[:, sl] = _hgrn_direction(
            qb_ref[:, sl], zb_ref[:, sl], vb_ref[:, sl],
            par_ref[3:4, sl], par_ref[4:5, sl], par_ref[5:6, sl], tri_b, stb_ref.at[h], False)


def hgrn_scan(proj, params, seq):
    T = proj.shape[0]
    n_seq = T // seq
    C = HGRN_CHUNK
    nc = seq // C
    w = HGRN_HEADS_PER_STEP * HGRN_DIM
    n_hh = HGRN_WIDTH // w

    def fwd_spec(col):
        return pl.BlockSpec((C, w), lambda b, hh, j: (b * nc + j, col // w + hh))

    def bwd_spec(col):
        return pl.BlockSpec((C, w), lambda b, hh, j: (b * nc + nc - 1 - j, col // w + hh))

    return pl.pallas_call(
        _hgrn_body,
        grid=(n_seq, n_hh, nc),
        in_specs=[
            fwd_spec(COL_HQ), fwd_spec(COL_ZF), fwd_spec(COL_HI),
            bwd_spec(COL_HQ), bwd_spec(COL_ZB), bwd_spec(COL_HI),
            pl.BlockSpec((8, w), lambda b, hh, j: (0, hh)),
        ],
        out_specs=[
            pl.BlockSpec((C, w), lambda b, hh, j: (b * nc + j, hh)),
            pl.BlockSpec((C, w), lambda b, hh, j: (b * nc + nc - 1 - j, hh)),
        ],
        out_shape=[jax.ShapeDtypeStruct((T, HGRN_WIDTH), F32), jax.ShapeDtypeStruct((T, HGRN_WIDTH), F32)],
        scratch_shapes=[
            pltpu.VMEM((HGRN_HEADS_PER_STEP, HGRN_DIM, HGRN_DIM), F32),
            pltpu.VMEM((HGRN_HEADS_PER_STEP, HGRN_DIM, HGRN_DIM), F32),
        ],
        compiler_params=_cparams(3, 32),
        name="hgrn_scan",
    )(proj, proj, proj, proj, proj, proj, params)


def hgrn_params(lower_bounds, layer):
    lb_all = jnp.cumsum(jax.nn.softmax(lower_bounds.astype(F32), axis=1), axis=1)
    lb_all = lb_all - lb_all[:, :1]
    rows = []
    for d in range(2):
        lb = lb_all[d, layer]
        rows += [jnp.log(lb), jnp.log1p(-lb), 1.0 - lb]
    rows += [jnp.zeros_like(rows[0])] * 2
    return jnp.stack(rows, axis=0)


def _merge_body(attn_ref, of_ref, ob_ref, hog_ref, ga_ref, gb_ref, hg_ref, wa_ref, wh_ref, o_ref):
    ya = jnp.dot(attn_ref[...], wa_ref[...], preferred_element_type=F32)
    hs = []
    for h in range(N_HGRN_HEADS):
        sl = slice(h * HGRN_DIM, (h + 1) * HGRN_DIM)
        o = _rms(of_ref[:, sl] + ob_ref[:, sl], hg_ref[...])
        og = hog_ref[:, sl]
        hs.append((o * (og * _sigmoid(og))).astype(BF16))
    yh = jnp.dot(jnp.concatenate(hs, axis=1), wh_ref[...], preferred_element_type=F32)
    o_ref[...] = (_sigmoid(ga_ref[...]) * ya + _sigmoid(gb_ref[...]) * yh).astype(o_ref.dtype)


def merge_branches(attn, o_f, o_b, proj, hgrn_gain, w_attn, w_hgrn, tm=256):
    T = attn.shape[0]
    D = D_MODEL
    return pl.pallas_call(
        _merge_body,
        grid=(T // tm,),
        in_specs=[
            pl.BlockSpec((tm, ATTN_WIDTH), lambda i: (i, 0)),
            pl.BlockSpec((tm, HGRN_WIDTH), lambda i: (i, 0)),
            pl.BlockSpec((tm, HGRN_WIDTH), lambda i: (i, 0)),
            pl.BlockSpec((tm, HGRN_WIDTH), lambda i: (i, COL_HOG // HGRN_WIDTH)),
            pl.BlockSpec((tm, D), lambda i: (i, COL_GA // D)),
            pl.BlockSpec((tm, D), lambda i: (i, COL_GB // D)),
            pl.BlockSpec((1, HGRN_DIM), lambda i: (0, 0)),
            pl.BlockSpec((ATTN_WIDTH, D), lambda i: (0, 0)),
            pl.BlockSpec((HGRN_WIDTH, D), lambda i: (0, 0)),
        ],
        out_specs=pl.BlockSpec((tm, D), lambda i: (i, 0)),
        out_shape=jax.ShapeDtypeStruct((T, D), BF16),
        compiler_params=_cparams(1, 56),
        name="merge_branches",
    )(attn, o_f, o_b, proj, proj, proj, hgrn_gain.reshape(1, HGRN_DIM), w_attn, w_hgrn)


def _outproj_body(m_ref, w_ref, x_ref, g_ref, wr_ref, xo_ref, h_ref, aff_ref):
    xn = x_ref[...] + jnp.dot(m_ref[...], w_ref[...], preferred_element_type=F32)
    xo_ref[...] = xn
    h = _rms(xn, g_ref[...])
    h_ref[...] = h.astype(h_ref.dtype)
    logits = lax.dot_general(wr_ref[...], h, (((1,), (1,)), ((), ())),
                             preferred_element_type=F32, precision=lax.Precision.HIGHEST)
    mx = jnp.max(logits, axis=0, keepdims=True)
    ex = jnp.exp(logits - mx)
    aff_ref[...] = ex / jnp.sum(ex, axis=0, keepdims=True)


def out_proj_router(merged, w_out, x, ffn_gain, w_router_t, tm=256):
    T, D = x.shape
    return pl.pallas_call(
        _outproj_body,
        grid=(T // tm,),
        in_specs=[
            pl.BlockSpec((tm, D), lambda i: (i, 0)),
            pl.BlockSpec((D, D), lambda i: (0, 0)),
            pl.BlockSpec((tm, D), lambda i: (i, 0)),
            pl.BlockSpec((1, D), lambda i: (0, 0)),
            pl.BlockSpec((N_EXPERTS, D), lambda i: (0, 0)),
        ],
        out_specs=[
            pl.BlockSpec((tm, D), lambda i: (i, 0)),
            pl.BlockSpec((tm, D), lambda i: (i, 0)),
            pl.BlockSpec((N_EXPERTS, tm), lambda i: (0, i)),
        ],
        out_shape=[
            jax.ShapeDtypeStruct((T, D), F32),
            jax.ShapeDtypeStruct((T, D), BF16),
            jax.ShapeDtypeStruct((N_EXPERTS, T), F32),
        ],
        compiler_params=_cparams(1, 48),
        name="out_proj_router",
    )(merged, w_out, x, ffn_gain.reshape(1, D), w_router_t)


def _ffn_body(x_ref, wg_ref, wu_ref, wd_ref, gv_ref, o_ref):
    x = x_ref[0]
    g = jnp.dot(x, wg_ref[0], preferred_element_type=F32)
    u = jnp.dot(x, wu_ref[0], preferred_element_type=F32)
    a = (g * _sigmoid(g) * u).astype(BF16)
    o_ref[0] = jnp.dot(a, wd_ref[0], preferred_element_type=F32) * gv_ref[0]


def expert_ffn(xe, w_gate, w_up, w_down, gate_vals, tm=256):
    E, cap, D = xe.shape
    F = w_gate.shape[-1]
    return pl.pallas_call(
        _ffn_body,
        grid=(E, cap // tm),
        in_specs=[
            pl.BlockSpec((1, tm, D), lambda e, i: (e, i, 0)),
            pl.BlockSpec((1, D, F), lambda e, i: (e, 0, 0)),
            pl.BlockSpec((1, D, F), lambda e, i: (e, 0, 0)),
            pl.BlockSpec((1, F, D), lambda e, i: (e, 0, 0)),
            pl.BlockSpec((1, tm, 1), lambda e, i: (e, i, 0)),
        ],
        out_specs=pl.BlockSpec((1, tm, D), lambda e, i: (e, i, 0)),
        out_shape=jax.ShapeDtypeStruct((E, cap, D), F32),
        compiler_params=_cparams(2, 48),
        name="expert_ffn",
    )(xe, w_gate, w_up, w_down, gate_vals.reshape(E, cap, 1))


def kernel(x_prompt, x_sample, w_in, q_norm, k_norm, lower_bounds, hgrn_norm, w_proj_attn, w_proj_hgrn,
           w_out, norm_mix, norm_ffn, w_router, w_gate, w_up, w_down, norm_final):
    depth = w_in.shape[0]
    groups = (x_prompt, x_sample)
    seq = x_prompt.shape[1]
    group_tokens = [g.shape[0] * g.shape[1] for g in groups]
    x = jnp.concatenate([g.reshape(-1, D_MODEL) for g in groups], axis=0)
    cos, sin = rope_tables(seq)

    off = np.cumsum((0, ATTN_WIDTH, KV_WIDTH, KV_WIDTH, HGRN_WIDTH, HGRN_WIDTH, HGRN_WIDTH, HGRN_WIDTH,
                     HGRN_WIDTH, D_MODEL, D_MODEL))
    order = (8, 9, 3, 4, 5, 6, 7, 0, 1, 2)

    for l in range(depth):
        w_in_l = jnp.concatenate([w_in[l][:, off[s]:off[s + 1]] for s in order], axis=1).astype(BF16)
        h = rms_norm_rows(x, norm_mix[l], BF16)
        proj = matmul(h, w_in_l, F32)
        q, k, v = qk_prep(proj, cos, sin, q_norm[l], k_norm[l], seq)
        attn = attention(q, k, v, seq)
        o_f, o_b = hgrn_scan(proj, hgrn_params(lower_bounds, l), seq)
        merged = merge_branches(attn, o_f, o_b, proj, hgrn_norm[l],
                                w_proj_attn[l].astype(BF16), w_proj_hgrn[l].astype(BF16))
        x, h_ffn, aff_t = out_proj_router(merged, w_out[l].astype(BF16), x, norm_ffn[l], w_router[l].T)
        wg, wu, wd = w_gate[l].astype(BF16), w_up[l].astype(BF16), w_down[l].astype(BF16)
        start = 0
        for n_tok in group_tokens:
            cap = EC_CAPACITY * n_tok // N_EXPERTS
            gate_vals, idx = lax.top_k(aff_t[:, start:start + n_tok], cap)
            idx = idx + start
            ye = expert_ffn(h_ffn[idx], wg, wu, wd, gate_vals)
            x = x.at[idx.reshape(-1)].add(ye.reshape(-1, D_MODEL))
            start += n_tok

    y = rms_norm_rows(x, norm_final, F32)
    outs = []
    start = 0
    for g, n_tok in zip(groups, group_tokens):
        outs.append(y[start:start + n_tok].reshape(g.shape))
        start += n_tok
    return tuple(outs)
```

```python
import functools

import numpy as np
import jax
import jax.numpy as jnp
from jax import lax
from jax.experimental import pallas as pl
from jax.experimental.pallas import tpu as pltpu

F32 = jnp.float32
BF16 = jnp.bfloat16

D_MODEL = 2048
SEQ = 4096
GRID_W = 64
HEAD_DIM = 128
N_Q_HEADS = 8
N_KV_HEADS = 2
Q_GROUP = N_Q_HEADS // N_KV_HEADS
ATTN_WIDTH = N_Q_HEADS * HEAD_DIM
KV_WIDTH = N_KV_HEADS * HEAD_DIM
ROPE_THETA = 10000.0
ROPE_AXIS_PAIRS = HEAD_DIM // 4
N_HGRN_HEADS = 8
HGRN_DIM = 128
HGRN_WIDTH = N_HGRN_HEADS * HGRN_DIM
N_EXPERTS = 16
EC_CAPACITY = 2
EXPERT_FF = 1024
NORM_EPS = 1e-6
IN_WIDTH = 10752

COL_GA = 0
COL_GB = 2048
COL_HQ = 4096
COL_ZF = 5120
COL_ZB = 6144
COL_HI = 7168
COL_HOG = 8192
COL_QKV = 9216
QKV_WIDTH = ATTN_WIDTH + 2 * KV_WIDTH

HGRN_CHUNK = 64
HGRN_SUB = 16
HGRN_HEADS_PER_STEP = 8
EXP_CLAMP = 80.0

V7X_VMEM_BYTES = 64 * 1024 * 1024


def _cparams(n_grid, vmem_mb):
    assert vmem_mb * 1024 * 1024 < V7X_VMEM_BYTES
    return pltpu.CompilerParams(
        dimension_semantics=("arbitrary",) * n_grid,
        vmem_limit_bytes=vmem_mb * 1024 * 1024,
    )


def _sigmoid(x):
    return 1.0 / (1.0 + jnp.exp(-x))


def _rms(x, gain):
    return x * lax.rsqrt(jnp.mean(x * x, axis=-1, keepdims=True) + NORM_EPS) * gain


def _dot_nt(a, b):
    return lax.dot_general(a, b, (((1,), (1,)), ((), ())), preferred_element_type=F32)


def _dot_tn(a, b):
    return lax.dot_general(a, b, (((0,), (0,)), ((), ())), preferred_element_type=F32)


def _norm_body(x_ref, g_ref, o_ref):
    o_ref[...] = _rms(x_ref[...], g_ref[...]).astype(o_ref.dtype)


def rms_norm_rows(x, gain, out_dtype, tm=512):
    T, D = x.shape
    return pl.pallas_call(
        _norm_body,
        grid=(T // tm,),
        in_specs=[pl.BlockSpec((tm, D), lambda i: (i, 0)), pl.BlockSpec((1, D), lambda i: (0, 0))],
        out_specs=pl.BlockSpec((tm, D), lambda i: (i, 0)),
        out_shape=jax.ShapeDtypeStruct((T, D), out_dtype),
        compiler_params=_cparams(1, 32),
        name="rms_norm_rows",
    )(x, gain.reshape(1, D))


def _mm_body(a_ref, w_ref, o_ref):
    o_ref[...] = jnp.dot(a_ref[...], w_ref[...], preferred_element_type=F32).astype(o_ref.dtype)


def matmul(a, w, out_dtype, tm=512, tn=1536):
    M, K = a.shape
    N = w.shape[1]
    return pl.pallas_call(
        _mm_body,
        grid=(N // tn, M // tm),
        in_specs=[pl.BlockSpec((tm, K), lambda j, i: (i, 0)), pl.BlockSpec((K, tn), lambda j, i: (0, j))],
        out_specs=pl.BlockSpec((tm, tn), lambda j, i: (i, j)),
        out_shape=jax.ShapeDtypeStruct((M, N), out_dtype),
        compiler_params=_cparams(2, 48),
        name="in_proj",
    )(a, w)


def _qkprep_body(p_ref, cos_ref, sin_ref, qg_ref, kg_ref, q_ref, k_ref, v_ref, *, scale):
    cos = cos_ref[...]
    sin = sin_ref[...]
    lane = lax.broadcasted_iota(jnp.int32, cos.shape, 1)
    first = (lane & ROPE_AXIS_PAIRS) == 0

    def prep(x, g, s):
        y = _rms(x, g)
        partner = jnp.where(first, pltpu.roll(y, HEAD_DIM - ROPE_AXIS_PAIRS, 1), pltpu.roll(y, ROPE_AXIS_PAIRS, 1))
        return (y * cos + partner * sin) * s

    for h in range(N_Q_HEADS):
        sl = slice(h * HEAD_DIM, (h + 1) * HEAD_DIM)
        q_ref[:, sl] = prep(p_ref[:, sl], qg_ref[...], scale).astype(q_ref.dtype)
    for h in range(N_KV_HEADS):
        sl = slice(h * HEAD_DIM, (h + 1) * HEAD_DIM)
        src = slice(ATTN_WIDTH + h * HEAD_DIM, ATTN_WIDTH + (h + 1) * HEAD_DIM)
        k_ref[:, sl] = prep(p_ref[:, src], kg_ref[...], 1.0).astype(k_ref.dtype)
    v_ref[...] = p_ref[:, ATTN_WIDTH + KV_WIDTH:].astype(v_ref.dtype)


def qk_prep(proj, cos, sin, q_gain, k_gain, seq, tm=256):
    T = proj.shape[0]
    nblk = seq // tm
    return pl.pallas_call(
        functools.partial(_qkprep_body, scale=HEAD_DIM ** -0.5),
        grid=(T // tm,),
        in_specs=[
            pl.BlockSpec((tm, QKV_WIDTH), lambda i: (i, COL_QKV // QKV_WIDTH)),
            pl.BlockSpec((tm, HEAD_DIM), lambda i: (i % nblk, 0)),
            pl.BlockSpec((tm, HEAD_DIM), lambda i: (i % nblk, 0)),
            pl.BlockSpec((1, HEAD_DIM), lambda i: (0, 0)),
            pl.BlockSpec((1, HEAD_DIM), lambda i: (0, 0)),
        ],
        out_specs=[
            pl.BlockSpec((tm, ATTN_WIDTH), lambda i: (i, 0)),
            pl.BlockSpec((tm, KV_WIDTH), lambda i: (i, 0)),
            pl.BlockSpec((tm, KV_WIDTH), lambda i: (i, 0)),
        ],
        out_shape=[
            jax.ShapeDtypeStruct((T, ATTN_WIDTH), BF16),
            jax.ShapeDtypeStruct((T, KV_WIDTH), BF16),
            jax.ShapeDtypeStruct((T, KV_WIDTH), BF16),
        ],
        compiler_params=_cparams(1, 32),
        name="qk_prep",
    )(proj, cos, sin, q_gain.reshape(1, HEAD_DIM), k_gain.reshape(1, HEAD_DIM))


def rope_tables(seq):
    rows = seq // GRID_W
    row = jnp.repeat(jnp.arange(rows, dtype=F32), GRID_W)
    col = jnp.tile(jnp.arange(GRID_W, dtype=F32), rows)
    inv_freq = 1.0 / (ROPE_THETA ** (jnp.arange(ROPE_AXIS_PAIRS, dtype=F32) / ROPE_AXIS_PAIRS))
    ang_r = row[:, None] * inv_freq
    ang_c = col[:, None] * inv_freq
    cos = jnp.concatenate([jnp.cos(ang_r), jnp.cos(ang_r), jnp.cos(ang_c), jnp.cos(ang_c)], axis=1)
    sin = jnp.concatenate([-jnp.sin(ang_r), jnp.sin(ang_r), -jnp.sin(ang_c), jnp.sin(ang_c)], axis=1)
    return cos, sin


def _attn_body(q_ref, k_ref, v_ref, o_ref):
    k = k_ref[...]
    v = v_ref[...]
    for g in range(Q_GROUP):
        sl = slice(g * HEAD_DIM, (g + 1) * HEAD_DIM)
        s = _dot_nt(q_ref[:, sl], k)
        m = jnp.max(s, axis=-1, keepdims=True)
        p = jnp.exp(s - m)
        l = jnp.sum(p, axis=-1, keepdims=True)
        o = jnp.dot(p.astype(BF16), v, preferred_element_type=F32)
        o_ref[:, sl] = (o / l).astype(o_ref.dtype)


def attention(q, k, v, seq, tq=128):
    T = q.shape[0]
    n_seq = T // seq
    nq = seq // tq
    gw = Q_GROUP * HEAD_DIM
    return pl.pallas_call(
        _attn_body,
        grid=(n_seq, N_KV_HEADS, nq),
        in_specs=[
            pl.BlockSpec((tq, gw), lambda b, h, i: (b * nq + i, h)),
            pl.BlockSpec((seq, HEAD_DIM), lambda b, h, i: (b, h)),
            pl.BlockSpec((seq, HEAD_DIM), lambda b, h, i: (b, h)),
        ],
        out_specs=pl.BlockSpec((tq, gw), lambda b, h, i: (b * nq + i, h)),
        out_shape=jax.ShapeDtypeStruct((T, ATTN_WIDTH), BF16),
        compiler_params=_cparams(3, 48),
        name="attention",
    )(q, k, v)


def _split3(x):
    hi = x.astype(BF16)
    r1 = x - hi.astype(F32)
    mid = r1.astype(BF16)
    lo = (r1 - mid.astype(F32)).astype(BF16)
    return hi, mid, lo


def _hgrn_gates(z, loglb, log1mlb, omlb):
    e = jnp.exp(-jnp.abs(z))
    r = 1.0 / (1.0 + e)
    kk = omlb * jnp.where(z >= 0, e * r, r)
    cc = log1mlb + (jnp.minimum(z, 0.0) - jnp.log1p(e))
    g = jnp.maximum(loglb, cc) + jnp.log1p(jnp.exp(-jnp.abs(loglb - cc)))
    return g, kk


def _hgrn_body(qf_ref, zf_ref, vf_ref, qb_ref, zb_ref, vb_ref, par_ref, of_ref, ob_ref, stf_ref, stb_ref):
    C, SB, W = HGRN_CHUNK, HGRN_SUB, HGRN_DIM
    NH = HGRN_HEADS_PER_STEP

    @pl.when(pl.program_id(2) == 0)
    def _():
        stf_ref[...] = jnp.zeros(stf_ref.shape, F32)
        stb_ref[...] = jnp.zeros(stb_ref.shape, F32)

    row = lax.broadcasted_iota(jnp.int32, (C, C), 0)
    col = lax.broadcasted_iota(jnp.int32, (C, C), 1)
    dirs = (
        (True, qf_ref, zf_ref, vf_ref, 0, jnp.where(col <= row, 1.0, 0.0).astype(BF16), stf_ref, of_ref),
        (False, qb_ref, zb_ref, vb_ref, 3, jnp.where(col >= row, 1.0, 0.0).astype(BF16), stb_ref, ob_ref),
    )
    gs, kks = [], []
    for fwd, q_ref, z_ref, v_ref, p0, tri, st_ref, o_ref in dirs:
        g, kk = _hgrn_gates(z_ref[...], par_ref[p0:p0 + 1, :], par_ref[p0 + 1:p0 + 2, :], par_ref[p0 + 2:p0 + 3, :])
        gs.append(g)
        kks.append(kk)
    bs = []
    for (fwd, q_ref, z_ref, v_ref, p0, tri, st_ref, o_ref), g in zip(dirs, gs):
        hi, mid, lo = _split3(g)
        bs.append(jnp.dot(tri, hi, preferred_element_type=F32) + jnp.dot(tri, mid, preferred_element_type=F32)
                  + jnp.dot(tri, lo, preferred_element_type=F32))
    work = []
    for (fwd, q_ref, z_ref, v_ref, p0, tri, st_ref, o_ref), b, kk in zip(dirs, bs, kks):
        q = q_ref[...]
        vb = v_ref[...].astype(BF16)
        tot = b[C - 1:C, :] if fwd else b[0:1, :]
        q_in = (q * jnp.exp(b)).astype(BF16)
        k_st = (kk * jnp.exp(tot - b)).astype(BF16)
        dec = jnp.exp(tot)
        subs = []
        for i in range(C // SB):
            r0 = i * SB
            if fwd:
                ref = b[r0 - 1:r0, :] if i > 0 else jnp.zeros((1, NH * W), F32)
                k0, k1 = 0, r0 + SB
            else:
                ref = b[r0 + SB:r0 + SB + 1, :] if r0 + SB < C else jnp.zeros((1, NH * W), F32)
                k0, k1 = r0, C
            qs = (q[r0:r0 + SB] * jnp.exp(b[r0:r0 + SB] - ref)).astype(BF16)
            ks = (kk[k0:k1] * jnp.exp(jnp.minimum(ref - b[k0:k1], EXP_CLAMP))).astype(BF16)
            subs.append((r0, k0, k1, qs, ks))
        work.append((fwd, vb, q_in, k_st, dec, subs, st_ref, o_ref))
    res = []
    for fwd, vb, q_in, k_st, dec, subs, st_ref, o_ref in work:
        for h in range(NH):
            sl = slice(h * W, (h + 1) * W)
            st = st_ref[h]
            o_inter = _dot_nt(q_in[:, sl], st.astype(BF16))
            st_ref[h] = st * dec[:, sl] + _dot_tn(vb[:, sl], k_st[:, sl])
            scores = [_dot_nt(qs[:, sl], ks[:, sl]) for (r0, k0, k1, qs, ks) in subs]
            res.append((fwd, h, o_inter, scores, vb, subs, o_ref))
    masked = []
    for fwd, h, o_inter, scores, vb, subs, o_ref in res:
        ms = []
        for s, (r0, k0, k1, qs, ks) in zip(scores, subs):
            t_idx = r0 + lax.broadcasted_iota(jnp.int32, s.shape, 0)
            s_idx = k0 + lax.broadcasted_iota(jnp.int32, s.shape, 1)
            keep = (s_idx <= t_idx) if fwd else (s_idx >= t_idx)
            ms.append(jnp.where(keep, s, 0.0).astype(BF16))
        masked.append(ms)
    for (fwd, h, o_inter, scores, vb, subs, o_ref), ms in zip(res, masked):
        sl = slice(h * W, (h + 1) * W)
        outs = [jnp.dot(m, vb[k0:k1, sl], preferred_element_type=F32) for m, (r0, k0, k1, qs, ks) in zip(ms, subs)]
        o_ref[:, sl] = o_inter + jnp.concatenate(outs, axis=0)


def hgrn_scan(proj, params, seq):
    T = proj.shape[0]
    n_seq = T // seq
    C = HGRN_CHUNK
    nc = seq // C
    w = HGRN_HEADS_PER_STEP * HGRN_DIM
    n_hh = HGRN_WIDTH // w

    def fwd_spec(col):
        return pl.BlockSpec((C, w), lambda b, hh, j: (b * nc + j, col // w + hh))

    def bwd_spec(col):
        return pl.BlockSpec((C, w), lambda b, hh, j: (b * nc + nc - 1 - j, col // w + hh))

    return pl.pallas_call(
        _hgrn_body,
        grid=(n_seq, n_hh, nc),
        in_specs=[
            fwd_spec(COL_HQ), fwd_spec(COL_ZF), fwd_spec(COL_HI),
            bwd_spec(COL_HQ), bwd_spec(COL_ZB), bwd_spec(COL_HI),
            pl.BlockSpec((8, w), lambda b, hh, j: (0, hh)),
        ],
        out_specs=[
            pl.BlockSpec((C, w), lambda b, hh, j: (b * nc + j, hh)),
            pl.BlockSpec((C, w), lambda b, hh, j: (b * nc + nc - 1 - j, hh)),
        ],
        out_shape=[jax.ShapeDtypeStruct((T, HGRN_WIDTH), F32), jax.ShapeDtypeStruct((T, HGRN_WIDTH), F32)],
        scratch_shapes=[
            pltpu.VMEM((HGRN_HEADS_PER_STEP, HGRN_DIM, HGRN_DIM), F32),
            pltpu.VMEM((HGRN_HEADS_PER_STEP, HGRN_DIM, HGRN_DIM), F32),
        ],
        compiler_params=_cparams(3, 32),
        name="hgrn_scan",
    )(proj, proj, proj, proj, proj, proj, params)


def hgrn_params(lower_bounds, layer):
    lb_all = jnp.cumsum(jax.nn.softmax(lower_bounds.astype(F32), axis=1), axis=1)
    lb_all = lb_all - lb_all[:, :1]
    rows = []
    for d in range(2):
        lb = lb_all[d, layer]
        rows += [jnp.log(lb), jnp.log1p(-lb), 1.0 - lb]
    rows += [jnp.zeros_like(rows[0])] * 2
    return jnp.stack(rows, axis=0)


def _merge_body(attn_ref, of_ref, ob_ref, hog_ref, ga_ref, gb_ref, hg_ref, wa_ref, wh_ref, o_ref):
    ya = jnp.dot(attn_ref[...], wa_ref[...], preferred_element_type=F32)
    hs = []
    for h in range(N_HGRN_HEADS):
        sl = slice(h * HGRN_DIM, (h + 1) * HGRN_DIM)
        o = _rms(of_ref[:, sl] + ob_ref[:, sl], hg_ref[...])
        og = hog_ref[:, sl]
        hs.append((o * (og * _sigmoid(og))).astype(BF16))
    yh = jnp.dot(jnp.concatenate(hs, axis=1), wh_ref[...], preferred_element_type=F32)
    o_ref[...] = (_sigmoid(ga_ref[...]) * ya + _sigmoid(gb_ref[...]) * yh).astype(o_ref.dtype)


def merge_branches(attn, o_f, o_b, proj, hgrn_gain, w_attn, w_hgrn, tm=256):
    T = attn.shape[0]
    D = D_MODEL
    return pl.pallas_call(
        _merge_body,
        grid=(T // tm,),
        in_specs=[
            pl.BlockSpec((tm, ATTN_WIDTH), lambda i: (i, 0)),
            pl.BlockSpec((tm, HGRN_WIDTH), lambda i: (i, 0)),
            pl.BlockSpec((tm, HGRN_WIDTH), lambda i: (i, 0)),
            pl.BlockSpec((tm, HGRN_WIDTH), lambda i: (i, COL_HOG // HGRN_WIDTH)),
            pl.BlockSpec((tm, D), lambda i: (i, COL_GA // D)),
            pl.BlockSpec((tm, D), lambda i: (i, COL_GB // D)),
            pl.BlockSpec((1, HGRN_DIM), lambda i: (0, 0)),
            pl.BlockSpec((ATTN_WIDTH, D), lambda i: (0, 0)),
            pl.BlockSpec((HGRN_WIDTH, D), lambda i: (0, 0)),
        ],
        out_specs=pl.BlockSpec((tm, D), lambda i: (i, 0)),
        out_shape=jax.ShapeDtypeStruct((T, D), BF16),
        compiler_params=_cparams(1, 56),
        name="merge_branches",
    )(attn, o_f, o_b, proj, proj, proj, hgrn_gain.reshape(1, HGRN_DIM), w_attn, w_hgrn)


def _outproj_body(m_ref, w_ref, x_ref, g_ref, wr_ref, xo_ref, h_ref, aff_ref):
    xn = x_ref[...] + jnp.dot(m_ref[...], w_ref[...], preferred_element_type=F32)
    xo_ref[...] = xn
    h = _rms(xn, g_ref[...])
    h_ref[...] = h.astype(h_ref.dtype)
    hb = h.astype(BF16)
    hl = (h - hb.astype(F32)).astype(BF16)
    logits = (jnp.dot(hb, wr_ref[0], preferred_element_type=F32) + jnp.dot(hl, wr_ref[0], preferred_element_type=F32)
              + jnp.dot(hb, wr_ref[1], preferred_element_type=F32))
    mx = jnp.max(logits, axis=1, keepdims=True)
    ex = jnp.exp(logits - mx)
    aff_ref[...] = ex / jnp.sum(ex, axis=1, keepdims=True)


def out_proj_router(merged, w_out, x, ffn_gain, w_router, tm=256):
    T, D = x.shape
    wr_hi = w_router.astype(BF16)
    wr_lo = (w_router - wr_hi.astype(F32)).astype(BF16)
    wr = jnp.stack([wr_hi, wr_lo], axis=0)
    return pl.pallas_call(
        _outproj_body,
        grid=(T // tm,),
        in_specs=[
            pl.BlockSpec((tm, D), lambda i: (i, 0)),
            pl.BlockSpec((D, D), lambda i: (0, 0)),
            pl.BlockSpec((tm, D), lambda i: (i, 0)),
            pl.BlockSpec((1, D), lambda i: (0, 0)),
            pl.BlockSpec((2, D, N_EXPERTS), lambda i: (0, 0, 0)),
        ],
        out_specs=[
            pl.BlockSpec((tm, D), lambda i: (i, 0)),
            pl.BlockSpec((tm, D), lambda i: (i, 0)),
            pl.BlockSpec((tm, N_EXPERTS), lambda i: (i, 0)),
        ],
        out_shape=[
            jax.ShapeDtypeStruct((T, D), F32),
            jax.ShapeDtypeStruct((T, D), BF16),
            jax.ShapeDtypeStruct((T, N_EXPERTS), F32),
        ],
        compiler_params=_cparams(1, 48),
        name="out_proj_router",
    )(merged, w_out, x, ffn_gain.reshape(1, D), wr)


def _ffn_body(x_ref, wg_ref, wu_ref, wd_ref, gv_ref, o_ref):
    x = x_ref[0]
    g = jnp.dot(x, wg_ref[0], preferred_element_type=F32)
    u = jnp.dot(x, wu_ref[0], preferred_element_type=F32)
    a = (g * _sigmoid(g) * u).astype(BF16)
    o_ref[0] = jnp.dot(a, wd_ref[0], preferred_element_type=F32) * gv_ref[0]


def expert_ffn(xe, w_gate, w_up, w_down, gate_vals, tm=256):
    E, cap, D = xe.shape
    F = w_gate.shape[-1]
    return pl.pallas_call(
        _ffn_body,
        grid=(E, cap // tm),
        in_specs=[
            pl.BlockSpec((1, tm, D), lambda e, i: (e, i, 0)),
            pl.BlockSpec((1, D, F), lambda e, i: (e, 0, 0)),
            pl.BlockSpec((1, D, F), lambda e, i: (e, 0, 0)),
            pl.BlockSpec((1, F, D), lambda e, i: (e, 0, 0)),
            pl.BlockSpec((1, tm, 1), lambda e, i: (e, i, 0)),
        ],
        out_specs=pl.BlockSpec((1, tm, D), lambda e, i: (e, i, 0)),
        out_shape=jax.ShapeDtypeStruct((E, cap, D), F32),
        compiler_params=_cparams(2, 48),
        name="expert_ffn",
    )(xe, w_gate, w_up, w_down, gate_vals.reshape(E, cap, 1))


def kernel(x_prompt, x_sample, w_in, q_norm, k_norm, lower_bounds, hgrn_norm, w_proj_attn, w_proj_hgrn,
           w_out, norm_mix, norm_ffn, w_router, w_gate, w_up, w_down, norm_final):
    depth = w_in.shape[0]
    groups = (x_prompt, x_sample)
    seq = x_prompt.shape[1]
    group_tokens = [g.shape[0] * g.shape[1] for g in groups]
    x = jnp.concatenate([g.reshape(-1, D_MODEL) for g in groups], axis=0)
    cos, sin = rope_tables(seq)

    off = np.cumsum((0, ATTN_WIDTH, KV_WIDTH, KV_WIDTH, HGRN_WIDTH, HGRN_WIDTH, HGRN_WIDTH, HGRN_WIDTH,
                     HGRN_WIDTH, D_MODEL, D_MODEL))
    order = (8, 9, 3, 4, 5, 6, 7, 0, 1, 2)

    for l in range(depth):
        w_in_l = jnp.concatenate([w_in[l][:, off[s]:off[s + 1]] for s in order], axis=1).astype(BF16)
        h = rms_norm_rows(x, norm_mix[l], BF16)
        proj = matmul(h, w_in_l, F32)
        q, k, v = qk_prep(proj, cos, sin, q_norm[l], k_norm[l], seq)
        attn = attention(q, k, v, seq)
        o_f, o_b = hgrn_scan(proj, hgrn_params(lower_bounds, l), seq)
        merged = merge_branches(attn, o_f, o_b, proj, hgrn_norm[l],
                                w_proj_attn[l].astype(BF16), w_proj_hgrn[l].astype(BF16))
        x, h_ffn, aff = out_proj_router(merged, w_out[l].astype(BF16), x, norm_ffn[l], w_router[l])
        aff_t = aff.T
        wg, wu, wd = w_gate[l].astype(BF16), w_up[l].astype(BF16), w_down[l].astype(BF16)
        start = 0
        for n_tok in group_tokens:
            cap = EC_CAPACITY * n_tok // N_EXPERTS
            gate_vals, idx = lax.top_k(aff_t[:, start:start + n_tok], cap)
            idx = idx + start
            ye = expert_ffn(h_ffn[idx], wg, wu, wd, gate_vals)
            x = x.at[idx.reshape(-1)].add(ye.reshape(-1, D_MODEL))
            start += n_tok

    y = rms_norm_rows(x, norm_final, F32)
    outs = []
    start = 0
    for g, n_tok in zip(groups, group_tokens):
        outs.append(y[start:start + n_tok].reshape(g.shape))
        start += n_tok
    return tuple(outs)
```

```python
import functools

import numpy as np
import jax
import jax.numpy as jnp
from jax import lax
from jax.experimental import pallas as pl
from jax.experimental.pallas import tpu as pltpu

F32 = jnp.float32
BF16 = jnp.bfloat16

D_MODEL = 2048
SEQ = 4096
GRID_W = 64
HEAD_DIM = 128
N_Q_HEADS = 8
N_KV_HEADS = 2
Q_GROUP = N_Q_HEADS // N_KV_HEADS
ATTN_WIDTH = N_Q_HEADS * HEAD_DIM
KV_WIDTH = N_KV_HEADS * HEAD_DIM
ROPE_THETA = 10000.0
ROPE_AXIS_PAIRS = HEAD_DIM // 4
N_HGRN_HEADS = 8
HGRN_DIM = 128
HGRN_WIDTH = N_HGRN_HEADS * HGRN_DIM
N_EXPERTS = 16
EC_CAPACITY = 2
EXPERT_FF = 1024
NORM_EPS = 1e-6
IN_WIDTH = 10752

COL_GA = 0
COL_GB = 2048
COL_HQ = 4096
COL_ZF = 5120
COL_ZB = 6144
COL_HI = 7168
COL_HOG = 8192
COL_QKV = 9216
QKV_WIDTH = ATTN_WIDTH + 2 * KV_WIDTH

HGRN_CHUNK = 64
HGRN_SUB = 16
HGRN_HEADS_PER_STEP = 8
EXP_CLAMP = 80.0

META_WIDTH = 128
COMBINE_TM = 256
WIN_BLK = 16
WIN_BLKS = 4
WIN_ROWS = WIN_BLK * WIN_BLKS
COMBINE_ROUNDS = -(-(WIN_BLK - 1 + COMBINE_TM) // WIN_ROWS)

V7X_VMEM_BYTES = 64 * 1024 * 1024


def _cparams(n_grid, vmem_mb):
    assert vmem_mb * 1024 * 1024 < V7X_VMEM_BYTES
    return pltpu.CompilerParams(
        dimension_semantics=("arbitrary",) * n_grid,
        vmem_limit_bytes=vmem_mb * 1024 * 1024,
    )


def _sigmoid(x):
    return 1.0 / (1.0 + jnp.exp(-x))


def _rms(x, gain):
    return x * lax.rsqrt(jnp.mean(x * x, axis=-1, keepdims=True) + NORM_EPS) * gain


def _dot_nt(a, b):
    return lax.dot_general(a, b, (((1,), (1,)), ((), ())), preferred_element_type=F32)


def _dot_tn(a, b):
    return lax.dot_general(a, b, (((0,), (0,)), ((), ())), preferred_element_type=F32)


def _norm_body(x_ref, g_ref, o_ref):
    o_ref[...] = _rms(x_ref[...], g_ref[...]).astype(o_ref.dtype)


def rms_norm_rows(x, gain, out_dtype, tm=512):
    T, D = x.shape
    return pl.pallas_call(
        _norm_body,
        grid=(T // tm,),
        in_specs=[pl.BlockSpec((tm, D), lambda i: (i, 0)), pl.BlockSpec((1, D), lambda i: (0, 0))],
        out_specs=pl.BlockSpec((tm, D), lambda i: (i, 0)),
        out_shape=jax.ShapeDtypeStruct((T, D), out_dtype),
        compiler_params=_cparams(1, 32),
        name="rms_norm_rows",
    )(x, gain.reshape(1, D))


def _mm_body(a_ref, w_ref, o_ref):
    o_ref[...] = jnp.dot(a_ref[...], w_ref[...], preferred_element_type=F32).astype(o_ref.dtype)


def matmul(a, w, out_dtype, tm=512, tn=1536):
    M, K = a.shape
    N = w.shape[1]
    return pl.pallas_call(
        _mm_body,
        grid=(N // tn, M // tm),
        in_specs=[pl.BlockSpec((tm, K), lambda j, i: (i, 0)), pl.BlockSpec((K, tn), lambda j, i: (0, j))],
        out_specs=pl.BlockSpec((tm, tn), lambda j, i: (i, j)),
        out_shape=jax.ShapeDtypeStruct((M, N), out_dtype),
        compiler_params=_cparams(2, 48),
        name="in_proj",
    )(a, w)


def _qkprep_body(p_ref, cos_ref, sin_ref, qg_ref, kg_ref, q_ref, k_ref, v_ref, *, scale):
    cos = cos_ref[...]
    sin = sin_ref[...]
    lane = lax.broadcasted_iota(jnp.int32, cos.shape, 1)
    first = (lane & ROPE_AXIS_PAIRS) == 0

    def prep(x, g, s):
        y = _rms(x, g)
        partner = jnp.where(first, pltpu.roll(y, HEAD_DIM - ROPE_AXIS_PAIRS, 1), pltpu.roll(y, ROPE_AXIS_PAIRS, 1))
        return (y * cos + partner * sin) * s

    for h in range(N_Q_HEADS):
        sl = slice(h * HEAD_DIM, (h + 1) * HEAD_DIM)
        q_ref[:, sl] = prep(p_ref[:, sl], qg_ref[...], scale).astype(q_ref.dtype)
    for h in range(N_KV_HEADS):
        sl = slice(h * HEAD_DIM, (h + 1) * HEAD_DIM)
        src = slice(ATTN_WIDTH + h * HEAD_DIM, ATTN_WIDTH + (h + 1) * HEAD_DIM)
        k_ref[:, sl] = prep(p_ref[:, src], kg_ref[...], 1.0).astype(k_ref.dtype)
    v_ref[...] = p_ref[:, ATTN_WIDTH + KV_WIDTH:].astype(v_ref.dtype)


def qk_prep(proj, cos, sin, q_gain, k_gain, seq, tm=256):
    T = proj.shape[0]
    nblk = seq // tm
    return pl.pallas_call(
        functools.partial(_qkprep_body, scale=HEAD_DIM ** -0.5),
        grid=(T // tm,),
        in_specs=[
            pl.BlockSpec((tm, QKV_WIDTH), lambda i: (i, COL_QKV // QKV_WIDTH)),
            pl.BlockSpec((tm, HEAD_DIM), lambda i: (i % nblk, 0)),
            pl.BlockSpec((tm, HEAD_DIM), lambda i: (i % nblk, 0)),
            pl.BlockSpec((1, HEAD_DIM), lambda i: (0, 0)),
            pl.BlockSpec((1, HEAD_DIM), lambda i: (0, 0)),
        ],
        out_specs=[
            pl.BlockSpec((tm, ATTN_WIDTH), lambda i: (i, 0)),
            pl.BlockSpec((tm, KV_WIDTH), lambda i: (i, 0)),
            pl.BlockSpec((tm, KV_WIDTH), lambda i: (i, 0)),
        ],
        out_shape=[
            jax.ShapeDtypeStruct((T, ATTN_WIDTH), BF16),
            jax.ShapeDtypeStruct((T, KV_WIDTH), BF16),
            jax.ShapeDtypeStruct((T, KV_WIDTH), BF16),
        ],
        compiler_params=_cparams(1, 32),
        name="qk_prep",
    )(proj, cos, sin, q_gain.reshape(1, HEAD_DIM), k_gain.reshape(1, HEAD_DIM))


def rope_tables(seq):
    rows = seq // GRID_W
    row = jnp.repeat(jnp.arange(rows, dtype=F32), GRID_W)
    col = jnp.tile(jnp.arange(GRID_W, dtype=F32), rows)
    inv_freq = 1.0 / (ROPE_THETA ** (jnp.arange(ROPE_AXIS_PAIRS, dtype=F32) / ROPE_AXIS_PAIRS))
    ang_r = row[:, None] * inv_freq
    ang_c = col[:, None] * inv_freq
    cos = jnp.concatenate([jnp.cos(ang_r), jnp.cos(ang_r), jnp.cos(ang_c), jnp.cos(ang_c)], axis=1)
    sin = jnp.concatenate([-jnp.sin(ang_r), jnp.sin(ang_r), -jnp.sin(ang_c), jnp.sin(ang_c)], axis=1)
    return cos, sin


def _attn_body(q_ref, k_ref, v_ref, o_ref):
    k = k_ref[...]
    v = v_ref[...]
    for g in range(Q_GROUP):
        sl = slice(g * HEAD_DIM, (g + 1) * HEAD_DIM)
        s = _dot_nt(q_ref[:, sl], k)
        m = jnp.max(s, axis=-1, keepdims=True)
        p = jnp.exp(s - m)
        l = jnp.sum(p, axis=-1, keepdims=True)
        o = jnp.dot(p.astype(BF16), v, preferred_element_type=F32)
        o_ref[:, sl] = (o / l).astype(o_ref.dtype)


def attention(q, k, v, seq, tq=128):
    T = q.shape[0]
    n_seq = T // seq
    nq = seq // tq
    gw = Q_GROUP * HEAD_DIM
    return pl.pallas_call(
        _attn_body,
        grid=(n_seq, N_KV_HEADS, nq),
        in_specs=[
            pl.BlockSpec((tq, gw), lambda b, h, i: (b * nq + i, h)),
            pl.BlockSpec((seq, HEAD_DIM), lambda b, h, i: (b, h)),
            pl.BlockSpec((seq, HEAD_DIM), lambda b, h, i: (b, h)),
        ],
        out_specs=pl.BlockSpec((tq, gw), lambda b, h, i: (b * nq + i, h)),
        out_shape=jax.ShapeDtypeStruct((T, ATTN_WIDTH), BF16),
        compiler_params=_cparams(3, 48),
        name="attention",
    )(q, k, v)


def _split3(x):
    hi = x.astype(BF16)
    r1 = x - hi.astype(F32)
    mid = r1.astype(BF16)
    lo = (r1 - mid.astype(F32)).astype(BF16)
    return hi, mid, lo


def _hgrn_gates(z, loglb, log1mlb, omlb):
    e = jnp.exp(-jnp.abs(z))
    r = 1.0 / (1.0 + e)
    kk = omlb * jnp.where(z >= 0, e * r, r)
    cc = log1mlb + (jnp.minimum(z, 0.0) - jnp.log1p(e))
    g = jnp.maximum(loglb, cc) + jnp.log1p(jnp.exp(-jnp.abs(loglb - cc)))
    return g, kk


def _hgrn_body(qf_ref, zf_ref, vf_ref, qb_ref, zb_ref, vb_ref, par_ref, of_ref, ob_ref, stf_ref, stb_ref):
    C, SB, W = HGRN_CHUNK, HGRN_SUB, HGRN_DIM
    NH = HGRN_HEADS_PER_STEP

    @pl.when(pl.program_id(2) == 0)
    def _():
        stf_ref[...] = jnp.zeros(stf_ref.shape, F32)
        stb_ref[...] = jnp.zeros(stb_ref.shape, F32)

    row = lax.broadcasted_iota(jnp.int32, (C, C), 0)
    col = lax.broadcasted_iota(jnp.int32, (C, C), 1)
    dirs = (
        (True, qf_ref, zf_ref, vf_ref, 0, jnp.where(col <= row, 1.0, 0.0).astype(BF16), stf_ref, of_ref),
        (False, qb_ref, zb_ref, vb_ref, 3, jnp.where(col >= row, 1.0, 0.0).astype(BF16), stb_ref, ob_ref),
    )
    gs, kks = [], []
    for fwd, q_ref, z_ref, v_ref, p0, tri, st_ref, o_ref in dirs:
        g, kk = _hgrn_gates(z_ref[...], par_ref[p0:p0 + 1, :], par_ref[p0 + 1:p0 + 2, :], par_ref[p0 + 2:p0 + 3, :])
        gs.append(g)
        kks.append(kk)
    bs = []
    for (fwd, q_ref, z_ref, v_ref, p0, tri, st_ref, o_ref), g in zip(dirs, gs):
        hi, mid, lo = _split3(g)
        bs.append(jnp.dot(tri, hi, preferred_element_type=F32) + jnp.dot(tri, mid, preferred_element_type=F32)
                  + jnp.dot(tri, lo, preferred_element_type=F32))
    work = []
    for (fwd, q_ref, z_ref, v_ref, p0, tri, st_ref, o_ref), b, kk in zip(dirs, bs, kks):
        q = q_ref[...]
        vb = v_ref[...].astype(BF16)
        tot = b[C - 1:C, :] if fwd else b[0:1, :]
        q_in = (q * jnp.exp(b)).astype(BF16)
        k_st = (kk * jnp.exp(tot - b)).astype(BF16)
        dec = jnp.exp(tot)
        subs = []
        for i in range(C // SB):
            r0 = i * SB
            if fwd:
                ref = b[r0 - 1:r0, :] if i > 0 else jnp.zeros((1, NH * W), F32)
                k0, k1 = 0, r0 + SB
            else:
                ref = b[r0 + SB:r0 + SB + 1, :] if r0 + SB < C else jnp.zeros((1, NH * W), F32)
                k0, k1 = r0, C
            qs = (q[r0:r0 + SB] * jnp.exp(b[r0:r0 + SB] - ref)).astype(BF16)
            ks = (kk[k0:k1] * jnp.exp(jnp.minimum(ref - b[k0:k1], EXP_CLAMP))).astype(BF16)
            subs.append((r0, k0, k1, qs, ks))
        work.append((fwd, vb, q_in, k_st, dec, subs, st_ref, o_ref))
    res = []
    for fwd, vb, q_in, k_st, dec, subs, st_ref, o_ref in work:
        for h in range(NH):
            sl = slice(h * W, (h + 1) * W)
            st = st_ref[h]
            o_inter = _dot_nt(q_in[:, sl], st.astype(BF16))
            st_ref[h] = st * dec[:, sl] + _dot_tn(vb[:, sl], k_st[:, sl])
            scores = [_dot_nt(qs[:, sl], ks[:, sl]) for (r0, k0, k1, qs, ks) in subs]
            res.append((fwd, h, o_inter, scores, vb, subs, o_ref))
    masked = []
    for fwd, h, o_inter, scores, vb, subs, o_ref in res:
        ms = []
        for s, (r0, k0, k1, qs, ks) in zip(scores, subs):
            t_idx = r0 + lax.broadcasted_iota(jnp.int32, s.shape, 0)
            s_idx = k0 + lax.broadcasted_iota(jnp.int32, s.shape, 1)
            keep = (s_idx <= t_idx) if fwd else (s_idx >= t_idx)
            ms.append(jnp.where(keep, s, 0.0).astype(BF16))
        masked.append(ms)
    for (fwd, h, o_inter, scores, vb, subs, o_ref), ms in zip(res, masked):
        sl = slice(h * W, (h + 1) * W)
        outs = [jnp.dot(m, vb[k0:k1, sl], preferred_element_type=F32) for m, (r0, k0, k1, qs, ks) in zip(ms, subs)]
        o_ref[:, sl] = o_inter + jnp.concatenate(outs, axis=0)


def hgrn_scan(proj, params, seq):
    T = proj.shape[0]
    n_seq = T // seq
    C = HGRN_CHUNK
    nc = seq // C
    w = HGRN_HEADS_PER_STEP * HGRN_DIM
    n_hh = HGRN_WIDTH // w

    def fwd_spec(col):
        return pl.BlockSpec((C, w), lambda b, hh, j: (b * nc + j, col // w + hh))

    def bwd_spec(col):
        return pl.BlockSpec((C, w), lambda b, hh, j: (b * nc + nc - 1 - j, col // w + hh))

    return pl.pallas_call(
        _hgrn_body,
        grid=(n_seq, n_hh, nc),
        in_specs=[
            fwd_spec(COL_HQ), fwd_spec(COL_ZF), fwd_spec(COL_HI),
            bwd_spec(COL_HQ), bwd_spec(COL_ZB), bwd_spec(COL_HI),
            pl.BlockSpec((8, w), lambda b, hh, j: (0, hh)),
        ],
        out_specs=[
            pl.BlockSpec((C, w), lambda b, hh, j: (b * nc + j, hh)),
            pl.BlockSpec((C, w), lambda b, hh, j: (b * nc + nc - 1 - j, hh)),
        ],
        out_shape=[jax.ShapeDtypeStruct((T, HGRN_WIDTH), F32), jax.ShapeDtypeStruct((T, HGRN_WIDTH), F32)],
        scratch_shapes=[
            pltpu.VMEM((HGRN_HEADS_PER_STEP, HGRN_DIM, HGRN_DIM), F32),
            pltpu.VMEM((HGRN_HEADS_PER_STEP, HGRN_DIM, HGRN_DIM), F32),
        ],
        compiler_params=_cparams(3, 32),
        name="hgrn_scan",
    )(proj, proj, proj, proj, proj, proj, params)


def hgrn_params(lower_bounds, layer):
    lb_all = jnp.cumsum(jax.nn.softmax(lower_bounds.astype(F32), axis=1), axis=1)
    lb_all = lb_all - lb_all[:, :1]
    rows = []
    for d in range(2):
        lb = lb_all[d, layer]
        rows += [jnp.log(lb), jnp.log1p(-lb), 1.0 - lb]
    rows += [jnp.zeros_like(rows[0])] * 2
    return jnp.stack(rows, axis=0)


def _merge_body(attn_ref, of_ref, ob_ref, hog_ref, ga_ref, gb_ref, hg_ref, wa_ref, wh_ref, o_ref):
    ya = jnp.dot(attn_ref[...], wa_ref[...], preferred_element_type=F32)
    hs = []
    for h in range(N_HGRN_HEADS):
        sl = slice(h * HGRN_DIM, (h + 1) * HGRN_DIM)
        o = _rms(of_ref[:, sl] + ob_ref[:, sl], hg_ref[...])
        og = hog_ref[:, sl]
        hs.append((o * (og * _sigmoid(og))).astype(BF16))
    yh = jnp.dot(jnp.concatenate(hs, axis=1), wh_ref[...], preferred_element_type=F32)
    o_ref[...] = (_sigmoid(ga_ref[...]) * ya + _sigmoid(gb_ref[...]) * yh).astype(o_ref.dtype)


def merge_branches(attn, o_f, o_b, proj, hgrn_gain, w_attn, w_hgrn, tm=256):
    T = attn.shape[0]
    D = D_MODEL
    return pl.pallas_call(
        _merge_body,
        grid=(T // tm,),
        in_specs=[
            pl.BlockSpec((tm, ATTN_WIDTH), lambda i: (i, 0)),
            pl.BlockSpec((tm, HGRN_WIDTH), lambda i: (i, 0)),
            pl.BlockSpec((tm, HGRN_WIDTH), lambda i: (i, 0)),
            pl.BlockSpec((tm, HGRN_WIDTH), lambda i: (i, COL_HOG // HGRN_WIDTH)),
            pl.BlockSpec((tm, D), lambda i: (i, COL_GA // D)),
            pl.BlockSpec((tm, D), lambda i: (i, COL_GB // D)),
            pl.BlockSpec((1, HGRN_DIM), lambda i: (0, 0)),
            pl.BlockSpec((ATTN_WIDTH, D), lambda i: (0, 0)),
            pl.BlockSpec((HGRN_WIDTH, D), lambda i: (0, 0)),
        ],
        out_specs=pl.BlockSpec((tm, D), lambda i: (i, 0)),
        out_shape=jax.ShapeDtypeStruct((T, D), BF16),
        compiler_params=_cparams(1, 56),
        name="merge_branches",
    )(attn, o_f, o_b, proj, proj, proj, hgrn_gain.reshape(1, HGRN_DIM), w_attn, w_hgrn)


def _outproj_body(m_ref, w_ref, x_ref, g_ref, wr_ref, xo_ref, h_ref, aff_ref):
    xn = x_ref[...] + jnp.dot(m_ref[...], w_ref[...], preferred_element_type=F32)
    xo_ref[...] = xn
    h = _rms(xn, g_ref[...])
    h_ref[...] = h.astype(h_ref.dtype)
    hb = h.astype(BF16)
    hl = (h - hb.astype(F32)).astype(BF16)
    logits = (jnp.dot(hb, wr_ref[0], preferred_element_type=F32) + jnp.dot(hl, wr_ref[0], preferred_element_type=F32)
              + jnp.dot(hb, wr_ref[1], preferred_element_type=F32))
    mx = jnp.max(logits, axis=1, keepdims=True)
    ex = jnp.exp(logits - mx)
    aff_ref[...] = ex / jnp.sum(ex, axis=1, keepdims=True)


def out_proj_router(merged, w_out, x, ffn_gain, w_router, tm=256):
    T, D = x.shape
    wr_hi = w_router.astype(BF16)
    wr_lo = (w_router - wr_hi.astype(F32)).astype(BF16)
    wr = jnp.stack([wr_hi, wr_lo], axis=0)
    return pl.pallas_call(
        _outproj_body,
        grid=(T // tm,),
        in_specs=[
            pl.BlockSpec((tm, D), lambda i: (i, 0)),
            pl.BlockSpec((D, D), lambda i: (0, 0)),
            pl.BlockSpec((tm, D), lambda i: (i, 0)),
            pl.BlockSpec((1, D), lambda i: (0, 0)),
            pl.BlockSpec((2, D, N_EXPERTS), lambda i: (0, 0, 0)),
        ],
        out_specs=[
            pl.BlockSpec((tm, D), lambda i: (i, 0)),
            pl.BlockSpec((tm, D), lambda i: (i, 0)),
            pl.BlockSpec((tm, N_EXPERTS), lambda i: (i, 0)),
        ],
        out_shape=[
            jax.ShapeDtypeStruct((T, D), F32),
            jax.ShapeDtypeStruct((T, D), BF16),
            jax.ShapeDtypeStruct((T, N_EXPERTS), F32),
        ],
        compiler_params=_cparams(1, 48),
        name="out_proj_router",
    )(merged, w_out, x, ffn_gain.reshape(1, D), wr)


def _ffn_body(x_ref, wg_ref, wu_ref, wd_ref, gv_ref, meta_ref, o_ref):
    x = x_ref[0]
    g = jnp.dot(x, wg_ref[0], preferred_element_type=F32)
    u = jnp.dot(x, wu_ref[0], preferred_element_type=F32)
    a = (g * _sigmoid(g) * u).astype(BF16)
    y = jnp.dot(a, wd_ref[0], preferred_element_type=F32) * gv_ref[0]
    o_ref[0, :, :D_MODEL] = y.astype(o_ref.dtype)
    o_ref[0, :, D_MODEL:] = meta_ref[0]


def expert_ffn(xe, w_gate, w_up, w_down, gate_vals, meta, tm=256):
    E, cap, D = xe.shape
    F = w_gate.shape[-1]
    return pl.pallas_call(
        _ffn_body,
        grid=(E, cap // tm),
        in_specs=[
            pl.BlockSpec((1, tm, D), lambda e, i: (e, i, 0)),
            pl.BlockSpec((1, D, F), lambda e, i: (e, 0, 0)),
            pl.BlockSpec((1, D, F), lambda e, i: (e, 0, 0)),
            pl.BlockSpec((1, F, D), lambda e, i: (e, 0, 0)),
            pl.BlockSpec((1, tm, 1), lambda e, i: (e, i, 0)),
            pl.BlockSpec((1, tm, META_WIDTH), lambda e, i: (e, i, 0)),
        ],
        out_specs=pl.BlockSpec((1, tm, D + META_WIDTH), lambda e, i: (e, i, 0)),
        out_shape=jax.ShapeDtypeStruct((E, cap, D + META_WIDTH), BF16),
        compiler_params=_cparams(2, 48),
        name="expert_ffn",
    )(xe, w_gate, w_up, w_down, gate_vals.reshape(E, cap, 1), meta)


def _combine_body(win_ref, minslot_ref, nr_ref, x_ref, *refs, base_tile, n_tiles):
    ye_refs = refs[:N_EXPERTS]
    o_ref = refs[N_EXPERTS]
    j = pl.program_id(0)
    k = pl.program_id(1)

    @pl.when(k == 0)
    def _():
        o_ref[...] = x_ref[...]

    @pl.when(k < nr_ref[j])
    def _():
        t0 = (j + base_tile) * COMBINE_TM
        rows, toks = [], []
        for e in range(N_EXPERTS):
            w = ye_refs[e][...].reshape(WIN_ROWS, D_MODEL + META_WIDTH)
            meta = w[:, D_MODEL:].astype(F32)
            tok = meta[:, 0:1] * 128.0 + meta[:, 1:2] - t0.astype(F32)
            flat = e * (n_tiles * COMBINE_ROUNDS) + j * COMBINE_ROUNDS + k
            slot = win_ref[flat] * WIN_BLK + lax.broadcasted_iota(jnp.int32, (WIN_ROWS, 1), 0)
            toks.append(jnp.where(slot >= minslot_ref[flat], tok, -1.0))
            rows.append(w[:, :D_MODEL])
        tok_all = jnp.concatenate(toks, axis=0)
        lane_t = lax.broadcasted_iota(jnp.int32, (N_EXPERTS * WIN_ROWS, COMBINE_TM), 1).astype(F32)
        place = jnp.where(tok_all == lane_t, 1.0, 0.0).astype(BF16)
        o_ref[...] += _dot_tn(place, jnp.concatenate(rows, axis=0))


def combine(x, ye_ext, win, minslot, n_rounds, base_tile):
    E, cap, dext = ye_ext.shape
    n_tiles = n_rounds.shape[0]
    ye4 = ye_ext.reshape(E, cap // WIN_BLK, WIN_BLK, dext)

    def ye_spec(e):
        return pl.BlockSpec(
            (pl.Element(1), pl.Element(WIN_BLKS), pl.Element(WIN_BLK), pl.Element(dext)),
            lambda j, k, win_r, ms_r, nr_r: (e, win_r[e * (n_tiles * COMBINE_ROUNDS) + j * COMBINE_ROUNDS + k], 0, 0))

    grid_spec = pltpu.PrefetchScalarGridSpec(
        num_scalar_prefetch=3,
        grid=(n_tiles, COMBINE_ROUNDS),
        in_specs=[pl.BlockSpec((COMBINE_TM, D_MODEL), lambda j, k, *_: (j + base_tile, 0))]
        + [ye_spec(e) for e in range(E)],
        out_specs=pl.BlockSpec((COMBINE_TM, D_MODEL), lambda j, k, *_: (j + base_tile, 0)),
    )
    return pl.pallas_call(
        functools.partial(_combine_body, base_tile=base_tile, n_tiles=n_tiles),
        grid_spec=grid_spec,
        out_shape=jax.ShapeDtypeStruct(x.shape, x.dtype),
        input_output_aliases={3: 0},
        compiler_params=_cparams(2, 48),
        name="combine",
    )(win.reshape(-1), minslot.reshape(-1), n_rounds, x, *([ye4] * E))


def combine_tables(idx_sorted, cap, n_tok, base_tok):
    n_tiles = n_tok // COMBINE_TM
    edges = base_tok + COMBINE_TM * jnp.arange(n_tiles + 1, dtype=jnp.int32)
    starts = jax.vmap(lambda row: jnp.searchsorted(row, edges, side="left"))(idx_sorted).astype(jnp.int32)
    lo, hi = starts[:, :-1], starts[:, 1:]
    a = (lo // WIN_BLK) * WIN_BLK
    need = jnp.where(hi > lo, -(-(hi - a) // WIN_ROWS), 0)
    k = jnp.arange(COMBINE_ROUNDS, dtype=jnp.int32)[None, None, :]
    kk = jnp.minimum(k, jnp.maximum(need[..., None] - 1, 0))
    win = jnp.minimum(a[..., None] + WIN_ROWS * kk, cap - WIN_ROWS) // WIN_BLK
    minslot = jnp.where(k < need[..., None], a[..., None] + WIN_ROWS * k, cap)
    return win.astype(jnp.int32), minslot.astype(jnp.int32), jnp.max(need, axis=0).astype(jnp.int32)


def kernel(x_prompt, x_sample, w_in, q_norm, k_norm, lower_bounds, hgrn_norm, w_proj_attn, w_proj_hgrn,
           w_out, norm_mix, norm_ffn, w_router, w_gate, w_up, w_down, norm_final):
    depth = w_in.shape[0]
    groups = (x_prompt, x_sample)
    seq = x_prompt.shape[1]
    group_tokens = [g.shape[0] * g.shape[1] for g in groups]
    x = jnp.concatenate([g.reshape(-1, D_MODEL) for g in groups], axis=0)
    cos, sin = rope_tables(seq)

    off = np.cumsum((0, ATTN_WIDTH, KV_WIDTH, KV_WIDTH, HGRN_WIDTH, HGRN_WIDTH, HGRN_WIDTH, HGRN_WIDTH,
                     HGRN_WIDTH, D_MODEL, D_MODEL))
    order = (8, 9, 3, 4, 5, 6, 7, 0, 1, 2)

    for l in range(depth):
        w_in_l = jnp.concatenate([w_in[l][:, off[s]:off[s + 1]] for s in order], axis=1).astype(BF16)
        h = rms_norm_rows(x, norm_mix[l], BF16)
        proj = matmul(h, w_in_l, F32)
        q, k, v = qk_prep(proj, cos, sin, q_norm[l], k_norm[l], seq)
        attn = attention(q, k, v, seq)
        o_f, o_b = hgrn_scan(proj, hgrn_params(lower_bounds, l), seq)
        merged = merge_branches(attn, o_f, o_b, proj, hgrn_norm[l],
                                w_proj_attn[l].astype(BF16), w_proj_hgrn[l].astype(BF16))
        x, h_ffn, aff = out_proj_router(merged, w_out[l].astype(BF16), x, norm_ffn[l], w_router[l])
        aff_t = aff.T
        wg, wu, wd = w_gate[l].astype(BF16), w_up[l].astype(BF16), w_down[l].astype(BF16)
        start = 0
        for n_tok in group_tokens:
            cap = EC_CAPACITY * n_tok // N_EXPERTS
            aff_g = aff_t[:, start:start + n_tok]
            _, idx = lax.top_k(aff_g, cap)
            idx = jnp.sort(idx, axis=1)
            gate_vals = jnp.take_along_axis(aff_g, idx, axis=1)
            idx = idx + start
            meta = jnp.concatenate(
                [(idx // 128)[..., None], (idx % 128)[..., None],
                 jnp.zeros((N_EXPERTS, cap, META_WIDTH - 2), jnp.int32)], axis=-1).astype(BF16)
            ye = expert_ffn(h_ffn[idx], wg, wu, wd, gate_vals, meta)
            win, minslot, n_rounds = combine_tables(idx, cap, n_tok, start)
            x = combine(x, ye, win, minslot, n_rounds, start // COMBINE_TM)
            start += n_tok

    y = rms_norm_rows(x, norm_final, F32)
    outs = []
    start = 0
    for g, n_tok in zip(groups, group_tokens):
        outs.append(y[start:start + n_tok].reshape(g.shape))
        start += n_tok
    return tuple(outs)
```

```python
import functools

import numpy as np
import jax
import jax.numpy as jnp
from jax import lax
from jax.experimental import pallas as pl
from jax.experimental.pallas import tpu as pltpu

F32 = jnp.float32
BF16 = jnp.bfloat16

D_MODEL = 2048
SEQ = 4096
GRID_W = 64
HEAD_DIM = 128
N_Q_HEADS = 8
N_KV_HEADS = 2
Q_GROUP = N_Q_HEADS // N_KV_HEADS
ATTN_WIDTH = N_Q_HEADS * HEAD_DIM
KV_WIDTH = N_KV_HEADS * HEAD_DIM
ROPE_THETA = 10000.0
ROPE_AXIS_PAIRS = HEAD_DIM // 4
N_HGRN_HEADS = 8
HGRN_DIM = 128
HGRN_WIDTH = N_HGRN_HEADS * HGRN_DIM
N_EXPERTS = 16
EC_CAPACITY = 2
EXPERT_FF = 1024
NORM_EPS = 1e-6
IN_WIDTH = 10752

COL_GA = 0
COL_GB = 2048
COL_HQ = 4096
COL_ZF = 5120
COL_ZB = 6144
COL_HI = 7168
COL_HOG = 8192
COL_QKV = 9216
QKV_WIDTH = ATTN_WIDTH + 2 * KV_WIDTH

HGRN_CHUNK = 64
HGRN_SUB = 16
HGRN_HEADS_PER_STEP = 8
EXP_CLAMP = 80.0

META_WIDTH = 128
COMBINE_TM = 256
WIN_BLK = 16
WIN_BLKS = 4
WIN_ROWS = WIN_BLK * WIN_BLKS
COMBINE_ROUNDS = -(-(WIN_BLK - 1 + COMBINE_TM) // WIN_ROWS)

V7X_VMEM_BYTES = 64 * 1024 * 1024


def _cparams(n_grid, vmem_mb):
    assert vmem_mb * 1024 * 1024 < V7X_VMEM_BYTES
    return pltpu.CompilerParams(
        dimension_semantics=("arbitrary",) * n_grid,
        vmem_limit_bytes=vmem_mb * 1024 * 1024,
    )


def _sigmoid(x):
    return 1.0 / (1.0 + jnp.exp(-x))


def _rms(x, gain):
    return x * lax.rsqrt(jnp.mean(x * x, axis=-1, keepdims=True) + NORM_EPS) * gain


def _dot_nt(a, b):
    return lax.dot_general(a, b, (((1,), (1,)), ((), ())), preferred_element_type=F32)


def _dot_tn(a, b):
    return lax.dot_general(a, b, (((0,), (0,)), ((), ())), preferred_element_type=F32)


def _norm_body(x_ref, g_ref, o_ref):
    o_ref[...] = _rms(x_ref[...], g_ref[...]).astype(o_ref.dtype)


def rms_norm_rows(x, gain, out_dtype, tm=512):
    T, D = x.shape
    return pl.pallas_call(
        _norm_body,
        grid=(T // tm,),
        in_specs=[pl.BlockSpec((tm, D), lambda i: (i, 0)), pl.BlockSpec((1, D), lambda i: (0, 0))],
        out_specs=pl.BlockSpec((tm, D), lambda i: (i, 0)),
        out_shape=jax.ShapeDtypeStruct((T, D), out_dtype),
        compiler_params=_cparams(1, 32),
        name="rms_norm_rows",
    )(x, gain.reshape(1, D))


def _mm_body(a_ref, w_ref, o_ref):
    o_ref[...] = jnp.dot(a_ref[...], w_ref[...], preferred_element_type=F32).astype(o_ref.dtype)


IN_PROJ_TN = 512


def _in_proj_source_block(j):
    n_gate = 2 * D_MODEL // IN_PROJ_TN
    n_qkv = QKV_WIDTH // IN_PROJ_TN
    n_all = IN_WIDTH // IN_PROJ_TN
    return jnp.where(j < n_gate, j + (n_all - n_gate), jnp.where(j < n_all - n_qkv, j - n_gate + n_qkv, j - (n_all - n_qkv)))


def in_proj(a, w, out_dtype, tm=1024):
    M, K = a.shape
    N = w.shape[1]
    tn = IN_PROJ_TN
    return pl.pallas_call(
        _mm_body,
        grid=(M // tm, N // tn),
        in_specs=[pl.BlockSpec((tm, K), lambda i, j: (i, 0)),
                  pl.BlockSpec((K, tn), lambda i, j: (0, _in_proj_source_block(j)))],
        out_specs=pl.BlockSpec((tm, tn), lambda i, j: (i, j)),
        out_shape=jax.ShapeDtypeStruct((M, N), out_dtype),
        compiler_params=_cparams(2, 48),
        name="in_proj",
    )(a, w)


def _qkprep_body(p_ref, cos_ref, sin_ref, qg_ref, kg_ref, q_ref, k_ref, v_ref, *, scale):
    cos = cos_ref[...]
    sin = sin_ref[...]
    lane = lax.broadcasted_iota(jnp.int32, cos.shape, 1)
    first = (lane & ROPE_AXIS_PAIRS) == 0

    def prep(x, g, s):
        y = _rms(x, g)
        partner = jnp.where(first, pltpu.roll(y, HEAD_DIM - ROPE_AXIS_PAIRS, 1), pltpu.roll(y, ROPE_AXIS_PAIRS, 1))
        return (y * cos + partner * sin) * s

    for h in range(N_Q_HEADS):
        sl = slice(h * HEAD_DIM, (h + 1) * HEAD_DIM)
        q_ref[:, sl] = prep(p_ref[:, sl], qg_ref[...], scale).astype(q_ref.dtype)
    for h in range(N_KV_HEADS):
        sl = slice(h * HEAD_DIM, (h + 1) * HEAD_DIM)
        src = slice(ATTN_WIDTH + h * HEAD_DIM, ATTN_WIDTH + (h + 1) * HEAD_DIM)
        k_ref[:, sl] = prep(p_ref[:, src], kg_ref[...], 1.0).astype(k_ref.dtype)
    v_ref[...] = p_ref[:, ATTN_WIDTH + KV_WIDTH:].astype(v_ref.dtype)


def qk_prep(proj, cos, sin, q_gain, k_gain, seq, tm=256):
    T = proj.shape[0]
    nblk = seq // tm
    return pl.pallas_call(
        functools.partial(_qkprep_body, scale=HEAD_DIM ** -0.5),
        grid=(T // tm,),
        in_specs=[
            pl.BlockSpec((tm, QKV_WIDTH), lambda i: (i, COL_QKV // QKV_WIDTH)),
            pl.BlockSpec((tm, HEAD_DIM), lambda i: (i % nblk, 0)),
            pl.BlockSpec((tm, HEAD_DIM), lambda i: (i % nblk, 0)),
            pl.BlockSpec((1, HEAD_DIM), lambda i: (0, 0)),
            pl.BlockSpec((1, HEAD_DIM), lambda i: (0, 0)),
        ],
        out_specs=[
            pl.BlockSpec((tm, ATTN_WIDTH), lambda i: (i, 0)),
            pl.BlockSpec((tm, KV_WIDTH), lambda i: (i, 0)),
            pl.BlockSpec((tm, KV_WIDTH), lambda i: (i, 0)),
        ],
        out_shape=[
            jax.ShapeDtypeStruct((T, ATTN_WIDTH), BF16),
            jax.ShapeDtypeStruct((T, KV_WIDTH), BF16),
            jax.ShapeDtypeStruct((T, KV_WIDTH), BF16),
        ],
        compiler_params=_cparams(1, 32),
        name="qk_prep",
    )(proj, cos, sin, q_gain.reshape(1, HEAD_DIM), k_gain.reshape(1, HEAD_DIM))


def rope_tables(seq):
    rows = seq // GRID_W
    row = jnp.repeat(jnp.arange(rows, dtype=F32), GRID_W)
    col = jnp.tile(jnp.arange(GRID_W, dtype=F32), rows)
    inv_freq = 1.0 / (ROPE_THETA ** (jnp.arange(ROPE_AXIS_PAIRS, dtype=F32) / ROPE_AXIS_PAIRS))
    ang_r = row[:, None] * inv_freq
    ang_c = col[:, None] * inv_freq
    cos = jnp.concatenate([jnp.cos(ang_r), jnp.cos(ang_r), jnp.cos(ang_c), jnp.cos(ang_c)], axis=1)
    sin = jnp.concatenate([-jnp.sin(ang_r), jnp.sin(ang_r), -jnp.sin(ang_c), jnp.sin(ang_c)], axis=1)
    return cos, sin


def _attn_body(q_ref, k_ref, v_ref, o_ref):
    k = k_ref[...]
    v = v_ref[...]
    for g in range(Q_GROUP):
        sl = slice(g * HEAD_DIM, (g + 1) * HEAD_DIM)
        s = _dot_nt(q_ref[:, sl], k)
        m = jnp.max(s, axis=-1, keepdims=True)
        p = jnp.exp(s - m)
        l = jnp.sum(p, axis=-1, keepdims=True)
        o = jnp.dot(p.astype(BF16), v, preferred_element_type=F32)
        o_ref[:, sl] = (o / l).astype(o_ref.dtype)


def attention(q, k, v, seq, tq=128):
    T = q.shape[0]
    n_seq = T // seq
    nq = seq // tq
    gw = Q_GROUP * HEAD_DIM
    return pl.pallas_call(
        _attn_body,
        grid=(n_seq, N_KV_HEADS, nq),
        in_specs=[
            pl.BlockSpec((tq, gw), lambda b, h, i: (b * nq + i, h)),
            pl.BlockSpec((seq, HEAD_DIM), lambda b, h, i: (b, h)),
            pl.BlockSpec((seq, HEAD_DIM), lambda b, h, i: (b, h)),
        ],
        out_specs=pl.BlockSpec((tq, gw), lambda b, h, i: (b * nq + i, h)),
        out_shape=jax.ShapeDtypeStruct((T, ATTN_WIDTH), BF16),
        compiler_params=_cparams(3, 48),
        name="attention",
    )(q, k, v)


def _split3(x):
    hi = x.astype(BF16)
    r1 = x - hi.astype(F32)
    mid = r1.astype(BF16)
    lo = (r1 - mid.astype(F32)).astype(BF16)
    return hi, mid, lo


def _hgrn_gates(z, loglb, log1mlb, omlb):
    e = jnp.exp(-jnp.abs(z))
    r = 1.0 / (1.0 + e)
    kk = omlb * jnp.where(z >= 0, e * r, r)
    cc = log1mlb + (jnp.minimum(z, 0.0) - jnp.log1p(e))
    g = jnp.maximum(loglb, cc) + jnp.log1p(jnp.exp(-jnp.abs(loglb - cc)))
    return g, kk


def _hgrn_body(qf_ref, zf_ref, vf_ref, qb_ref, zb_ref, vb_ref, par_ref, of_ref, ob_ref, stf_ref, stb_ref):
    C, SB, W = HGRN_CHUNK, HGRN_SUB, HGRN_DIM
    NH = HGRN_HEADS_PER_STEP

    @pl.when(pl.program_id(2) == 0)
    def _():
        stf_ref[...] = jnp.zeros(stf_ref.shape, F32)
        stb_ref[...] = jnp.zeros(stb_ref.shape, F32)

    row = lax.broadcasted_iota(jnp.int32, (C, C), 0)
    col = lax.broadcasted_iota(jnp.int32, (C, C), 1)
    dirs = (
        (True, qf_ref, zf_ref, vf_ref, 0, jnp.where(col <= row, 1.0, 0.0).astype(BF16), stf_ref, of_ref),
        (False, qb_ref, zb_ref, vb_ref, 3, jnp.where(col >= row, 1.0, 0.0).astype(BF16), stb_ref, ob_ref),
    )
    gs, kks = [], []
    for fwd, q_ref, z_ref, v_ref, p0, tri, st_ref, o_ref in dirs:
        g, kk = _hgrn_gates(z_ref[...], par_ref[p0:p0 + 1, :], par_ref[p0 + 1:p0 + 2, :], par_ref[p0 + 2:p0 + 3, :])
        gs.append(g)
        kks.append(kk)
    bs = []
    for (fwd, q_ref, z_ref, v_ref, p0, tri, st_ref, o_ref), g in zip(dirs, gs):
        hi, mid, lo = _split3(g)
        bs.append(jnp.dot(tri, hi, preferred_element_type=F32) + jnp.dot(tri, mid, preferred_element_type=F32)
                  + jnp.dot(tri, lo, preferred_element_type=F32))
    work = []
    for (fwd, q_ref, z_ref, v_ref, p0, tri, st_ref, o_ref), b, kk in zip(dirs, bs, kks):
        q = q_ref[...]
        vb = v_ref[...].astype(BF16)
        tot = b[C - 1:C, :] if fwd else b[0:1, :]
        q_in = (q * jnp.exp(b)).astype(BF16)
        k_st = (kk * jnp.exp(tot - b)).astype(BF16)
        dec = jnp.exp(tot)
        subs = []
        for i in range(C // SB):
            r0 = i * SB
            if fwd:
                ref = b[r0 - 1:r0, :] if i > 0 else jnp.zeros((1, NH * W), F32)
                k0, k1 = 0, r0 + SB
            else:
                ref = b[r0 + SB:r0 + SB + 1, :] if r0 + SB < C else jnp.zeros((1, NH * W), F32)
                k0, k1 = r0, C
            qs = (q[r0:r0 + SB] * jnp.exp(b[r0:r0 + SB] - ref)).astype(BF16)
            ks = (kk[k0:k1] * jnp.exp(jnp.minimum(ref - b[k0:k1], EXP_CLAMP))).astype(BF16)
            subs.append((r0, k0, k1, qs, ks))
        work.append((fwd, vb, q_in, k_st, dec, subs, st_ref, o_ref))
    res = []
    for fwd, vb, q_in, k_st, dec, subs, st_ref, o_ref in work:
        for h in range(NH):
            sl = slice(h * W, (h + 1) * W)
            st = st_ref[h]
            o_inter = _dot_nt(q_in[:, sl], st.astype(BF16))
            st_ref[h] = st * dec[:, sl] + _dot_tn(vb[:, sl], k_st[:, sl])
            scores = [_dot_nt(qs[:, sl], ks[:, sl]) for (r0, k0, k1, qs, ks) in subs]
            res.append((fwd, h, o_inter, scores, vb, subs, o_ref))
    masked = []
    for fwd, h, o_inter, scores, vb, subs, o_ref in res:
        ms = []
        for s, (r0, k0, k1, qs, ks) in zip(scores, subs):
            t_idx = r0 + lax.broadcasted_iota(jnp.int32, s.shape, 0)
            s_idx = k0 + lax.broadcasted_iota(jnp.int32, s.shape, 1)
            keep = (s_idx <= t_idx) if fwd else (s_idx >= t_idx)
            ms.append(jnp.where(keep, s, 0.0).astype(BF16))
        masked.append(ms)
    for (fwd, h, o_inter, scores, vb, subs, o_ref), ms in zip(res, masked):
        sl = slice(h * W, (h + 1) * W)
        outs = [jnp.dot(m, vb[k0:k1, sl], preferred_element_type=F32) for m, (r0, k0, k1, qs, ks) in zip(ms, subs)]
        o_ref[:, sl] = o_inter + jnp.concatenate(outs, axis=0)


def hgrn_scan(proj, params, seq):
    T = proj.shape[0]
    n_seq = T // seq
    C = HGRN_CHUNK
    nc = seq // C
    w = HGRN_HEADS_PER_STEP * HGRN_DIM
    n_hh = HGRN_WIDTH // w

    def fwd_spec(col):
        return pl.BlockSpec((C, w), lambda b, hh, j: (b * nc + j, col // w + hh))

    def bwd_spec(col):
        return pl.BlockSpec((C, w), lambda b, hh, j: (b * nc + nc - 1 - j, col // w + hh))

    return pl.pallas_call(
        _hgrn_body,
        grid=(n_seq, n_hh, nc),
        in_specs=[
            fwd_spec(COL_HQ), fwd_spec(COL_ZF), fwd_spec(COL_HI),
            bwd_spec(COL_HQ), bwd_spec(COL_ZB), bwd_spec(COL_HI),
            pl.BlockSpec((8, w), lambda b, hh, j: (0, hh)),
        ],
        out_specs=[
            pl.BlockSpec((C, w), lambda b, hh, j: (b * nc + j, hh)),
            pl.BlockSpec((C, w), lambda b, hh, j: (b * nc + nc - 1 - j, hh)),
        ],
        out_shape=[jax.ShapeDtypeStruct((T, HGRN_WIDTH), F32), jax.ShapeDtypeStruct((T, HGRN_WIDTH), F32)],
        scratch_shapes=[
            pltpu.VMEM((HGRN_HEADS_PER_STEP, HGRN_DIM, HGRN_DIM), F32),
            pltpu.VMEM((HGRN_HEADS_PER_STEP, HGRN_DIM, HGRN_DIM), F32),
        ],
        compiler_params=_cparams(3, 32),
        name="hgrn_scan",
    )(proj, proj, proj, proj, proj, proj, params)


def hgrn_params(lower_bounds, layer):
    lb_all = jnp.cumsum(jax.nn.softmax(lower_bounds.astype(F32), axis=1), axis=1)
    lb_all = lb_all - lb_all[:, :1]
    rows = []
    for d in range(2):
        lb = lb_all[d, layer]
        rows += [jnp.log(lb), jnp.log1p(-lb), 1.0 - lb]
    rows += [jnp.zeros_like(rows[0])] * 2
    return jnp.stack(rows, axis=0)


def _merge_body(attn_ref, of_ref, ob_ref, hog_ref, ga_ref, gb_ref, hg_ref, wa_ref, wh_ref, o_ref):
    ya = jnp.dot(attn_ref[...], wa_ref[...], preferred_element_type=F32)
    hs = []
    for h in range(N_HGRN_HEADS):
        sl = slice(h * HGRN_DIM, (h + 1) * HGRN_DIM)
        o = _rms(of_ref[:, sl] + ob_ref[:, sl], hg_ref[...])
        og = hog_ref[:, sl]
        hs.append((o * (og * _sigmoid(og))).astype(BF16))
    yh = jnp.dot(jnp.concatenate(hs, axis=1), wh_ref[...], preferred_element_type=F32)
    o_ref[...] = (_sigmoid(ga_ref[...]) * ya + _sigmoid(gb_ref[...]) * yh).astype(o_ref.dtype)


def merge_branches(attn, o_f, o_b, proj, hgrn_gain, w_attn, w_hgrn, tm=256):
    T = attn.shape[0]
    D = D_MODEL
    return pl.pallas_call(
        _merge_body,
        grid=(T // tm,),
        in_specs=[
            pl.BlockSpec((tm, ATTN_WIDTH), lambda i: (i, 0)),
            pl.BlockSpec((tm, HGRN_WIDTH), lambda i: (i, 0)),
            pl.BlockSpec((tm, HGRN_WIDTH), lambda i: (i, 0)),
            pl.BlockSpec((tm, HGRN_WIDTH), lambda i: (i, COL_HOG // HGRN_WIDTH)),
            pl.BlockSpec((tm, D), lambda i: (i, COL_GA // D)),
            pl.BlockSpec((tm, D), lambda i: (i, COL_GB // D)),
            pl.BlockSpec((1, HGRN_DIM), lambda i: (0, 0)),
            pl.BlockSpec((ATTN_WIDTH, D), lambda i: (0, 0)),
            pl.BlockSpec((HGRN_WIDTH, D), lambda i: (0, 0)),
        ],
        out_specs=pl.BlockSpec((tm, D), lambda i: (i, 0)),
        out_shape=jax.ShapeDtypeStruct((T, D), BF16),
        compiler_params=_cparams(1, 56),
        name="merge_branches",
    )(attn, o_f, o_b, proj, proj, proj, hgrn_gain.reshape(1, HGRN_DIM), w_attn, w_hgrn)


def _outproj_body(m_ref, w_ref, x_ref, g_ref, wr_ref, xo_ref, h_ref, aff_ref):
    xn = x_ref[...] + jnp.dot(m_ref[...], w_ref[...], preferred_element_type=F32)
    xo_ref[...] = xn
    h = _rms(xn, g_ref[...])
    h_ref[...] = h.astype(h_ref.dtype)
    hb = h.astype(BF16)
    hl = (h - hb.astype(F32)).astype(BF16)
    logits = (jnp.dot(hb, wr_ref[0], preferred_element_type=F32) + jnp.dot(hl, wr_ref[0], preferred_element_type=F32)
              + jnp.dot(hb, wr_ref[1], preferred_element_type=F32))
    mx = jnp.max(logits, axis=1, keepdims=True)
    ex = jnp.exp(logits - mx)
    aff_ref[...] = ex / jnp.sum(ex, axis=1, keepdims=True)


def out_proj_router(merged, w_out, x, ffn_gain, w_router, tm=256):
    T, D = x.shape
    wr_hi = w_router.astype(BF16)
    wr_lo = (w_router - wr_hi.astype(F32)).astype(BF16)
    wr = jnp.stack([wr_hi, wr_lo], axis=0)
    return pl.pallas_call(
        _outproj_body,
        grid=(T // tm,),
        in_specs=[
            pl.BlockSpec((tm, D), lambda i: (i, 0)),
            pl.BlockSpec((D, D), lambda i: (0, 0)),
            pl.BlockSpec((tm, D), lambda i: (i, 0)),
            pl.BlockSpec((1, D), lambda i: (0, 0)),
            pl.BlockSpec((2, D, N_EXPERTS), lambda i: (0, 0, 0)),
        ],
        out_specs=[
            pl.BlockSpec((tm, D), lambda i: (i, 0)),
            pl.BlockSpec((tm, D), lambda i: (i, 0)),
            pl.BlockSpec((tm, N_EXPERTS), lambda i: (i, 0)),
        ],
        out_shape=[
            jax.ShapeDtypeStruct((T, D), F32),
            jax.ShapeDtypeStruct((T, D), BF16),
            jax.ShapeDtypeStruct((T, N_EXPERTS), F32),
        ],
        compiler_params=_cparams(1, 48),
        name="out_proj_router",
    )(merged, w_out, x, ffn_gain.reshape(1, D), wr)


def _ffn_body(x_ref, wg_ref, wu_ref, wd_ref, gv_ref, meta_ref, o_ref):
    x = x_ref[0]
    g = jnp.dot(x, wg_ref[0], preferred_element_type=F32)
    u = jnp.dot(x, wu_ref[0], preferred_element_type=F32)
    a = (g * _sigmoid(g) * u).astype(BF16)
    y = jnp.dot(a, wd_ref[0], preferred_element_type=F32) * gv_ref[0]
    o_ref[0, :, :D_MODEL] = y.astype(o_ref.dtype)
    o_ref[0, :, D_MODEL:] = meta_ref[0]


def expert_ffn(xe, w_gate, w_up, w_down, gate_vals, meta, tm=256):
    E, cap, D = xe.shape
    F = w_gate.shape[-1]
    return pl.pallas_call(
        _ffn_body,
        grid=(E, cap // tm),
        in_specs=[
            pl.BlockSpec((1, tm, D), lambda e, i: (e, i, 0)),
            pl.BlockSpec((1, D, F), lambda e, i: (e, 0, 0)),
            pl.BlockSpec((1, D, F), lambda e, i: (e, 0, 0)),
            pl.BlockSpec((1, F, D), lambda e, i: (e, 0, 0)),
            pl.BlockSpec((1, tm, 1), lambda e, i: (e, i, 0)),
            pl.BlockSpec((1, tm, META_WIDTH), lambda e, i: (e, i, 0)),
        ],
        out_specs=pl.BlockSpec((1, tm, D + META_WIDTH), lambda e, i: (e, i, 0)),
        out_shape=jax.ShapeDtypeStruct((E, cap, D + META_WIDTH), BF16),
        compiler_params=_cparams(2, 48),
        name="expert_ffn",
    )(xe, w_gate, w_up, w_down, gate_vals.reshape(E, cap, 1), meta)


def _combine_body(win_ref, minslot_ref, nr_ref, x_ref, *refs, base_tile, n_tiles):
    ye_win = refs[:N_EXPERTS]
    ye_hbm, o_ref, buf_ref, sem_ref = refs[N_EXPERTS:]
    j = pl.program_id(0)
    t0 = ((j + base_tile) * COMBINE_TM).astype(F32)

    def table_index(e, k):
        return e * (n_tiles * COMBINE_ROUNDS) + j * COMBINE_ROUNDS + k

    def placed(window, k):
        rows, toks = [], []
        for e in range(N_EXPERTS):
            w = window(e)
            meta = w[:, D_MODEL:].astype(F32)
            tok = meta[:, 0:1] * 128.0 + meta[:, 1:2] - t0
            flat = table_index(e, k)
            slot = win_ref[flat] * WIN_BLK + lax.broadcasted_iota(jnp.int32, (WIN_ROWS, 1), 0)
            toks.append(jnp.where(slot >= minslot_ref[flat], tok, -1.0))
            rows.append(w[:, :D_MODEL])
        tok_all = jnp.concatenate(toks, axis=0)
        lane_t = lax.broadcasted_iota(jnp.int32, (N_EXPERTS * WIN_ROWS, COMBINE_TM), 1).astype(F32)
        place = jnp.where(tok_all == lane_t, 1.0, 0.0).astype(BF16)
        return _dot_tn(place, jnp.concatenate(rows, axis=0))

    o_ref[...] = x_ref[...] + placed(lambda e: ye_win[e][...].reshape(WIN_ROWS, D_MODEL + META_WIDTH), 0)

    def window_copy(e, k):
        return pltpu.make_async_copy(
            ye_hbm.at[e, pl.ds(win_ref[table_index(e, k)], WIN_BLKS)], buf_ref.at[e], sem_ref.at[e])

    def extra_round(k, carry):
        for e in range(N_EXPERTS):
            window_copy(e, k).start()
        for e in range(N_EXPERTS):
            window_copy(e, k).wait()
        o_ref[...] += placed(lambda e: buf_ref[e].reshape(WIN_ROWS, D_MODEL + META_WIDTH), k)
        return carry

    lax.fori_loop(1, nr_ref[j], extra_round, 0)


def combine(x, ye_ext, win, minslot, n_rounds, base_tile):
    E, cap, dext = ye_ext.shape
    n_tiles = n_rounds.shape[0]
    ye4 = ye_ext.reshape(E, cap // WIN_BLK, WIN_BLK, dext)

    def ye_spec(e):
        return pl.BlockSpec(
            (pl.Element(1), pl.Element(WIN_BLKS), pl.Element(WIN_BLK), pl.Element(dext)),
            lambda j, win_r, ms_r, nr_r: (e, win_r[e * (n_tiles * COMBINE_ROUNDS) + j * COMBINE_ROUNDS], 0, 0))

    grid_spec = pltpu.PrefetchScalarGridSpec(
        num_scalar_prefetch=3,
        grid=(n_tiles,),
        in_specs=[pl.BlockSpec((COMBINE_TM, D_MODEL), lambda j, *_: (j + base_tile, 0))]
        + [ye_spec(e) for e in range(E)]
        + [pl.BlockSpec(memory_space=pl.ANY)],
        out_specs=pl.BlockSpec((COMBINE_TM, D_MODEL), lambda j, *_: (j + base_tile, 0)),
        scratch_shapes=[
            pltpu.VMEM((E, WIN_BLKS, WIN_BLK, dext), BF16),
            pltpu.SemaphoreType.DMA((E,)),
        ],
    )
    return pl.pallas_call(
        functools.partial(_combine_body, base_tile=base_tile, n_tiles=n_tiles),
        grid_spec=grid_spec,
        out_shape=jax.ShapeDtypeStruct(x.shape, x.dtype),
        input_output_aliases={3: 0},
        compiler_params=_cparams(1, 48),
        name="combine",
    )(win.reshape(-1), minslot.reshape(-1), n_rounds, x, *([ye4] * E), ye4)


def combine_tables(idx_sorted, cap, n_tok, base_tok):
    n_tiles = n_tok // COMBINE_TM
    edges = base_tok + COMBINE_TM * jnp.arange(n_tiles + 1, dtype=jnp.int32)
    starts = jax.vmap(lambda row: jnp.searchsorted(row, edges, side="left"))(idx_sorted).astype(jnp.int32)
    lo, hi = starts[:, :-1], starts[:, 1:]
    a = (lo // WIN_BLK) * WIN_BLK
    need = jnp.where(hi > lo, -(-(hi - a) // WIN_ROWS), 0)
    k = jnp.arange(COMBINE_ROUNDS, dtype=jnp.int32)[None, None, :]
    kk = jnp.minimum(k, jnp.maximum(need[..., None] - 1, 0))
    win = jnp.minimum(a[..., None] + WIN_ROWS * kk, cap - WIN_ROWS) // WIN_BLK
    minslot = jnp.where(k < need[..., None], a[..., None] + WIN_ROWS * k, cap)
    return win.astype(jnp.int32), minslot.astype(jnp.int32), jnp.max(need, axis=0).astype(jnp.int32)


def kernel(x_prompt, x_sample, w_in, q_norm, k_norm, lower_bounds, hgrn_norm, w_proj_attn, w_proj_hgrn,
           w_out, norm_mix, norm_ffn, w_router, w_gate, w_up, w_down, norm_final):
    depth = w_in.shape[0]
    groups = (x_prompt, x_sample)
    seq = x_prompt.shape[1]
    group_tokens = [g.shape[0] * g.shape[1] for g in groups]
    x = jnp.concatenate([g.reshape(-1, D_MODEL) for g in groups], axis=0)
    cos, sin = rope_tables(seq)

    for l in range(depth):
        h = rms_norm_rows(x, norm_mix[l], BF16)
        proj = in_proj(h, w_in[l].astype(BF16), F32)
        q, k, v = qk_prep(proj, cos, sin, q_norm[l], k_norm[l], seq)
        attn = attention(q, k, v, seq)
        o_f, o_b = hgrn_scan(proj, hgrn_params(lower_bounds, l), seq)
        merged = merge_branches(attn, o_f, o_b, proj, hgrn_norm[l],
                                w_proj_attn[l].astype(BF16), w_proj_hgrn[l].astype(BF16))
        x, h_ffn, aff = out_proj_router(merged, w_out[l].astype(BF16), x, norm_ffn[l], w_router[l])
        aff_t = aff.T
        wg, wu, wd = w_gate[l].astype(BF16), w_up[l].astype(BF16), w_down[l].astype(BF16)
        start = 0
        for n_tok in group_tokens:
            cap = EC_CAPACITY * n_tok // N_EXPERTS
            aff_g = aff_t[:, start:start + n_tok]
            _, idx = lax.top_k(aff_g, cap)
            idx = jnp.sort(idx, axis=1)
            gate_vals = jnp.take_along_axis(aff_g, idx, axis=1)
            idx = idx + start
            meta = jnp.concatenate(
                [(idx // 128)[..., None], (idx % 128)[..., None],
                 jnp.zeros((N_EXPERTS, cap, META_WIDTH - 2), jnp.int32)], axis=-1).astype(BF16)
            ye = expert_ffn(h_ffn[idx], wg, wu, wd, gate_vals, meta)
            win, minslot, n_rounds = combine_tables(idx, cap, n_tok, start)
            x = combine(x, ye, win, minslot, n_rounds, start // COMBINE_TM)
            start += n_tok

    y = rms_norm_rows(x, norm_final, F32)
    outs = []
    start = 0
    for g, n_tok in zip(groups, group_tokens):
        outs.append(y[start:start + n_tok].reshape(g.shape))
        start += n_tok
    return tuple(outs)
```

```python
import functools

import numpy as np
import jax
import jax.numpy as jnp
from jax import lax
from jax.experimental import pallas as pl
from jax.experimental.pallas import tpu as pltpu

F32 = jnp.float32
BF16 = jnp.bfloat16

D_MODEL = 2048
SEQ = 4096
GRID_W = 64
HEAD_DIM = 128
N_Q_HEADS = 8
N_KV_HEADS = 2
Q_GROUP = N_Q_HEADS // N_KV_HEADS
ATTN_WIDTH = N_Q_HEADS * HEAD_DIM
KV_WIDTH = N_KV_HEADS * HEAD_DIM
ROPE_THETA = 10000.0
ROPE_AXIS_PAIRS = HEAD_DIM // 4
N_HGRN_HEADS = 8
HGRN_DIM = 128
HGRN_WIDTH = N_HGRN_HEADS * HGRN_DIM
N_EXPERTS = 16
EC_CAPACITY = 2
EXPERT_FF = 1024
NORM_EPS = 1e-6
IN_WIDTH = 10752

QKV_WIDTH = ATTN_WIDTH + 2 * KV_WIDTH
COL_QKV = 0
COL_HQ = COL_QKV + QKV_WIDTH
COL_ZF = COL_HQ + HGRN_WIDTH
COL_ZB = COL_ZF + HGRN_WIDTH
COL_HI = COL_ZB + HGRN_WIDTH
COL_HOG = COL_HI + HGRN_WIDTH
COL_GA = COL_HOG + HGRN_WIDTH
COL_GB = COL_GA + D_MODEL

ATTN_TK = 1024

HGRN_CHUNK = 64
HGRN_SUB = 16
HGRN_HEADS_PER_STEP = 8
EXP_CLAMP = 80.0

META_WIDTH = 128
COMBINE_TM = 256
WIN_BLK = 16
WIN_BLKS = 4
WIN_ROWS = WIN_BLK * WIN_BLKS
COMBINE_ROUNDS = -(-(WIN_BLK - 1 + COMBINE_TM) // WIN_ROWS)

V7X_VMEM_BYTES = 64 * 1024 * 1024


def _cparams(n_grid, vmem_mb):
    assert vmem_mb * 1024 * 1024 < V7X_VMEM_BYTES
    return pltpu.CompilerParams(
        dimension_semantics=("arbitrary",) * n_grid,
        vmem_limit_bytes=vmem_mb * 1024 * 1024,
    )


def _sigmoid(x):
    return 1.0 / (1.0 + jnp.exp(-x))


def _rms(x, gain):
    return x * lax.rsqrt(jnp.mean(x * x, axis=-1, keepdims=True) + NORM_EPS) * gain


def _dot_nt(a, b):
    return lax.dot_general(a, b, (((1,), (1,)), ((), ())), preferred_element_type=F32)


def _dot_tn(a, b):
    return lax.dot_general(a, b, (((0,), (0,)), ((), ())), preferred_element_type=F32)


def _norm_body(x_ref, g_ref, o_ref):
    o_ref[...] = _rms(x_ref[...], g_ref[...]).astype(o_ref.dtype)


def rms_norm_rows(x, gain, out_dtype, row_start=0, n_rows=None, tm=512):
    T, D = x.shape
    n_rows = T if n_rows is None else n_rows
    first = row_start // tm
    return pl.pallas_call(
        _norm_body,
        grid=(n_rows // tm,),
        in_specs=[pl.BlockSpec((tm, D), lambda i: (i + first, 0)), pl.BlockSpec((1, D), lambda i: (0, 0))],
        out_specs=pl.BlockSpec((tm, D), lambda i: (i, 0)),
        out_shape=jax.ShapeDtypeStruct((n_rows, D), out_dtype),
        compiler_params=_cparams(1, 32),
        name="rms_norm_rows",
    )(x, gain.reshape(1, D))


def _mm_body(a_ref, w_ref, o_ref):
    o_ref[...] = jnp.dot(a_ref[...], w_ref[...], preferred_element_type=F32).astype(o_ref.dtype)


def in_proj(a, w, out_dtype, tm=512, tn=1536):
    M, K = a.shape
    N = w.shape[1]
    return pl.pallas_call(
        _mm_body,
        grid=(N // tn, M // tm),
        in_specs=[pl.BlockSpec((tm, K), lambda j, i: (i, 0)), pl.BlockSpec((K, tn), lambda j, i: (0, j))],
        out_specs=pl.BlockSpec((tm, tn), lambda j, i: (i, j)),
        out_shape=jax.ShapeDtypeStruct((M, N), out_dtype),
        compiler_params=_cparams(2, 48),
        name="in_proj",
    )(a, w)


def _col_window(rows, width, col, row_block):
    return pl.BlockSpec((pl.Element(rows), pl.Element(width)), lambda *g: (row_block(*g) * rows, col))


def _qkprep_body(p_ref, cos_ref, sin_ref, qg_ref, kg_ref, q_ref, k_ref, v_ref, *, scale):
    cos = cos_ref[...]
    sin = sin_ref[...]
    lane = lax.broadcasted_iota(jnp.int32, cos.shape, 1)
    first = (lane & ROPE_AXIS_PAIRS) == 0

    def prep(x, g, s):
        y = _rms(x, g)
        partner = jnp.where(first, pltpu.roll(y, HEAD_DIM - ROPE_AXIS_PAIRS, 1), pltpu.roll(y, ROPE_AXIS_PAIRS, 1))
        return (y * cos + partner * sin) * s

    for h in range(N_Q_HEADS):
        sl = slice(h * HEAD_DIM, (h + 1) * HEAD_DIM)
        q_ref[:, sl] = prep(p_ref[:, sl], qg_ref[...], scale).astype(q_ref.dtype)
    for h in range(N_KV_HEADS):
        sl = slice(h * HEAD_DIM, (h + 1) * HEAD_DIM)
        src = slice(ATTN_WIDTH + h * HEAD_DIM, ATTN_WIDTH + (h + 1) * HEAD_DIM)
        k_ref[:, sl] = prep(p_ref[:, src], kg_ref[...], 1.0).astype(k_ref.dtype)
    v_ref[...] = p_ref[:, ATTN_WIDTH + KV_WIDTH:].astype(v_ref.dtype)


def qk_prep(proj, cos, sin, q_gain, k_gain, seq, tm=256):
    T = proj.shape[0]
    nblk = seq // tm
    return pl.pallas_call(
        functools.partial(_qkprep_body, scale=HEAD_DIM ** -0.5),
        grid=(T // tm,),
        in_specs=[
            pl.BlockSpec((tm, QKV_WIDTH), lambda i: (i, COL_QKV // QKV_WIDTH)),
            pl.BlockSpec((tm, HEAD_DIM), lambda i: (i % nblk, 0)),
            pl.BlockSpec((tm, HEAD_DIM), lambda i: (i % nblk, 0)),
            pl.BlockSpec((1, HEAD_DIM), lambda i: (0, 0)),
            pl.BlockSpec((1, HEAD_DIM), lambda i: (0, 0)),
        ],
        out_specs=[
            pl.BlockSpec((tm, ATTN_WIDTH), lambda i: (i, 0)),
            pl.BlockSpec((tm, KV_WIDTH), lambda i: (i, 0)),
            pl.BlockSpec((tm, KV_WIDTH), lambda i: (i, 0)),
        ],
        out_shape=[
            jax.ShapeDtypeStruct((T, ATTN_WIDTH), BF16),
            jax.ShapeDtypeStruct((T, KV_WIDTH), BF16),
            jax.ShapeDtypeStruct((T, KV_WIDTH), BF16),
        ],
        compiler_params=_cparams(1, 32),
        name="qk_prep",
    )(proj, cos, sin, q_gain.reshape(1, HEAD_DIM), k_gain.reshape(1, HEAD_DIM))


def rope_tables(seq):
    rows = seq // GRID_W
    row = jnp.repeat(jnp.arange(rows, dtype=F32), GRID_W)
    col = jnp.tile(jnp.arange(GRID_W, dtype=F32), rows)
    inv_freq = 1.0 / (ROPE_THETA ** (jnp.arange(ROPE_AXIS_PAIRS, dtype=F32) / ROPE_AXIS_PAIRS))
    ang_r = row[:, None] * inv_freq
    ang_c = col[:, None] * inv_freq
    cos = jnp.concatenate([jnp.cos(ang_r), jnp.cos(ang_r), jnp.cos(ang_c), jnp.cos(ang_c)], axis=1)
    sin = jnp.concatenate([-jnp.sin(ang_r), jnp.sin(ang_r), -jnp.sin(ang_c), jnp.sin(ang_c)], axis=1)
    return cos, sin


def _attn_body(q_ref, k_ref, v_ref, o_ref):
    n_chunks = k_ref.shape[0] // ATTN_TK
    sls = [slice(g * HEAD_DIM, (g + 1) * HEAD_DIM) for g in range(Q_GROUP)]
    qs = [q_ref[:, sl] for sl in sls]
    ms = ls = accs = None
    for c in range(n_chunks):
        k = k_ref[c * ATTN_TK:(c + 1) * ATTN_TK, :]
        v = v_ref[c * ATTN_TK:(c + 1) * ATTN_TK, :]
        ss = [_dot_nt(q, k) for q in qs]
        new_ms = [jnp.max(s, axis=-1, keepdims=True) for s in ss]
        if c > 0:
            new_ms = [jnp.maximum(m, cm) for m, cm in zip(ms, new_ms)]
        ps = [jnp.exp(s - m) for s, m in zip(ss, new_ms)]
        pvs = [jnp.dot(p.astype(BF16), v, preferred_element_type=F32) for p in ps]
        sums = [jnp.sum(p, axis=-1, keepdims=True) for p in ps]
        if c == 0:
            ls, accs = sums, pvs
        else:
            alphas = [jnp.exp(m - nm) for m, nm in zip(ms, new_ms)]
            ls = [l * a + s for l, a, s in zip(ls, alphas, sums)]
            accs = [acc * a + pv for acc, a, pv in zip(accs, alphas, pvs)]
        ms = new_ms
    for sl, acc, l in zip(sls, accs, ls):
        o_ref[:, sl] = (acc / l).astype(o_ref.dtype)


def attention(q, k, v, seq, tq=256):
    T = q.shape[0]
    n_seq = T // seq
    nq = seq // tq
    gw = Q_GROUP * HEAD_DIM
    return pl.pallas_call(
        _attn_body,
        grid=(n_seq, N_KV_HEADS, nq),
        in_specs=[
            pl.BlockSpec((tq, gw), lambda b, h, i: (b * nq + i, h)),
            pl.BlockSpec((seq, HEAD_DIM), lambda b, h, i: (b, h)),
            pl.BlockSpec((seq, HEAD_DIM), lambda b, h, i: (b, h)),
        ],
        out_specs=pl.BlockSpec((tq, gw), lambda b, h, i: (b * nq + i, h)),
        out_shape=jax.ShapeDtypeStruct((T, ATTN_WIDTH), BF16),
        compiler_params=_cparams(3, 48),
        name="attention",
    )(q, k, v)


def _split3(x):
    hi = x.astype(BF16)
    r1 = x - hi.astype(F32)
    mid = r1.astype(BF16)
    lo = (r1 - mid.astype(F32)).astype(BF16)
    return hi, mid, lo


def _hgrn_gates(z, loglb, log1mlb, omlb):
    e = jnp.exp(-jnp.abs(z))
    r = 1.0 / (1.0 + e)
    kk = omlb * jnp.where(z >= 0, e * r, r)
    cc = log1mlb + (jnp.minimum(z, 0.0) - jnp.log1p(e))
    g = jnp.maximum(loglb, cc) + jnp.log1p(jnp.exp(-jnp.abs(loglb - cc)))
    return g, kk


def _hgrn_body(qf_ref, zf_ref, vf_ref, qb_ref, zb_ref, vb_ref, par_ref, of_ref, ob_ref, stf_ref, stb_ref):
    C, SB, W = HGRN_CHUNK, HGRN_SUB, HGRN_DIM
    NH = HGRN_HEADS_PER_STEP

    @pl.when(pl.program_id(2) == 0)
    def _():
        stf_ref[...] = jnp.zeros(stf_ref.shape, F32)
        stb_ref[...] = jnp.zeros(stb_ref.shape, F32)

    row = lax.broadcasted_iota(jnp.int32, (C, C), 0)
    col = lax.broadcasted_iota(jnp.int32, (C, C), 1)
    dirs = (
        (True, qf_ref, zf_ref, vf_ref, 0, jnp.where(col <= row, 1.0, 0.0).astype(BF16), stf_ref, of_ref),
        (False, qb_ref, zb_ref, vb_ref, 3, jnp.where(col >= row, 1.0, 0.0).astype(BF16), stb_ref, ob_ref),
    )
    gs, kks = [], []
    for fwd, q_ref, z_ref, v_ref, p0, tri, st_ref, o_ref in dirs:
        g, kk = _hgrn_gates(z_ref[...], par_ref[p0:p0 + 1, :], par_ref[p0 + 1:p0 + 2, :], par_ref[p0 + 2:p0 + 3, :])
        gs.append(g)
        kks.append(kk)
    bs = []
    for (fwd, q_ref, z_ref, v_ref, p0, tri, st_ref, o_ref), g in zip(dirs, gs):
        hi, mid, lo = _split3(g)
        bs.append(jnp.dot(tri, hi, preferred_element_type=F32) + jnp.dot(tri, mid, preferred_element_type=F32)
                  + jnp.dot(tri, lo, preferred_element_type=F32))
    work = []
    for (fwd, q_ref, z_ref, v_ref, p0, tri, st_ref, o_ref), b, kk in zip(dirs, bs, kks):
        q = q_ref[...]
        vb = v_ref[...].astype(BF16)
        tot = b[C - 1:C, :] if fwd else b[0:1, :]
        q_in = (q * jnp.exp(b)).astype(BF16)
        k_st = (kk * jnp.exp(tot - b)).astype(BF16)
        dec = jnp.exp(tot)
        subs = []
        for i in range(C // SB):
            r0 = i * SB
            if fwd:
                ref = b[r0 - 1:r0, :] if i > 0 else jnp.zeros((1, NH * W), F32)
                k0, k1 = 0, r0 + SB
            else:
                ref = b[r0 + SB:r0 + SB + 1, :] if r0 + SB < C else jnp.zeros((1, NH * W), F32)
                k0, k1 = r0, C
            qs = (q[r0:r0 + SB] * jnp.exp(b[r0:r0 + SB] - ref)).astype(BF16)
            ks = (kk[k0:k1] * jnp.exp(jnp.minimum(ref - b[k0:k1], EXP_CLAMP))).astype(BF16)
            subs.append((r0, k0, k1, qs, ks))
        work.append((fwd, vb, q_in, k_st, dec, subs, st_ref, o_ref))
    res = []
    for fwd, vb, q_in, k_st, dec, subs, st_ref, o_ref in work:
        for h in range(NH):
            sl = slice(h * W, (h + 1) * W)
            st = st_ref[h]
            o_inter = _dot_nt(q_in[:, sl], st.astype(BF16))
            st_ref[h] = st * dec[:, sl] + _dot_tn(vb[:, sl], k_st[:, sl])
            scores = [_dot_nt(qs[:, sl], ks[:, sl]) for (r0, k0, k1, qs, ks) in subs]
            res.append((fwd, h, o_inter, scores, vb, subs, o_ref))
    masked = []
    for fwd, h, o_inter, scores, vb, subs, o_ref in res:
        ms = []
        for s, (r0, k0, k1, qs, ks) in zip(scores, subs):
            t_idx = r0 + lax.broadcasted_iota(jnp.int32, s.shape, 0)
            s_idx = k0 + lax.broadcasted_iota(jnp.int32, s.shape, 1)
            keep = (s_idx <= t_idx) if fwd else (s_idx >= t_idx)
            ms.append(jnp.where(keep, s, 0.0).astype(BF16))
        masked.append(ms)
    for (fwd, h, o_inter, scores, vb, subs, o_ref), ms in zip(res, masked):
        sl = slice(h * W, (h + 1) * W)
        outs = [jnp.dot(m, vb[k0:k1, sl], preferred_element_type=F32) for m, (r0, k0, k1, qs, ks) in zip(ms, subs)]
        o_ref[:, sl] = o_inter + jnp.concatenate(outs, axis=0)


def hgrn_scan(proj, params, seq):
    T = proj.shape[0]
    n_seq = T // seq
    C = HGRN_CHUNK
    nc = seq // C
    w = HGRN_HEADS_PER_STEP * HGRN_DIM
    n_hh = HGRN_WIDTH // w

    assert n_hh == 1

    def fwd_spec(col):
        return _col_window(C, w, col, lambda b, hh, j: b * nc + j)

    def bwd_spec(col):
        return _col_window(C, w, col, lambda b, hh, j: b * nc + nc - 1 - j)

    return pl.pallas_call(
        _hgrn_body,
        grid=(n_seq, n_hh, nc),
        in_specs=[
            fwd_spec(COL_HQ), fwd_spec(COL_ZF), fwd_spec(COL_HI),
            bwd_spec(COL_HQ), bwd_spec(COL_ZB), bwd_spec(COL_HI),
            pl.BlockSpec((8, w), lambda b, hh, j: (0, hh)),
        ],
        out_specs=[
            pl.BlockSpec((C, w), lambda b, hh, j: (b * nc + j, hh)),
            pl.BlockSpec((C, w), lambda b, hh, j: (b * nc + nc - 1 - j, hh)),
        ],
        out_shape=[jax.ShapeDtypeStruct((T, HGRN_WIDTH), F32), jax.ShapeDtypeStruct((T, HGRN_WIDTH), F32)],
        scratch_shapes=[
            pltpu.VMEM((HGRN_HEADS_PER_STEP, HGRN_DIM, HGRN_DIM), F32),
            pltpu.VMEM((HGRN_HEADS_PER_STEP, HGRN_DIM, HGRN_DIM), F32),
        ],
        compiler_params=_cparams(3, 32),
        name="hgrn_scan",
    )(proj, proj, proj, proj, proj, proj, params)


def hgrn_params(lower_bounds, layer):
    lb_all = jnp.cumsum(jax.nn.softmax(lower_bounds.astype(F32), axis=1), axis=1)
    lb_all = lb_all - lb_all[:, :1]
    rows = []
    for d in range(2):
        lb = lb_all[d, layer]
        rows += [jnp.log(lb), jnp.log1p(-lb), 1.0 - lb]
    rows += [jnp.zeros_like(rows[0])] * 2
    return jnp.stack(rows, axis=0)


def _merge_body(attn_ref, of_ref, ob_ref, hog_ref, ga_ref, gb_ref, hg_ref, wa_ref, wh_ref, o_ref):
    ya = jnp.dot(attn_ref[...], wa_ref[...], preferred_element_type=F32)
    hs = []
    for h in range(N_HGRN_HEADS):
        sl = slice(h * HGRN_DIM, (h + 1) * HGRN_DIM)
        o = _rms(of_ref[:, sl] + ob_ref[:, sl], hg_ref[...])
        og = hog_ref[:, sl]
        hs.append((o * (og * _sigmoid(og))).astype(BF16))
    yh = jnp.dot(jnp.concatenate(hs, axis=1), wh_ref[...], preferred_element_type=F32)
    o_ref[...] = (_sigmoid(ga_ref[...]) * ya + _sigmoid(gb_ref[...]) * yh).astype(o_ref.dtype)


def merge_branches(attn, o_f, o_b, proj, hgrn_gain, w_attn, w_hgrn, tm=256):
    T = attn.shape[0]
    D = D_MODEL
    return pl.pallas_call(
        _merge_body,
        grid=(T // tm,),
        in_specs=[
            pl.BlockSpec((tm, ATTN_WIDTH), lambda i: (i, 0)),
            pl.BlockSpec((tm, HGRN_WIDTH), lambda i: (i, 0)),
            pl.BlockSpec((tm, HGRN_WIDTH), lambda i: (i, 0)),
            _col_window(tm, HGRN_WIDTH, COL_HOG, lambda i: i),
            _col_window(tm, D, COL_GA, lambda i: i),
            _col_window(tm, D, COL_GB, lambda i: i),
            pl.BlockSpec((1, HGRN_DIM), lambda i: (0, 0)),
            pl.BlockSpec((ATTN_WIDTH, D), lambda i: (0, 0)),
            pl.BlockSpec((HGRN_WIDTH, D), lambda i: (0, 0)),
        ],
        out_specs=pl.BlockSpec((tm, D), lambda i: (i, 0)),
        out_shape=jax.ShapeDtypeStruct((T, D), BF16),
        compiler_params=_cparams(1, 56),
        name="merge_branches",
    )(attn, o_f, o_b, proj, proj, proj, hgrn_gain.reshape(1, HGRN_DIM), w_attn, w_hgrn)


def _outproj_body(m_ref, w_ref, x_ref, g_ref, wr_ref, xo_ref, h_ref, aff_ref):
    xn = x_ref[...] + jnp.dot(m_ref[...], w_ref[...], preferred_element_type=F32)
    xo_ref[...] = xn
    h = _rms(xn, g_ref[...])
    h_ref[...] = h.astype(h_ref.dtype)
    hb = h.astype(BF16)
    hl = (h - hb.astype(F32)).astype(BF16)
    logits = (jnp.dot(hb, wr_ref[0], preferred_element_type=F32) + jnp.dot(hl, wr_ref[0], preferred_element_type=F32)
              + jnp.dot(hb, wr_ref[1], preferred_element_type=F32))
    mx = jnp.max(logits, axis=1, keepdims=True)
    ex = jnp.exp(logits - mx)
    aff_ref[...] = ex / jnp.sum(ex, axis=1, keepdims=True)


def out_proj_router(merged, w_out, x, ffn_gain, w_router, tm=256):
    T, D = x.shape
    wr_hi = w_router.astype(BF16)
    wr_lo = (w_router - wr_hi.astype(F32)).astype(BF16)
    wr = jnp.stack([wr_hi, wr_lo], axis=0)
    return pl.pallas_call(
        _outproj_body,
        grid=(T // tm,),
        in_specs=[
            pl.BlockSpec((tm, D), lambda i: (i, 0)),
            pl.BlockSpec((D, D), lambda i: (0, 0)),
            pl.BlockSpec((tm, D), lambda i: (i, 0)),
            pl.BlockSpec((1, D), lambda i: (0, 0)),
            pl.BlockSpec((2, D, N_EXPERTS), lambda i: (0, 0, 0)),
        ],
        out_specs=[
            pl.BlockSpec((tm, D), lambda i: (i, 0)),
            pl.BlockSpec((tm, D), lambda i: (i, 0)),
            pl.BlockSpec((tm, N_EXPERTS), lambda i: (i, 0)),
        ],
        out_shape=[
            jax.ShapeDtypeStruct((T, D), F32),
            jax.ShapeDtypeStruct((T, D), BF16),
            jax.ShapeDtypeStruct((T, N_EXPERTS), F32),
        ],
        compiler_params=_cparams(1, 48),
        name="out_proj_router",
    )(merged, w_out, x, ffn_gain.reshape(1, D), wr)


def _ffn_body(x_ref, wg_ref, wu_ref, wd_ref, gv_ref, meta_ref, o_ref):
    x = x_ref[0]
    g = jnp.dot(x, wg_ref[0], preferred_element_type=F32)
    u = jnp.dot(x, wu_ref[0], preferred_element_type=F32)
    a = (g * _sigmoid(g) * u).astype(BF16)
    y = jnp.dot(a, wd_ref[0], preferred_element_type=F32) * gv_ref[0]
    o_ref[0, :, :D_MODEL] = y.astype(o_ref.dtype)
    o_ref[0, :, D_MODEL:] = meta_ref[0]


def expert_ffn(xe, w_gate, w_up, w_down, gate_vals, meta, tm=256):
    E, cap, D = xe.shape
    F = w_gate.shape[-1]
    return pl.pallas_call(
        _ffn_body,
        grid=(E, cap // tm),
        in_specs=[
            pl.BlockSpec((1, tm, D), lambda e, i: (e, i, 0)),
            pl.BlockSpec((1, D, F), lambda e, i: (e, 0, 0)),
            pl.BlockSpec((1, D, F), lambda e, i: (e, 0, 0)),
            pl.BlockSpec((1, F, D), lambda e, i: (e, 0, 0)),
            pl.BlockSpec((1, tm, 1), lambda e, i: (e, i, 0)),
            pl.BlockSpec((1, tm, META_WIDTH), lambda e, i: (e, i, 0)),
        ],
        out_specs=pl.BlockSpec((1, tm, D + META_WIDTH), lambda e, i: (e, i, 0)),
        out_shape=jax.ShapeDtypeStruct((E, cap, D + META_WIDTH), BF16),
        compiler_params=_cparams(2, 48),
        name="expert_ffn",
    )(xe, w_gate, w_up, w_down, gate_vals.reshape(E, cap, 1), meta)


def _combine_body(win_ref, minslot_ref, nr_ref, x_ref, *refs, base_tile, n_tiles):
    ye_win = refs[:N_EXPERTS]
    ye_hbm, o_ref, buf_ref, sem_ref = refs[N_EXPERTS:]
    j = pl.program_id(0)
    t0 = ((j + base_tile) * COMBINE_TM).astype(F32)

    def table_index(e, k):
        return e * (n_tiles * COMBINE_ROUNDS) + j * COMBINE_ROUNDS + k

    def placed(window, k):
        rows, toks = [], []
        for e in range(N_EXPERTS):
            w = window(e)
            meta = w[:, D_MODEL:].astype(F32)
            tok = meta[:, 0:1] * 128.0 + meta[:, 1:2] - t0
            flat = table_index(e, k)
            slot = win_ref[flat] * WIN_BLK + lax.broadcasted_iota(jnp.int32, (WIN_ROWS, 1), 0)
            toks.append(jnp.where(slot >= minslot_ref[flat], tok, -1.0))
            rows.append(w[:, :D_MODEL])
        tok_all = jnp.concatenate(toks, axis=0)
        lane_t = lax.broadcasted_iota(jnp.int32, (N_EXPERTS * WIN_ROWS, COMBINE_TM), 1).astype(F32)
        place = jnp.where(tok_all == lane_t, 1.0, 0.0).astype(BF16)
        return _dot_tn(place, jnp.concatenate(rows, axis=0))

    o_ref[...] = x_ref[...] + placed(lambda e: ye_win[e][...].reshape(WIN_ROWS, D_MODEL + META_WIDTH), 0)

    def window_copy(e, k):
        return pltpu.make_async_copy(
            ye_hbm.at[e, pl.ds(win_ref[table_index(e, k)], WIN_BLKS)], buf_ref.at[e], sem_ref.at[e])

    def extra_round(k, carry):
        for e in range(N_EXPERTS):
            window_copy(e, k).start()
        for e in range(N_EXPERTS):
            window_copy(e, k).wait()
        o_ref[...] += placed(lambda e: buf_ref[e].reshape(WIN_ROWS, D_MODEL + META_WIDTH), k)
        return carry

    lax.fori_loop(1, nr_ref[j], extra_round, 0)


def combine(x, ye_ext, win, minslot, n_rounds, base_tile):
    E, cap, dext = ye_ext.shape
    n_tiles = n_rounds.shape[0]
    ye4 = ye_ext.reshape(E, cap // WIN_BLK, WIN_BLK, dext)

    def ye_spec(e):
        return pl.BlockSpec(
            (pl.Element(1), pl.Element(WIN_BLKS), pl.Element(WIN_BLK), pl.Element(dext)),
            lambda j, win_r, ms_r, nr_r: (e, win_r[e * (n_tiles * COMBINE_ROUNDS) + j * COMBINE_ROUNDS], 0, 0))

    grid_spec = pltpu.PrefetchScalarGridSpec(
        num_scalar_prefetch=3,
        grid=(n_tiles,),
        in_specs=[pl.BlockSpec((COMBINE_TM, D_MODEL), lambda j, *_: (j + base_tile, 0))]
        + [ye_spec(e) for e in range(E)]
        + [pl.BlockSpec(memory_space=pl.ANY)],
        out_specs=pl.BlockSpec((COMBINE_TM, D_MODEL), lambda j, *_: (j + base_tile, 0)),
        scratch_shapes=[
            pltpu.VMEM((E, WIN_BLKS, WIN_BLK, dext), BF16),
            pltpu.SemaphoreType.DMA((E,)),
        ],
    )
    return pl.pallas_call(
        functools.partial(_combine_body, base_tile=base_tile, n_tiles=n_tiles),
        grid_spec=grid_spec,
        out_shape=jax.ShapeDtypeStruct(x.shape, x.dtype),
        input_output_aliases={3: 0},
        compiler_params=_cparams(1, 48),
        name="combine",
    )(win.reshape(-1), minslot.reshape(-1), n_rounds, x, *([ye4] * E), ye4)


def combine_tables(idx_sorted, cap, n_tok, base_tok):
    n_tiles = n_tok // COMBINE_TM
    edges = base_tok + COMBINE_TM * jnp.arange(n_tiles + 1, dtype=jnp.int32)
    starts = jax.vmap(lambda row: jnp.searchsorted(row, edges, side="left"))(idx_sorted).astype(jnp.int32)
    lo, hi = starts[:, :-1], starts[:, 1:]
    a = (lo // WIN_BLK) * WIN_BLK
    need = jnp.where(hi > lo, -(-(hi - a) // WIN_ROWS), 0)
    k = jnp.arange(COMBINE_ROUNDS, dtype=jnp.int32)[None, None, :]
    kk = jnp.minimum(k, jnp.maximum(need[..., None] - 1, 0))
    win = jnp.minimum(a[..., None] + WIN_ROWS * kk, cap - WIN_ROWS) // WIN_BLK
    minslot = jnp.where(k < need[..., None], a[..., None] + WIN_ROWS * k, cap)
    return win.astype(jnp.int32), minslot.astype(jnp.int32), jnp.max(need, axis=0).astype(jnp.int32)


def kernel(x_prompt, x_sample, w_in, q_norm, k_norm, lower_bounds, hgrn_norm, w_proj_attn, w_proj_hgrn,
           w_out, norm_mix, norm_ffn, w_router, w_gate, w_up, w_down, norm_final):
    depth = w_in.shape[0]
    groups = (x_prompt, x_sample)
    seq = x_prompt.shape[1]
    group_tokens = [g.shape[0] * g.shape[1] for g in groups]
    x = jnp.concatenate([g.reshape(-1, D_MODEL) for g in groups], axis=0)
    cos, sin = rope_tables(seq)

    for l in range(depth):
        h = rms_norm_rows(x, norm_mix[l], BF16)
        proj = in_proj(h, w_in[l].astype(BF16), F32)
        q, k, v = qk_prep(proj, cos, sin, q_norm[l], k_norm[l], seq)
        attn = attention(q, k, v, seq)
        o_f, o_b = hgrn_scan(proj, hgrn_params(lower_bounds, l), seq)
        merged = merge_branches(attn, o_f, o_b, proj, hgrn_norm[l],
                                w_proj_attn[l].astype(BF16), w_proj_hgrn[l].astype(BF16))
        x, h_ffn, aff = out_proj_router(merged, w_out[l].astype(BF16), x, norm_ffn[l], w_router[l])
        aff_t = aff.T
        wg, wu, wd = w_gate[l].astype(BF16), w_up[l].astype(BF16), w_down[l].astype(BF16)
        start = 0
        for n_tok in group_tokens:
            cap = EC_CAPACITY * n_tok // N_EXPERTS
            aff_g = aff_t[:, start:start + n_tok]
            _, idx = lax.top_k(aff_g, cap)
            idx = jnp.sort(idx, axis=1)
            gate_vals = jnp.take_along_axis(aff_g, idx, axis=1)
            idx = idx + start
            meta = jnp.concatenate(
                [(idx // 128)[..., None], (idx % 128)[..., None],
                 jnp.zeros((N_EXPERTS, cap, META_WIDTH - 2), jnp.int32)], axis=-1).astype(BF16)
            ye = expert_ffn(h_ffn[idx], wg, wu, wd, gate_vals, meta)
            win, minslot, n_rounds = combine_tables(idx, cap, n_tok, start)
            x = combine(x, ye, win, minslot, n_rounds, start // COMBINE_TM)
            start += n_tok

    outs = []
    start = 0
    for g, n_tok in zip(groups, group_tokens):
        outs.append(rms_norm_rows(x, norm_final, F32, start, n_tok).reshape(g.shape))
        start += n_tok
    return tuple(outs)
```

```python
import functools

import jax
import jax.numpy as jnp
from jax import lax
from jax.experimental import pallas as pl
from jax.experimental.pallas import tpu as pltpu

F32 = jnp.float32
BF16 = jnp.bfloat16

D_MODEL = 2048
SEQ = 4096
GRID_W = 64
HEAD_DIM = 128
N_Q_HEADS = 8
N_KV_HEADS = 2
Q_GROUP = N_Q_HEADS // N_KV_HEADS
ATTN_WIDTH = N_Q_HEADS * HEAD_DIM
KV_WIDTH = N_KV_HEADS * HEAD_DIM
ROPE_THETA = 10000.0
ROPE_AXIS_PAIRS = HEAD_DIM // 4
N_HGRN_HEADS = 8
HGRN_DIM = 128
HGRN_WIDTH = N_HGRN_HEADS * HGRN_DIM
N_EXPERTS = 16
EC_CAPACITY = 2
EXPERT_FF = 1024
NORM_EPS = 1e-6
IN_WIDTH = 10752

QKV_WIDTH = ATTN_WIDTH + 2 * KV_WIDTH
COL_QKV = 0
COL_HQ = COL_QKV + QKV_WIDTH
COL_ZF = COL_HQ + HGRN_WIDTH
COL_ZB = COL_ZF + HGRN_WIDTH
COL_HI = COL_ZB + HGRN_WIDTH
COL_HOG = COL_HI + HGRN_WIDTH
COL_GA = COL_HOG + HGRN_WIDTH
COL_GB = COL_GA + D_MODEL

ATTN_TK = 1024

HGRN_CHUNK = 64
HGRN_SUB = 16
HGRN_HEADS_PER_STEP = 8
EXP_CLAMP = 80.0

META_WIDTH = 128
COMBINE_TM = 256
WIN_BLK = 16
WIN_BLKS = 4
WIN_ROWS = WIN_BLK * WIN_BLKS
COMBINE_ROUNDS = -(-(WIN_BLK - 1 + COMBINE_TM) // WIN_ROWS)

V7X_VMEM_BYTES = 64 * 1024 * 1024


def _cparams(n_grid, vmem_mb):
    assert vmem_mb * 1024 * 1024 < V7X_VMEM_BYTES
    return pltpu.CompilerParams(
        dimension_semantics=("arbitrary",) * n_grid,
        vmem_limit_bytes=vmem_mb * 1024 * 1024,
    )


def _sigmoid(x):
    return 1.0 / (1.0 + jnp.exp(-x))


def _rms(x, gain):
    return x * lax.rsqrt(jnp.mean(x * x, axis=-1, keepdims=True) + NORM_EPS) * gain


def _dot_nt(a, b):
    return lax.dot_general(a, b, (((1,), (1,)), ((), ())), preferred_element_type=F32)


def _dot_tn(a, b):
    return lax.dot_general(a, b, (((0,), (0,)), ((), ())), preferred_element_type=F32)


def _norm_body(x_ref, g_ref, o_ref):
    o_ref[...] = _rms(x_ref[...], g_ref[...]).astype(o_ref.dtype)


def rms_norm_rows(x, gain, out_dtype, row_start=0, n_rows=None, tm=512):
    T, D = x.shape
    n_rows = T if n_rows is None else n_rows
    first = row_start // tm
    return pl.pallas_call(
        _norm_body,
        grid=(n_rows // tm,),
        in_specs=[pl.BlockSpec((tm, D), lambda i: (i + first, 0)), pl.BlockSpec((1, D), lambda i: (0, 0))],
        out_specs=pl.BlockSpec((tm, D), lambda i: (i, 0)),
        out_shape=jax.ShapeDtypeStruct((n_rows, D), out_dtype),
        compiler_params=_cparams(1, 32),
        name="rms_norm_rows",
    )(x, gain.reshape(1, D))


def _split_rows_specs(tm, D, n_first):
    return [pl.BlockSpec((tm, D), lambda i: (jnp.minimum(i, n_first - 1), 0)),
            pl.BlockSpec((tm, D), lambda i: (jnp.maximum(i - n_first, 0), 0))]


def _pick_rows(xa_ref, xb_ref, n_first):
    return jnp.where(pl.program_id(0) < n_first, xa_ref[...], xb_ref[...])


def _norm2_body(xa_ref, xb_ref, g_ref, o_ref, *, n_first):
    o_ref[...] = _rms(_pick_rows(xa_ref, xb_ref, n_first), g_ref[...]).astype(o_ref.dtype)


def rms_norm_rows2(xa, xb, gain, out_dtype, tm=512):
    D = xa.shape[1]
    T = xa.shape[0] + xb.shape[0]
    n_first = xa.shape[0] // tm
    return pl.pallas_call(
        functools.partial(_norm2_body, n_first=n_first),
        grid=(T // tm,),
        in_specs=_split_rows_specs(tm, D, n_first) + [pl.BlockSpec((1, D), lambda i: (0, 0))],
        out_specs=pl.BlockSpec((tm, D), lambda i: (i, 0)),
        out_shape=jax.ShapeDtypeStruct((T, D), out_dtype),
        compiler_params=_cparams(1, 32),
        name="rms_norm_rows2",
    )(xa, xb, gain.reshape(1, D))


def _mm_body(a_ref, w_ref, o_ref):
    o_ref[...] = jnp.dot(a_ref[...], w_ref[...], preferred_element_type=F32).astype(o_ref.dtype)


def in_proj(a, w, out_dtype, tm=512, tn=1536):
    M, K = a.shape
    N = w.shape[1]
    return pl.pallas_call(
        _mm_body,
        grid=(N // tn, M // tm),
        in_specs=[pl.BlockSpec((tm, K), lambda j, i: (i, 0)), pl.BlockSpec((K, tn), lambda j, i: (0, j))],
        out_specs=pl.BlockSpec((tm, tn), lambda j, i: (i, j)),
        out_shape=jax.ShapeDtypeStruct((M, N), out_dtype),
        compiler_params=_cparams(2, 48),
        name="in_proj",
    )(a, w)


def _col_window(rows, width, col, row_block):
    return pl.BlockSpec((pl.Element(rows), pl.Element(width)), lambda *g: (row_block(*g) * rows, col))


def _qkprep_body(p_ref, cos_ref, sin_ref, qg_ref, kg_ref, q_ref, k_ref, v_ref, *, scale):
    cos = cos_ref[...]
    sin = sin_ref[...]
    lane = lax.broadcasted_iota(jnp.int32, cos.shape, 1)
    first = (lane & ROPE_AXIS_PAIRS) == 0

    def prep(x, g, s):
        y = _rms(x, g)
        partner = jnp.where(first, pltpu.roll(y, HEAD_DIM - ROPE_AXIS_PAIRS, 1), pltpu.roll(y, ROPE_AXIS_PAIRS, 1))
        return (y * cos + partner * sin) * s

    for h in range(N_Q_HEADS):
        sl = slice(h * HEAD_DIM, (h + 1) * HEAD_DIM)
        q_ref[:, sl] = prep(p_ref[:, sl], qg_ref[...], scale).astype(q_ref.dtype)
    for h in range(N_KV_HEADS):
        sl = slice(h * HEAD_DIM, (h + 1) * HEAD_DIM)
        src = slice(ATTN_WIDTH + h * HEAD_DIM, ATTN_WIDTH + (h + 1) * HEAD_DIM)
        k_ref[:, sl] = prep(p_ref[:, src], kg_ref[...], 1.0).astype(k_ref.dtype)
    v_ref[...] = p_ref[:, ATTN_WIDTH + KV_WIDTH:].astype(v_ref.dtype)


def qk_prep(proj, cos, sin, q_gain, k_gain, seq, tm=256):
    T = proj.shape[0]
    nblk = seq // tm
    return pl.pallas_call(
        functools.partial(_qkprep_body, scale=HEAD_DIM ** -0.5),
        grid=(T // tm,),
        in_specs=[
            pl.BlockSpec((tm, QKV_WIDTH), lambda i: (i, COL_QKV // QKV_WIDTH)),
            pl.BlockSpec((tm, HEAD_DIM), lambda i: (i % nblk, 0)),
            pl.BlockSpec((tm, HEAD_DIM), lambda i: (i % nblk, 0)),
            pl.BlockSpec((1, HEAD_DIM), lambda i: (0, 0)),
            pl.BlockSpec((1, HEAD_DIM), lambda i: (0, 0)),
        ],
        out_specs=[
            pl.BlockSpec((tm, ATTN_WIDTH), lambda i: (i, 0)),
            pl.BlockSpec((tm, KV_WIDTH), lambda i: (i, 0)),
            pl.BlockSpec((tm, KV_WIDTH), lambda i: (i, 0)),
        ],
        out_shape=[
            jax.ShapeDtypeStruct((T, ATTN_WIDTH), BF16),
            jax.ShapeDtypeStruct((T, KV_WIDTH), BF16),
            jax.ShapeDtypeStruct((T, KV_WIDTH), BF16),
        ],
        compiler_params=_cparams(1, 32),
        name="qk_prep",
    )(proj, cos, sin, q_gain.reshape(1, HEAD_DIM), k_gain.reshape(1, HEAD_DIM))


def rope_tables(seq):
    rows = seq // GRID_W
    row = jnp.repeat(jnp.arange(rows, dtype=F32), GRID_W)
    col = jnp.tile(jnp.arange(GRID_W, dtype=F32), rows)
    inv_freq = 1.0 / (ROPE_THETA ** (jnp.arange(ROPE_AXIS_PAIRS, dtype=F32) / ROPE_AXIS_PAIRS))
    ang_r = row[:, None] * inv_freq
    ang_c = col[:, None] * inv_freq
    cos = jnp.concatenate([jnp.cos(ang_r), jnp.cos(ang_r), jnp.cos(ang_c), jnp.cos(ang_c)], axis=1)
    sin = jnp.concatenate([-jnp.sin(ang_r), jnp.sin(ang_r), -jnp.sin(ang_c), jnp.sin(ang_c)], axis=1)
    return cos, sin


def _attn_body(q_ref, k_ref, v_ref, o_ref):
    n_chunks = k_ref.shape[0] // ATTN_TK
    sls = [slice(g * HEAD_DIM, (g + 1) * HEAD_DIM) for g in range(Q_GROUP)]
    qs = [q_ref[:, sl] for sl in sls]
    ms = ls = accs = None
    for c in range(n_chunks):
        k = k_ref[c * ATTN_TK:(c + 1) * ATTN_TK, :]
        v = v_ref[c * ATTN_TK:(c + 1) * ATTN_TK, :]
        ss = [_dot_nt(q, k) for q in qs]
        new_ms = [jnp.max(s, axis=-1, keepdims=True) for s in ss]
        if c > 0:
            new_ms = [jnp.maximum(m, cm) for m, cm in zip(ms, new_ms)]
        ps = [jnp.exp(s - m) for s, m in zip(ss, new_ms)]
        pvs = [jnp.dot(p.astype(BF16), v, preferred_element_type=F32) for p in ps]
        sums = [jnp.sum(p, axis=-1, keepdims=True) for p in ps]
        if c == 0:
            ls, accs = sums, pvs
        else:
            alphas = [jnp.exp(m - nm) for m, nm in zip(ms, new_ms)]
            ls = [l * a + s for l, a, s in zip(ls, alphas, sums)]
            accs = [acc * a + pv for acc, a, pv in zip(accs, alphas, pvs)]
        ms = new_ms
    for sl, acc, l in zip(sls, accs, ls):
        o_ref[:, sl] = (acc / l).astype(o_ref.dtype)


def attention(q, k, v, seq, tq=256):
    T = q.shape[0]
    n_seq = T // seq
    nq = seq // tq
    gw = Q_GROUP * HEAD_DIM
    return pl.pallas_call(
        _attn_body,
        grid=(n_seq, N_KV_HEADS, nq),
        in_specs=[
            pl.BlockSpec((tq, gw), lambda b, h, i: (b * nq + i, h)),
            pl.BlockSpec((seq, HEAD_DIM), lambda b, h, i: (b, h)),
            pl.BlockSpec((seq, HEAD_DIM), lambda b, h, i: (b, h)),
        ],
        out_specs=pl.BlockSpec((tq, gw), lambda b, h, i: (b * nq + i, h)),
        out_shape=jax.ShapeDtypeStruct((T, ATTN_WIDTH), BF16),
        compiler_params=_cparams(3, 48),
        name="attention",
    )(q, k, v)


def _split3(x):
    hi = x.astype(BF16)
    r1 = x - hi.astype(F32)
    mid = r1.astype(BF16)
    lo = (r1 - mid.astype(F32)).astype(BF16)
    return hi, mid, lo


def _hgrn_gates(z, loglb, log1mlb, omlb):
    e = jnp.exp(-jnp.abs(z))
    r = 1.0 / (1.0 + e)
    kk = omlb * jnp.where(z >= 0, e * r, r)
    cc = log1mlb + (jnp.minimum(z, 0.0) - jnp.log1p(e))
    g = jnp.maximum(loglb, cc) + jnp.log1p(jnp.exp(-jnp.abs(loglb - cc)))
    return g, kk


def _hgrn_body(qf_ref, zf_ref, vf_ref, qb_ref, zb_ref, vb_ref, par_ref, of_ref, ob_ref, stf_ref, stb_ref):
    C, SB, W = HGRN_CHUNK, HGRN_SUB, HGRN_DIM
    NH = HGRN_HEADS_PER_STEP

    @pl.when(pl.program_id(2) == 0)
    def _():
        stf_ref[...] = jnp.zeros(stf_ref.shape, F32)
        stb_ref[...] = jnp.zeros(stb_ref.shape, F32)

    row = lax.broadcasted_iota(jnp.int32, (C, C), 0)
    col = lax.broadcasted_iota(jnp.int32, (C, C), 1)
    dirs = (
        (True, qf_ref, zf_ref, vf_ref, 0, jnp.where(col <= row, 1.0, 0.0).astype(BF16), stf_ref, of_ref),
        (False, qb_ref, zb_ref, vb_ref, 3, jnp.where(col >= row, 1.0, 0.0).astype(BF16), stb_ref, ob_ref),
    )
    gs, kks = [], []
    for fwd, q_ref, z_ref, v_ref, p0, tri, st_ref, o_ref in dirs:
        g, kk = _hgrn_gates(z_ref[...], par_ref[p0:p0 + 1, :], par_ref[p0 + 1:p0 + 2, :], par_ref[p0 + 2:p0 + 3, :])
        gs.append(g)
        kks.append(kk)
    bs = []
    for (fwd, q_ref, z_ref, v_ref, p0, tri, st_ref, o_ref), g in zip(dirs, gs):
        hi, mid, lo = _split3(g)
        bs.append(jnp.dot(tri, hi, preferred_element_type=F32) + jnp.dot(tri, mid, preferred_element_type=F32)
                  + jnp.dot(tri, lo, preferred_element_type=F32))
    work = []
    for (fwd, q_ref, z_ref, v_ref, p0, tri, st_ref, o_ref), b, kk in zip(dirs, bs, kks):
        q = q_ref[...]
        vb = v_ref[...].astype(BF16)
        tot = b[C - 1:C, :] if fwd else b[0:1, :]
        q_in = (q * jnp.exp(b)).astype(BF16)
        k_st = (kk * jnp.exp(tot - b)).astype(BF16)
        dec = jnp.exp(tot)
        subs = []
        for i in range(C // SB):
            r0 = i * SB
            if fwd:
                ref = b[r0 - 1:r0, :] if i > 0 else jnp.zeros((1, NH * W), F32)
                k0, k1 = 0, r0 + SB
            else:
                ref = b[r0 + SB:r0 + SB + 1, :] if r0 + SB < C else jnp.zeros((1, NH * W), F32)
                k0, k1 = r0, C
            qs = (q[r0:r0 + SB] * jnp.exp(b[r0:r0 + SB] - ref)).astype(BF16)
            ks = (kk[k0:k1] * jnp.exp(jnp.minimum(ref - b[k0:k1], EXP_CLAMP))).astype(BF16)
            subs.append((r0, k0, k1, qs, ks))
        work.append((fwd, vb, q_in, k_st, dec, subs, st_ref, o_ref))
    res = []
    for fwd, vb, q_in, k_st, dec, subs, st_ref, o_ref in work:
        for h in range(NH):
            sl = slice(h * W, (h + 1) * W)
            st = st_ref[h]
            o_inter = _dot_nt(q_in[:, sl], st.astype(BF16))
            st_ref[h] = st * dec[:, sl] + _dot_tn(vb[:, sl], k_st[:, sl])
            scores = [_dot_nt(qs[:, sl], ks[:, sl]) for (r0, k0, k1, qs, ks) in subs]
            res.append((fwd, h, o_inter, scores, vb, subs, o_ref))
    masked = []
    for fwd, h, o_inter, scores, vb, subs, o_ref in res:
        ms = []
        for s, (r0, k0, k1, qs, ks) in zip(scores, subs):
            t_idx = r0 + lax.broadcasted_iota(jnp.int32, s.shape, 0)
            s_idx = k0 + lax.broadcasted_iota(jnp.int32, s.shape, 1)
            keep = (s_idx <= t_idx) if fwd else (s_idx >= t_idx)
            ms.append(jnp.where(keep, s, 0.0).astype(BF16))
        masked.append(ms)
    for (fwd, h, o_inter, scores, vb, subs, o_ref), ms in zip(res, masked):
        sl = slice(h * W, (h + 1) * W)
        outs = [jnp.dot(m, vb[k0:k1, sl], preferred_element_type=F32) for m, (r0, k0, k1, qs, ks) in zip(ms, subs)]
        o_ref[:, sl] = o_inter + jnp.concatenate(outs, axis=0)


def hgrn_scan(proj, params, seq):
    T = proj.shape[0]
    n_seq = T // seq
    C = HGRN_CHUNK
    nc = seq // C
    w = HGRN_HEADS_PER_STEP * HGRN_DIM
    n_hh = HGRN_WIDTH // w

    assert n_hh == 1

    def fwd_spec(col):
        return _col_window(C, w, col, lambda b, hh, j: b * nc + j)

    def bwd_spec(col):
        return _col_window(C, w, col, lambda b, hh, j: b * nc + nc - 1 - j)

    return pl.pallas_call(
        _hgrn_body,
        grid=(n_seq, n_hh, nc),
        in_specs=[
            fwd_spec(COL_HQ), fwd_spec(COL_ZF), fwd_spec(COL_HI),
            bwd_spec(COL_HQ), bwd_spec(COL_ZB), bwd_spec(COL_HI),
            pl.BlockSpec((8, w), lambda b, hh, j: (0, hh)),
        ],
        out_specs=[
            pl.BlockSpec((C, w), lambda b, hh, j: (b * nc + j, hh)),
            pl.BlockSpec((C, w), lambda b, hh, j: (b * nc + nc - 1 - j, hh)),
        ],
        out_shape=[jax.ShapeDtypeStruct((T, HGRN_WIDTH), F32), jax.ShapeDtypeStruct((T, HGRN_WIDTH), F32)],
        scratch_shapes=[
            pltpu.VMEM((HGRN_HEADS_PER_STEP, HGRN_DIM, HGRN_DIM), F32),
            pltpu.VMEM((HGRN_HEADS_PER_STEP, HGRN_DIM, HGRN_DIM), F32),
        ],
        compiler_params=_cparams(3, 32),
        name="hgrn_scan",
    )(proj, proj, proj, proj, proj, proj, params)


def hgrn_params(lower_bounds, layer):
    lb_all = jnp.cumsum(jax.nn.softmax(lower_bounds.astype(F32), axis=1), axis=1)
    lb_all = lb_all - lb_all[:, :1]
    rows = []
    for d in range(2):
        lb = lb_all[d, layer]
        rows += [jnp.log(lb), jnp.log1p(-lb), 1.0 - lb]
    rows += [jnp.zeros_like(rows[0])] * 2
    return jnp.stack(rows, axis=0)


def _merge_body(attn_ref, of_ref, ob_ref, hog_ref, ga_ref, gb_ref, hg_ref, wa_ref, wh_ref, o_ref):
    ya = jnp.dot(attn_ref[...], wa_ref[...], preferred_element_type=F32)
    hs = []
    for h in range(N_HGRN_HEADS):
        sl = slice(h * HGRN_DIM, (h + 1) * HGRN_DIM)
        o = _rms(of_ref[:, sl] + ob_ref[:, sl], hg_ref[...])
        og = hog_ref[:, sl]
        hs.append((o * (og * _sigmoid(og))).astype(BF16))
    yh = jnp.dot(jnp.concatenate(hs, axis=1), wh_ref[...], preferred_element_type=F32)
    o_ref[...] = (_sigmoid(ga_ref[...]) * ya + _sigmoid(gb_ref[...]) * yh).astype(o_ref.dtype)


def merge_branches(attn, o_f, o_b, proj, hgrn_gain, w_attn, w_hgrn, tm=256):
    T = attn.shape[0]
    D = D_MODEL
    return pl.pallas_call(
        _merge_body,
        grid=(T // tm,),
        in_specs=[
            pl.BlockSpec((tm, ATTN_WIDTH), lambda i: (i, 0)),
            pl.BlockSpec((tm, HGRN_WIDTH), lambda i: (i, 0)),
            pl.BlockSpec((tm, HGRN_WIDTH), lambda i: (i, 0)),
            _col_window(tm, HGRN_WIDTH, COL_HOG, lambda i: i),
            _col_window(tm, D, COL_GA, lambda i: i),
            _col_window(tm, D, COL_GB, lambda i: i),
            pl.BlockSpec((1, HGRN_DIM), lambda i: (0, 0)),
            pl.BlockSpec((ATTN_WIDTH, D), lambda i: (0, 0)),
            pl.BlockSpec((HGRN_WIDTH, D), lambda i: (0, 0)),
        ],
        out_specs=pl.BlockSpec((tm, D), lambda i: (i, 0)),
        out_shape=jax.ShapeDtypeStruct((T, D), BF16),
        compiler_params=_cparams(1, 56),
        name="merge_branches",
    )(attn, o_f, o_b, proj, proj, proj, hgrn_gain.reshape(1, HGRN_DIM), w_attn, w_hgrn)


def _outproj_body(m_ref, w_ref, xa_ref, xb_ref, g_ref, wr_ref, xo_ref, h_ref, aff_ref, *, n_first):
    xn = _pick_rows(xa_ref, xb_ref, n_first) + jnp.dot(m_ref[...], w_ref[...], preferred_element_type=F32)
    xo_ref[...] = xn
    h = _rms(xn, g_ref[...])
    h_ref[...] = h.astype(h_ref.dtype)
    hb = h.astype(BF16)
    hl = (h - hb.astype(F32)).astype(BF16)
    logits = (jnp.dot(hb, wr_ref[0], preferred_element_type=F32) + jnp.dot(hl, wr_ref[0], preferred_element_type=F32)
              + jnp.dot(hb, wr_ref[1], preferred_element_type=F32))
    mx = jnp.max(logits, axis=1, keepdims=True)
    ex = jnp.exp(logits - mx)
    aff_ref[...] = ex / jnp.sum(ex, axis=1, keepdims=True)


def out_proj_router(merged, w_out, xa, xb, ffn_gain, w_router, tm=256):
    D = xa.shape[1]
    n_first = xa.shape[0] // tm
    xb = xa if xb is None else xb
    T = merged.shape[0]
    wr_hi = w_router.astype(BF16)
    wr_lo = (w_router - wr_hi.astype(F32)).astype(BF16)
    wr = jnp.stack([wr_hi, wr_lo], axis=0)
    return pl.pallas_call(
        functools.partial(_outproj_body, n_first=n_first),
        grid=(T // tm,),
        in_specs=[
            pl.BlockSpec((tm, D), lambda i: (i, 0)),
            pl.BlockSpec((D, D), lambda i: (0, 0)),
            *_split_rows_specs(tm, D, n_first),
            pl.BlockSpec((1, D), lambda i: (0, 0)),
            pl.BlockSpec((2, D, N_EXPERTS), lambda i: (0, 0, 0)),
        ],
        out_specs=[
            pl.BlockSpec((tm, D), lambda i: (i, 0)),
            pl.BlockSpec((tm, D), lambda i: (i, 0)),
            pl.BlockSpec((tm, N_EXPERTS), lambda i: (i, 0)),
        ],
        out_shape=[
            jax.ShapeDtypeStruct((T, D), F32),
            jax.ShapeDtypeStruct((T, D), BF16),
            jax.ShapeDtypeStruct((T, N_EXPERTS), F32),
        ],
        compiler_params=_cparams(1, 48),
        name="out_proj_router",
    )(merged, w_out, xa, xb, ffn_gain.reshape(1, D), wr)


ROUTE_LANES = 128
NOT_SELECTED = 2 ** 30


def _route_body(a_ref, key_ref, rowpref_ref, *, cap, base_tok):
    a = a_ref[...]
    E, nc, L = a.shape
    bits = pltpu.bitcast(a, jnp.int32)

    def count(mask):
        ones = jnp.where(mask, 1.0, 0.0)
        return jnp.sum(jnp.sum(ones, axis=1, keepdims=True), axis=2, keepdims=True)

    def bisect(_, carry):
        lo, hi = carry
        mid = lo + jnp.right_shift(hi - lo, 1)
        ge = count(bits >= mid) >= cap
        return jnp.where(ge, mid, lo), jnp.where(ge, hi, mid)

    lo0 = jnp.zeros((E, 1, 1), jnp.int32)
    hi0 = jnp.full((E, 1, 1), 0x7F800000, jnp.int32)
    thr, _ = lax.fori_loop(0, 31, bisect, (lo0, hi0))

    s_i = lax.broadcasted_iota(jnp.int32, (L, L), 0)
    t_i = lax.broadcasted_iota(jnp.int32, (L, L), 1)
    incl_lane = jnp.where(s_i <= t_i, 1.0, 0.0).astype(BF16)
    r_i = lax.broadcasted_iota(jnp.int32, (E, nc, nc), 1)
    c_i = lax.broadcasted_iota(jnp.int32, (E, nc, nc), 2)
    rows_before = jnp.where(c_i < r_i, 1.0, 0.0).astype(BF16)

    def prefix(mask):
        ones = jnp.where(mask, 1.0, 0.0)
        incl = jnp.dot(ones.astype(BF16).reshape(E * nc, L), incl_lane, preferred_element_type=F32).reshape(E, nc, L)
        row_tot = jnp.broadcast_to(incl[:, :, L - 1:L], (E, nc, L)).astype(BF16)
        before = lax.dot_general(rows_before, row_tot, (((2,), (1,)), ((0,), (0,))), preferred_element_type=F32)
        return ones, incl, before

    gt = bits > thr
    eq = bits == thr
    need_eq = cap - count(gt)
    eq_f, eq_incl, eq_before = prefix(eq)
    take_eq = jnp.where(eq_before + eq_incl - eq_f < need_eq, eq_f, 0.0)
    sel = jnp.where(gt, 1.0, take_eq) > 0.5
    _, _, sel_before = prefix(sel)
    tok = (base_tok + lax.broadcasted_iota(jnp.int32, (E, nc, L), 1) * L
           + lax.broadcasted_iota(jnp.int32, (E, nc, L), 2))
    key_ref[...] = jnp.where(sel, tok, NOT_SELECTED)
    rowpref_ref[...] = sel_before.astype(jnp.int32)


def route_select(aff_group_t, cap, base_tok):
    E, n_tok = aff_group_t.shape
    nc = n_tok // ROUTE_LANES
    a3 = aff_group_t.reshape(E, nc, ROUTE_LANES)
    keys, rowpref = pl.pallas_call(
        functools.partial(_route_body, cap=cap, base_tok=base_tok),
        out_shape=[jax.ShapeDtypeStruct((E, nc, ROUTE_LANES), jnp.int32)] * 2,
        compiler_params=pltpu.CompilerParams(vmem_limit_bytes=48 * 1024 * 1024),
        name="route_select",
    )(a3)
    return keys.reshape(E, n_tok), rowpref[:, :, 0]


def _ffn_body(x_ref, wg_ref, wu_ref, wd_ref, gv_ref, meta_ref, o_ref):
    x = x_ref[0]
    g = jnp.dot(x, wg_ref[0], preferred_element_type=F32)
    u = jnp.dot(x, wu_ref[0], preferred_element_type=F32)
    a = (g * _sigmoid(g) * u).astype(BF16)
    y = jnp.dot(a, wd_ref[0], preferred_element_type=F32) * gv_ref[0]
    o_ref[0, :, :D_MODEL] = y.astype(o_ref.dtype)
    o_ref[0, :, D_MODEL:] = meta_ref[0]


def expert_ffn(xe, w_gate, w_up, w_down, gate_vals, meta, tm=256):
    E, cap, D = xe.shape
    F = w_gate.shape[-1]
    return pl.pallas_call(
        _ffn_body,
        grid=(E, cap // tm),
        in_specs=[
            pl.BlockSpec((1, tm, D), lambda e, i: (e, i, 0)),
            pl.BlockSpec((1, D, F), lambda e, i: (e, 0, 0)),
            pl.BlockSpec((1, D, F), lambda e, i: (e, 0, 0)),
            pl.BlockSpec((1, F, D), lambda e, i: (e, 0, 0)),
            pl.BlockSpec((1, tm, 1), lambda e, i: (e, i, 0)),
            pl.BlockSpec((1, tm, META_WIDTH), lambda e, i: (e, i, 0)),
        ],
        out_specs=pl.BlockSpec((1, tm, D + META_WIDTH), lambda e, i: (e, i, 0)),
        out_shape=jax.ShapeDtypeStruct((E, cap, D + META_WIDTH), BF16),
        compiler_params=_cparams(2, 48),
        name="expert_ffn",
    )(xe, w_gate, w_up, w_down, gate_vals.reshape(E, cap, 1), meta)


def _combine_body(win_ref, minslot_ref, nr_ref, x_ref, *refs, base_tile, n_tiles):
    ye_win = refs[:N_EXPERTS]
    ye_hbm, o_ref, buf_ref, sem_ref = refs[N_EXPERTS:]
    j = pl.program_id(0)
    t0 = ((j + base_tile) * COMBINE_TM).astype(F32)

    def table_index(e, k):
        return e * (n_tiles * COMBINE_ROUNDS) + j * COMBINE_ROUNDS + k

    def placed(window, k):
        rows, toks = [], []
        for e in range(N_EXPERTS):
            w = window(e)
            meta = w[:, D_MODEL:].astype(F32)
            tok = meta[:, 0:1] * 128.0 + meta[:, 1:2] - t0
            flat = table_index(e, k)
            slot = win_ref[flat] * WIN_BLK + lax.broadcasted_iota(jnp.int32, (WIN_ROWS, 1), 0)
            toks.append(jnp.where(slot >= minslot_ref[flat], tok, -1.0))
            rows.append(w[:, :D_MODEL])
        tok_all = jnp.concatenate(toks, axis=0)
        lane_t = lax.broadcasted_iota(jnp.int32, (N_EXPERTS * WIN_ROWS, COMBINE_TM), 1).astype(F32)
        place = jnp.where(tok_all == lane_t, 1.0, 0.0).astype(BF16)
        return _dot_tn(place, jnp.concatenate(rows, axis=0))

    o_ref[...] = x_ref[...] + placed(lambda e: ye_win[e][...].reshape(WIN_ROWS, D_MODEL + META_WIDTH), 0)

    def window_copy(e, k):
        return pltpu.make_async_copy(
            ye_hbm.at[e, pl.ds(win_ref[table_index(e, k)], WIN_BLKS)], buf_ref.at[e], sem_ref.at[e])

    def extra_round(k, carry):
        for e in range(N_EXPERTS):
            window_copy(e, k).start()
        for e in range(N_EXPERTS):
            window_copy(e, k).wait()
        o_ref[...] += placed(lambda e: buf_ref[e].reshape(WIN_ROWS, D_MODEL + META_WIDTH), k)
        return carry

    lax.fori_loop(1, nr_ref[j], extra_round, 0)


def combine(x, ye_ext, win, minslot, n_rounds, base_tile):
    E, cap, dext = ye_ext.shape
    n_tiles = n_rounds.shape[0]
    ye4 = ye_ext.reshape(E, cap // WIN_BLK, WIN_BLK, dext)

    def ye_spec(e):
        return pl.BlockSpec(
            (pl.Element(1), pl.Element(WIN_BLKS), pl.Element(WIN_BLK), pl.Element(dext)),
            lambda j, win_r, ms_r, nr_r: (e, win_r[e * (n_tiles * COMBINE_ROUNDS) + j * COMBINE_ROUNDS], 0, 0))

    grid_spec = pltpu.PrefetchScalarGridSpec(
        num_scalar_prefetch=3,
        grid=(n_tiles,),
        in_specs=[pl.BlockSpec((COMBINE_TM, D_MODEL), lambda j, *_: (j + base_tile, 0))]
        + [ye_spec(e) for e in range(E)]
        + [pl.BlockSpec(memory_space=pl.ANY)],
        out_specs=pl.BlockSpec((COMBINE_TM, D_MODEL), lambda j, *_: (j + base_tile, 0)),
        scratch_shapes=[
            pltpu.VMEM((E, WIN_BLKS, WIN_BLK, dext), BF16),
            pltpu.SemaphoreType.DMA((E,)),
        ],
    )
    return pl.pallas_call(
        functools.partial(_combine_body, base_tile=base_tile, n_tiles=n_tiles),
        grid_spec=grid_spec,
        out_shape=jax.ShapeDtypeStruct(x.shape, x.dtype),
        input_output_aliases={3: 0},
        compiler_params=_cparams(1, 48),
        name="combine",
    )(win.reshape(-1), minslot.reshape(-1), n_rounds, x, *([ye4] * E), ye4)


def combine_tables(rowpref, cap):
    E = rowpref.shape[0]
    starts = jnp.concatenate([rowpref[:, ::COMBINE_TM // ROUTE_LANES], jnp.full((E, 1), cap, jnp.int32)], axis=1)
    lo, hi = starts[:, :-1], starts[:, 1:]
    a = (lo // WIN_BLK) * WIN_BLK
    need = jnp.where(hi > lo, -(-(hi - a) // WIN_ROWS), 0)
    k = jnp.arange(COMBINE_ROUNDS, dtype=jnp.int32)[None, None, :]
    kk = jnp.minimum(k, jnp.maximum(need[..., None] - 1, 0))
    win = jnp.minimum(a[..., None] + WIN_ROWS * kk, cap - WIN_ROWS) // WIN_BLK
    minslot = jnp.where(k < need[..., None], a[..., None] + WIN_ROWS * k, cap)
    return win.astype(jnp.int32), minslot.astype(jnp.int32), jnp.max(need, axis=0).astype(jnp.int32)


def kernel(x_prompt, x_sample, w_in, q_norm, k_norm, lower_bounds, hgrn_norm, w_proj_attn, w_proj_hgrn,
           w_out, norm_mix, norm_ffn, w_router, w_gate, w_up, w_down, norm_final):
    depth = w_in.shape[0]
    groups = (x_prompt, x_sample)
    seq = x_prompt.shape[1]
    group_tokens = [g.shape[0] * g.shape[1] for g in groups]
    xa, xb = (g.reshape(-1, D_MODEL) for g in groups)
    cos, sin = rope_tables(seq)

    for l in range(depth):
        h = rms_norm_rows2(xa, xb, norm_mix[l], BF16) if l == 0 else rms_norm_rows(x, norm_mix[l], BF16)
        proj = in_proj(h, w_in[l].astype(BF16), F32)
        q, k, v = qk_prep(proj, cos, sin, q_norm[l], k_norm[l], seq)
        attn = attention(q, k, v, seq)
        o_f, o_b = hgrn_scan(proj, hgrn_params(lower_bounds, l), seq)
        merged = merge_branches(attn, o_f, o_b, proj, hgrn_norm[l],
                                w_proj_attn[l].astype(BF16), w_proj_hgrn[l].astype(BF16))
        x, h_ffn, aff = out_proj_router(merged, w_out[l].astype(BF16), *((xa, xb) if l == 0 else (x, None)),
                                        norm_ffn[l], w_router[l])
        aff_t = aff.T
        wg, wu, wd = w_gate[l].astype(BF16), w_up[l].astype(BF16), w_down[l].astype(BF16)
        start = 0
        for n_tok in group_tokens:
            cap = EC_CAPACITY * n_tok // N_EXPERTS
            aff_g = aff_t[:, start:start + n_tok]
            keys, rowpref = route_select(aff_g, cap, start)
            idx = jnp.sort(keys, axis=1)[:, :cap]
            gate_vals = jnp.take_along_axis(aff_g, idx - start, axis=1)
            meta = jnp.concatenate(
                [(idx // 128)[..., None], (idx % 128)[..., None],
                 jnp.zeros((N_EXPERTS, cap, META_WIDTH - 2), jnp.int32)], axis=-1).astype(BF16)
            ye = expert_ffn(h_ffn[idx], wg, wu, wd, gate_vals, meta)
            win, minslot, n_rounds = combine_tables(rowpref, cap)
            x = combine(x, ye, win, minslot, n_rounds, start // COMBINE_TM)
            start += n_tok

    outs = []
    start = 0
    for g, n_tok in zip(groups, group_tokens):
        outs.append(rms_norm_rows(x, norm_final, F32, start, n_tok).reshape(g.shape))
        start += n_tok
    return tuple(outs)
```

```python
import functools

import jax
import jax.numpy as jnp
from jax import lax
from jax.experimental import pallas as pl
from jax.experimental.pallas import tpu as pltpu

F32 = jnp.float32
BF16 = jnp.bfloat16

D_MODEL = 2048
SEQ = 4096
GRID_W = 64
HEAD_DIM = 128
N_Q_HEADS = 8
N_KV_HEADS = 2
Q_GROUP = N_Q_HEADS // N_KV_HEADS
ATTN_WIDTH = N_Q_HEADS * HEAD_DIM
KV_WIDTH = N_KV_HEADS * HEAD_DIM
ROPE_THETA = 10000.0
ROPE_AXIS_PAIRS = HEAD_DIM // 4
N_HGRN_HEADS = 8
HGRN_DIM = 128
HGRN_WIDTH = N_HGRN_HEADS * HGRN_DIM
N_EXPERTS = 16
EC_CAPACITY = 2
EXPERT_FF = 1024
NORM_EPS = 1e-6
IN_WIDTH = 10752

QKV_WIDTH = ATTN_WIDTH + 2 * KV_WIDTH
COL_QKV = 0
COL_HQ = COL_QKV + QKV_WIDTH
COL_ZF = COL_HQ + HGRN_WIDTH
COL_ZB = COL_ZF + HGRN_WIDTH
COL_HI = COL_ZB + HGRN_WIDTH
COL_HOG = COL_HI + HGRN_WIDTH
COL_GA = COL_HOG + HGRN_WIDTH
COL_GB = COL_GA + D_MODEL

ATTN_TK = 1024

HGRN_CHUNK = 64
HGRN_SUB = 16
HGRN_HEADS_PER_STEP = 8
EXP_CLAMP = 80.0

META_WIDTH = 128
COMBINE_TM = 256
WIN_BLK = 16
WIN_BLKS = 4
WIN_ROWS = WIN_BLK * WIN_BLKS
COMBINE_ROUNDS = -(-(WIN_BLK - 1 + COMBINE_TM) // WIN_ROWS)

V7X_VMEM_BYTES = 64 * 1024 * 1024


def _cparams(n_grid, vmem_mb):
    assert vmem_mb * 1024 * 1024 < V7X_VMEM_BYTES
    return pltpu.CompilerParams(
        dimension_semantics=("arbitrary",) * n_grid,
        vmem_limit_bytes=vmem_mb * 1024 * 1024,
    )


def _sigmoid(x):
    return 1.0 / (1.0 + jnp.exp(-x))


def _rms(x, gain):
    return x * lax.rsqrt(jnp.mean(x * x, axis=-1, keepdims=True) + NORM_EPS) * gain


def _dot_nt(a, b):
    return lax.dot_general(a, b, (((1,), (1,)), ((), ())), preferred_element_type=F32)


def _dot_tn(a, b):
    return lax.dot_general(a, b, (((0,), (0,)), ((), ())), preferred_element_type=F32)


def _norm_body(x_ref, g_ref, o_ref):
    o_ref[...] = _rms(x_ref[...], g_ref[...]).astype(o_ref.dtype)


def rms_norm_rows(x, gain, out_dtype, row_start=0, n_rows=None, tm=512):
    T, D = x.shape
    n_rows = T if n_rows is None else n_rows
    first = row_start // tm
    return pl.pallas_call(
        _norm_body,
        grid=(n_rows // tm,),
        in_specs=[pl.BlockSpec((tm, D), lambda i: (i + first, 0)), pl.BlockSpec((1, D), lambda i: (0, 0))],
        out_specs=pl.BlockSpec((tm, D), lambda i: (i, 0)),
        out_shape=jax.ShapeDtypeStruct((n_rows, D), out_dtype),
        compiler_params=_cparams(1, 32),
        name="rms_norm_rows",
    )(x, gain.reshape(1, D))


def _split_rows_specs(tm, D, n_first):
    return [pl.BlockSpec((tm, D), lambda i: (jnp.minimum(i, n_first - 1), 0)),
            pl.BlockSpec((tm, D), lambda i: (jnp.maximum(i - n_first, 0), 0))]


def _pick_rows(xa_ref, xb_ref, n_first):
    return jnp.where(pl.program_id(0) < n_first, xa_ref[...], xb_ref[...])


def _norm2_body(xa_ref, xb_ref, g_ref, o_ref, *, n_first):
    o_ref[...] = _rms(_pick_rows(xa_ref, xb_ref, n_first), g_ref[...]).astype(o_ref.dtype)


def rms_norm_rows2(xa, xb, gain, out_dtype, tm=512):
    D = xa.shape[1]
    T = xa.shape[0] + xb.shape[0]
    n_first = xa.shape[0] // tm
    return pl.pallas_call(
        functools.partial(_norm2_body, n_first=n_first),
        grid=(T // tm,),
        in_specs=_split_rows_specs(tm, D, n_first) + [pl.BlockSpec((1, D), lambda i: (0, 0))],
        out_specs=pl.BlockSpec((tm, D), lambda i: (i, 0)),
        out_shape=jax.ShapeDtypeStruct((T, D), out_dtype),
        compiler_params=_cparams(1, 32),
        name="rms_norm_rows2",
    )(xa, xb, gain.reshape(1, D))


def _mm_body(a_ref, w_ref, o_ref):
    o_ref[...] = jnp.dot(a_ref[...], w_ref[...], preferred_element_type=F32).astype(o_ref.dtype)


def in_proj(a, w, layer, out_dtype, tm=512, tn=1536):
    M, K = a.shape
    N = w.shape[2]
    return pl.pallas_call(
        _mm_body,
        grid=(N // tn, M // tm),
        in_specs=[pl.BlockSpec((tm, K), lambda j, i: (i, 0)),
                  pl.BlockSpec((None, K, tn), lambda j, i: (layer, 0, j))],
        out_specs=pl.BlockSpec((tm, tn), lambda j, i: (i, j)),
        out_shape=jax.ShapeDtypeStruct((M, N), out_dtype),
        compiler_params=_cparams(2, 48),
        name="in_proj",
    )(a, w)


def _col_window(rows, width, col, row_block):
    return pl.BlockSpec((pl.Element(rows), pl.Element(width)), lambda *g: (row_block(*g) * rows, col))


def _qkprep_body(p_ref, cos_ref, sin_ref, qg_ref, kg_ref, q_ref, k_ref, v_ref, *, scale):
    cos = cos_ref[...]
    sin = sin_ref[...]
    lane = lax.broadcasted_iota(jnp.int32, cos.shape, 1)
    first = (lane & ROPE_AXIS_PAIRS) == 0

    def prep(x, g, s):
        y = _rms(x, g)
        partner = jnp.where(first, pltpu.roll(y, HEAD_DIM - ROPE_AXIS_PAIRS, 1), pltpu.roll(y, ROPE_AXIS_PAIRS, 1))
        return (y * cos + partner * sin) * s

    for h in range(N_Q_HEADS):
        sl = slice(h * HEAD_DIM, (h + 1) * HEAD_DIM)
        q_ref[:, sl] = prep(p_ref[:, sl], qg_ref[...], scale).astype(q_ref.dtype)
    for h in range(N_KV_HEADS):
        sl = slice(h * HEAD_DIM, (h + 1) * HEAD_DIM)
        src = slice(ATTN_WIDTH + h * HEAD_DIM, ATTN_WIDTH + (h + 1) * HEAD_DIM)
        k_ref[:, sl] = prep(p_ref[:, src], kg_ref[...], 1.0).astype(k_ref.dtype)
    v_ref[...] = p_ref[:, ATTN_WIDTH + KV_WIDTH:].astype(v_ref.dtype)


def qk_prep(proj, cos, sin, q_gain, k_gain, seq, tm=256):
    T = proj.shape[0]
    nblk = seq // tm
    return pl.pallas_call(
        functools.partial(_qkprep_body, scale=HEAD_DIM ** -0.5),
        grid=(T // tm,),
        in_specs=[
            pl.BlockSpec((tm, QKV_WIDTH), lambda i: (i, COL_QKV // QKV_WIDTH)),
            pl.BlockSpec((tm, HEAD_DIM), lambda i: (i % nblk, 0)),
            pl.BlockSpec((tm, HEAD_DIM), lambda i: (i % nblk, 0)),
            pl.BlockSpec((1, HEAD_DIM), lambda i: (0, 0)),
            pl.BlockSpec((1, HEAD_DIM), lambda i: (0, 0)),
        ],
        out_specs=[
            pl.BlockSpec((tm, ATTN_WIDTH), lambda i: (i, 0)),
            pl.BlockSpec((tm, KV_WIDTH), lambda i: (i, 0)),
            pl.BlockSpec((tm, KV_WIDTH), lambda i: (i, 0)),
        ],
        out_shape=[
            jax.ShapeDtypeStruct((T, ATTN_WIDTH), BF16),
            jax.ShapeDtypeStruct((T, KV_WIDTH), BF16),
            jax.ShapeDtypeStruct((T, KV_WIDTH), BF16),
        ],
        compiler_params=_cparams(1, 32),
        name="qk_prep",
    )(proj, cos, sin, q_gain.reshape(1, HEAD_DIM), k_gain.reshape(1, HEAD_DIM))


def rope_tables(seq):
    rows = seq // GRID_W
    row = jnp.repeat(jnp.arange(rows, dtype=F32), GRID_W)
    col = jnp.tile(jnp.arange(GRID_W, dtype=F32), rows)
    inv_freq = 1.0 / (ROPE_THETA ** (jnp.arange(ROPE_AXIS_PAIRS, dtype=F32) / ROPE_AXIS_PAIRS))
    ang_r = row[:, None] * inv_freq
    ang_c = col[:, None] * inv_freq
    cos = jnp.concatenate([jnp.cos(ang_r), jnp.cos(ang_r), jnp.cos(ang_c), jnp.cos(ang_c)], axis=1)
    sin = jnp.concatenate([-jnp.sin(ang_r), jnp.sin(ang_r), -jnp.sin(ang_c), jnp.sin(ang_c)], axis=1)
    return cos, sin


def _attn_body(q_ref, k_ref, v_ref, o_ref):
    n_chunks = k_ref.shape[0] // ATTN_TK
    sls = [slice(g * HEAD_DIM, (g + 1) * HEAD_DIM) for g in range(Q_GROUP)]
    qs = [q_ref[:, sl] for sl in sls]
    ms = ls = accs = None
    for c in range(n_chunks):
        k = k_ref[c * ATTN_TK:(c + 1) * ATTN_TK, :]
        v = v_ref[c * ATTN_TK:(c + 1) * ATTN_TK, :]
        ss = [_dot_nt(q, k) for q in qs]
        new_ms = [jnp.max(s, axis=-1, keepdims=True) for s in ss]
        if c > 0:
            new_ms = [jnp.maximum(m, cm) for m, cm in zip(ms, new_ms)]
        ps = [jnp.exp(s - m) for s, m in zip(ss, new_ms)]
        pvs = [jnp.dot(p.astype(BF16), v, preferred_element_type=F32) for p in ps]
        sums = [jnp.sum(p, axis=-1, keepdims=True) for p in ps]
        if c == 0:
            ls, accs = sums, pvs
        else:
            alphas = [jnp.exp(m - nm) for m, nm in zip(ms, new_ms)]
            ls = [l * a + s for l, a, s in zip(ls, alphas, sums)]
            accs = [acc * a + pv for acc, a, pv in zip(accs, alphas, pvs)]
        ms = new_ms
    for sl, acc, l in zip(sls, accs, ls):
        o_ref[:, sl] = (acc / l).astype(o_ref.dtype)


def attention(q, k, v, seq, tq=256):
    T = q.shape[0]
    n_seq = T // seq
    nq = seq // tq
    gw = Q_GROUP * HEAD_DIM
    return pl.pallas_call(
        _attn_body,
        grid=(n_seq, N_KV_HEADS, nq),
        in_specs=[
            pl.BlockSpec((tq, gw), lambda b, h, i: (b * nq + i, h)),
            pl.BlockSpec((seq, HEAD_DIM), lambda b, h, i: (b, h)),
            pl.BlockSpec((seq, HEAD_DIM), lambda b, h, i: (b, h)),
        ],
        out_specs=pl.BlockSpec((tq, gw), lambda b, h, i: (b * nq + i, h)),
        out_shape=jax.ShapeDtypeStruct((T, ATTN_WIDTH), BF16),
        compiler_params=_cparams(3, 48),
        name="attention",
    )(q, k, v)


def _split3(x):
    hi = x.astype(BF16)
    r1 = x - hi.astype(F32)
    mid = r1.astype(BF16)
    lo = (r1 - mid.astype(F32)).astype(BF16)
    return hi, mid, lo


def _hgrn_gates(z, loglb, log1mlb, omlb):
    e = jnp.exp(-jnp.abs(z))
    r = 1.0 / (1.0 + e)
    kk = omlb * jnp.where(z >= 0, e * r, r)
    cc = log1mlb + (jnp.minimum(z, 0.0) - jnp.log1p(e))
    g = jnp.maximum(loglb, cc) + jnp.log1p(jnp.exp(-jnp.abs(loglb - cc)))
    return g, kk


def _hgrn_body(qf_ref, zf_ref, vf_ref, qb_ref, zb_ref, vb_ref, par_ref, of_ref, ob_ref, stf_ref, stb_ref):
    C, SB, W = HGRN_CHUNK, HGRN_SUB, HGRN_DIM
    NH = HGRN_HEADS_PER_STEP

    @pl.when(pl.program_id(2) == 0)
    def _():
        stf_ref[...] = jnp.zeros(stf_ref.shape, F32)
        stb_ref[...] = jnp.zeros(stb_ref.shape, F32)

    row = lax.broadcasted_iota(jnp.int32, (C, C), 0)
    col = lax.broadcasted_iota(jnp.int32, (C, C), 1)
    dirs = (
        (True, qf_ref, zf_ref, vf_ref, 0, jnp.where(col <= row, 1.0, 0.0).astype(BF16), stf_ref, of_ref),
        (False, qb_ref, zb_ref, vb_ref, 3, jnp.where(col >= row, 1.0, 0.0).astype(BF16), stb_ref, ob_ref),
    )
    gs, kks = [], []
    for fwd, q_ref, z_ref, v_ref, p0, tri, st_ref, o_ref in dirs:
        g, kk = _hgrn_gates(z_ref[...], par_ref[p0:p0 + 1, :], par_ref[p0 + 1:p0 + 2, :], par_ref[p0 + 2:p0 + 3, :])
        gs.append(g)
        kks.append(kk)
    bs = []
    for (fwd, q_ref, z_ref, v_ref, p0, tri, st_ref, o_ref), g in zip(dirs, gs):
        hi, mid, lo = _split3(g)
        bs.append(jnp.dot(tri, hi, preferred_element_type=F32) + jnp.dot(tri, mid, preferred_element_type=F32)
                  + jnp.dot(tri, lo, preferred_element_type=F32))
    work = []
    for (fwd, q_ref, z_ref, v_ref, p0, tri, st_ref, o_ref), b, kk in zip(dirs, bs, kks):
        q = q_ref[...]
        vb = v_ref[...].astype(BF16)
        tot = b[C - 1:C, :] if fwd else b[0:1, :]
        q_in = (q * jnp.exp(b)).astype(BF16)
        k_st = (kk * jnp.exp(tot - b)).astype(BF16)
        dec = jnp.exp(tot)
        subs = []
        for i in range(C // SB):
            r0 = i * SB
            if fwd:
                ref = b[r0 - 1:r0, :] if i > 0 else jnp.zeros((1, NH * W), F32)
                k0, k1 = 0, r0 + SB
            else:
                ref = b[r0 + SB:r0 + SB + 1, :] if r0 + SB < C else jnp.zeros((1, NH * W), F32)
                k0, k1 = r0, C
            qs = (q[r0:r0 + SB] * jnp.exp(b[r0:r0 + SB] - ref)).astype(BF16)
            ks = (kk[k0:k1] * jnp.exp(jnp.minimum(ref - b[k0:k1], EXP_CLAMP))).astype(BF16)
            subs.append((r0, k0, k1, qs, ks))
        work.append((fwd, vb, q_in, k_st, dec, subs, st_ref, o_ref))
    res = []
    for fwd, vb, q_in, k_st, dec, subs, st_ref, o_ref in work:
        for h in range(NH):
            sl = slice(h * W, (h + 1) * W)
            st = st_ref[h]
            o_inter = _dot_nt(q_in[:, sl], st.astype(BF16))
            st_ref[h] = st * dec[:, sl] + _dot_tn(vb[:, sl], k_st[:, sl])
            scores = [_dot_nt(qs[:, sl], ks[:, sl]) for (r0, k0, k1, qs, ks) in subs]
            res.append((fwd, h, o_inter, scores, vb, subs, o_ref))
    masked = []
    for fwd, h, o_inter, scores, vb, subs, o_ref in res:
        ms = []
        for s, (r0, k0, k1, qs, ks) in zip(scores, subs):
            t_idx = r0 + lax.broadcasted_iota(jnp.int32, s.shape, 0)
            s_idx = k0 + lax.broadcasted_iota(jnp.int32, s.shape, 1)
            keep = (s_idx <= t_idx) if fwd else (s_idx >= t_idx)
            ms.append(jnp.where(keep, s, 0.0).astype(BF16))
        masked.append(ms)
    for (fwd, h, o_inter, scores, vb, subs, o_ref), ms in zip(res, masked):
        sl = slice(h * W, (h + 1) * W)
        outs = [jnp.dot(m, vb[k0:k1, sl], preferred_element_type=F32) for m, (r0, k0, k1, qs, ks) in zip(ms, subs)]
        o_ref[:, sl] = o_inter + jnp.concatenate(outs, axis=0)


def hgrn_scan(proj, params, seq):
    T = proj.shape[0]
    n_seq = T // seq
    C = HGRN_CHUNK
    nc = seq // C
    w = HGRN_HEADS_PER_STEP * HGRN_DIM
    n_hh = HGRN_WIDTH // w

    assert n_hh == 1

    def fwd_spec(col):
        return _col_window(C, w, col, lambda b, hh, j: b * nc + j)

    def bwd_spec(col):
        return _col_window(C, w, col, lambda b, hh, j: b * nc + nc - 1 - j)

    return pl.pallas_call(
        _hgrn_body,
        grid=(n_seq, n_hh, nc),
        in_specs=[
            fwd_spec(COL_HQ), fwd_spec(COL_ZF), fwd_spec(COL_HI),
            bwd_spec(COL_HQ), bwd_spec(COL_ZB), bwd_spec(COL_HI),
            pl.BlockSpec((8, w), lambda b, hh, j: (0, hh)),
        ],
        out_specs=[
            pl.BlockSpec((C, w), lambda b, hh, j: (b * nc + j, hh)),
            pl.BlockSpec((C, w), lambda b, hh, j: (b * nc + nc - 1 - j, hh)),
        ],
        out_shape=[jax.ShapeDtypeStruct((T, HGRN_WIDTH), F32), jax.ShapeDtypeStruct((T, HGRN_WIDTH), F32)],
        scratch_shapes=[
            pltpu.VMEM((HGRN_HEADS_PER_STEP, HGRN_DIM, HGRN_DIM), F32),
            pltpu.VMEM((HGRN_HEADS_PER_STEP, HGRN_DIM, HGRN_DIM), F32),
        ],
        compiler_params=_cparams(3, 32),
        name="hgrn_scan",
    )(proj, proj, proj, proj, proj, proj, params)


def hgrn_params(lower_bounds, layer):
    lb_all = jnp.cumsum(jax.nn.softmax(lower_bounds.astype(F32), axis=1), axis=1)
    lb_all = lb_all - lb_all[:, :1]
    rows = []
    for d in range(2):
        lb = lb_all[d, layer]
        rows += [jnp.log(lb), jnp.log1p(-lb), 1.0 - lb]
    rows += [jnp.zeros_like(rows[0])] * 2
    return jnp.stack(rows, axis=0)


def _merge_body(attn_ref, of_ref, ob_ref, hog_ref, ga_ref, gb_ref, hg_ref, wa_ref, wh_ref, o_ref):
    ya = jnp.dot(attn_ref[...], wa_ref[...], preferred_element_type=F32)
    hs = []
    for h in range(N_HGRN_HEADS):
        sl = slice(h * HGRN_DIM, (h + 1) * HGRN_DIM)
        o = _rms(of_ref[:, sl] + ob_ref[:, sl], hg_ref[...])
        og = hog_ref[:, sl]
        hs.append((o * (og * _sigmoid(og))).astype(BF16))
    yh = jnp.dot(jnp.concatenate(hs, axis=1), wh_ref[...], preferred_element_type=F32)
    o_ref[...] = (_sigmoid(ga_ref[...]) * ya + _sigmoid(gb_ref[...]) * yh).astype(o_ref.dtype)


def merge_branches(attn, o_f, o_b, proj, hgrn_gain, w_attn, w_hgrn, layer, tm=256):
    T = attn.shape[0]
    D = D_MODEL
    return pl.pallas_call(
        _merge_body,
        grid=(T // tm,),
        in_specs=[
            pl.BlockSpec((tm, ATTN_WIDTH), lambda i: (i, 0)),
            pl.BlockSpec((tm, HGRN_WIDTH), lambda i: (i, 0)),
            pl.BlockSpec((tm, HGRN_WIDTH), lambda i: (i, 0)),
            _col_window(tm, HGRN_WIDTH, COL_HOG, lambda i: i),
            _col_window(tm, D, COL_GA, lambda i: i),
            _col_window(tm, D, COL_GB, lambda i: i),
            pl.BlockSpec((1, HGRN_DIM), lambda i: (0, 0)),
            pl.BlockSpec((None, ATTN_WIDTH, D), lambda i: (layer, 0, 0)),
            pl.BlockSpec((None, HGRN_WIDTH, D), lambda i: (layer, 0, 0)),
        ],
        out_specs=pl.BlockSpec((tm, D), lambda i: (i, 0)),
        out_shape=jax.ShapeDtypeStruct((T, D), BF16),
        compiler_params=_cparams(1, 56),
        name="merge_branches",
    )(attn, o_f, o_b, proj, proj, proj, hgrn_gain.reshape(1, HGRN_DIM), w_attn, w_hgrn)


def _outproj_body(m_ref, w_ref, xa_ref, xb_ref, g_ref, wr_ref, xo_ref, h_ref, aff_ref, *, n_first):
    xn = _pick_rows(xa_ref, xb_ref, n_first) + jnp.dot(m_ref[...], w_ref[...], preferred_element_type=F32)
    xo_ref[...] = xn
    h = _rms(xn, g_ref[...])
    h_ref[...] = h.astype(h_ref.dtype)
    hb = h.astype(BF16)
    hl = (h - hb.astype(F32)).astype(BF16)
    logits = (jnp.dot(hb, wr_ref[0], preferred_element_type=F32) + jnp.dot(hl, wr_ref[0], preferred_element_type=F32)
              + jnp.dot(hb, wr_ref[1], preferred_element_type=F32))
    mx = jnp.max(logits, axis=1, keepdims=True)
    ex = jnp.exp(logits - mx)
    aff_ref[...] = ex / jnp.sum(ex, axis=1, keepdims=True)


def out_proj_router(merged, w_out, layer, xa, xb, ffn_gain, w_router, tm=256):
    D = xa.shape[1]
    n_first = xa.shape[0] // tm
    xb = xa if xb is None else xb
    T = merged.shape[0]
    wr_hi = w_router.astype(BF16)
    wr_lo = (w_router - wr_hi.astype(F32)).astype(BF16)
    wr = jnp.stack([wr_hi, wr_lo], axis=0)
    return pl.pallas_call(
        functools.partial(_outproj_body, n_first=n_first),
        grid=(T // tm,),
        in_specs=[
            pl.BlockSpec((tm, D), lambda i: (i, 0)),
            pl.BlockSpec((None, D, D), lambda i: (layer, 0, 0)),
            *_split_rows_specs(tm, D, n_first),
            pl.BlockSpec((1, D), lambda i: (0, 0)),
            pl.BlockSpec((2, D, N_EXPERTS), lambda i: (0, 0, 0)),
        ],
        out_specs=[
            pl.BlockSpec((tm, D), lambda i: (i, 0)),
            pl.BlockSpec((tm, D), lambda i: (i, 0)),
            pl.BlockSpec((tm, N_EXPERTS), lambda i: (i, 0)),
        ],
        out_shape=[
            jax.ShapeDtypeStruct((T, D), F32),
            jax.ShapeDtypeStruct((T, D), BF16),
            jax.ShapeDtypeStruct((T, N_EXPERTS), F32),
        ],
        compiler_params=_cparams(1, 48),
        name="out_proj_router",
    )(merged, w_out, xa, xb, ffn_gain.reshape(1, D), wr)


ROUTE_LANES = 128
NOT_SELECTED = 2 ** 30


def _route_body(a_ref, key_ref, rowpref_ref, *, cap, base_tok):
    a = a_ref[...]
    E, nc, L = a.shape
    bits = pltpu.bitcast(a, jnp.int32)

    def count(mask):
        ones = jnp.where(mask, 1.0, 0.0)
        return jnp.sum(jnp.sum(ones, axis=1, keepdims=True), axis=2, keepdims=True)

    def bisect(_, carry):
        lo, hi = carry
        mid = lo + jnp.right_shift(hi - lo, 1)
        ge = count(bits >= mid) >= cap
        return jnp.where(ge, mid, lo), jnp.where(ge, hi, mid)

    lo0 = jnp.zeros((E, 1, 1), jnp.int32)
    hi0 = jnp.full((E, 1, 1), 0x7F800000, jnp.int32)
    thr, _ = lax.fori_loop(0, 31, bisect, (lo0, hi0))

    s_i = lax.broadcasted_iota(jnp.int32, (L, L), 0)
    t_i = lax.broadcasted_iota(jnp.int32, (L, L), 1)
    incl_lane = jnp.where(s_i <= t_i, 1.0, 0.0).astype(BF16)
    r_i = lax.broadcasted_iota(jnp.int32, (E, nc, nc), 1)
    c_i = lax.broadcasted_iota(jnp.int32, (E, nc, nc), 2)
    rows_before = jnp.where(c_i < r_i, 1.0, 0.0).astype(BF16)

    def prefix(mask):
        ones = jnp.where(mask, 1.0, 0.0)
        incl = jnp.dot(ones.astype(BF16).reshape(E * nc, L), incl_lane, preferred_element_type=F32).reshape(E, nc, L)
        row_tot = jnp.broadcast_to(incl[:, :, L - 1:L], (E, nc, L)).astype(BF16)
        before = lax.dot_general(rows_before, row_tot, (((2,), (1,)), ((0,), (0,))), preferred_element_type=F32)
        return ones, incl, before

    gt = bits > thr
    eq = bits == thr
    need_eq = cap - count(gt)
    eq_f, eq_incl, eq_before = prefix(eq)
    take_eq = jnp.where(eq_before + eq_incl - eq_f < need_eq, eq_f, 0.0)
    sel = jnp.where(gt, 1.0, take_eq) > 0.5
    _, _, sel_before = prefix(sel)
    tok = (base_tok + lax.broadcasted_iota(jnp.int32, (E, nc, L), 1) * L
           + lax.broadcasted_iota(jnp.int32, (E, nc, L), 2))
    key_ref[...] = jnp.where(sel, tok, NOT_SELECTED)
    rowpref_ref[...] = sel_before.astype(jnp.int32)


def route_select(aff_group_t, cap, base_tok):
    E, n_tok = aff_group_t.shape
    nc = n_tok // ROUTE_LANES
    a3 = aff_group_t.reshape(E, nc, ROUTE_LANES)
    keys, rowpref = pl.pallas_call(
        functools.partial(_route_body, cap=cap, base_tok=base_tok),
        out_shape=[jax.ShapeDtypeStruct((E, nc, ROUTE_LANES), jnp.int32)] * 2,
        compiler_params=pltpu.CompilerParams(vmem_limit_bytes=48 * 1024 * 1024),
        name="route_select",
    )(a3)
    return keys.reshape(E, n_tok), rowpref[:, :, 0]


FFN_CAST_ROWS = 256


def _ffn_body(x_ref, gv_ref, meta_ref, wg_hbm, wu_hbm, wd_hbm, o_ref,
              wg_f32, wu_f32, wd_f32, wg_bf, wu_bf, wd_bf, sem_ref, *, layer):
    e = pl.program_id(0)
    n_experts = pl.num_programs(0)

    def weight_copies(expert):
        return (pltpu.make_async_copy(wg_hbm.at[layer, expert], wg_f32, sem_ref.at[0]),
                pltpu.make_async_copy(wu_hbm.at[layer, expert], wu_f32, sem_ref.at[1]),
                pltpu.make_async_copy(wd_hbm.at[layer, expert], wd_f32, sem_ref.at[2]))

    @pl.when(pl.program_id(1) == 0)
    def _():
        @pl.when(e == 0)
        def _():
            for c in weight_copies(e):
                c.start()

        for c in weight_copies(e):
            c.wait()
        for src, dst in ((wg_f32, wg_bf), (wu_f32, wu_bf), (wd_f32, wd_bf)):
            def cast_rows(r, carry, src=src, dst=dst):
                rows = pl.ds(pl.multiple_of(r * FFN_CAST_ROWS, FFN_CAST_ROWS), FFN_CAST_ROWS)
                dst[rows, :] = src[rows, :].astype(BF16)
                return carry
            lax.fori_loop(0, src.shape[0] // FFN_CAST_ROWS, cast_rows, 0)

        @pl.when(e + 1 < n_experts)
        def _():
            for c in weight_copies(e + 1):
                c.start()

    x = x_ref[0]
    g = jnp.dot(x, wg_bf[...], preferred_element_type=F32)
    u = jnp.dot(x, wu_bf[...], preferred_element_type=F32)
    a = (g * _sigmoid(g) * u).astype(BF16)
    y = jnp.dot(a, wd_bf[...], preferred_element_type=F32) * gv_ref[0]
    o_ref[0, :, :D_MODEL] = y.astype(o_ref.dtype)
    o_ref[0, :, D_MODEL:] = meta_ref[0]


def expert_ffn(xe, w_gate, w_up, w_down, layer, gate_vals, meta, tm=256):
    E, rows, D = xe.shape
    F = w_gate.shape[-1]
    return pl.pallas_call(
        functools.partial(_ffn_body, layer=layer),
        grid=(E, rows // tm),
        in_specs=[
            pl.BlockSpec((1, tm, D), lambda e, i: (e, i, 0)),
            pl.BlockSpec((1, tm, 1), lambda e, i: (e, i, 0)),
            pl.BlockSpec((1, tm, META_WIDTH), lambda e, i: (e, i, 0)),
            pl.BlockSpec(memory_space=pl.ANY),
            pl.BlockSpec(memory_space=pl.ANY),
            pl.BlockSpec(memory_space=pl.ANY),
        ],
        out_specs=pl.BlockSpec((1, tm, D + META_WIDTH), lambda e, i: (e, i, 0)),
        out_shape=jax.ShapeDtypeStruct((E, rows, D + META_WIDTH), BF16),
        scratch_shapes=[
            pltpu.VMEM((D, F), F32), pltpu.VMEM((D, F), F32), pltpu.VMEM((F, D), F32),
            pltpu.VMEM((D, F), BF16), pltpu.VMEM((D, F), BF16), pltpu.VMEM((F, D), BF16),
            pltpu.SemaphoreType.DMA((3,)),
        ],
        compiler_params=_cparams(2, 56),
        name="expert_ffn",
    )(xe, gate_vals.reshape(E, rows, 1), meta, w_gate, w_up, w_down)


def _combine_body(win_ref, minslot_ref, nr_ref, x_ref, *refs, base_tile, n_tiles):
    ye_win = refs[:N_EXPERTS]
    ye_hbm, o_ref, buf_ref, sem_ref = refs[N_EXPERTS:]
    j = pl.program_id(0)
    t0 = ((j + base_tile) * COMBINE_TM).astype(F32)

    def table_index(e, k):
        return e * (n_tiles * COMBINE_ROUNDS) + j * COMBINE_ROUNDS + k

    def placed(window, k):
        rows, toks = [], []
        for e in range(N_EXPERTS):
            w = window(e)
            meta = w[:, D_MODEL:].astype(F32)
            tok = meta[:, 0:1] * 128.0 + meta[:, 1:2] - t0
            flat = table_index(e, k)
            slot = win_ref[flat] * WIN_BLK + lax.broadcasted_iota(jnp.int32, (WIN_ROWS, 1), 0)
            toks.append(jnp.where(slot >= minslot_ref[flat], tok, -1.0))
            rows.append(w[:, :D_MODEL])
        tok_all = jnp.concatenate(toks, axis=0)
        lane_t = lax.broadcasted_iota(jnp.int32, (N_EXPERTS * WIN_ROWS, COMBINE_TM), 1).astype(F32)
        place = jnp.where(tok_all == lane_t, 1.0, 0.0).astype(BF16)
        return _dot_tn(place, jnp.concatenate(rows, axis=0))

    o_ref[...] = x_ref[...] + placed(lambda e: ye_win[e][...].reshape(WIN_ROWS, D_MODEL + META_WIDTH), 0)

    def window_copy(e, k):
        return pltpu.make_async_copy(
            ye_hbm.at[e, pl.ds(win_ref[table_index(e, k)], WIN_BLKS)], buf_ref.at[e], sem_ref.at[e])

    def extra_round(k, carry):
        for e in range(N_EXPERTS):
            window_copy(e, k).start()
        for e in range(N_EXPERTS):
            window_copy(e, k).wait()
        o_ref[...] += placed(lambda e: buf_ref[e].reshape(WIN_ROWS, D_MODEL + META_WIDTH), k)
        return carry

    lax.fori_loop(1, nr_ref[j], extra_round, 0)


def combine(x, ye_ext, win, minslot, n_rounds, base_tile):
    E, cap, dext = ye_ext.shape
    n_tiles = n_rounds.shape[0]
    ye4 = ye_ext.reshape(E, cap // WIN_BLK, WIN_BLK, dext)

    def ye_spec(e):
        return pl.BlockSpec(
            (pl.Element(1), pl.Element(WIN_BLKS), pl.Element(WIN_BLK), pl.Element(dext)),
            lambda j, win_r, ms_r, nr_r: (e, win_r[e * (n_tiles * COMBINE_ROUNDS) + j * COMBINE_ROUNDS], 0, 0))

    grid_spec = pltpu.PrefetchScalarGridSpec(
        num_scalar_prefetch=3,
        grid=(n_tiles,),
        in_specs=[pl.BlockSpec((COMBINE_TM, D_MODEL), lambda j, *_: (j + base_tile, 0))]
        + [ye_spec(e) for e in range(E)]
        + [pl.BlockSpec(memory_space=pl.ANY)],
        out_specs=pl.BlockSpec((COMBINE_TM, D_MODEL), lambda j, *_: (j + base_tile, 0)),
        scratch_shapes=[
            pltpu.VMEM((E, WIN_BLKS, WIN_BLK, dext), BF16),
            pltpu.SemaphoreType.DMA((E,)),
        ],
    )
    return pl.pallas_call(
        functools.partial(_combine_body, base_tile=base_tile, n_tiles=n_tiles),
        grid_spec=grid_spec,
        out_shape=jax.ShapeDtypeStruct(x.shape, x.dtype),
        input_output_aliases={3: 0},
        compiler_params=_cparams(1, 48),
        name="combine",
    )(win.reshape(-1), minslot.reshape(-1), n_rounds, x, *([ye4] * E), ye4)


def combine_tables(rowpref, cap, slot_base):
    E = rowpref.shape[0]
    end = slot_base + cap
    starts = slot_base + jnp.concatenate(
        [rowpref[:, ::COMBINE_TM // ROUTE_LANES], jnp.full((E, 1), cap, jnp.int32)], axis=1)
    lo, hi = starts[:, :-1], starts[:, 1:]
    a = (lo // WIN_BLK) * WIN_BLK
    need = jnp.where(hi > lo, -(-(hi - a) // WIN_ROWS), 0)
    k = jnp.arange(COMBINE_ROUNDS, dtype=jnp.int32)[None, None, :]
    kk = jnp.minimum(k, jnp.maximum(need[..., None] - 1, 0))
    win = jnp.minimum(a[..., None] + WIN_ROWS * kk, end - WIN_ROWS) // WIN_BLK
    minslot = jnp.where(k < need[..., None], a[..., None] + WIN_ROWS * k, end)
    return win.astype(jnp.int32), minslot.astype(jnp.int32), jnp.max(need, axis=0).astype(jnp.int32)


def kernel(x_prompt, x_sample, w_in, q_norm, k_norm, lower_bounds, hgrn_norm, w_proj_attn, w_proj_hgrn,
           w_out, norm_mix, norm_ffn, w_router, w_gate, w_up, w_down, norm_final):
    depth = w_in.shape[0]
    groups = (x_prompt, x_sample)
    seq = x_prompt.shape[1]
    group_tokens = [g.shape[0] * g.shape[1] for g in groups]
    xa, xb = (g.reshape(-1, D_MODEL) for g in groups)
    cos, sin = rope_tables(seq)
    w_in_bf, w_attn_bf, w_hgrn_bf, w_out_bf = (w.astype(BF16) for w in (w_in, w_proj_attn, w_proj_hgrn, w_out))

    for l in range(depth):
        h = rms_norm_rows2(xa, xb, norm_mix[l], BF16) if l == 0 else rms_norm_rows(x, norm_mix[l], BF16)
        proj = in_proj(h, w_in_bf, l, F32)
        q, k, v = qk_prep(proj, cos, sin, q_norm[l], k_norm[l], seq)
        attn = attention(q, k, v, seq)
        o_f, o_b = hgrn_scan(proj, hgrn_params(lower_bounds, l), seq)
        merged = merge_branches(attn, o_f, o_b, proj, hgrn_norm[l], w_attn_bf, w_hgrn_bf, l)
        x, h_ffn, aff = out_proj_router(merged, w_out_bf, l, *((xa, xb) if l == 0 else (x, None)),
                                        norm_ffn[l], w_router[l])
        aff_t = aff.T
        idx_parts, gate_parts, tables = [], [], []
        start = slot_base = 0
        for n_tok in group_tokens:
            cap = EC_CAPACITY * n_tok // N_EXPERTS
            aff_g = aff_t[:, start:start + n_tok]
            keys, rowpref = route_select(aff_g, cap, start)
            idx = jnp.sort(keys, axis=1)[:, :cap]
            idx_parts.append(idx)
            gate_parts.append(jnp.take_along_axis(aff_g, idx - start, axis=1))
            tables.append(combine_tables(rowpref, cap, slot_base) + (start // COMBINE_TM,))
            start += n_tok
            slot_base += cap
        idx = jnp.concatenate(idx_parts, axis=1)
        meta = jnp.concatenate(
            [(idx // 128)[..., None], (idx % 128)[..., None],
             jnp.zeros(idx.shape + (META_WIDTH - 2,), jnp.int32)], axis=-1).astype(BF16)
        ye = expert_ffn(h_ffn[idx], w_gate, w_up, w_down, l, jnp.concatenate(gate_parts, axis=1), meta)
        for win, minslot, n_rounds, base_tile in tables:
            x = combine(x, ye, win, minslot, n_rounds, base_tile)

    outs = []
    start = 0
    for g, n_tok in zip(groups, group_tokens):
        outs.append(rms_norm_rows(x, norm_final, F32, start, n_tok).reshape(g.shape))
        start += n_tok
    return tuple(outs)
```

```python
import functools

import jax
import jax.numpy as jnp
from jax import lax
from jax.experimental import pallas as pl
from jax.experimental.pallas import tpu as pltpu

F32 = jnp.float32
BF16 = jnp.bfloat16

D_MODEL = 2048
SEQ = 4096
GRID_W = 64
HEAD_DIM = 128
N_Q_HEADS = 8
N_KV_HEADS = 2
Q_GROUP = N_Q_HEADS // N_KV_HEADS
ATTN_WIDTH = N_Q_HEADS * HEAD_DIM
KV_WIDTH = N_KV_HEADS * HEAD_DIM
ROPE_THETA = 10000.0
ROPE_AXIS_PAIRS = HEAD_DIM // 4
N_HGRN_HEADS = 8
HGRN_DIM = 128
HGRN_WIDTH = N_HGRN_HEADS * HGRN_DIM
N_EXPERTS = 16
EC_CAPACITY = 2
EXPERT_FF = 1024
NORM_EPS = 1e-6
IN_WIDTH = 10752

QKV_WIDTH = ATTN_WIDTH + 2 * KV_WIDTH
COL_QKV = 0
COL_HQ = COL_QKV + QKV_WIDTH
COL_ZF = COL_HQ + HGRN_WIDTH
COL_ZB = COL_ZF + HGRN_WIDTH
COL_HI = COL_ZB + HGRN_WIDTH
COL_HOG = COL_HI + HGRN_WIDTH
COL_GA = COL_HOG + HGRN_WIDTH
COL_GB = COL_GA + D_MODEL

ATTN_TK = 1024

HGRN_CHUNK = 64
HGRN_SUB = 16
HGRN_HEADS_PER_STEP = 8
EXP_CLAMP = 80.0

META_WIDTH = 128
COMBINE_TM = 256
WIN_BLK = 16
WIN_BLKS = 4
WIN_ROWS = WIN_BLK * WIN_BLKS
COMBINE_ROUNDS = -(-(WIN_BLK - 1 + COMBINE_TM) // WIN_ROWS)

V7X_VMEM_BYTES = 64 * 1024 * 1024


def _cparams(n_grid, vmem_mb):
    assert vmem_mb * 1024 * 1024 < V7X_VMEM_BYTES
    return pltpu.CompilerParams(
        dimension_semantics=("arbitrary",) * n_grid,
        vmem_limit_bytes=vmem_mb * 1024 * 1024,
    )


def _sigmoid(x):
    return 1.0 / (1.0 + jnp.exp(-x))


def _rms(x, gain):
    return x * lax.rsqrt(jnp.mean(x * x, axis=-1, keepdims=True) + NORM_EPS) * gain


def _dot_nt(a, b):
    return lax.dot_general(a, b, (((1,), (1,)), ((), ())), preferred_element_type=F32)


def _dot_tn(a, b):
    return lax.dot_general(a, b, (((0,), (0,)), ((), ())), preferred_element_type=F32)


def _split_rows_specs(tm, D, n_first):
    return [pl.BlockSpec((tm, D), lambda i: (jnp.minimum(i, n_first - 1), 0)),
            pl.BlockSpec((tm, D), lambda i: (jnp.maximum(i - n_first, 0), 0))]


def _pick_rows(xa_ref, xb_ref, n_first):
    return jnp.where(pl.program_id(0) < n_first, xa_ref[...], xb_ref[...])


def _norm2_body(xa_ref, xb_ref, g_ref, o_ref, *, n_first):
    o_ref[...] = _rms(_pick_rows(xa_ref, xb_ref, n_first), g_ref[...]).astype(o_ref.dtype)


def rms_norm_rows2(xa, xb, gain, out_dtype, tm=512):
    D = xa.shape[1]
    T = xa.shape[0] + xb.shape[0]
    n_first = xa.shape[0] // tm
    return pl.pallas_call(
        functools.partial(_norm2_body, n_first=n_first),
        grid=(T // tm,),
        in_specs=_split_rows_specs(tm, D, n_first) + [pl.BlockSpec((1, D), lambda i: (0, 0))],
        out_specs=pl.BlockSpec((tm, D), lambda i: (i, 0)),
        out_shape=jax.ShapeDtypeStruct((T, D), out_dtype),
        compiler_params=_cparams(1, 32),
        name="rms_norm_rows2",
    )(xa, xb, gain.reshape(1, D))


def _mm_body(aa_ref, ab_ref, w_ref, o_ref, *, n_first):
    i = pl.program_id(1)

    @pl.when(i < n_first)
    def _():
        o_ref[...] = jnp.dot(aa_ref[...], w_ref[...], preferred_element_type=F32).astype(o_ref.dtype)

    @pl.when(i >= n_first)
    def _():
        o_ref[...] = jnp.dot(ab_ref[...], w_ref[...], preferred_element_type=F32).astype(o_ref.dtype)


def in_proj(aa, ab, w, layer, out_dtype, tm=512, tn=1536):
    K = aa.shape[1]
    n_first = aa.shape[0] // tm
    ab = aa if ab is None else ab
    M = aa.shape[0] if ab is aa else aa.shape[0] + ab.shape[0]
    N = w.shape[2]
    return pl.pallas_call(
        functools.partial(_mm_body, n_first=n_first),
        grid=(N // tn, M // tm),
        in_specs=[pl.BlockSpec((tm, K), lambda j, i: (jnp.minimum(i, n_first - 1), 0)),
                  pl.BlockSpec((tm, K), lambda j, i: (jnp.maximum(i - n_first, 0), 0)),
                  pl.BlockSpec((None, K, tn), lambda j, i: (layer, 0, j))],
        out_specs=pl.BlockSpec((tm, tn), lambda j, i: (i, j)),
        out_shape=jax.ShapeDtypeStruct((M, N), out_dtype),
        compiler_params=_cparams(2, 48),
        name="in_proj",
    )(aa, ab, w)


def _col_window(rows, width, col, row_block):
    return pl.BlockSpec((pl.Element(rows), pl.Element(width)), lambda *g: (row_block(*g) * rows, col))


def _qkprep_body(p_ref, cos_ref, sin_ref, qg_ref, kg_ref, q_ref, k_ref, v_ref, *, scale):
    cos = cos_ref[...]
    sin = sin_ref[...]
    lane = lax.broadcasted_iota(jnp.int32, cos.shape, 1)
    first = (lane & ROPE_AXIS_PAIRS) == 0

    def prep(x, g, s):
        y = _rms(x, g)
        partner = jnp.where(first, pltpu.roll(y, HEAD_DIM - ROPE_AXIS_PAIRS, 1), pltpu.roll(y, ROPE_AXIS_PAIRS, 1))
        return (y * cos + partner * sin) * s

    for h in range(N_Q_HEADS):
        sl = slice(h * HEAD_DIM, (h + 1) * HEAD_DIM)
        q_ref[:, sl] = prep(p_ref[:, sl], qg_ref[...], scale).astype(q_ref.dtype)
    for h in range(N_KV_HEADS):
        sl = slice(h * HEAD_DIM, (h + 1) * HEAD_DIM)
        src = slice(ATTN_WIDTH + h * HEAD_DIM, ATTN_WIDTH + (h + 1) * HEAD_DIM)
        k_ref[:, sl] = prep(p_ref[:, src], kg_ref[...], 1.0).astype(k_ref.dtype)
    v_ref[...] = p_ref[:, ATTN_WIDTH + KV_WIDTH:].astype(v_ref.dtype)


def qk_prep(proj, cos, sin, q_gain, k_gain, seq, tm=256):
    T = proj.shape[0]
    nblk = seq // tm
    return pl.pallas_call(
        functools.partial(_qkprep_body, scale=HEAD_DIM ** -0.5),
        grid=(T // tm,),
        in_specs=[
            pl.BlockSpec((tm, QKV_WIDTH), lambda i: (i, COL_QKV // QKV_WIDTH)),
            pl.BlockSpec((tm, HEAD_DIM), lambda i: (i % nblk, 0)),
            pl.BlockSpec((tm, HEAD_DIM), lambda i: (i % nblk, 0)),
            pl.BlockSpec((1, HEAD_DIM), lambda i: (0, 0)),
            pl.BlockSpec((1, HEAD_DIM), lambda i: (0, 0)),
        ],
        out_specs=[
            pl.BlockSpec((tm, ATTN_WIDTH), lambda i: (i, 0)),
            pl.BlockSpec((tm, KV_WIDTH), lambda i: (i, 0)),
            pl.BlockSpec((tm, KV_WIDTH), lambda i: (i, 0)),
        ],
        out_shape=[
            jax.ShapeDtypeStruct((T, ATTN_WIDTH), BF16),
            jax.ShapeDtypeStruct((T, KV_WIDTH), BF16),
            jax.ShapeDtypeStruct((T, KV_WIDTH), BF16),
        ],
        compiler_params=_cparams(1, 32),
        name="qk_prep",
    )(proj, cos, sin, q_gain.reshape(1, HEAD_DIM), k_gain.reshape(1, HEAD_DIM))


def rope_tables(seq):
    rows = seq // GRID_W
    row = jnp.repeat(jnp.arange(rows, dtype=F32), GRID_W)
    col = jnp.tile(jnp.arange(GRID_W, dtype=F32), rows)
    inv_freq = 1.0 / (ROPE_THETA ** (jnp.arange(ROPE_AXIS_PAIRS, dtype=F32) / ROPE_AXIS_PAIRS))
    ang_r = row[:, None] * inv_freq
    ang_c = col[:, None] * inv_freq
    cos = jnp.concatenate([jnp.cos(ang_r), jnp.cos(ang_r), jnp.cos(ang_c), jnp.cos(ang_c)], axis=1)
    sin = jnp.concatenate([-jnp.sin(ang_r), jnp.sin(ang_r), -jnp.sin(ang_c), jnp.sin(ang_c)], axis=1)
    return cos, sin


def _attn_body(q_ref, k_ref, v_ref, o_ref):
    n_chunks = k_ref.shape[0] // ATTN_TK
    sls = [slice(g * HEAD_DIM, (g + 1) * HEAD_DIM) for g in range(Q_GROUP)]
    qs = [q_ref[:, sl] for sl in sls]
    ms = ls = accs = None
    for c in range(n_chunks):
        k = k_ref[c * ATTN_TK:(c + 1) * ATTN_TK, :]
        v = v_ref[c * ATTN_TK:(c + 1) * ATTN_TK, :]
        ss = [_dot_nt(q, k) for q in qs]
        new_ms = [jnp.max(s, axis=-1, keepdims=True) for s in ss]
        if c > 0:
            new_ms = [jnp.maximum(m, cm) for m, cm in zip(ms, new_ms)]
        ps = [jnp.exp(s - m) for s, m in zip(ss, new_ms)]
        pvs = [jnp.dot(p.astype(BF16), v, preferred_element_type=F32) for p in ps]
        sums = [jnp.sum(p, axis=-1, keepdims=True) for p in ps]
        if c == 0:
            ls, accs = sums, pvs
        else:
            alphas = [jnp.exp(m - nm) for m, nm in zip(ms, new_ms)]
            ls = [l * a + s for l, a, s in zip(ls, alphas, sums)]
            accs = [acc * a + pv for acc, a, pv in zip(accs, alphas, pvs)]
        ms = new_ms
    for sl, acc, l in zip(sls, accs, ls):
        o_ref[:, sl] = (acc / l).astype(o_ref.dtype)


def attention(q, k, v, seq, tq=512):
    T = q.shape[0]
    n_seq = T // seq
    nq = seq // tq
    gw = Q_GROUP * HEAD_DIM
    return pl.pallas_call(
        _attn_body,
        grid=(n_seq, N_KV_HEADS, nq),
        in_specs=[
            pl.BlockSpec((tq, gw), lambda b, h, i: (b * nq + i, h)),
            pl.BlockSpec((seq, HEAD_DIM), lambda b, h, i: (b, h)),
            pl.BlockSpec((seq, HEAD_DIM), lambda b, h, i: (b, h)),
        ],
        out_specs=pl.BlockSpec((tq, gw), lambda b, h, i: (b * nq + i, h)),
        out_shape=jax.ShapeDtypeStruct((T, ATTN_WIDTH), BF16),
        compiler_params=_cparams(3, 48),
        name="attention",
    )(q, k, v)


def _split3(x):
    hi = x.astype(BF16)
    r1 = x - hi.astype(F32)
    mid = r1.astype(BF16)
    lo = (r1 - mid.astype(F32)).astype(BF16)
    return hi, mid, lo


def _hgrn_gates(z, loglb, log1mlb, omlb):
    e = jnp.exp(-jnp.abs(z))
    r = 1.0 / (1.0 + e)
    kk = omlb * jnp.where(z >= 0, e * r, r)
    cc = log1mlb + (jnp.minimum(z, 0.0) - jnp.log(1.0 + e))
    g = jnp.maximum(loglb, cc) + jnp.log(1.0 + jnp.exp(-jnp.abs(loglb - cc)))
    return g, kk


def _hgrn_body(qf_ref, zf_ref, vf_ref, qb_ref, zb_ref, vb_ref, par_ref, of_ref, ob_ref, stf_ref, stb_ref):
    C, SB, W = HGRN_CHUNK, HGRN_SUB, HGRN_DIM
    NH = HGRN_HEADS_PER_STEP

    @pl.when(pl.program_id(2) == 0)
    def _():
        stf_ref[...] = jnp.zeros(stf_ref.shape, F32)
        stb_ref[...] = jnp.zeros(stb_ref.shape, F32)

    row = lax.broadcasted_iota(jnp.int32, (C, C), 0)
    col = lax.broadcasted_iota(jnp.int32, (C, C), 1)
    dirs = (
        (True, qf_ref, zf_ref, vf_ref, 0, jnp.where(col <= row, 1.0, 0.0).astype(BF16), stf_ref, of_ref),
        (False, qb_ref, zb_ref, vb_ref, 3, jnp.where(col >= row, 1.0, 0.0).astype(BF16), stb_ref, ob_ref),
    )
    gs, kks = [], []
    for fwd, q_ref, z_ref, v_ref, p0, tri, st_ref, o_ref in dirs:
        g, kk = _hgrn_gates(z_ref[...], par_ref[p0:p0 + 1, :], par_ref[p0 + 1:p0 + 2, :], par_ref[p0 + 2:p0 + 3, :])
        gs.append(g)
        kks.append(kk)
    bs = []
    for (fwd, q_ref, z_ref, v_ref, p0, tri, st_ref, o_ref), g in zip(dirs, gs):
        hi, mid, lo = _split3(g)
        bs.append(jnp.dot(tri, hi, preferred_element_type=F32) + jnp.dot(tri, mid, preferred_element_type=F32)
                  + jnp.dot(tri, lo, preferred_element_type=F32))
    work = []
    for (fwd, q_ref, z_ref, v_ref, p0, tri, st_ref, o_ref), b, kk in zip(dirs, bs, kks):
        q = q_ref[...]
        vb = v_ref[...].astype(BF16)
        tot = b[C - 1:C, :] if fwd else b[0:1, :]
        q_in = (q * jnp.exp(b)).astype(BF16)
        k_st = (kk * jnp.exp(tot - b)).astype(BF16)
        dec = jnp.exp(tot)
        subs = []
        for i in range(C // SB):
            r0 = i * SB
            if fwd:
                ref = b[r0 - 1:r0, :] if i > 0 else jnp.zeros((1, NH * W), F32)
                k0, k1 = 0, r0 + SB
            else:
                ref = b[r0 + SB:r0 + SB + 1, :] if r0 + SB < C else jnp.zeros((1, NH * W), F32)
                k0, k1 = r0, C
            qs = (q[r0:r0 + SB] * jnp.exp(b[r0:r0 + SB] - ref)).astype(BF16)
            ks = (kk[k0:k1] * jnp.exp(jnp.minimum(ref - b[k0:k1], EXP_CLAMP))).astype(BF16)
            subs.append((r0, k0, k1, qs, ks))
        work.append((fwd, vb, q_in, k_st, dec, subs, st_ref, o_ref))
    res = []
    for fwd, vb, q_in, k_st, dec, subs, st_ref, o_ref in work:
        for h in range(NH):
            sl = slice(h * W, (h + 1) * W)
            st = st_ref[h]
            o_inter = _dot_nt(q_in[:, sl], st.astype(BF16))
            st_ref[h] = st * dec[:, sl] + _dot_tn(vb[:, sl], k_st[:, sl])
            scores = [_dot_nt(qs[:, sl], ks[:, sl]) for (r0, k0, k1, qs, ks) in subs]
            res.append((fwd, h, o_inter, scores, vb, subs, o_ref))
    masked = []
    for fwd, h, o_inter, scores, vb, subs, o_ref in res:
        ms = []
        for s, (r0, k0, k1, qs, ks) in zip(scores, subs):
            t_idx = r0 + lax.broadcasted_iota(jnp.int32, s.shape, 0)
            s_idx = k0 + lax.broadcasted_iota(jnp.int32, s.shape, 1)
            keep = (s_idx <= t_idx) if fwd else (s_idx >= t_idx)
            ms.append(jnp.where(keep, s, 0.0).astype(BF16))
        masked.append(ms)
    for (fwd, h, o_inter, scores, vb, subs, o_ref), ms in zip(res, masked):
        sl = slice(h * W, (h + 1) * W)
        outs = [jnp.dot(m, vb[k0:k1, sl], preferred_element_type=F32) for m, (r0, k0, k1, qs, ks) in zip(ms, subs)]
        o_ref[:, sl] = o_inter + jnp.concatenate(outs, axis=0)


def hgrn_scan(proj, params, seq):
    T = proj.shape[0]
    n_seq = T // seq
    C = HGRN_CHUNK
    nc = seq // C
    w = HGRN_HEADS_PER_STEP * HGRN_DIM
    n_hh = HGRN_WIDTH // w

    assert n_hh == 1

    def fwd_spec(col):
        return _col_window(C, w, col, lambda b, hh, j: b * nc + j)

    def bwd_spec(col):
        return _col_window(C, w, col, lambda b, hh, j: b * nc + nc - 1 - j)

    return pl.pallas_call(
        _hgrn_body,
        grid=(n_seq, n_hh, nc),
        in_specs=[
            fwd_spec(COL_HQ), fwd_spec(COL_ZF), fwd_spec(COL_HI),
            bwd_spec(COL_HQ), bwd_spec(COL_ZB), bwd_spec(COL_HI),
            pl.BlockSpec((8, w), lambda b, hh, j: (0, hh)),
        ],
        out_specs=[
            pl.BlockSpec((C, w), lambda b, hh, j: (b * nc + j, hh)),
            pl.BlockSpec((C, w), lambda b, hh, j: (b * nc + nc - 1 - j, hh)),
        ],
        out_shape=[jax.ShapeDtypeStruct((T, HGRN_WIDTH), F32), jax.ShapeDtypeStruct((T, HGRN_WIDTH), F32)],
        scratch_shapes=[
            pltpu.VMEM((HGRN_HEADS_PER_STEP, HGRN_DIM, HGRN_DIM), F32),
            pltpu.VMEM((HGRN_HEADS_PER_STEP, HGRN_DIM, HGRN_DIM), F32),
        ],
        compiler_params=_cparams(3, 32),
        name="hgrn_scan",
    )(proj, proj, proj, proj, proj, proj, params)


def hgrn_params(lower_bounds, layer):
    lb_all = jnp.cumsum(jax.nn.softmax(lower_bounds.astype(F32), axis=1), axis=1)
    lb_all = lb_all - lb_all[:, :1]
    rows = []
    for d in range(2):
        lb = lb_all[d, layer]
        rows += [jnp.log(lb), jnp.log1p(-lb), 1.0 - lb]
    rows += [jnp.zeros_like(rows[0])] * 2
    return jnp.stack(rows, axis=0)


def _merge_body(attn_ref, of_ref, ob_ref, hog_ref, ga_ref, gb_ref, hg_ref, wa_ref, wh_ref, o_ref):
    ya = jnp.dot(attn_ref[...], wa_ref[...], preferred_element_type=F32)
    hs = []
    for h in range(N_HGRN_HEADS):
        sl = slice(h * HGRN_DIM, (h + 1) * HGRN_DIM)
        o = _rms(of_ref[:, sl] + ob_ref[:, sl], hg_ref[...])
        og = hog_ref[:, sl]
        hs.append((o * (og * _sigmoid(og))).astype(BF16))
    yh = jnp.dot(jnp.concatenate(hs, axis=1), wh_ref[...], preferred_element_type=F32)
    o_ref[...] = (_sigmoid(ga_ref[...]) * ya + _sigmoid(gb_ref[...]) * yh).astype(o_ref.dtype)


def merge_branches(attn, o_f, o_b, proj, hgrn_gain, w_attn, w_hgrn, layer, tm=256):
    T = attn.shape[0]
    D = D_MODEL
    return pl.pallas_call(
        _merge_body,
        grid=(T // tm,),
        in_specs=[
            pl.BlockSpec((tm, ATTN_WIDTH), lambda i: (i, 0)),
            pl.BlockSpec((tm, HGRN_WIDTH), lambda i: (i, 0)),
            pl.BlockSpec((tm, HGRN_WIDTH), lambda i: (i, 0)),
            _col_window(tm, HGRN_WIDTH, COL_HOG, lambda i: i),
            _col_window(tm, D, COL_GA, lambda i: i),
            _col_window(tm, D, COL_GB, lambda i: i),
            pl.BlockSpec((1, HGRN_DIM), lambda i: (0, 0)),
            pl.BlockSpec((None, ATTN_WIDTH, D), lambda i: (layer, 0, 0)),
            pl.BlockSpec((None, HGRN_WIDTH, D), lambda i: (layer, 0, 0)),
        ],
        out_specs=pl.BlockSpec((tm, D), lambda i: (i, 0)),
        out_shape=jax.ShapeDtypeStruct((T, D), BF16),
        compiler_params=_cparams(1, 56),
        name="merge_branches",
    )(attn, o_f, o_b, proj, proj, proj, hgrn_gain.reshape(1, HGRN_DIM), w_attn, w_hgrn)


def _outproj_body(m_ref, w_ref, xa_ref, xb_ref, g_ref, wr_ref, xo_ref, h_ref, aff_ref, *, n_first):
    xn = _pick_rows(xa_ref, xb_ref, n_first) + jnp.dot(m_ref[...], w_ref[...], preferred_element_type=F32)
    xo_ref[...] = xn
    h = _rms(xn, g_ref[...])
    h_ref[...] = h.astype(h_ref.dtype)
    hb = h.astype(BF16)
    hl = (h - hb.astype(F32)).astype(BF16)
    logits = (jnp.dot(hb, wr_ref[0], preferred_element_type=F32) + jnp.dot(hl, wr_ref[0], preferred_element_type=F32)
              + jnp.dot(hb, wr_ref[1], preferred_element_type=F32))
    mx = jnp.max(logits, axis=1, keepdims=True)
    ex = jnp.exp(logits - mx)
    aff_ref[...] = ex / jnp.sum(ex, axis=1, keepdims=True)


def out_proj_router(merged, w_out, layer, xa, xb, ffn_gain, w_router, tm=256):
    D = xa.shape[1]
    n_first = xa.shape[0] // tm
    xb = xa if xb is None else xb
    T = merged.shape[0]
    wr_hi = w_router.astype(BF16)
    wr_lo = (w_router - wr_hi.astype(F32)).astype(BF16)
    wr = jnp.stack([wr_hi, wr_lo], axis=0)
    return pl.pallas_call(
        functools.partial(_outproj_body, n_first=n_first),
        grid=(T // tm,),
        in_specs=[
            pl.BlockSpec((tm, D), lambda i: (i, 0)),
            pl.BlockSpec((None, D, D), lambda i: (layer, 0, 0)),
            *_split_rows_specs(tm, D, n_first),
            pl.BlockSpec((1, D), lambda i: (0, 0)),
            pl.BlockSpec((2, D, N_EXPERTS), lambda i: (0, 0, 0)),
        ],
        out_specs=[
            pl.BlockSpec((tm, D), lambda i: (i, 0)),
            pl.BlockSpec((tm, D), lambda i: (i, 0)),
            pl.BlockSpec((tm, N_EXPERTS), lambda i: (i, 0)),
        ],
        out_shape=[
            jax.ShapeDtypeStruct((T, D), F32),
            jax.ShapeDtypeStruct((T, D), BF16),
            jax.ShapeDtypeStruct((T, N_EXPERTS), F32),
        ],
        compiler_params=_cparams(1, 48),
        name="out_proj_router",
    )(merged, w_out, xa, xb, ffn_gain.reshape(1, D), wr)


ROUTE_LANES = 128
NOT_SELECTED = 2 ** 30


def _route_body(a_ref, key_ref, rowpref_ref, *, cap, base_tok):
    a = a_ref[...]
    E, nc, L = a.shape
    bits = pltpu.bitcast(a, jnp.int32)

    def count(mask):
        ones = jnp.where(mask, 1.0, 0.0)
        return jnp.sum(jnp.sum(ones, axis=1, keepdims=True), axis=2, keepdims=True)

    def bisect(_, carry):
        lo, hi = carry
        mid = lo + jnp.right_shift(hi - lo, 1)
        ge = count(bits >= mid) >= cap
        return jnp.where(ge, mid, lo), jnp.where(ge, hi, mid)

    lo0 = jnp.zeros((E, 1, 1), jnp.int32)
    hi0 = jnp.full((E, 1, 1), 0x7F800000, jnp.int32)
    thr, _ = lax.fori_loop(0, 31, bisect, (lo0, hi0))

    s_i = lax.broadcasted_iota(jnp.int32, (L, L), 0)
    t_i = lax.broadcasted_iota(jnp.int32, (L, L), 1)
    incl_lane = jnp.where(s_i <= t_i, 1.0, 0.0).astype(BF16)
    r_i = lax.broadcasted_iota(jnp.int32, (E, nc, nc), 1)
    c_i = lax.broadcasted_iota(jnp.int32, (E, nc, nc), 2)
    rows_before = jnp.where(c_i < r_i, 1.0, 0.0).astype(BF16)

    def prefix(mask):
        ones = jnp.where(mask, 1.0, 0.0)
        incl = jnp.dot(ones.astype(BF16).reshape(E * nc, L), incl_lane, preferred_element_type=F32).reshape(E, nc, L)
        row_tot = jnp.broadcast_to(incl[:, :, L - 1:L], (E, nc, L)).astype(BF16)
        before = lax.dot_general(rows_before, row_tot, (((2,), (1,)), ((0,), (0,))), preferred_element_type=F32)
        return ones, incl, before

    gt = bits > thr
    eq = bits == thr
    need_eq = cap - count(gt)
    eq_f, eq_incl, eq_before = prefix(eq)
    take_eq = jnp.where(eq_before + eq_incl - eq_f < need_eq, eq_f, 0.0)
    sel = jnp.where(gt, 1.0, take_eq) > 0.5
    _, _, sel_before = prefix(sel)
    tok = (base_tok + lax.broadcasted_iota(jnp.int32, (E, nc, L), 1) * L
           + lax.broadcasted_iota(jnp.int32, (E, nc, L), 2))
    key_ref[...] = jnp.where(sel, tok, NOT_SELECTED)
    rowpref_ref[...] = sel_before.astype(jnp.int32)


def route_select(aff_group_t, cap, base_tok):
    E, n_tok = aff_group_t.shape
    nc = n_tok // ROUTE_LANES
    a3 = aff_group_t.reshape(E, nc, ROUTE_LANES)
    keys, rowpref = pl.pallas_call(
        functools.partial(_route_body, cap=cap, base_tok=base_tok),
        out_shape=[jax.ShapeDtypeStruct((E, nc, ROUTE_LANES), jnp.int32)] * 2,
        compiler_params=pltpu.CompilerParams(vmem_limit_bytes=48 * 1024 * 1024),
        name="route_select",
    )(a3)
    return keys.reshape(E, n_tok), rowpref[:, :, 0]


FFN_CAST_ROWS = 256


def _ffn_body(x_ref, gv_ref, meta_ref, wg_hbm, wu_hbm, wd_hbm, o_ref,
              wg_f32, wu_f32, wd_f32, wg_bf, wu_bf, wd_bf, sem_ref, *, layer):
    e = pl.program_id(0)
    n_experts = pl.num_programs(0)

    def weight_copies(expert):
        return (pltpu.make_async_copy(wg_hbm.at[layer, expert], wg_f32, sem_ref.at[0]),
                pltpu.make_async_copy(wu_hbm.at[layer, expert], wu_f32, sem_ref.at[1]),
                pltpu.make_async_copy(wd_hbm.at[layer, expert], wd_f32, sem_ref.at[2]))

    @pl.when(pl.program_id(1) == 0)
    def _():
        @pl.when(e == 0)
        def _():
            for c in weight_copies(e):
                c.start()

        for c in weight_copies(e):
            c.wait()
        for src, dst in ((wg_f32, wg_bf), (wu_f32, wu_bf), (wd_f32, wd_bf)):
            def cast_rows(r, carry, src=src, dst=dst):
                rows = pl.ds(pl.multiple_of(r * FFN_CAST_ROWS, FFN_CAST_ROWS), FFN_CAST_ROWS)
                dst[rows, :] = src[rows, :].astype(BF16)
                return carry
            lax.fori_loop(0, src.shape[0] // FFN_CAST_ROWS, cast_rows, 0)

        @pl.when(e + 1 < n_experts)
        def _():
            for c in weight_copies(e + 1):
                c.start()

    x = x_ref[0]
    g = jnp.dot(x, wg_bf[...], preferred_element_type=F32)
    u = jnp.dot(x, wu_bf[...], preferred_element_type=F32)
    a = (g * _sigmoid(g) * u).astype(BF16)
    y = jnp.dot(a, wd_bf[...], preferred_element_type=F32) * gv_ref[0]
    o_ref[0, :, :D_MODEL] = y.astype(o_ref.dtype)
    o_ref[0, :, D_MODEL:] = meta_ref[0]


def expert_ffn(xe, w_gate, w_up, w_down, layer, gate_vals, meta, tm=256):
    E, rows, D = xe.shape
    F = w_gate.shape[-1]
    return pl.pallas_call(
        functools.partial(_ffn_body, layer=layer),
        grid=(E, rows // tm),
        in_specs=[
            pl.BlockSpec((1, tm, D), lambda e, i: (e, i, 0)),
            pl.BlockSpec((1, tm, 1), lambda e, i: (e, i, 0)),
            pl.BlockSpec((1, tm, META_WIDTH), lambda e, i: (e, i, 0)),
            pl.BlockSpec(memory_space=pl.ANY),
            pl.BlockSpec(memory_space=pl.ANY),
            pl.BlockSpec(memory_space=pl.ANY),
        ],
        out_specs=pl.BlockSpec((1, tm, D + META_WIDTH), lambda e, i: (e, i, 0)),
        out_shape=jax.ShapeDtypeStruct((E, rows, D + META_WIDTH), BF16),
        scratch_shapes=[
            pltpu.VMEM((D, F), F32), pltpu.VMEM((D, F), F32), pltpu.VMEM((F, D), F32),
            pltpu.VMEM((D, F), BF16), pltpu.VMEM((D, F), BF16), pltpu.VMEM((F, D), BF16),
            pltpu.SemaphoreType.DMA((3,)),
        ],
        compiler_params=_cparams(2, 56),
        name="expert_ffn",
    )(xe, gate_vals.reshape(E, rows, 1), meta, w_gate, w_up, w_down)


def _combine_body(win_ref, minslot_ref, nr_ref, x_ref, gain_ref, *refs, base_tile, n_tiles):
    ye_win = refs[:N_EXPERTS]
    ye_hbm, o_ref, h_ref, buf_ref, sem_ref = refs[N_EXPERTS:]
    j = pl.program_id(0)
    t0 = ((j + base_tile) * COMBINE_TM).astype(F32)

    def table_index(e, k):
        return e * (n_tiles * COMBINE_ROUNDS) + j * COMBINE_ROUNDS + k

    def placed(window, k):
        rows, toks = [], []
        for e in range(N_EXPERTS):
            w = window(e)
            meta = w[:, D_MODEL:].astype(F32)
            tok = meta[:, 0:1] * 128.0 + meta[:, 1:2] - t0
            flat = table_index(e, k)
            slot = win_ref[flat] * WIN_BLK + lax.broadcasted_iota(jnp.int32, (WIN_ROWS, 1), 0)
            toks.append(jnp.where(slot >= minslot_ref[flat], tok, -1.0))
            rows.append(w[:, :D_MODEL])
        tok_all = jnp.concatenate(toks, axis=0)
        lane_t = lax.broadcasted_iota(jnp.int32, (N_EXPERTS * WIN_ROWS, COMBINE_TM), 1).astype(F32)
        place = jnp.where(tok_all == lane_t, 1.0, 0.0).astype(BF16)
        return _dot_tn(place, jnp.concatenate(rows, axis=0))

    o_ref[...] = x_ref[...] + placed(lambda e: ye_win[e][...].reshape(WIN_ROWS, D_MODEL + META_WIDTH), 0)

    def window_copy(e, k):
        return pltpu.make_async_copy(
            ye_hbm.at[e, pl.ds(win_ref[table_index(e, k)], WIN_BLKS)], buf_ref.at[e], sem_ref.at[e])

    def extra_round(k, carry):
        for e in range(N_EXPERTS):
            window_copy(e, k).start()
        for e in range(N_EXPERTS):
            window_copy(e, k).wait()
        o_ref[...] += placed(lambda e: buf_ref[e].reshape(WIN_ROWS, D_MODEL + META_WIDTH), k)
        return carry

    lax.fori_loop(1, nr_ref[j], extra_round, 0)
    h_ref[...] = _rms(o_ref[...], gain_ref[...]).astype(h_ref.dtype)


def combine(x, ye_ext, win, minslot, n_rounds, base_tile, gain, h_dtype):
    E, cap, dext = ye_ext.shape
    n_tiles = n_rounds.shape[0]
    ye4 = ye_ext.reshape(E, cap // WIN_BLK, WIN_BLK, dext)

    def ye_spec(e):
        return pl.BlockSpec(
            (pl.Element(1), pl.Element(WIN_BLKS), pl.Element(WIN_BLK), pl.Element(dext)),
            lambda j, win_r, ms_r, nr_r: (e, win_r[e * (n_tiles * COMBINE_ROUNDS) + j * COMBINE_ROUNDS], 0, 0))

    grid_spec = pltpu.PrefetchScalarGridSpec(
        num_scalar_prefetch=3,
        grid=(n_tiles,),
        in_specs=[pl.BlockSpec((COMBINE_TM, D_MODEL), lambda j, *_: (j + base_tile, 0)),
                  pl.BlockSpec((1, D_MODEL), lambda j, *_: (0, 0))]
        + [ye_spec(e) for e in range(E)]
        + [pl.BlockSpec(memory_space=pl.ANY)],
        out_specs=[pl.BlockSpec((COMBINE_TM, D_MODEL), lambda j, *_: (j + base_tile, 0)),
                   pl.BlockSpec((COMBINE_TM, D_MODEL), lambda j, *_: (j, 0))],
        scratch_shapes=[
            pltpu.VMEM((E, WIN_BLKS, WIN_BLK, dext), BF16),
            pltpu.SemaphoreType.DMA((E,)),
        ],
    )
    return pl.pallas_call(
        functools.partial(_combine_body, base_tile=base_tile, n_tiles=n_tiles),
        grid_spec=grid_spec,
        out_shape=[jax.ShapeDtypeStruct(x.shape, x.dtype),
                   jax.ShapeDtypeStruct((n_tiles * COMBINE_TM, D_MODEL), h_dtype)],
        input_output_aliases={3: 0},
        compiler_params=_cparams(1, 48),
        name="combine",
    )(win.reshape(-1), minslot.reshape(-1), n_rounds, x, gain.reshape(1, D_MODEL), *([ye4] * E), ye4)


def combine_tables(rowpref, cap, slot_base):
    E = rowpref.shape[0]
    end = slot_base + cap
    starts = slot_base + jnp.concatenate(
        [rowpref[:, ::COMBINE_TM // ROUTE_LANES], jnp.full((E, 1), cap, jnp.int32)], axis=1)
    lo, hi = starts[:, :-1], starts[:, 1:]
    a = (lo // WIN_BLK) * WIN_BLK
    need = jnp.where(hi > lo, -(-(hi - a) // WIN_ROWS), 0)
    k = jnp.arange(COMBINE_ROUNDS, dtype=jnp.int32)[None, None, :]
    kk = jnp.minimum(k, jnp.maximum(need[..., None] - 1, 0))
    win = jnp.minimum(a[..., None] + WIN_ROWS * kk, end - WIN_ROWS) // WIN_BLK
    minslot = jnp.where(k < need[..., None], a[..., None] + WIN_ROWS * k, end)
    return win.astype(jnp.int32), minslot.astype(jnp.int32), jnp.max(need, axis=0).astype(jnp.int32)


def kernel(x_prompt, x_sample, w_in, q_norm, k_norm, lower_bounds, hgrn_norm, w_proj_attn, w_proj_hgrn,
           w_out, norm_mix, norm_ffn, w_router, w_gate, w_up, w_down, norm_final):
    depth = w_in.shape[0]
    groups = (x_prompt, x_sample)
    seq = x_prompt.shape[1]
    group_tokens = [g.shape[0] * g.shape[1] for g in groups]
    xa, xb = (g.reshape(-1, D_MODEL) for g in groups)
    cos, sin = rope_tables(seq)
    w_in_bf, w_attn_bf, w_hgrn_bf, w_out_bf = (w.astype(BF16) for w in (w_in, w_proj_attn, w_proj_hgrn, w_out))

    h_parts = (rms_norm_rows2(xa, xb, norm_mix[0], BF16), None)
    for l in range(depth):
        proj = in_proj(*h_parts, w_in_bf, l, F32)
        q, k, v = qk_prep(proj, cos, sin, q_norm[l], k_norm[l], seq)
        attn = attention(q, k, v, seq)
        o_f, o_b = hgrn_scan(proj, hgrn_params(lower_bounds, l), seq)
        merged = merge_branches(attn, o_f, o_b, proj, hgrn_norm[l], w_attn_bf, w_hgrn_bf, l)
        x, h_ffn, aff = out_proj_router(merged, w_out_bf, l, *((xa, xb) if l == 0 else (x, None)),
                                        norm_ffn[l], w_router[l])
        aff_t = aff.T
        idx_parts, gate_parts, tables = [], [], []
        start = slot_base = 0
        for n_tok in group_tokens:
            cap = EC_CAPACITY * n_tok // N_EXPERTS
            aff_g = aff_t[:, start:start + n_tok]
            keys, rowpref = route_select(aff_g, cap, start)
            idx = jnp.sort(keys, axis=1)[:, :cap]
            idx_parts.append(idx)
            gate_parts.append(jnp.take_along_axis(aff_g, idx - start, axis=1))
            tables.append(combine_tables(rowpref, cap, slot_base) + (start // COMBINE_TM,))
            start += n_tok
            slot_base += cap
        idx = jnp.concatenate(idx_parts, axis=1)
        meta = jnp.concatenate(
            [(idx // 128)[..., None], (idx % 128)[..., None],
             jnp.zeros(idx.shape + (META_WIDTH - 2,), jnp.int32)], axis=-1).astype(BF16)
        ye = expert_ffn(h_ffn[idx], w_gate, w_up, w_down, l, jnp.concatenate(gate_parts, axis=1), meta)
        last = l == depth - 1
        h_parts = []
        for win, minslot, n_rounds, base_tile in tables:
            x, h_g = combine(x, ye, win, minslot, n_rounds, base_tile,
                             norm_final if last else norm_mix[l + 1], F32 if last else BF16)
            h_parts.append(h_g)

    return tuple(h_g.reshape(g.shape) for h_g, g in zip(h_parts, groups))
```

```python
import functools

import jax
import jax.numpy as jnp
from jax import lax
from jax.experimental import pallas as pl
from jax.experimental.pallas import tpu as pltpu

F32 = jnp.float32
BF16 = jnp.bfloat16

D_MODEL = 2048
SEQ = 4096
GRID_W = 64
HEAD_DIM = 128
N_Q_HEADS = 8
N_KV_HEADS = 2
Q_GROUP = N_Q_HEADS // N_KV_HEADS
ATTN_WIDTH = N_Q_HEADS * HEAD_DIM
KV_WIDTH = N_KV_HEADS * HEAD_DIM
ROPE_THETA = 10000.0
ROPE_AXIS_PAIRS = HEAD_DIM // 4
N_HGRN_HEADS = 8
HGRN_DIM = 128
HGRN_WIDTH = N_HGRN_HEADS * HGRN_DIM
N_EXPERTS = 16
EC_CAPACITY = 2
EXPERT_FF = 1024
NORM_EPS = 1e-6
IN_WIDTH = 10752

QKV_WIDTH = ATTN_WIDTH + 2 * KV_WIDTH
COL_QKV = 0
COL_HQ = COL_QKV + QKV_WIDTH
COL_ZF = COL_HQ + HGRN_WIDTH
COL_ZB = COL_ZF + HGRN_WIDTH
COL_HI = COL_ZB + HGRN_WIDTH
COL_HOG = COL_HI + HGRN_WIDTH
COL_GA = COL_HOG + HGRN_WIDTH
COL_GB = COL_GA + D_MODEL

ATTN_TK = 1024

HGRN_CHUNK = 64
HGRN_SUB = 16
HGRN_HEADS_PER_STEP = 8
EXP_CLAMP = 80.0

META_WIDTH = 128
COMBINE_TM = 256
WIN_BLK = 16
WIN_BLKS = 4
WIN_ROWS = WIN_BLK * WIN_BLKS
COMBINE_ROUNDS = -(-(WIN_BLK - 1 + COMBINE_TM) // WIN_ROWS)

V7X_VMEM_BYTES = 64 * 1024 * 1024


def _cparams(n_grid, vmem_mb):
    assert vmem_mb * 1024 * 1024 < V7X_VMEM_BYTES
    return pltpu.CompilerParams(
        dimension_semantics=("arbitrary",) * n_grid,
        vmem_limit_bytes=vmem_mb * 1024 * 1024,
    )


def _sigmoid(x):
    return 1.0 / (1.0 + jnp.exp(-x))


def _rms(x, gain):
    return x * lax.rsqrt(jnp.mean(x * x, axis=-1, keepdims=True) + NORM_EPS) * gain


def _dot_nt(a, b):
    return lax.dot_general(a, b, (((1,), (1,)), ((), ())), preferred_element_type=F32)


def _dot_tn(a, b):
    return lax.dot_general(a, b, (((0,), (0,)), ((), ())), preferred_element_type=F32)


def _split_rows_specs(tm, D, n_first):
    return [pl.BlockSpec((tm, D), lambda i: (jnp.minimum(i, n_first - 1), 0)),
            pl.BlockSpec((tm, D), lambda i: (jnp.maximum(i - n_first, 0), 0))]


def _pick_rows(xa_ref, xb_ref, n_first):
    return jnp.where(pl.program_id(0) < n_first, xa_ref[...], xb_ref[...])


def _norm2_body(xa_ref, xb_ref, g_ref, o_ref, *, n_first):
    o_ref[...] = _rms(_pick_rows(xa_ref, xb_ref, n_first), g_ref[...]).astype(o_ref.dtype)


def rms_norm_rows2(xa, xb, gain, out_dtype, tm=512):
    D = xa.shape[1]
    T = xa.shape[0] + xb.shape[0]
    n_first = xa.shape[0] // tm
    return pl.pallas_call(
        functools.partial(_norm2_body, n_first=n_first),
        grid=(T // tm,),
        in_specs=_split_rows_specs(tm, D, n_first) + [pl.BlockSpec((1, D), lambda i: (0, 0))],
        out_specs=pl.BlockSpec((tm, D), lambda i: (i, 0)),
        out_shape=jax.ShapeDtypeStruct((T, D), out_dtype),
        compiler_params=_cparams(1, 32),
        name="rms_norm_rows2",
    )(xa, xb, gain.reshape(1, D))


def _mm_body(*refs, n_first):
    *a_refs, w_ref, o_ref = refs
    if len(a_refs) == 1:
        o_ref[...] = jnp.dot(a_refs[0][...], w_ref[...], preferred_element_type=F32).astype(o_ref.dtype)
        return
    i = pl.program_id(1)

    @pl.when(i < n_first)
    def _():
        o_ref[...] = jnp.dot(a_refs[0][...], w_ref[...], preferred_element_type=F32).astype(o_ref.dtype)

    @pl.when(i >= n_first)
    def _():
        o_ref[...] = jnp.dot(a_refs[1][...], w_ref[...], preferred_element_type=F32).astype(o_ref.dtype)


def in_proj(a_parts, w, layer, out_dtype, tm=512, tn=1536):
    K = a_parts[0].shape[1]
    n_first = a_parts[0].shape[0] // tm
    M = sum(a.shape[0] for a in a_parts)
    N = w.shape[2]
    if len(a_parts) == 1:
        a_specs = [pl.BlockSpec((tm, K), lambda j, i: (i, 0))]
    else:
        a_specs = [pl.BlockSpec((tm, K), lambda j, i: (jnp.minimum(i, n_first - 1), 0)),
                   pl.BlockSpec((tm, K), lambda j, i: (jnp.maximum(i - n_first, 0), 0))]
    return pl.pallas_call(
        functools.partial(_mm_body, n_first=n_first),
        grid=(N // tn, M // tm),
        in_specs=a_specs + [pl.BlockSpec((None, K, tn), lambda j, i: (layer, 0, j))],
        out_specs=pl.BlockSpec((tm, tn), lambda j, i: (i, j)),
        out_shape=jax.ShapeDtypeStruct((M, N), out_dtype),
        compiler_params=_cparams(2, 48),
        name="in_proj",
    )(*a_parts, w)


def _col_window(rows, width, col, row_block):
    return pl.BlockSpec((pl.Element(rows), pl.Element(width)), lambda *g: (row_block(*g) * rows, col))


def _qkprep_body(p_ref, cos_ref, sin_ref, qg_ref, kg_ref, q_ref, k_ref, v_ref, *, scale):
    cos = cos_ref[...]
    sin = sin_ref[...]
    lane = lax.broadcasted_iota(jnp.int32, cos.shape, 1)
    first = (lane & ROPE_AXIS_PAIRS) == 0

    def prep(x, g, s):
        y = _rms(x, g)
        partner = jnp.where(first, pltpu.roll(y, HEAD_DIM - ROPE_AXIS_PAIRS, 1), pltpu.roll(y, ROPE_AXIS_PAIRS, 1))
        return (y * cos + partner * sin) * s

    for h in range(N_Q_HEADS):
        sl = slice(h * HEAD_DIM, (h + 1) * HEAD_DIM)
        q_ref[:, sl] = prep(p_ref[:, sl], qg_ref[...], scale).astype(q_ref.dtype)
    for h in range(N_KV_HEADS):
        sl = slice(h * HEAD_DIM, (h + 1) * HEAD_DIM)
        src = slice(ATTN_WIDTH + h * HEAD_DIM, ATTN_WIDTH + (h + 1) * HEAD_DIM)
        k_ref[:, sl] = prep(p_ref[:, src], kg_ref[...], 1.0).astype(k_ref.dtype)
    v_ref[...] = p_ref[:, ATTN_WIDTH + KV_WIDTH:].astype(v_ref.dtype)


def qk_prep(proj, cos, sin, q_gain, k_gain, seq, tm=256):
    T = proj.shape[0]
    nblk = seq // tm
    return pl.pallas_call(
        functools.partial(_qkprep_body, scale=HEAD_DIM ** -0.5),
        grid=(T // tm,),
        in_specs=[
            pl.BlockSpec((tm, QKV_WIDTH), lambda i: (i, COL_QKV // QKV_WIDTH)),
            pl.BlockSpec((tm, HEAD_DIM), lambda i: (i % nblk, 0)),
            pl.BlockSpec((tm, HEAD_DIM), lambda i: (i % nblk, 0)),
            pl.BlockSpec((1, HEAD_DIM), lambda i: (0, 0)),
            pl.BlockSpec((1, HEAD_DIM), lambda i: (0, 0)),
        ],
        out_specs=[
            pl.BlockSpec((tm, ATTN_WIDTH), lambda i: (i, 0)),
            pl.BlockSpec((tm, KV_WIDTH), lambda i: (i, 0)),
            pl.BlockSpec((tm, KV_WIDTH), lambda i: (i, 0)),
        ],
        out_shape=[
            jax.ShapeDtypeStruct((T, ATTN_WIDTH), BF16),
            jax.ShapeDtypeStruct((T, KV_WIDTH), BF16),
            jax.ShapeDtypeStruct((T, KV_WIDTH), BF16),
        ],
        compiler_params=_cparams(1, 32),
        name="qk_prep",
    )(proj, cos, sin, q_gain.reshape(1, HEAD_DIM), k_gain.reshape(1, HEAD_DIM))


def rope_tables(seq):
    rows = seq // GRID_W
    row = jnp.repeat(jnp.arange(rows, dtype=F32), GRID_W)
    col = jnp.tile(jnp.arange(GRID_W, dtype=F32), rows)
    inv_freq = 1.0 / (ROPE_THETA ** (jnp.arange(ROPE_AXIS_PAIRS, dtype=F32) / ROPE_AXIS_PAIRS))
    ang_r = row[:, None] * inv_freq
    ang_c = col[:, None] * inv_freq
    cos = jnp.concatenate([jnp.cos(ang_r), jnp.cos(ang_r), jnp.cos(ang_c), jnp.cos(ang_c)], axis=1)
    sin = jnp.concatenate([-jnp.sin(ang_r), jnp.sin(ang_r), -jnp.sin(ang_c), jnp.sin(ang_c)], axis=1)
    return cos, sin


def _attn_body(q_ref, k_ref, v_ref, o_ref):
    n_chunks = k_ref.shape[0] // ATTN_TK
    sls = [slice(g * HEAD_DIM, (g + 1) * HEAD_DIM) for g in range(Q_GROUP)]
    qs = [q_ref[:, sl] for sl in sls]
    ms = ls = accs = None
    for c in range(n_chunks):
        k = k_ref[c * ATTN_TK:(c + 1) * ATTN_TK, :]
        v = v_ref[c * ATTN_TK:(c + 1) * ATTN_TK, :]
        ss = [_dot_nt(q, k) for q in qs]
        new_ms = [jnp.max(s, axis=-1, keepdims=True) for s in ss]
        if c > 0:
            new_ms = [jnp.maximum(m, cm) for m, cm in zip(ms, new_ms)]
        ps = [jnp.exp(s - m) for s, m in zip(ss, new_ms)]
        pvs = [jnp.dot(p.astype(BF16), v, preferred_element_type=F32) for p in ps]
        sums = [jnp.sum(p, axis=-1, keepdims=True) for p in ps]
        if c == 0:
            ls, accs = sums, pvs
        else:
            alphas = [jnp.exp(m - nm) for m, nm in zip(ms, new_ms)]
            ls = [l * a + s for l, a, s in zip(ls, alphas, sums)]
            accs = [acc * a + pv for acc, a, pv in zip(accs, alphas, pvs)]
        ms = new_ms
    for sl, acc, l in zip(sls, accs, ls):
        o_ref[:, sl] = (acc / l).astype(o_ref.dtype)


def attention(q, k, v, seq, tq=512):
    T = q.shape[0]
    n_seq = T // seq
    nq = seq // tq
    gw = Q_GROUP * HEAD_DIM
    return pl.pallas_call(
        _attn_body,
        grid=(n_seq, N_KV_HEADS, nq),
        in_specs=[
            pl.BlockSpec((tq, gw), lambda b, h, i: (b * nq + i, h)),
            pl.BlockSpec((seq, HEAD_DIM), lambda b, h, i: (b, h)),
            pl.BlockSpec((seq, HEAD_DIM), lambda b, h, i: (b, h)),
        ],
        out_specs=pl.BlockSpec((tq, gw), lambda b, h, i: (b * nq + i, h)),
        out_shape=jax.ShapeDtypeStruct((T, ATTN_WIDTH), BF16),
        compiler_params=_cparams(3, 48),
        name="attention",
    )(q, k, v)


def _split3(x):
    hi = x.astype(BF16)
    r1 = x - hi.astype(F32)
    mid = r1.astype(BF16)
    lo = (r1 - mid.astype(F32)).astype(BF16)
    return hi, mid, lo


def _hgrn_gates(z, loglb, log1mlb, omlb):
    e = jnp.exp(-jnp.abs(z))
    r = 1.0 / (1.0 + e)
    kk = omlb * jnp.where(z >= 0, e * r, r)
    cc = log1mlb + (jnp.minimum(z, 0.0) - jnp.log(1.0 + e))
    g = jnp.maximum(loglb, cc) + jnp.log(1.0 + jnp.exp(-jnp.abs(loglb - cc)))
    return g, kk


def _hgrn_body(qf_ref, zf_ref, vf_ref, qb_ref, zb_ref, vb_ref, par_ref, of_ref, ob_ref, stf_ref, stb_ref):
    C, SB, W = HGRN_CHUNK, HGRN_SUB, HGRN_DIM
    NH = HGRN_HEADS_PER_STEP

    @pl.when(pl.program_id(2) == 0)
    def _():
        stf_ref[...] = jnp.zeros(stf_ref.shape, F32)
        stb_ref[...] = jnp.zeros(stb_ref.shape, F32)

    row = lax.broadcasted_iota(jnp.int32, (C, C), 0)
    col = lax.broadcasted_iota(jnp.int32, (C, C), 1)
    dirs = (
        (True, qf_ref, zf_ref, vf_ref, 0, jnp.where(col <= row, 1.0, 0.0).astype(BF16), stf_ref, of_ref),
        (False, qb_ref, zb_ref, vb_ref, 3, jnp.where(col >= row, 1.0, 0.0).astype(BF16), stb_ref, ob_ref),
    )
    gs, kks = [], []
    for fwd, q_ref, z_ref, v_ref, p0, tri, st_ref, o_ref in dirs:
        g, kk = _hgrn_gates(z_ref[...], par_ref[p0:p0 + 1, :], par_ref[p0 + 1:p0 + 2, :], par_ref[p0 + 2:p0 + 3, :])
        gs.append(g)
        kks.append(kk)
    bs = []
    for (fwd, q_ref, z_ref, v_ref, p0, tri, st_ref, o_ref), g in zip(dirs, gs):
        hi, mid, lo = _split3(g)
        bs.append(jnp.dot(tri, hi, preferred_element_type=F32) + jnp.dot(tri, mid, preferred_element_type=F32)
                  + jnp.dot(tri, lo, preferred_element_type=F32))
    work = []
    for (fwd, q_ref, z_ref, v_ref, p0, tri, st_ref, o_ref), b, kk in zip(dirs, bs, kks):
        q = q_ref[...]
        vb = v_ref[...].astype(BF16)
        tot = b[C - 1:C, :] if fwd else b[0:1, :]
        q_in = (q * jnp.exp(b)).astype(BF16)
        k_st = (kk * jnp.exp(tot - b)).astype(BF16)
        dec = jnp.exp(tot)
        subs = []
        for i in range(C // SB):
            r0 = i * SB
            if fwd:
                ref = b[r0 - 1:r0, :] if i > 0 else jnp.zeros((1, NH * W), F32)
                k0, k1 = 0, r0 + SB
            else:
                ref = b[r0 + SB:r0 + SB + 1, :] if r0 + SB < C else jnp.zeros((1, NH * W), F32)
                k0, k1 = r0, C
            qs = (q[r0:r0 + SB] * jnp.exp(b[r0:r0 + SB] - ref)).astype(BF16)
            ks = (kk[k0:k1] * jnp.exp(jnp.minimum(ref - b[k0:k1], EXP_CLAMP))).astype(BF16)
            subs.append((r0, k0, k1, qs, ks))
        work.append((fwd, vb, q_in, k_st, dec, subs, st_ref, o_ref))
    res = []
    for fwd, vb, q_in, k_st, dec, subs, st_ref, o_ref in work:
        for h in range(NH):
            sl = slice(h * W, (h + 1) * W)
            st = st_ref[h]
            o_inter = _dot_nt(q_in[:, sl], st.astype(BF16))
            st_ref[h] = st * dec[:, sl] + _dot_tn(vb[:, sl], k_st[:, sl])
            scores = [_dot_nt(qs[:, sl], ks[:, sl]) for (r0, k0, k1, qs, ks) in subs]
            res.append((fwd, h, o_inter, scores, vb, subs, o_ref))
    masked = []
    for fwd, h, o_inter, scores, vb, subs, o_ref in res:
        ms = []
        for s, (r0, k0, k1, qs, ks) in zip(scores, subs):
            t_idx = r0 + lax.broadcasted_iota(jnp.int32, s.shape, 0)
            s_idx = k0 + lax.broadcasted_iota(jnp.int32, s.shape, 1)
            keep = (s_idx <= t_idx) if fwd else (s_idx >= t_idx)
            ms.append(jnp.where(keep, s, 0.0).astype(BF16))
        masked.append(ms)
    for (fwd, h, o_inter, scores, vb, subs, o_ref), ms in zip(res, masked):
        sl = slice(h * W, (h + 1) * W)
        outs = [jnp.dot(m, vb[k0:k1, sl], preferred_element_type=F32) for m, (r0, k0, k1, qs, ks) in zip(ms, subs)]
        o_ref[:, sl] = o_inter + jnp.concatenate(outs, axis=0)


def hgrn_scan(proj, params, seq):
    T = proj.shape[0]
    n_seq = T // seq
    C = HGRN_CHUNK
    nc = seq // C
    w = HGRN_HEADS_PER_STEP * HGRN_DIM
    n_hh = HGRN_WIDTH // w

    assert n_hh == 1

    def fwd_spec(col):
        return _col_window(C, w, col, lambda b, hh, j: b * nc + j)

    def bwd_spec(col):
        return _col_window(C, w, col, lambda b, hh, j: b * nc + nc - 1 - j)

    return pl.pallas_call(
        _hgrn_body,
        grid=(n_seq, n_hh, nc),
        in_specs=[
            fwd_spec(COL_HQ), fwd_spec(COL_ZF), fwd_spec(COL_HI),
            bwd_spec(COL_HQ), bwd_spec(COL_ZB), bwd_spec(COL_HI),
            pl.BlockSpec((8, w), lambda b, hh, j: (0, hh)),
        ],
        out_specs=[
            pl.BlockSpec((C, w), lambda b, hh, j: (b * nc + j, hh)),
            pl.BlockSpec((C, w), lambda b, hh, j: (b * nc + nc - 1 - j, hh)),
        ],
        out_shape=[jax.ShapeDtypeStruct((T, HGRN_WIDTH), F32), jax.ShapeDtypeStruct((T, HGRN_WIDTH), F32)],
        scratch_shapes=[
            pltpu.VMEM((HGRN_HEADS_PER_STEP, HGRN_DIM, HGRN_DIM), F32),
            pltpu.VMEM((HGRN_HEADS_PER_STEP, HGRN_DIM, HGRN_DIM), F32),
        ],
        compiler_params=_cparams(3, 32),
        name="hgrn_scan",
    )(proj, proj, proj, proj, proj, proj, params)


def hgrn_params(lower_bounds, layer):
    lb_all = jnp.cumsum(jax.nn.softmax(lower_bounds.astype(F32), axis=1), axis=1)
    lb_all = lb_all - lb_all[:, :1]
    rows = []
    for d in range(2):
        lb = lb_all[d, layer]
        rows += [jnp.log(lb), jnp.log1p(-lb), 1.0 - lb]
    rows += [jnp.zeros_like(rows[0])] * 2
    return jnp.stack(rows, axis=0)


def _merge_body(attn_ref, of_ref, ob_ref, hog_ref, ga_ref, gb_ref, hg_ref, wa_ref, wh_ref, o_ref):
    ya = jnp.dot(attn_ref[...], wa_ref[...], preferred_element_type=F32)
    hs = []
    for h in range(N_HGRN_HEADS):
        sl = slice(h * HGRN_DIM, (h + 1) * HGRN_DIM)
        o = _rms(of_ref[:, sl] + ob_ref[:, sl], hg_ref[...])
        og = hog_ref[:, sl]
        hs.append((o * (og * _sigmoid(og))).astype(BF16))
    yh = jnp.dot(jnp.concatenate(hs, axis=1), wh_ref[...], preferred_element_type=F32)
    o_ref[...] = (_sigmoid(ga_ref[...]) * ya + _sigmoid(gb_ref[...]) * yh).astype(o_ref.dtype)


def merge_branches(attn, o_f, o_b, proj, hgrn_gain, w_attn, w_hgrn, layer, tm=256):
    T = attn.shape[0]
    D = D_MODEL
    return pl.pallas_call(
        _merge_body,
        grid=(T // tm,),
        in_specs=[
            pl.BlockSpec((tm, ATTN_WIDTH), lambda i: (i, 0)),
            pl.BlockSpec((tm, HGRN_WIDTH), lambda i: (i, 0)),
            pl.BlockSpec((tm, HGRN_WIDTH), lambda i: (i, 0)),
            _col_window(tm, HGRN_WIDTH, COL_HOG, lambda i: i),
            _col_window(tm, D, COL_GA, lambda i: i),
            _col_window(tm, D, COL_GB, lambda i: i),
            pl.BlockSpec((1, HGRN_DIM), lambda i: (0, 0)),
            pl.BlockSpec((None, ATTN_WIDTH, D), lambda i: (layer, 0, 0)),
            pl.BlockSpec((None, HGRN_WIDTH, D), lambda i: (layer, 0, 0)),
        ],
        out_specs=pl.BlockSpec((tm, D), lambda i: (i, 0)),
        out_shape=jax.ShapeDtypeStruct((T, D), BF16),
        compiler_params=_cparams(1, 56),
        name="merge_branches",
    )(attn, o_f, o_b, proj, proj, proj, hgrn_gain.reshape(1, HGRN_DIM), w_attn, w_hgrn)


def _outproj_body(m_ref, w_ref, xa_ref, xb_ref, g_ref, wr_ref, xo_ref, h_ref, aff_ref, *, n_first):
    xn = _pick_rows(xa_ref, xb_ref, n_first) + jnp.dot(m_ref[...], w_ref[...], preferred_element_type=F32)
    xo_ref[...] = xn
    h = _rms(xn, g_ref[...])
    h_ref[...] = h.astype(h_ref.dtype)
    hb = h.astype(BF16)
    hl = (h - hb.astype(F32)).astype(BF16)
    logits = (jnp.dot(hb, wr_ref[0], preferred_element_type=F32) + jnp.dot(hl, wr_ref[0], preferred_element_type=F32)
              + jnp.dot(hb, wr_ref[1], preferred_element_type=F32))
    mx = jnp.max(logits, axis=1, keepdims=True)
    ex = jnp.exp(logits - mx)
    aff_ref[...] = ex / jnp.sum(ex, axis=1, keepdims=True)


def out_proj_router(merged, w_out, layer, xa, xb, ffn_gain, w_router, tm=256):
    D = xa.shape[1]
    n_first = xa.shape[0] // tm
    xb = xa if xb is None else xb
    T = merged.shape[0]
    wr_hi = w_router.astype(BF16)
    wr_lo = (w_router - wr_hi.astype(F32)).astype(BF16)
    wr = jnp.stack([wr_hi, wr_lo], axis=0)
    return pl.pallas_call(
        functools.partial(_outproj_body, n_first=n_first),
        grid=(T // tm,),
        in_specs=[
            pl.BlockSpec((tm, D), lambda i: (i, 0)),
            pl.BlockSpec((None, D, D), lambda i: (layer, 0, 0)),
            *_split_rows_specs(tm, D, n_first),
            pl.BlockSpec((1, D), lambda i: (0, 0)),
            pl.BlockSpec((2, D, N_EXPERTS), lambda i: (0, 0, 0)),
        ],
        out_specs=[
            pl.BlockSpec((tm, D), lambda i: (i, 0)),
            pl.BlockSpec((tm, D), lambda i: (i, 0)),
            pl.BlockSpec((tm, N_EXPERTS), lambda i: (i, 0)),
        ],
        out_shape=[
            jax.ShapeDtypeStruct((T, D), F32),
            jax.ShapeDtypeStruct((T, D), BF16),
            jax.ShapeDtypeStruct((T, N_EXPERTS), F32),
        ],
        compiler_params=_cparams(1, 48),
        name="out_proj_router",
    )(merged, w_out, xa, xb, ffn_gain.reshape(1, D), wr)


ROUTE_LANES = 128
NOT_SELECTED = 2 ** 30


def _route_body(a_ref, key_ref, rowpref_ref, *, cap, base_tok):
    a = a_ref[...]
    E, nc, L = a.shape
    bits = pltpu.bitcast(a, jnp.int32)

    def count(mask):
        ones = jnp.where(mask, 1.0, 0.0)
        return jnp.sum(jnp.sum(ones, axis=1, keepdims=True), axis=2, keepdims=True)

    def bisect(_, carry):
        lo, hi = carry
        mid = lo + jnp.right_shift(hi - lo, 1)
        ge = count(bits >= mid) >= cap
        return jnp.where(ge, mid, lo), jnp.where(ge, hi, mid)

    lo0 = jnp.zeros((E, 1, 1), jnp.int32)
    hi0 = jnp.full((E, 1, 1), 0x7F800000, jnp.int32)
    thr, _ = lax.fori_loop(0, 31, bisect, (lo0, hi0))

    s_i = lax.broadcasted_iota(jnp.int32, (L, L), 0)
    t_i = lax.broadcasted_iota(jnp.int32, (L, L), 1)
    incl_lane = jnp.where(s_i <= t_i, 1.0, 0.0).astype(BF16)
    r_i = lax.broadcasted_iota(jnp.int32, (E, nc, nc), 1)
    c_i = lax.broadcasted_iota(jnp.int32, (E, nc, nc), 2)
    rows_before = jnp.where(c_i < r_i, 1.0, 0.0).astype(BF16)

    def prefix(mask):
        ones = jnp.where(mask, 1.0, 0.0)
        incl = jnp.dot(ones.astype(BF16).reshape(E * nc, L), incl_lane, preferred_element_type=F32).reshape(E, nc, L)
        row_tot = jnp.broadcast_to(incl[:, :, L - 1:L], (E, nc, L)).astype(BF16)
        before = lax.dot_general(rows_before, row_tot, (((2,), (1,)), ((0,), (0,))), preferred_element_type=F32)
        return ones, incl, before

    gt = bits > thr
    eq = bits == thr
    need_eq = cap - count(gt)
    eq_f, eq_incl, eq_before = prefix(eq)
    take_eq = jnp.where(eq_before + eq_incl - eq_f < need_eq, eq_f, 0.0)
    sel = jnp.where(gt, 1.0, take_eq) > 0.5
    _, _, sel_before = prefix(sel)
    tok = (base_tok + lax.broadcasted_iota(jnp.int32, (E, nc, L), 1) * L
           + lax.broadcasted_iota(jnp.int32, (E, nc, L), 2))
    key_ref[...] = jnp.where(sel, tok, NOT_SELECTED)
    rowpref_ref[...] = sel_before.astype(jnp.int32)


def route_select(aff_group_t, cap, base_tok):
    E, n_tok = aff_group_t.shape
    nc = n_tok // ROUTE_LANES
    a3 = aff_group_t.reshape(E, nc, ROUTE_LANES)
    keys, rowpref = pl.pallas_call(
        functools.partial(_route_body, cap=cap, base_tok=base_tok),
        out_shape=[jax.ShapeDtypeStruct((E, nc, ROUTE_LANES), jnp.int32)] * 2,
        compiler_params=pltpu.CompilerParams(vmem_limit_bytes=48 * 1024 * 1024),
        name="route_select",
    )(a3)
    return keys.reshape(E, n_tok), rowpref[:, :, 0]


FFN_CAST_ROWS = 256
FFN_SPLITS = 2


def _ffn_body(x_ref, gv_ref, meta_ref, wg_hbm, wu_hbm, wd_hbm, o_ref,
              wg_f32, wu_f32, wd_f32, wg_bf, wu_bf, wd_bf, sem_ref, *, layer, first_expert):
    e = pl.program_id(0)
    n_experts = pl.num_programs(0)

    def weight_copies(local_expert):
        expert = first_expert + local_expert
        return (pltpu.make_async_copy(wg_hbm.at[layer, expert], wg_f32, sem_ref.at[0]),
                pltpu.make_async_copy(wu_hbm.at[layer, expert], wu_f32, sem_ref.at[1]),
                pltpu.make_async_copy(wd_hbm.at[layer, expert], wd_f32, sem_ref.at[2]))

    @pl.when(pl.program_id(1) == 0)
    def _():
        @pl.when(e == 0)
        def _():
            for c in weight_copies(e):
                c.start()

        for c in weight_copies(e):
            c.wait()
        for src, dst in ((wg_f32, wg_bf), (wu_f32, wu_bf), (wd_f32, wd_bf)):
            def cast_rows(r, carry, src=src, dst=dst):
                rows = pl.ds(pl.multiple_of(r * FFN_CAST_ROWS, FFN_CAST_ROWS), FFN_CAST_ROWS)
                dst[rows, :] = src[rows, :].astype(BF16)
                return carry
            lax.fori_loop(0, src.shape[0] // FFN_CAST_ROWS, cast_rows, 0)

        @pl.when(e + 1 < n_experts)
        def _():
            for c in weight_copies(e + 1):
                c.start()

    x = x_ref[0]
    g = jnp.dot(x, wg_bf[...], preferred_element_type=F32)
    u = jnp.dot(x, wu_bf[...], preferred_element_type=F32)
    a = (g * _sigmoid(g) * u).astype(BF16)
    y = jnp.dot(a, wd_bf[...], preferred_element_type=F32) * gv_ref[0]
    o_ref[0, :, :D_MODEL] = y.astype(o_ref.dtype)
    o_ref[0, :, D_MODEL:] = meta_ref[0]


def expert_ffn(xe, w_gate, w_up, w_down, layer, first_expert, gate_vals, meta, tm=256):
    E, rows, D = xe.shape
    F = w_gate.shape[-1]
    return pl.pallas_call(
        functools.partial(_ffn_body, layer=layer, first_expert=first_expert),
        grid=(E, rows // tm),
        in_specs=[
            pl.BlockSpec((1, tm, D), lambda e, i: (e, i, 0)),
            pl.BlockSpec((1, tm, 1), lambda e, i: (e, i, 0)),
            pl.BlockSpec((1, tm, META_WIDTH), lambda e, i: (e, i, 0)),
            pl.BlockSpec(memory_space=pl.ANY),
            pl.BlockSpec(memory_space=pl.ANY),
            pl.BlockSpec(memory_space=pl.ANY),
        ],
        out_specs=pl.BlockSpec((1, tm, D + META_WIDTH), lambda e, i: (e, i, 0)),
        out_shape=jax.ShapeDtypeStruct((E, rows, D + META_WIDTH), BF16),
        scratch_shapes=[
            pltpu.VMEM((D, F), F32), pltpu.VMEM((D, F), F32), pltpu.VMEM((F, D), F32),
            pltpu.VMEM((D, F), BF16), pltpu.VMEM((D, F), BF16), pltpu.VMEM((F, D), BF16),
            pltpu.SemaphoreType.DMA((3,)),
        ],
        compiler_params=_cparams(2, 56),
        name="expert_ffn",
    )(xe, gate_vals.reshape(E, rows, 1), meta, w_gate, w_up, w_down)


def _combine_body(win_ref, minslot_ref, nr_ref, x_ref, gain_ref, *refs, base_tile, n_tiles, n_parts):
    ye_win = refs[:N_EXPERTS]
    ye_hbm = refs[N_EXPERTS:N_EXPERTS + n_parts]
    o_ref, h_ref, buf_ref, sem_ref = refs[N_EXPERTS + n_parts:]
    per_part = N_EXPERTS // n_parts
    j = pl.program_id(0)
    t0 = ((j + base_tile) * COMBINE_TM).astype(F32)

    def table_index(e, k):
        return e * (n_tiles * COMBINE_ROUNDS) + j * COMBINE_ROUNDS + k

    def placed(window, k):
        rows, toks = [], []
        for e in range(N_EXPERTS):
            w = window(e)
            meta = w[:, D_MODEL:].astype(F32)
            tok = meta[:, 0:1] * 128.0 + meta[:, 1:2] - t0
            flat = table_index(e, k)
            slot = win_ref[flat] * WIN_BLK + lax.broadcasted_iota(jnp.int32, (WIN_ROWS, 1), 0)
            toks.append(jnp.where(slot >= minslot_ref[flat], tok, -1.0))
            rows.append(w[:, :D_MODEL])
        tok_all = jnp.concatenate(toks, axis=0)
        lane_t = lax.broadcasted_iota(jnp.int32, (N_EXPERTS * WIN_ROWS, COMBINE_TM), 1).astype(F32)
        place = jnp.where(tok_all == lane_t, 1.0, 0.0).astype(BF16)
        return _dot_tn(place, jnp.concatenate(rows, axis=0))

    o_ref[...] = x_ref[...] + placed(lambda e: ye_win[e][...].reshape(WIN_ROWS, D_MODEL + META_WIDTH), 0)

    def window_copy(e, k):
        return pltpu.make_async_copy(
            ye_hbm[e // per_part].at[e % per_part, pl.ds(win_ref[table_index(e, k)], WIN_BLKS)],
            buf_ref.at[e], sem_ref.at[e])

    def extra_round(k, carry):
        for e in range(N_EXPERTS):
            window_copy(e, k).start()
        for e in range(N_EXPERTS):
            window_copy(e, k).wait()
        o_ref[...] += placed(lambda e: buf_ref[e].reshape(WIN_ROWS, D_MODEL + META_WIDTH), k)
        return carry

    lax.fori_loop(1, nr_ref[j], extra_round, 0)
    h_ref[...] = _rms(o_ref[...], gain_ref[...]).astype(h_ref.dtype)


def combine(x, ye_parts, win, minslot, n_rounds, base_tile, gain, h_dtype):
    per_part, rows, dext = ye_parts[0].shape
    E = per_part * len(ye_parts)
    n_tiles = n_rounds.shape[0]
    ye4 = [ye.reshape(per_part, rows // WIN_BLK, WIN_BLK, dext) for ye in ye_parts]

    def ye_spec(e):
        return pl.BlockSpec(
            (pl.Element(1), pl.Element(WIN_BLKS), pl.Element(WIN_BLK), pl.Element(dext)),
            lambda j, win_r, ms_r, nr_r: (e % per_part, win_r[e * (n_tiles * COMBINE_ROUNDS) + j * COMBINE_ROUNDS], 0, 0))

    grid_spec = pltpu.PrefetchScalarGridSpec(
        num_scalar_prefetch=3,
        grid=(n_tiles,),
        in_specs=[pl.BlockSpec((COMBINE_TM, D_MODEL), lambda j, *_: (j + base_tile, 0)),
                  pl.BlockSpec((1, D_MODEL), lambda j, *_: (0, 0))]
        + [ye_spec(e) for e in range(E)]
        + [pl.BlockSpec(memory_space=pl.ANY)] * len(ye4),
        out_specs=[pl.BlockSpec((COMBINE_TM, D_MODEL), lambda j, *_: (j + base_tile, 0)),
                   pl.BlockSpec((COMBINE_TM, D_MODEL), lambda j, *_: (j, 0))],
        scratch_shapes=[
            pltpu.VMEM((E, WIN_BLKS, WIN_BLK, dext), BF16),
            pltpu.SemaphoreType.DMA((E,)),
        ],
    )
    return pl.pallas_call(
        functools.partial(_combine_body, base_tile=base_tile, n_tiles=n_tiles, n_parts=len(ye4)),
        grid_spec=grid_spec,
        out_shape=[jax.ShapeDtypeStruct(x.shape, x.dtype),
                   jax.ShapeDtypeStruct((n_tiles * COMBINE_TM, D_MODEL), h_dtype)],
        input_output_aliases={3: 0},
        compiler_params=_cparams(1, 48),
        name="combine",
    )(win.reshape(-1), minslot.reshape(-1), n_rounds, x, gain.reshape(1, D_MODEL),
      *[ye4[e // per_part] for e in range(E)], *ye4)


def combine_tables(rowpref, cap, slot_base):
    E = rowpref.shape[0]
    end = slot_base + cap
    starts = slot_base + jnp.concatenate(
        [rowpref[:, ::COMBINE_TM // ROUTE_LANES], jnp.full((E, 1), cap, jnp.int32)], axis=1)
    lo, hi = starts[:, :-1], starts[:, 1:]
    a = (lo // WIN_BLK) * WIN_BLK
    need = jnp.where(hi > lo, -(-(hi - a) // WIN_ROWS), 0)
    k = jnp.arange(COMBINE_ROUNDS, dtype=jnp.int32)[None, None, :]
    kk = jnp.minimum(k, jnp.maximum(need[..., None] - 1, 0))
    win = jnp.minimum(a[..., None] + WIN_ROWS * kk, end - WIN_ROWS) // WIN_BLK
    minslot = jnp.where(k < need[..., None], a[..., None] + WIN_ROWS * k, end)
    return win.astype(jnp.int32), minslot.astype(jnp.int32), jnp.max(need, axis=0).astype(jnp.int32)


def kernel(x_prompt, x_sample, w_in, q_norm, k_norm, lower_bounds, hgrn_norm, w_proj_attn, w_proj_hgrn,
           w_out, norm_mix, norm_ffn, w_router, w_gate, w_up, w_down, norm_final):
    depth = w_in.shape[0]
    groups = (x_prompt, x_sample)
    seq = x_prompt.shape[1]
    group_tokens = [g.shape[0] * g.shape[1] for g in groups]
    xa, xb = (g.reshape(-1, D_MODEL) for g in groups)
    cos, sin = rope_tables(seq)
    w_in_bf, w_attn_bf, w_hgrn_bf, w_out_bf = (w.astype(BF16) for w in (w_in, w_proj_attn, w_proj_hgrn, w_out))

    h_parts = [rms_norm_rows2(xa, xb, norm_mix[0], BF16)]
    for l in range(depth):
        proj = in_proj(h_parts, w_in_bf, l, F32)
        q, k, v = qk_prep(proj, cos, sin, q_norm[l], k_norm[l], seq)
        attn = attention(q, k, v, seq)
        o_f, o_b = hgrn_scan(proj, hgrn_params(lower_bounds, l), seq)
        merged = merge_branches(attn, o_f, o_b, proj, hgrn_norm[l], w_attn_bf, w_hgrn_bf, l)
        x, h_ffn, aff = out_proj_router(merged, w_out_bf, l, *((xa, xb) if l == 0 else (x, None)),
                                        norm_ffn[l], w_router[l])
        aff_t = aff.T
        idx_parts, gate_parts, tables = [], [], []
        start = slot_base = 0
        for n_tok in group_tokens:
            cap = EC_CAPACITY * n_tok // N_EXPERTS
            aff_g = aff_t[:, start:start + n_tok]
            keys, rowpref = route_select(aff_g, cap, start)
            idx = jnp.sort(keys, axis=1)[:, :cap]
            idx_parts.append(idx)
            gate_parts.append(jnp.take_along_axis(aff_g, idx - start, axis=1))
            tables.append(combine_tables(rowpref, cap, slot_base) + (start // COMBINE_TM,))
            start += n_tok
            slot_base += cap
        idx = jnp.concatenate(idx_parts, axis=1)
        meta = jnp.concatenate(
            [(idx // 128)[..., None], (idx % 128)[..., None],
             jnp.zeros(idx.shape + (META_WIDTH - 2,), jnp.int32)], axis=-1).astype(BF16)
        gate_vals = jnp.concatenate(gate_parts, axis=1)
        per = N_EXPERTS // FFN_SPLITS
        ye = [expert_ffn(h_ffn[idx[e0:e0 + per]], w_gate, w_up, w_down, l, e0, gate_vals[e0:e0 + per], meta[e0:e0 + per])
              for e0 in range(0, N_EXPERTS, per)]
        last = l == depth - 1
        h_parts = []
        for win, minslot, n_rounds, base_tile in tables:
            x, h_g = combine(x, ye, win, minslot, n_rounds, base_tile,
                             norm_final if last else norm_mix[l + 1], F32 if last else BF16)
            h_parts.append(h_g)

    return tuple(h_g.reshape(g.shape) for h_g, g in zip(h_parts, groups))
```

```python
import functools

import jax
import jax.numpy as jnp
from jax import lax
from jax.experimental import pallas as pl
from jax.experimental.pallas import tpu as pltpu

F32 = jnp.float32
BF16 = jnp.bfloat16

D_MODEL = 2048
SEQ = 4096
GRID_W = 64
HEAD_DIM = 128
N_Q_HEADS = 8
N_KV_HEADS = 2
Q_GROUP = N_Q_HEADS // N_KV_HEADS
ATTN_WIDTH = N_Q_HEADS * HEAD_DIM
KV_WIDTH = N_KV_HEADS * HEAD_DIM
ROPE_THETA = 10000.0
ROPE_AXIS_PAIRS = HEAD_DIM // 4
N_HGRN_HEADS = 8
HGRN_DIM = 128
HGRN_WIDTH = N_HGRN_HEADS * HGRN_DIM
N_EXPERTS = 16
EC_CAPACITY = 2
EXPERT_FF = 1024
NORM_EPS = 1e-6
IN_WIDTH = 10752

QKV_WIDTH = ATTN_WIDTH + 2 * KV_WIDTH
COL_QKV = 0
COL_HQ = COL_QKV + QKV_WIDTH
COL_ZF = COL_HQ + HGRN_WIDTH
COL_ZB = COL_ZF + HGRN_WIDTH
COL_HI = COL_ZB + HGRN_WIDTH
COL_HOG = COL_HI + HGRN_WIDTH
COL_GA = COL_HOG + HGRN_WIDTH
COL_GB = COL_GA + D_MODEL

ATTN_TK = 1024

HGRN_CHUNK = 64
HGRN_SUB = 16
HGRN_HEADS_PER_STEP = 8
EXP_CLAMP = 80.0

META_WIDTH = 128
COMBINE_TM = 256
WIN_BLK = 16
WIN_BLKS = 4
WIN_ROWS = WIN_BLK * WIN_BLKS
COMBINE_ROUNDS = -(-(WIN_BLK - 1 + COMBINE_TM) // WIN_ROWS)

V7X_VMEM_BYTES = 64 * 1024 * 1024


def _cparams(n_grid, vmem_mb):
    assert vmem_mb * 1024 * 1024 < V7X_VMEM_BYTES
    return pltpu.CompilerParams(
        dimension_semantics=("arbitrary",) * n_grid,
        vmem_limit_bytes=vmem_mb * 1024 * 1024,
    )


def _sigmoid(x):
    return 1.0 / (1.0 + jnp.exp(-x))


def _rms(x, gain):
    return x * lax.rsqrt(jnp.mean(x * x, axis=-1, keepdims=True) + NORM_EPS) * gain


def _dot_nt(a, b):
    return lax.dot_general(a, b, (((1,), (1,)), ((), ())), preferred_element_type=F32)


def _dot_tn(a, b):
    return lax.dot_general(a, b, (((0,), (0,)), ((), ())), preferred_element_type=F32)


def _split_rows_specs(tm, D, n_first):
    return [pl.BlockSpec((tm, D), lambda i: (jnp.minimum(i, n_first - 1), 0)),
            pl.BlockSpec((tm, D), lambda i: (jnp.maximum(i - n_first, 0), 0))]


def _pick_rows(xa_ref, xb_ref, n_first):
    return jnp.where(pl.program_id(0) < n_first, xa_ref[...], xb_ref[...])


def _norm2_body(xa_ref, xb_ref, g_ref, o_ref, *, n_first):
    o_ref[...] = _rms(_pick_rows(xa_ref, xb_ref, n_first), g_ref[...]).astype(o_ref.dtype)


def rms_norm_rows2(xa, xb, gain, out_dtype, tm=512):
    D = xa.shape[1]
    T = xa.shape[0] + xb.shape[0]
    n_first = xa.shape[0] // tm
    return pl.pallas_call(
        functools.partial(_norm2_body, n_first=n_first),
        grid=(T // tm,),
        in_specs=_split_rows_specs(tm, D, n_first) + [pl.BlockSpec((1, D), lambda i: (0, 0))],
        out_specs=pl.BlockSpec((tm, D), lambda i: (i, 0)),
        out_shape=jax.ShapeDtypeStruct((T, D), out_dtype),
        compiler_params=_cparams(1, 32),
        name="rms_norm_rows2",
    )(xa, xb, gain.reshape(1, D))


def _mm_body(*refs, n_first):
    *a_refs, w_ref, o_ref = refs
    if len(a_refs) == 1:
        o_ref[...] = jnp.dot(a_refs[0][...], w_ref[...], preferred_element_type=F32).astype(o_ref.dtype)
        return
    i = pl.program_id(1)

    @pl.when(i < n_first)
    def _():
        o_ref[...] = jnp.dot(a_refs[0][...], w_ref[...], preferred_element_type=F32).astype(o_ref.dtype)

    @pl.when(i >= n_first)
    def _():
        o_ref[...] = jnp.dot(a_refs[1][...], w_ref[...], preferred_element_type=F32).astype(o_ref.dtype)


def in_proj(a_parts, w, layer, out_dtype, tm=512, tn=1536):
    K = a_parts[0].shape[1]
    n_first = a_parts[0].shape[0] // tm
    M = sum(a.shape[0] for a in a_parts)
    N = w.shape[2]
    if len(a_parts) == 1:
        a_specs = [pl.BlockSpec((tm, K), lambda j, i: (i, 0))]
    else:
        a_specs = [pl.BlockSpec((tm, K), lambda j, i: (jnp.minimum(i, n_first - 1), 0)),
                   pl.BlockSpec((tm, K), lambda j, i: (jnp.maximum(i - n_first, 0), 0))]
    return pl.pallas_call(
        functools.partial(_mm_body, n_first=n_first),
        grid=(N // tn, M // tm),
        in_specs=a_specs + [pl.BlockSpec((None, K, tn), lambda j, i: (layer, 0, j))],
        out_specs=pl.BlockSpec((tm, tn), lambda j, i: (i, j)),
        out_shape=jax.ShapeDtypeStruct((M, N), out_dtype),
        compiler_params=_cparams(2, 48),
        name="in_proj",
    )(*a_parts, w)


def _col_window(rows, width, col, row_block):
    return pl.BlockSpec((pl.Element(rows), pl.Element(width)), lambda *g: (row_block(*g) * rows, col))


def _qkprep_body(p_ref, cos_ref, sin_ref, qg_ref, kg_ref, q_ref, k_ref, v_ref, *, scale):
    cos = cos_ref[...]
    sin = sin_ref[...]
    lane = lax.broadcasted_iota(jnp.int32, cos.shape, 1)
    first = (lane & ROPE_AXIS_PAIRS) == 0
    xs = [p_ref[:, h * HEAD_DIM:(h + 1) * HEAD_DIM] for h in range(N_Q_HEADS + N_KV_HEADS)]
    gains = [qg_ref[...] * scale] * N_Q_HEADS + [kg_ref[...]] * N_KV_HEADS
    ms = [jnp.mean(x * x, axis=-1, keepdims=True) for x in xs]
    ys = [x * lax.rsqrt(m + NORM_EPS) * g for x, m, g in zip(xs, ms, gains)]
    ups = [pltpu.roll(y, HEAD_DIM - ROPE_AXIS_PAIRS, 1) for y in ys]
    downs = [pltpu.roll(y, ROPE_AXIS_PAIRS, 1) for y in ys]
    outs = [y * cos + jnp.where(first, u, d) * sin for y, u, d in zip(ys, ups, downs)]
    for h in range(N_Q_HEADS):
        q_ref[:, h * HEAD_DIM:(h + 1) * HEAD_DIM] = outs[h].astype(q_ref.dtype)
    for h in range(N_KV_HEADS):
        k_ref[:, h * HEAD_DIM:(h + 1) * HEAD_DIM] = outs[N_Q_HEADS + h].astype(k_ref.dtype)
    v_ref[...] = p_ref[:, ATTN_WIDTH + KV_WIDTH:].astype(v_ref.dtype)


def qk_prep(proj, cos, sin, q_gain, k_gain, seq, tm=512):
    T = proj.shape[0]
    nblk = seq // tm
    return pl.pallas_call(
        functools.partial(_qkprep_body, scale=HEAD_DIM ** -0.5),
        grid=(T // tm,),
        in_specs=[
            pl.BlockSpec((tm, QKV_WIDTH), lambda i: (i, COL_QKV // QKV_WIDTH)),
            pl.BlockSpec((tm, HEAD_DIM), lambda i: (i % nblk, 0)),
            pl.BlockSpec((tm, HEAD_DIM), lambda i: (i % nblk, 0)),
            pl.BlockSpec((1, HEAD_DIM), lambda i: (0, 0)),
            pl.BlockSpec((1, HEAD_DIM), lambda i: (0, 0)),
        ],
        out_specs=[
            pl.BlockSpec((tm, ATTN_WIDTH), lambda i: (i, 0)),
            pl.BlockSpec((tm, KV_WIDTH), lambda i: (i, 0)),
            pl.BlockSpec((tm, KV_WIDTH), lambda i: (i, 0)),
        ],
        out_shape=[
            jax.ShapeDtypeStruct((T, ATTN_WIDTH), BF16),
            jax.ShapeDtypeStruct((T, KV_WIDTH), BF16),
            jax.ShapeDtypeStruct((T, KV_WIDTH), BF16),
        ],
        compiler_params=_cparams(1, 32),
        name="qk_prep",
    )(proj, cos, sin, q_gain.reshape(1, HEAD_DIM), k_gain.reshape(1, HEAD_DIM))


def rope_tables(seq):
    rows = seq // GRID_W
    row = jnp.repeat(jnp.arange(rows, dtype=F32), GRID_W)
    col = jnp.tile(jnp.arange(GRID_W, dtype=F32), rows)
    inv_freq = 1.0 / (ROPE_THETA ** (jnp.arange(ROPE_AXIS_PAIRS, dtype=F32) / ROPE_AXIS_PAIRS))
    ang_r = row[:, None] * inv_freq
    ang_c = col[:, None] * inv_freq
    cos = jnp.concatenate([jnp.cos(ang_r), jnp.cos(ang_r), jnp.cos(ang_c), jnp.cos(ang_c)], axis=1)
    sin = jnp.concatenate([-jnp.sin(ang_r), jnp.sin(ang_r), -jnp.sin(ang_c), jnp.sin(ang_c)], axis=1)
    return cos, sin


def _attn_body(q_ref, k_ref, v_ref, o_ref):
    n_chunks = k_ref.shape[0] // ATTN_TK
    sls = [slice(g * HEAD_DIM, (g + 1) * HEAD_DIM) for g in range(Q_GROUP)]
    qs = [q_ref[:, sl] for sl in sls]
    ms = ls = accs = None
    for c in range(n_chunks):
        k = k_ref[c * ATTN_TK:(c + 1) * ATTN_TK, :]
        v = v_ref[c * ATTN_TK:(c + 1) * ATTN_TK, :]
        ss = [_dot_nt(q, k) for q in qs]
        new_ms = [jnp.max(s, axis=-1, keepdims=True) for s in ss]
        if c > 0:
            new_ms = [jnp.maximum(m, cm) for m, cm in zip(ms, new_ms)]
        ps = [jnp.exp(s - m) for s, m in zip(ss, new_ms)]
        pvs = [jnp.dot(p.astype(BF16), v, preferred_element_type=F32) for p in ps]
        sums = [jnp.sum(p, axis=-1, keepdims=True) for p in ps]
        if c == 0:
            ls, accs = sums, pvs
        else:
            alphas = [jnp.exp(m - nm) for m, nm in zip(ms, new_ms)]
            ls = [l * a + s for l, a, s in zip(ls, alphas, sums)]
            accs = [acc * a + pv for acc, a, pv in zip(accs, alphas, pvs)]
        ms = new_ms
    for sl, acc, l in zip(sls, accs, ls):
        o_ref[:, sl] = (acc / l).astype(o_ref.dtype)


def attention(q, k, v, seq, tq=512):
    T = q.shape[0]
    n_seq = T // seq
    nq = seq // tq
    gw = Q_GROUP * HEAD_DIM
    return pl.pallas_call(
        _attn_body,
        grid=(n_seq, N_KV_HEADS, nq),
        in_specs=[
            pl.BlockSpec((tq, gw), lambda b, h, i: (b * nq + i, h)),
            pl.BlockSpec((seq, HEAD_DIM), lambda b, h, i: (b, h)),
            pl.BlockSpec((seq, HEAD_DIM), lambda b, h, i: (b, h)),
        ],
        out_specs=pl.BlockSpec((tq, gw), lambda b, h, i: (b * nq + i, h)),
        out_shape=jax.ShapeDtypeStruct((T, ATTN_WIDTH), BF16),
        compiler_params=_cparams(3, 48),
        name="attention",
    )(q, k, v)


def _split3(x):
    hi = x.astype(BF16)
    r1 = x - hi.astype(F32)
    mid = r1.astype(BF16)
    lo = (r1 - mid.astype(F32)).astype(BF16)
    return hi, mid, lo


def _hgrn_gates(z, loglb, log1mlb, omlb):
    e = jnp.exp(-jnp.abs(z))
    r = 1.0 / (1.0 + e)
    kk = omlb * jnp.where(z >= 0, e * r, r)
    cc = log1mlb + (jnp.minimum(z, 0.0) - jnp.log(1.0 + e))
    g = jnp.maximum(loglb, cc) + jnp.log(1.0 + jnp.exp(-jnp.abs(loglb - cc)))
    return g, kk


def _hgrn_body(qf_ref, zf_ref, vf_ref, qb_ref, zb_ref, vb_ref, par_ref, of_ref, ob_ref, stf_ref, stb_ref):
    C, SB, W = HGRN_CHUNK, HGRN_SUB, HGRN_DIM
    NH = HGRN_HEADS_PER_STEP

    @pl.when(pl.program_id(2) == 0)
    def _():
        stf_ref[...] = jnp.zeros(stf_ref.shape, F32)
        stb_ref[...] = jnp.zeros(stb_ref.shape, F32)

    row = lax.broadcasted_iota(jnp.int32, (C, C), 0)
    col = lax.broadcasted_iota(jnp.int32, (C, C), 1)
    dirs = (
        (True, qf_ref, zf_ref, vf_ref, 0, jnp.where(col <= row, 1.0, 0.0).astype(BF16), stf_ref, of_ref),
        (False, qb_ref, zb_ref, vb_ref, 3, jnp.where(col >= row, 1.0, 0.0).astype(BF16), stb_ref, ob_ref),
    )
    gs, kks = [], []
    for fwd, q_ref, z_ref, v_ref, p0, tri, st_ref, o_ref in dirs:
        g, kk = _hgrn_gates(z_ref[...], par_ref[p0:p0 + 1, :], par_ref[p0 + 1:p0 + 2, :], par_ref[p0 + 2:p0 + 3, :])
        gs.append(g)
        kks.append(kk)
    bs = []
    for (fwd, q_ref, z_ref, v_ref, p0, tri, st_ref, o_ref), g in zip(dirs, gs):
        hi, mid, lo = _split3(g)
        bs.append(jnp.dot(tri, hi, preferred_element_type=F32) + jnp.dot(tri, mid, preferred_element_type=F32)
                  + jnp.dot(tri, lo, preferred_element_type=F32))
    work = []
    for (fwd, q_ref, z_ref, v_ref, p0, tri, st_ref, o_ref), b, kk in zip(dirs, bs, kks):
        q = q_ref[...]
        vb = v_ref[...].astype(BF16)
        tot = b[C - 1:C, :] if fwd else b[0:1, :]
        q_in = (q * jnp.exp(b)).astype(BF16)
        k_st = (kk * jnp.exp(tot - b)).astype(BF16)
        dec = jnp.exp(tot)
        subs = []
        for i in range(C // SB):
            r0 = i * SB
            if fwd:
                ref = b[r0 - 1:r0, :] if i > 0 else jnp.zeros((1, NH * W), F32)
                k0, k1 = 0, r0 + SB
            else:
                ref = b[r0 + SB:r0 + SB + 1, :] if r0 + SB < C else jnp.zeros((1, NH * W), F32)
                k0, k1 = r0, C
            qs = (q[r0:r0 + SB] * jnp.exp(b[r0:r0 + SB] - ref)).astype(BF16)
            ks = (kk[k0:k1] * jnp.exp(jnp.minimum(ref - b[k0:k1], EXP_CLAMP))).astype(BF16)
            subs.append((r0, k0, k1, qs, ks))
        work.append((fwd, vb, q_in, k_st, dec, subs, st_ref, o_ref))
    res = []
    for fwd, vb, q_in, k_st, dec, subs, st_ref, o_ref in work:
        for h in range(NH):
            sl = slice(h * W, (h + 1) * W)
            st = st_ref[h]
            o_inter = _dot_nt(q_in[:, sl], st.astype(BF16))
            st_ref[h] = st * dec[:, sl] + _dot_tn(vb[:, sl], k_st[:, sl])
            scores = [_dot_nt(qs[:, sl], ks[:, sl]) for (r0, k0, k1, qs, ks) in subs]
            res.append((fwd, h, o_inter, scores, vb, subs, o_ref))
    masked = []
    for fwd, h, o_inter, scores, vb, subs, o_ref in res:
        ms = []
        for s, (r0, k0, k1, qs, ks) in zip(scores, subs):
            t_idx = r0 + lax.broadcasted_iota(jnp.int32, s.shape, 0)
            s_idx = k0 + lax.broadcasted_iota(jnp.int32, s.shape, 1)
            keep = (s_idx <= t_idx) if fwd else (s_idx >= t_idx)
            ms.append(jnp.where(keep, s, 0.0).astype(BF16))
        masked.append(ms)
    for (fwd, h, o_inter, scores, vb, subs, o_ref), ms in zip(res, masked):
        sl = slice(h * W, (h + 1) * W)
        outs = [jnp.dot(m, vb[k0:k1, sl], preferred_element_type=F32) for m, (r0, k0, k1, qs, ks) in zip(ms, subs)]
        o_ref[:, sl] = o_inter + jnp.concatenate(outs, axis=0)


def hgrn_scan(proj, params, seq):
    T = proj.shape[0]
    n_seq = T // seq
    C = HGRN_CHUNK
    nc = seq // C
    w = HGRN_HEADS_PER_STEP * HGRN_DIM
    n_hh = HGRN_WIDTH // w

    assert n_hh == 1

    def fwd_spec(col):
        return _col_window(C, w, col, lambda b, hh, j: b * nc + j)

    def bwd_spec(col):
        return _col_window(C, w, col, lambda b, hh, j: b * nc + nc - 1 - j)

    return pl.pallas_call(
        _hgrn_body,
        grid=(n_seq, n_hh, nc),
        in_specs=[
            fwd_spec(COL_HQ), fwd_spec(COL_ZF), fwd_spec(COL_HI),
            bwd_spec(COL_HQ), bwd_spec(COL_ZB), bwd_spec(COL_HI),
            pl.BlockSpec((8, w), lambda b, hh, j: (0, hh)),
        ],
        out_specs=[
            pl.BlockSpec((C, w), lambda b, hh, j: (b * nc + j, hh)),
            pl.BlockSpec((C, w), lambda b, hh, j: (b * nc + nc - 1 - j, hh)),
        ],
        out_shape=[jax.ShapeDtypeStruct((T, HGRN_WIDTH), F32), jax.ShapeDtypeStruct((T, HGRN_WIDTH), F32)],
        scratch_shapes=[
            pltpu.VMEM((HGRN_HEADS_PER_STEP, HGRN_DIM, HGRN_DIM), F32),
            pltpu.VMEM((HGRN_HEADS_PER_STEP, HGRN_DIM, HGRN_DIM), F32),
        ],
        compiler_params=_cparams(3, 32),
        name="hgrn_scan",
    )(proj, proj, proj, proj, proj, proj, params)


def hgrn_params(lower_bounds, layer):
    lb_all = jnp.cumsum(jax.nn.softmax(lower_bounds.astype(F32), axis=1), axis=1)
    lb_all = lb_all - lb_all[:, :1]
    rows = []
    for d in range(2):
        lb = lb_all[d, layer]
        rows += [jnp.log(lb), jnp.log1p(-lb), 1.0 - lb]
    rows += [jnp.zeros_like(rows[0])] * 2
    return jnp.stack(rows, axis=0)


def _merge_body(attn_ref, of_ref, ob_ref, hog_ref, ga_ref, gb_ref, hg_ref, wa_ref, wh_ref, o_ref):
    ya = jnp.dot(attn_ref[...], wa_ref[...], preferred_element_type=F32)
    hs = []
    for h in range(N_HGRN_HEADS):
        sl = slice(h * HGRN_DIM, (h + 1) * HGRN_DIM)
        o = _rms(of_ref[:, sl] + ob_ref[:, sl], hg_ref[...])
        og = hog_ref[:, sl]
        hs.append((o * (og * _sigmoid(og))).astype(BF16))
    yh = jnp.dot(jnp.concatenate(hs, axis=1), wh_ref[...], preferred_element_type=F32)
    o_ref[...] = (_sigmoid(ga_ref[...]) * ya + _sigmoid(gb_ref[...]) * yh).astype(o_ref.dtype)


def merge_branches(attn, o_f, o_b, proj, hgrn_gain, w_attn, w_hgrn, layer, tm=256):
    T = attn.shape[0]
    D = D_MODEL
    return pl.pallas_call(
        _merge_body,
        grid=(T // tm,),
        in_specs=[
            pl.BlockSpec((tm, ATTN_WIDTH), lambda i: (i, 0)),
            pl.BlockSpec((tm, HGRN_WIDTH), lambda i: (i, 0)),
            pl.BlockSpec((tm, HGRN_WIDTH), lambda i: (i, 0)),
            _col_window(tm, HGRN_WIDTH, COL_HOG, lambda i: i),
            _col_window(tm, D, COL_GA, lambda i: i),
            _col_window(tm, D, COL_GB, lambda i: i),
            pl.BlockSpec((1, HGRN_DIM), lambda i: (0, 0)),
            pl.BlockSpec((None, ATTN_WIDTH, D), lambda i: (layer, 0, 0)),
            pl.BlockSpec((None, HGRN_WIDTH, D), lambda i: (layer, 0, 0)),
        ],
        out_specs=pl.BlockSpec((tm, D), lambda i: (i, 0)),
        out_shape=jax.ShapeDtypeStruct((T, D), BF16),
        compiler_params=_cparams(1, 56),
        name="merge_branches",
    )(attn, o_f, o_b, proj, proj, proj, hgrn_gain.reshape(1, HGRN_DIM), w_attn, w_hgrn)


def _outproj_body(m_ref, w_ref, xa_ref, xb_ref, g_ref, wr_ref, xo_ref, h_ref, aff_ref, *, n_first):
    xn = _pick_rows(xa_ref, xb_ref, n_first) + jnp.dot(m_ref[...], w_ref[...], preferred_element_type=F32)
    xo_ref[...] = xn
    h = _rms(xn, g_ref[...])
    h_ref[...] = h.astype(h_ref.dtype)
    hb = h.astype(BF16)
    hl = (h - hb.astype(F32)).astype(BF16)
    logits = (jnp.dot(hb, wr_ref[0], preferred_element_type=F32) + jnp.dot(hl, wr_ref[0], preferred_element_type=F32)
              + jnp.dot(hb, wr_ref[1], preferred_element_type=F32))
    mx = jnp.max(logits, axis=1, keepdims=True)
    ex = jnp.exp(logits - mx)
    aff_ref[...] = ex / jnp.sum(ex, axis=1, keepdims=True)


def out_proj_router(merged, w_out, layer, xa, xb, ffn_gain, w_router, tm=256):
    D = xa.shape[1]
    n_first = xa.shape[0] // tm
    xb = xa if xb is None else xb
    T = merged.shape[0]
    wr_hi = w_router.astype(BF16)
    wr_lo = (w_router - wr_hi.astype(F32)).astype(BF16)
    wr = jnp.stack([wr_hi, wr_lo], axis=0)
    return pl.pallas_call(
        functools.partial(_outproj_body, n_first=n_first),
        grid=(T // tm,),
        in_specs=[
            pl.BlockSpec((tm, D), lambda i: (i, 0)),
            pl.BlockSpec((None, D, D), lambda i: (layer, 0, 0)),
            *_split_rows_specs(tm, D, n_first),
            pl.BlockSpec((1, D), lambda i: (0, 0)),
            pl.BlockSpec((2, D, N_EXPERTS), lambda i: (0, 0, 0)),
        ],
        out_specs=[
            pl.BlockSpec((tm, D), lambda i: (i, 0)),
            pl.BlockSpec((tm, D), lambda i: (i, 0)),
            pl.BlockSpec((tm, N_EXPERTS), lambda i: (i, 0)),
        ],
        out_shape=[
            jax.ShapeDtypeStruct((T, D), F32),
            jax.ShapeDtypeStruct((T, D), BF16),
            jax.ShapeDtypeStruct((T, N_EXPERTS), F32),
        ],
        compiler_params=_cparams(1, 48),
        name="out_proj_router",
    )(merged, w_out, xa, xb, ffn_gain.reshape(1, D), wr)


ROUTE_LANES = 128
NOT_SELECTED = 2 ** 30


def _route_body(a_ref, key_ref, rowpref_ref, *, cap, base_tok):
    a = a_ref[...]
    E, nc, L = a.shape
    bits = pltpu.bitcast(a, jnp.int32)

    def count(mask):
        ones = jnp.where(mask, 1.0, 0.0)
        return jnp.sum(jnp.sum(ones, axis=1, keepdims=True), axis=2, keepdims=True)

    def bisect(_, carry):
        lo, hi = carry
        mid = lo + jnp.right_shift(hi - lo, 1)
        ge = count(bits >= mid) >= cap
        return jnp.where(ge, mid, lo), jnp.where(ge, hi, mid)

    lo0 = jnp.zeros((E, 1, 1), jnp.int32)
    hi0 = jnp.full((E, 1, 1), 0x7F800000, jnp.int32)
    thr, _ = lax.fori_loop(0, 31, bisect, (lo0, hi0))

    s_i = lax.broadcasted_iota(jnp.int32, (L, L), 0)
    t_i = lax.broadcasted_iota(jnp.int32, (L, L), 1)
    incl_lane = jnp.where(s_i <= t_i, 1.0, 0.0).astype(BF16)
    r_i = lax.broadcasted_iota(jnp.int32, (E, nc, nc), 1)
    c_i = lax.broadcasted_iota(jnp.int32, (E, nc, nc), 2)
    rows_before = jnp.where(c_i < r_i, 1.0, 0.0).astype(BF16)

    def prefix(mask):
        ones = jnp.where(mask, 1.0, 0.0)
        incl = jnp.dot(ones.astype(BF16).reshape(E * nc, L), incl_lane, preferred_element_type=F32).reshape(E, nc, L)
        row_tot = jnp.broadcast_to(incl[:, :, L - 1:L], (E, nc, L)).astype(BF16)
        before = lax.dot_general(rows_before, row_tot, (((2,), (1,)), ((0,), (0,))), preferred_element_type=F32)
        return ones, incl, before

    gt = bits > thr
    eq = bits == thr
    need_eq = cap - count(gt)
    eq_f, eq_incl, eq_before = prefix(eq)
    take_eq = jnp.where(eq_before + eq_incl - eq_f < need_eq, eq_f, 0.0)
    sel = jnp.where(gt, 1.0, take_eq) > 0.5
    _, _, sel_before = prefix(sel)
    tok = (base_tok + lax.broadcasted_iota(jnp.int32, (E, nc, L), 1) * L
           + lax.broadcasted_iota(jnp.int32, (E, nc, L), 2))
    key_ref[...] = jnp.where(sel, tok, NOT_SELECTED)
    rowpref_ref[...] = sel_before.astype(jnp.int32)


def route_select(aff_group_t, cap, base_tok):
    E, n_tok = aff_group_t.shape
    nc = n_tok // ROUTE_LANES
    a3 = aff_group_t.reshape(E, nc, ROUTE_LANES)
    keys, rowpref = pl.pallas_call(
        functools.partial(_route_body, cap=cap, base_tok=base_tok),
        out_shape=[jax.ShapeDtypeStruct((E, nc, ROUTE_LANES), jnp.int32)] * 2,
        compiler_params=pltpu.CompilerParams(vmem_limit_bytes=48 * 1024 * 1024),
        name="route_select",
    )(a3)
    return keys.reshape(E, n_tok), rowpref[:, :, 0]


FFN_CAST_ROWS = 256
FFN_SPLITS = 2


def _ffn_body(x_ref, gv_ref, meta_ref, wg_hbm, wu_hbm, wd_hbm, o_ref,
              wg_f32, wu_f32, wd_f32, wg_bf, wu_bf, wd_bf, sem_ref, *, layer, first_expert):
    e = pl.program_id(0)
    n_experts = pl.num_programs(0)

    def weight_copies(local_expert):
        expert = first_expert + local_expert
        return (pltpu.make_async_copy(wg_hbm.at[layer, expert], wg_f32, sem_ref.at[0]),
                pltpu.make_async_copy(wu_hbm.at[layer, expert], wu_f32, sem_ref.at[1]),
                pltpu.make_async_copy(wd_hbm.at[layer, expert], wd_f32, sem_ref.at[2]))

    @pl.when(pl.program_id(1) == 0)
    def _():
        @pl.when(e == 0)
        def _():
            for c in weight_copies(e):
                c.start()

        for c in weight_copies(e):
            c.wait()
        for src, dst in ((wg_f32, wg_bf), (wu_f32, wu_bf), (wd_f32, wd_bf)):
            def cast_rows(r, carry, src=src, dst=dst):
                rows = pl.ds(pl.multiple_of(r * FFN_CAST_ROWS, FFN_CAST_ROWS), FFN_CAST_ROWS)
                dst[rows, :] = src[rows, :].astype(BF16)
                return carry
            lax.fori_loop(0, src.shape[0] // FFN_CAST_ROWS, cast_rows, 0)

        @pl.when(e + 1 < n_experts)
        def _():
            for c in weight_copies(e + 1):
                c.start()

    x = x_ref[0]
    g = jnp.dot(x, wg_bf[...], preferred_element_type=F32)
    u = jnp.dot(x, wu_bf[...], preferred_element_type=F32)
    a = (g * _sigmoid(g) * u).astype(BF16)
    y = jnp.dot(a, wd_bf[...], preferred_element_type=F32) * gv_ref[0]
    o_ref[0, :, :D_MODEL] = y.astype(o_ref.dtype)
    o_ref[0, :, D_MODEL:] = meta_ref[0]


def expert_ffn(xe, w_gate, w_up, w_down, layer, first_expert, gate_vals, meta, tm=512):
    E, rows, D = xe.shape
    F = w_gate.shape[-1]
    return pl.pallas_call(
        functools.partial(_ffn_body, layer=layer, first_expert=first_expert),
        grid=(E, rows // tm),
        in_specs=[
            pl.BlockSpec((1, tm, D), lambda e, i: (e, i, 0)),
            pl.BlockSpec((1, tm, 1), lambda e, i: (e, i, 0)),
            pl.BlockSpec((1, tm, META_WIDTH), lambda e, i: (e, i, 0)),
            pl.BlockSpec(memory_space=pl.ANY),
            pl.BlockSpec(memory_space=pl.ANY),
            pl.BlockSpec(memory_space=pl.ANY),
        ],
        out_specs=pl.BlockSpec((1, tm, D + META_WIDTH), lambda e, i: (e, i, 0)),
        out_shape=jax.ShapeDtypeStruct((E, rows, D + META_WIDTH), BF16),
        scratch_shapes=[
            pltpu.VMEM((D, F), F32), pltpu.VMEM((D, F), F32), pltpu.VMEM((F, D), F32),
            pltpu.VMEM((D, F), BF16), pltpu.VMEM((D, F), BF16), pltpu.VMEM((F, D), BF16),
            pltpu.SemaphoreType.DMA((3,)),
        ],
        compiler_params=_cparams(2, 56),
        name="expert_ffn",
    )(xe, gate_vals.reshape(E, rows, 1), meta, w_gate, w_up, w_down)


def _combine_body(win_ref, minslot_ref, nr_ref, x_ref, gain_ref, *refs, base_tile, n_tiles, n_parts, keep_x):
    ye_win = refs[:N_EXPERTS]
    ye_hbm = refs[N_EXPERTS:N_EXPERTS + n_parts]
    if keep_x:
        o_ref, h_ref, buf_ref, sem_ref = refs[N_EXPERTS + n_parts:]
    else:
        h_ref, o_ref, buf_ref, sem_ref = refs[N_EXPERTS + n_parts:]
    per_part = N_EXPERTS // n_parts
    j = pl.program_id(0)
    t0 = ((j + base_tile) * COMBINE_TM).astype(F32)

    def table_index(e, k):
        return e * (n_tiles * COMBINE_ROUNDS) + j * COMBINE_ROUNDS + k

    def placed(window, k):
        rows, toks = [], []
        for e in range(N_EXPERTS):
            w = window(e)
            meta = w[:, D_MODEL:].astype(F32)
            tok = meta[:, 0:1] * 128.0 + meta[:, 1:2] - t0
            flat = table_index(e, k)
            slot = win_ref[flat] * WIN_BLK + lax.broadcasted_iota(jnp.int32, (WIN_ROWS, 1), 0)
            toks.append(jnp.where(slot >= minslot_ref[flat], tok, -1.0))
            rows.append(w[:, :D_MODEL])
        tok_all = jnp.concatenate(toks, axis=0)
        lane_t = lax.broadcasted_iota(jnp.int32, (N_EXPERTS * WIN_ROWS, COMBINE_TM), 1).astype(F32)
        place = jnp.where(tok_all == lane_t, 1.0, 0.0).astype(BF16)
        return _dot_tn(place, jnp.concatenate(rows, axis=0))

    o_ref[...] = x_ref[...] + placed(lambda e: ye_win[e][...].reshape(WIN_ROWS, D_MODEL + META_WIDTH), 0)

    def window_copy(e, k):
        return pltpu.make_async_copy(
            ye_hbm[e // per_part].at[e % per_part, pl.ds(win_ref[table_index(e, k)], WIN_BLKS)],
            buf_ref.at[e], sem_ref.at[e])

    def extra_round(k, carry):
        for e in range(N_EXPERTS):
            window_copy(e, k).start()
        for e in range(N_EXPERTS):
            window_copy(e, k).wait()
        o_ref[...] += placed(lambda e: buf_ref[e].reshape(WIN_ROWS, D_MODEL + META_WIDTH), k)
        return carry

    lax.fori_loop(1, nr_ref[j], extra_round, 0)
    h_ref[...] = _rms(o_ref[...], gain_ref[...]).astype(h_ref.dtype)


def combine(x, ye_parts, win, minslot, n_rounds, base_tile, gain, h_dtype, keep_x):
    per_part, rows, dext = ye_parts[0].shape
    E = per_part * len(ye_parts)
    n_tiles = n_rounds.shape[0]
    ye4 = [ye.reshape(per_part, rows // WIN_BLK, WIN_BLK, dext) for ye in ye_parts]

    def ye_spec(e):
        return pl.BlockSpec(
            (pl.Element(1), pl.Element(WIN_BLKS), pl.Element(WIN_BLK), pl.Element(dext)),
            lambda j, win_r, ms_r, nr_r: (e % per_part, win_r[e * (n_tiles * COMBINE_ROUNDS) + j * COMBINE_ROUNDS], 0, 0))

    grid_spec = pltpu.PrefetchScalarGridSpec(
        num_scalar_prefetch=3,
        grid=(n_tiles,),
        in_specs=[pl.BlockSpec((COMBINE_TM, D_MODEL), lambda j, *_: (j + base_tile, 0)),
                  pl.BlockSpec((1, D_MODEL), lambda j, *_: (0, 0))]
        + [ye_spec(e) for e in range(E)]
        + [pl.BlockSpec(memory_space=pl.ANY)] * len(ye4),
        out_specs=([pl.BlockSpec((COMBINE_TM, D_MODEL), lambda j, *_: (j + base_tile, 0))] if keep_x else [])
        + [pl.BlockSpec((COMBINE_TM, D_MODEL), lambda j, *_: (j, 0))],
        scratch_shapes=([] if keep_x else [pltpu.VMEM((COMBINE_TM, D_MODEL), F32)]) + [
            pltpu.VMEM((E, WIN_BLKS, WIN_BLK, dext), BF16),
            pltpu.SemaphoreType.DMA((E,)),
        ],
    )
    outs = pl.pallas_call(
        functools.partial(_combine_body, base_tile=base_tile, n_tiles=n_tiles, n_parts=len(ye4), keep_x=keep_x),
        grid_spec=grid_spec,
        out_shape=([jax.ShapeDtypeStruct(x.shape, x.dtype)] if keep_x else [])
        + [jax.ShapeDtypeStruct((n_tiles * COMBINE_TM, D_MODEL), h_dtype)],
        input_output_aliases={3: 0} if keep_x else {},
        compiler_params=_cparams(1, 48),
        name="combine",
    )(win.reshape(-1), minslot.reshape(-1), n_rounds, x, gain.reshape(1, D_MODEL),
      *[ye4[e // per_part] for e in range(E)], *ye4)
    return outs if keep_x else (None, outs[0])


def combine_tables(rowpref, cap, slot_base):
    E = rowpref.shape[0]
    end = slot_base + cap
    starts = slot_base + jnp.concatenate(
        [rowpref[:, ::COMBINE_TM // ROUTE_LANES], jnp.full((E, 1), cap, jnp.int32)], axis=1)
    lo, hi = starts[:, :-1], starts[:, 1:]
    a = (lo // WIN_BLK) * WIN_BLK
    need = jnp.where(hi > lo, -(-(hi - a) // WIN_ROWS), 0)
    k = jnp.arange(COMBINE_ROUNDS, dtype=jnp.int32)[None, None, :]
    kk = jnp.minimum(k, jnp.maximum(need[..., None] - 1, 0))
    win = jnp.minimum(a[..., None] + WIN_ROWS * kk, end - WIN_ROWS) // WIN_BLK
    minslot = jnp.where(k < need[..., None], a[..., None] + WIN_ROWS * k, end)
    return win.astype(jnp.int32), minslot.astype(jnp.int32), jnp.max(need, axis=0).astype(jnp.int32)


def kernel(x_prompt, x_sample, w_in, q_norm, k_norm, lower_bounds, hgrn_norm, w_proj_attn, w_proj_hgrn,
           w_out, norm_mix, norm_ffn, w_router, w_gate, w_up, w_down, norm_final):
    depth = w_in.shape[0]
    groups = (x_prompt, x_sample)
    seq = x_prompt.shape[1]
    group_tokens = [g.shape[0] * g.shape[1] for g in groups]
    xa, xb = (g.reshape(-1, D_MODEL) for g in groups)
    cos, sin = rope_tables(seq)
    w_in_bf, w_attn_bf, w_hgrn_bf, w_out_bf = (w.astype(BF16) for w in (w_in, w_proj_attn, w_proj_hgrn, w_out))

    h_parts = [rms_norm_rows2(xa, xb, norm_mix[0], BF16)]
    for l in range(depth):
        proj = in_proj(h_parts, w_in_bf, l, F32)
        q, k, v = qk_prep(proj, cos, sin, q_norm[l], k_norm[l], seq)
        attn = attention(q, k, v, seq)
        o_f, o_b = hgrn_scan(proj, hgrn_params(lower_bounds, l), seq)
        merged = merge_branches(attn, o_f, o_b, proj, hgrn_norm[l], w_attn_bf, w_hgrn_bf, l)
        x, h_ffn, aff = out_proj_router(merged, w_out_bf, l, *((xa, xb) if l == 0 else (x, None)),
                                        norm_ffn[l], w_router[l])
        aff_t = aff.T
        idx_parts, gate_parts, tables = [], [], []
        start = slot_base = 0
        for n_tok in group_tokens:
            cap = EC_CAPACITY * n_tok // N_EXPERTS
            aff_g = aff_t[:, start:start + n_tok]
            keys, rowpref = route_select(aff_g, cap, start)
            idx = jnp.sort(keys, axis=1)[:, :cap]
            idx_parts.append(idx)
            gate_parts.append(jnp.take_along_axis(aff_g, idx - start, axis=1))
            tables.append(combine_tables(rowpref, cap, slot_base) + (start // COMBINE_TM,))
            start += n_tok
            slot_base += cap
        idx = jnp.concatenate(idx_parts, axis=1)
        meta = jnp.concatenate(
            [(idx // 128)[..., None], (idx % 128)[..., None],
             jnp.zeros(idx.shape + (META_WIDTH - 2,), jnp.int32)], axis=-1).astype(BF16)
        gate_vals = jnp.concatenate(gate_parts, axis=1)
        per = N_EXPERTS // FFN_SPLITS
        ye = [expert_ffn(h_ffn[idx[e0:e0 + per]], w_gate, w_up, w_down, l, e0, gate_vals[e0:e0 + per], meta[e0:e0 + per])
              for e0 in range(0, N_EXPERTS, per)]
        last = l == depth - 1
        h_parts = []
        for win, minslot, n_rounds, base_tile in tables:
            x_next, h_g = combine(x, ye, win, minslot, n_rounds, base_tile,
                                  norm_final if last else norm_mix[l + 1], F32 if last else BF16, not last)
            x = x if last else x_next
            h_parts.append(h_g)

    return tuple(h_g.reshape(g.shape) for h_g, g in zip(h_parts, groups))
```

```python
import functools
import math

import jax
import jax.numpy as jnp
from jax import lax
from jax.experimental import pallas as pl
from jax.experimental.pallas import tpu as pltpu

F32 = jnp.float32
BF16 = jnp.bfloat16

D_MODEL = 2048
SEQ = 4096
GRID_W = 64
HEAD_DIM = 128
N_Q_HEADS = 8
N_KV_HEADS = 2
Q_GROUP = N_Q_HEADS // N_KV_HEADS
ATTN_WIDTH = N_Q_HEADS * HEAD_DIM
KV_WIDTH = N_KV_HEADS * HEAD_DIM
ROPE_THETA = 10000.0
ROPE_AXIS_PAIRS = HEAD_DIM // 4
N_HGRN_HEADS = 8
HGRN_DIM = 128
HGRN_WIDTH = N_HGRN_HEADS * HGRN_DIM
N_EXPERTS = 16
EC_CAPACITY = 2
EXPERT_FF = 1024
NORM_EPS = 1e-6
IN_WIDTH = 10752

QKV_WIDTH = ATTN_WIDTH + 2 * KV_WIDTH
COL_QKV = 0
COL_HQ = COL_QKV + QKV_WIDTH
COL_ZF = COL_HQ + HGRN_WIDTH
COL_ZB = COL_ZF + HGRN_WIDTH
COL_HI = COL_ZB + HGRN_WIDTH
COL_HOG = COL_HI + HGRN_WIDTH
COL_GA = COL_HOG + HGRN_WIDTH
COL_GB = COL_GA + D_MODEL

ATTN_TK = 2048
V_EXT = 2 * HEAD_DIM

HGRN_CHUNK = 64
HGRN_SUB = 16
HGRN_HEADS_PER_STEP = 8
EXP_CLAMP = 80.0

META_WIDTH = 128
COMBINE_TM = 256
WIN_BLK = 16
WIN_BLKS = 4
WIN_ROWS = WIN_BLK * WIN_BLKS
COMBINE_ROUNDS = -(-(WIN_BLK - 1 + COMBINE_TM) // WIN_ROWS)

WEIGHT_CAST_ROWS = 256

V7X_VMEM_BYTES = 64 * 1024 * 1024


def _cparams(n_grid, vmem_mb):
    assert vmem_mb * 1024 * 1024 < V7X_VMEM_BYTES
    return pltpu.CompilerParams(
        dimension_semantics=("arbitrary",) * n_grid,
        vmem_limit_bytes=vmem_mb * 1024 * 1024,
    )


def _sigmoid(x):
    return 1.0 / (1.0 + jnp.exp(-x))


def _rms(x, gain):
    return x * lax.rsqrt(jnp.mean(x * x, axis=-1, keepdims=True) + NORM_EPS) * gain


def _dot_nt(a, b):
    return lax.dot_general(a, b, (((1,), (1,)), ((), ())), preferred_element_type=F32)


def _dot_tn(a, b):
    return lax.dot_general(a, b, (((0,), (0,)), ((), ())), preferred_element_type=F32)


def _split_rows_specs(tm, D, n_first):
    return [pl.BlockSpec((tm, D), lambda i: (jnp.minimum(i, n_first - 1), 0)),
            pl.BlockSpec((tm, D), lambda i: (jnp.maximum(i - n_first, 0), 0))]


def _pick_rows(xa_ref, xb_ref, n_first):
    return jnp.where(pl.program_id(0) < n_first, xa_ref[...], xb_ref[...])


def _norm2_body(xa_ref, xb_ref, g_ref, o_ref, *, n_first):
    o_ref[...] = _rms(_pick_rows(xa_ref, xb_ref, n_first), g_ref[...]).astype(o_ref.dtype)


def rms_norm_rows2(xa, xb, gain, out_dtype, tm=512):
    D = xa.shape[1]
    T = xa.shape[0] + xb.shape[0]
    n_first = xa.shape[0] // tm
    return pl.pallas_call(
        functools.partial(_norm2_body, n_first=n_first),
        grid=(T // tm,),
        in_specs=_split_rows_specs(tm, D, n_first) + [pl.BlockSpec((1, D), lambda i: (0, 0))],
        out_specs=pl.BlockSpec((tm, D), lambda i: (i, 0)),
        out_shape=jax.ShapeDtypeStruct((T, D), out_dtype),
        compiler_params=_cparams(1, 32),
        name="rms_norm_rows2",
    )(xa, xb, gain.reshape(1, D))


def _mm_body(*refs, n_first):
    *a_refs, w_f32, o_ref, w_ref = refs
    i = pl.program_id(1)

    @pl.when(i == 0)
    def _():
        def cast_rows(r, carry):
            rows = pl.ds(pl.multiple_of(r * WEIGHT_CAST_ROWS, WEIGHT_CAST_ROWS), WEIGHT_CAST_ROWS)
            w_ref[rows, :] = w_f32[rows, :].astype(BF16)
            return carry
        lax.fori_loop(0, w_f32.shape[0] // WEIGHT_CAST_ROWS, cast_rows, 0)

    if len(a_refs) == 1:
        o_ref[...] = jnp.dot(a_refs[0][...], w_ref[...], preferred_element_type=F32).astype(o_ref.dtype)
        return

    @pl.when(i < n_first)
    def _():
        o_ref[...] = jnp.dot(a_refs[0][...], w_ref[...], preferred_element_type=F32).astype(o_ref.dtype)

    @pl.when(i >= n_first)
    def _():
        o_ref[...] = jnp.dot(a_refs[1][...], w_ref[...], preferred_element_type=F32).astype(o_ref.dtype)


def in_proj(a_parts, w, layer, out_dtype, tm=512, tn=1536):
    K = a_parts[0].shape[1]
    n_first = a_parts[0].shape[0] // tm
    M = sum(a.shape[0] for a in a_parts)
    N = w.shape[2]
    if len(a_parts) == 1:
        a_specs = [pl.BlockSpec((tm, K), lambda j, i: (i, 0))]
    else:
        a_specs = [pl.BlockSpec((tm, K), lambda j, i: (jnp.minimum(i, n_first - 1), 0)),
                   pl.BlockSpec((tm, K), lambda j, i: (jnp.maximum(i - n_first, 0), 0))]
    return pl.pallas_call(
        functools.partial(_mm_body, n_first=n_first),
        grid=(N // tn, M // tm),
        in_specs=a_specs + [pl.BlockSpec((None, K, tn), lambda j, i: (layer, 0, j))],
        out_specs=pl.BlockSpec((tm, tn), lambda j, i: (i, j)),
        out_shape=jax.ShapeDtypeStruct((M, N), out_dtype),
        scratch_shapes=[pltpu.VMEM((K, tn), BF16)],
        compiler_params=_cparams(2, 56),
        name="in_proj",
    )(*a_parts, w)


def _col_window(rows, width, col, row_block):
    return pl.BlockSpec((pl.Element(rows), pl.Element(width)), lambda *g: (row_block(*g) * rows, col))


def _qkprep_body(p_ref, cos_ref, sin_ref, qg_ref, kg_ref, q_ref, k_ref, v_ref, *, scale):
    cos = cos_ref[...]
    sin = sin_ref[...]
    lane = lax.broadcasted_iota(jnp.int32, cos.shape, 1)
    first = (lane & ROPE_AXIS_PAIRS) == 0
    xs = [p_ref[:, h * HEAD_DIM:(h + 1) * HEAD_DIM] for h in range(N_Q_HEADS + N_KV_HEADS)]
    gains = [qg_ref[...] * scale] * N_Q_HEADS + [kg_ref[...]] * N_KV_HEADS
    ms = [jnp.mean(x * x, axis=-1, keepdims=True) for x in xs]
    ys = [x * lax.rsqrt(m + NORM_EPS) * g for x, m, g in zip(xs, ms, gains)]
    ups = [pltpu.roll(y, HEAD_DIM - ROPE_AXIS_PAIRS, 1) for y in ys]
    downs = [pltpu.roll(y, ROPE_AXIS_PAIRS, 1) for y in ys]
    outs = [y * cos + jnp.where(first, u, d) * sin for y, u, d in zip(ys, ups, downs)]
    for h in range(N_Q_HEADS):
        q_ref[:, h * HEAD_DIM:(h + 1) * HEAD_DIM] = outs[h].astype(q_ref.dtype)
    for h in range(N_KV_HEADS):
        k_ref[:, h * HEAD_DIM:(h + 1) * HEAD_DIM] = outs[N_Q_HEADS + h].astype(k_ref.dtype)
    ones_col = jnp.where(lax.broadcasted_iota(jnp.int32, (cos.shape[0], V_EXT - HEAD_DIM), 1) == 0, 1.0, 0.0)
    for h in range(N_KV_HEADS):
        src = ATTN_WIDTH + KV_WIDTH + h * HEAD_DIM
        v_ref[:, h * V_EXT:h * V_EXT + HEAD_DIM] = p_ref[:, src:src + HEAD_DIM].astype(v_ref.dtype)
        v_ref[:, h * V_EXT + HEAD_DIM:(h + 1) * V_EXT] = ones_col.astype(v_ref.dtype)


def qk_prep(proj, cos, sin, q_gain, k_gain, seq, tm=512):
    T = proj.shape[0]
    nblk = seq // tm
    return pl.pallas_call(
        functools.partial(_qkprep_body, scale=HEAD_DIM ** -0.5 * math.log2(math.e)),
        grid=(T // tm,),
        in_specs=[
            pl.BlockSpec((tm, QKV_WIDTH), lambda i: (i, COL_QKV // QKV_WIDTH)),
            pl.BlockSpec((tm, HEAD_DIM), lambda i: (i % nblk, 0)),
            pl.BlockSpec((tm, HEAD_DIM), lambda i: (i % nblk, 0)),
            pl.BlockSpec((1, HEAD_DIM), lambda i: (0, 0)),
            pl.BlockSpec((1, HEAD_DIM), lambda i: (0, 0)),
        ],
        out_specs=[
            pl.BlockSpec((tm, ATTN_WIDTH), lambda i: (i, 0)),
            pl.BlockSpec((tm, KV_WIDTH), lambda i: (i, 0)),
            pl.BlockSpec((tm, N_KV_HEADS * V_EXT), lambda i: (i, 0)),
        ],
        out_shape=[
            jax.ShapeDtypeStruct((T, ATTN_WIDTH), BF16),
            jax.ShapeDtypeStruct((T, KV_WIDTH), BF16),
            jax.ShapeDtypeStruct((T, N_KV_HEADS * V_EXT), BF16),
        ],
        compiler_params=_cparams(1, 32),
        name="qk_prep",
    )(proj, cos, sin, q_gain.reshape(1, HEAD_DIM), k_gain.reshape(1, HEAD_DIM))


def rope_tables(seq):
    rows = seq // GRID_W
    row = jnp.repeat(jnp.arange(rows, dtype=F32), GRID_W)
    col = jnp.tile(jnp.arange(GRID_W, dtype=F32), rows)
    inv_freq = 1.0 / (ROPE_THETA ** (jnp.arange(ROPE_AXIS_PAIRS, dtype=F32) / ROPE_AXIS_PAIRS))
    ang_r = row[:, None] * inv_freq
    ang_c = col[:, None] * inv_freq
    cos = jnp.concatenate([jnp.cos(ang_r), jnp.cos(ang_r), jnp.cos(ang_c), jnp.cos(ang_c)], axis=1)
    sin = jnp.concatenate([-jnp.sin(ang_r), jnp.sin(ang_r), -jnp.sin(ang_c), jnp.sin(ang_c)], axis=1)
    return cos, sin


def _attn_body(q_ref, k_ref, v_ref, o_ref):
    n_chunks = k_ref.shape[0] // ATTN_TK
    sls = [slice(g * HEAD_DIM, (g + 1) * HEAD_DIM) for g in range(Q_GROUP)]
    qs = [q_ref[:, sl] for sl in sls]
    ms = accs = None
    for c in range(n_chunks):
        k = k_ref[c * ATTN_TK:(c + 1) * ATTN_TK, :]
        v = v_ref[c * ATTN_TK:(c + 1) * ATTN_TK, :]
        ss = [_dot_nt(q, k) for q in qs]
        new_ms = [jnp.max(s, axis=-1, keepdims=True) for s in ss]
        if c > 0:
            new_ms = [jnp.maximum(m, cm) for m, cm in zip(ms, new_ms)]
        ps = [jnp.exp2(s - m).astype(BF16) for s, m in zip(ss, new_ms)]
        pvs = [jnp.dot(p, v, preferred_element_type=F32) for p in ps]
        if c == 0:
            accs = pvs
        else:
            alphas = [jnp.exp2(m - nm) for m, nm in zip(ms, new_ms)]
            accs = [acc * a + pv for acc, a, pv in zip(accs, alphas, pvs)]
        ms = new_ms
    for sl, acc in zip(sls, accs):
        o_ref[:, sl] = (acc[:, :HEAD_DIM] / acc[:, HEAD_DIM:HEAD_DIM + 1]).astype(o_ref.dtype)


def attention(q, k, v, seq, tq=512):
    T = q.shape[0]
    n_seq = T // seq
    nq = seq // tq
    gw = Q_GROUP * HEAD_DIM
    return pl.pallas_call(
        _attn_body,
        grid=(n_seq, N_KV_HEADS, nq),
        in_specs=[
            pl.BlockSpec((tq, gw), lambda b, h, i: (b * nq + i, h)),
            pl.BlockSpec((seq, HEAD_DIM), lambda b, h, i: (b, h)),
            pl.BlockSpec((seq, V_EXT), lambda b, h, i: (b, h)),
        ],
        out_specs=pl.BlockSpec((tq, gw), lambda b, h, i: (b * nq + i, h)),
        out_shape=jax.ShapeDtypeStruct((T, ATTN_WIDTH), BF16),
        compiler_params=_cparams(3, 48),
        name="attention",
    )(q, k, v)


def _split3(x):
    hi = x.astype(BF16)
    r1 = x - hi.astype(F32)
    mid = r1.astype(BF16)
    lo = (r1 - mid.astype(F32)).astype(BF16)
    return hi, mid, lo


def _hgrn_gates(z, loglb, log1mlb, omlb):
    e = jnp.exp(-jnp.abs(z))
    r = 1.0 / (1.0 + e)
    kk = omlb * jnp.where(z >= 0, e * r, r)
    cc = log1mlb + (jnp.minimum(z, 0.0) - jnp.log(1.0 + e))
    g = jnp.maximum(loglb, cc) + jnp.log(1.0 + jnp.exp(-jnp.abs(loglb - cc)))
    return g, kk


def _hgrn_body(qf_ref, zf_ref, vf_ref, qb_ref, zb_ref, vb_ref, par_ref, of_ref, ob_ref, stf_ref, stb_ref):
    C, SB, W = HGRN_CHUNK, HGRN_SUB, HGRN_DIM
    NH = HGRN_HEADS_PER_STEP

    @pl.when(pl.program_id(2) == 0)
    def _():
        stf_ref[...] = jnp.zeros(stf_ref.shape, F32)
        stb_ref[...] = jnp.zeros(stb_ref.shape, F32)

    row = lax.broadcasted_iota(jnp.int32, (C, C), 0)
    col = lax.broadcasted_iota(jnp.int32, (C, C), 1)
    dirs = (
        (True, qf_ref, zf_ref, vf_ref, 0, jnp.where(col <= row, 1.0, 0.0).astype(BF16), stf_ref, of_ref),
        (False, qb_ref, zb_ref, vb_ref, 3, jnp.where(col >= row, 1.0, 0.0).astype(BF16), stb_ref, ob_ref),
    )
    gs, kks = [], []
    for fwd, q_ref, z_ref, v_ref, p0, tri, st_ref, o_ref in dirs:
        g, kk = _hgrn_gates(z_ref[...], par_ref[p0:p0 + 1, :], par_ref[p0 + 1:p0 + 2, :], par_ref[p0 + 2:p0 + 3, :])
        gs.append(g)
        kks.append(kk)
    bs = []
    for (fwd, q_ref, z_ref, v_ref, p0, tri, st_ref, o_ref), g in zip(dirs, gs):
        hi, mid, lo = _split3(g)
        bs.append(jnp.dot(tri, hi, preferred_element_type=F32) + jnp.dot(tri, mid, preferred_element_type=F32)
                  + jnp.dot(tri, lo, preferred_element_type=F32))
    work = []
    for (fwd, q_ref, z_ref, v_ref, p0, tri, st_ref, o_ref), b, kk in zip(dirs, bs, kks):
        q = q_ref[...]
        vb = v_ref[...].astype(BF16)
        tot = b[C - 1:C, :] if fwd else b[0:1, :]
        q_in = (q * jnp.exp(b)).astype(BF16)
        k_st = (kk * jnp.exp(tot - b)).astype(BF16)
        dec = jnp.exp(tot)
        subs = []
        for i in range(C // SB):
            r0 = i * SB
            if fwd:
                ref = b[r0 - 1:r0, :] if i > 0 else jnp.zeros((1, NH * W), F32)
                k0, k1 = 0, r0 + SB
            else:
                ref = b[r0 + SB:r0 + SB + 1, :] if r0 + SB < C else jnp.zeros((1, NH * W), F32)
                k0, k1 = r0, C
            qs = (q[r0:r0 + SB] * jnp.exp(b[r0:r0 + SB] - ref)).astype(BF16)
            ks = (kk[k0:k1] * jnp.exp(jnp.minimum(ref - b[k0:k1], EXP_CLAMP))).astype(BF16)
            subs.append((r0, k0, k1, qs, ks))
        work.append((fwd, vb, q_in, k_st, dec, subs, st_ref, o_ref))
    res = []
    for fwd, vb, q_in, k_st, dec, subs, st_ref, o_ref in work:
        for h in range(NH):
            sl = slice(h * W, (h + 1) * W)
            st = st_ref[h]
            o_inter = _dot_nt(q_in[:, sl], st.astype(BF16))
            st_ref[h] = st * dec[:, sl] + _dot_tn(vb[:, sl], k_st[:, sl])
            scores = [_dot_nt(qs[:, sl], ks[:, sl]) for (r0, k0, k1, qs, ks) in subs]
            res.append((fwd, h, o_inter, scores, vb, subs, o_ref))
    masked = []
    for fwd, h, o_inter, scores, vb, subs, o_ref in res:
        ms = []
        for s, (r0, k0, k1, qs, ks) in zip(scores, subs):
            t_idx = r0 + lax.broadcasted_iota(jnp.int32, s.shape, 0)
            s_idx = k0 + lax.broadcasted_iota(jnp.int32, s.shape, 1)
            keep = (s_idx <= t_idx) if fwd else (s_idx >= t_idx)
            ms.append(jnp.where(keep, s, 0.0).astype(BF16))
        masked.append(ms)
    for (fwd, h, o_inter, scores, vb, subs, o_ref), ms in zip(res, masked):
        sl = slice(h * W, (h + 1) * W)
        outs = [jnp.dot(m, vb[k0:k1, sl], preferred_element_type=F32) for m, (r0, k0, k1, qs, ks) in zip(ms, subs)]
        o_ref[:, sl] = o_inter + jnp.concatenate(outs, axis=0)


def hgrn_scan(proj, params, seq):
    T = proj.shape[0]
    n_seq = T // seq
    C = HGRN_CHUNK
    nc = seq // C
    w = HGRN_HEADS_PER_STEP * HGRN_DIM
    n_hh = HGRN_WIDTH // w

    assert n_hh == 1

    def fwd_spec(col):
        return _col_window(C, w, col, lambda b, hh, j: b * nc + j)

    def bwd_spec(col):
        return _col_window(C, w, col, lambda b, hh, j: b * nc + nc - 1 - j)

    return pl.pallas_call(
        _hgrn_body,
        grid=(n_seq, n_hh, nc),
        in_specs=[
            fwd_spec(COL_HQ), fwd_spec(COL_ZF), fwd_spec(COL_HI),
            bwd_spec(COL_HQ), bwd_spec(COL_ZB), bwd_spec(COL_HI),
            pl.BlockSpec((8, w), lambda b, hh, j: (0, hh)),
        ],
        out_specs=[
            pl.BlockSpec((C, w), lambda b, hh, j: (b * nc + j, hh)),
            pl.BlockSpec((C, w), lambda b, hh, j: (b * nc + nc - 1 - j, hh)),
        ],
        out_shape=[jax.ShapeDtypeStruct((T, HGRN_WIDTH), F32), jax.ShapeDtypeStruct((T, HGRN_WIDTH), F32)],
        scratch_shapes=[
            pltpu.VMEM((HGRN_HEADS_PER_STEP, HGRN_DIM, HGRN_DIM), F32),
            pltpu.VMEM((HGRN_HEADS_PER_STEP, HGRN_DIM, HGRN_DIM), F32),
        ],
        compiler_params=_cparams(3, 32),
        name="hgrn_scan",
    )(proj, proj, proj, proj, proj, proj, params)


def hgrn_params(lower_bounds, layer):
    lb_all = jnp.cumsum(jax.nn.softmax(lower_bounds.astype(F32), axis=1), axis=1)
    lb_all = lb_all - lb_all[:, :1]
    rows = []
    for d in range(2):
        lb = lb_all[d, layer]
        rows += [jnp.log(lb), jnp.log1p(-lb), 1.0 - lb]
    rows += [jnp.zeros_like(rows[0])] * 2
    return jnp.stack(rows, axis=0)


def _merge_body(attn_ref, of_ref, ob_ref, hog_ref, ga_ref, gb_ref, hg_ref, wa_ref, wh_ref, o_ref):
    ya = jnp.dot(attn_ref[...], wa_ref[...], preferred_element_type=F32)
    hs = []
    for h in range(N_HGRN_HEADS):
        sl = slice(h * HGRN_DIM, (h + 1) * HGRN_DIM)
        o = _rms(of_ref[:, sl] + ob_ref[:, sl], hg_ref[...])
        og = hog_ref[:, sl]
        hs.append((o * (og * _sigmoid(og))).astype(BF16))
    yh = jnp.dot(jnp.concatenate(hs, axis=1), wh_ref[...], preferred_element_type=F32)
    o_ref[...] = (_sigmoid(ga_ref[...]) * ya + _sigmoid(gb_ref[...]) * yh).astype(o_ref.dtype)


def merge_branches(attn, o_f, o_b, proj, hgrn_gain, w_attn, w_hgrn, layer, tm=256):
    T = attn.shape[0]
    D = D_MODEL
    return pl.pallas_call(
        _merge_body,
        grid=(T // tm,),
        in_specs=[
            pl.BlockSpec((tm, ATTN_WIDTH), lambda i: (i, 0)),
            pl.BlockSpec((tm, HGRN_WIDTH), lambda i: (i, 0)),
            pl.BlockSpec((tm, HGRN_WIDTH), lambda i: (i, 0)),
            _col_window(tm, HGRN_WIDTH, COL_HOG, lambda i: i),
            _col_window(tm, D, COL_GA, lambda i: i),
            _col_window(tm, D, COL_GB, lambda i: i),
            pl.BlockSpec((1, HGRN_DIM), lambda i: (0, 0)),
            pl.BlockSpec((None, ATTN_WIDTH, D), lambda i: (layer, 0, 0)),
            pl.BlockSpec((None, HGRN_WIDTH, D), lambda i: (layer, 0, 0)),
        ],
        out_specs=pl.BlockSpec((tm, D), lambda i: (i, 0)),
        out_shape=jax.ShapeDtypeStruct((T, D), BF16),
        compiler_params=_cparams(1, 56),
        name="merge_branches",
    )(attn, o_f, o_b, proj, proj, proj, hgrn_gain.reshape(1, HGRN_DIM), w_attn, w_hgrn)


def _outproj_body(m_ref, w_ref, xa_ref, xb_ref, g_ref, wr_ref, xo_ref, h_ref, aff_ref, *, n_first):
    xn = _pick_rows(xa_ref, xb_ref, n_first) + jnp.dot(m_ref[...], w_ref[...], preferred_element_type=F32)
    xo_ref[...] = xn
    h = _rms(xn, g_ref[...])
    h_ref[...] = h.astype(h_ref.dtype)
    hb = h.astype(BF16)
    hl = (h - hb.astype(F32)).astype(BF16)
    both = jnp.dot(hb, wr_ref[...], preferred_element_type=F32)
    logits = (both[:, :N_EXPERTS] + both[:, N_EXPERTS:]
              + jnp.dot(hl, wr_ref[:, :N_EXPERTS], preferred_element_type=F32))
    mx = jnp.max(logits, axis=1, keepdims=True)
    ex = jnp.exp(logits - mx)
    aff_ref[...] = ex / jnp.sum(ex, axis=1, keepdims=True)


def out_proj_router(merged, w_out, layer, xa, xb, ffn_gain, w_router, tm=256):
    D = xa.shape[1]
    n_first = xa.shape[0] // tm
    xb = xa if xb is None else xb
    T = merged.shape[0]
    wr_hi = w_router.astype(BF16)
    wr_lo = (w_router - wr_hi.astype(F32)).astype(BF16)
    wr = jnp.concatenate([wr_hi, wr_lo], axis=1)
    return pl.pallas_call(
        functools.partial(_outproj_body, n_first=n_first),
        grid=(T // tm,),
        in_specs=[
            pl.BlockSpec((tm, D), lambda i: (i, 0)),
            pl.BlockSpec((None, D, D), lambda i: (layer, 0, 0)),
            *_split_rows_specs(tm, D, n_first),
            pl.BlockSpec((1, D), lambda i: (0, 0)),
            pl.BlockSpec((D, 2 * N_EXPERTS), lambda i: (0, 0)),
        ],
        out_specs=[
            pl.BlockSpec((tm, D), lambda i: (i, 0)),
            pl.BlockSpec((tm, D), lambda i: (i, 0)),
            pl.BlockSpec((tm, N_EXPERTS), lambda i: (i, 0)),
        ],
        out_shape=[
            jax.ShapeDtypeStruct((T, D), F32),
            jax.ShapeDtypeStruct((T, D), BF16),
            jax.ShapeDtypeStruct((T, N_EXPERTS), F32),
        ],
        compiler_params=_cparams(1, 48),
        name="out_proj_router",
    )(merged, w_out, xa, xb, ffn_gain.reshape(1, D), wr)


ROUTE_LANES = 128
NOT_SELECTED = 2 ** 30


def _route_body(a_ref, key_ref, rowpref_ref, *, cap, base_tok):
    a = a_ref[...]
    E, nc, L = a.shape
    bits = pltpu.bitcast(a, jnp.int32)

    def count(mask):
        ones = jnp.where(mask, 1.0, 0.0)
        return jnp.sum(jnp.sum(ones, axis=1, keepdims=True), axis=2, keepdims=True)

    def bisect(_, carry):
        lo, hi = carry
        mid = lo + jnp.right_shift(hi - lo, 1)
        ge = count(bits >= mid) >= cap
        return jnp.where(ge, mid, lo), jnp.where(ge, hi, mid)

    lo0 = jnp.zeros((E, 1, 1), jnp.int32)
    hi0 = jnp.full((E, 1, 1), 0x7F800000, jnp.int32)
    thr, _ = lax.fori_loop(0, 31, bisect, (lo0, hi0))

    s_i = lax.broadcasted_iota(jnp.int32, (L, L), 0)
    t_i = lax.broadcasted_iota(jnp.int32, (L, L), 1)
    incl_lane = jnp.where(s_i <= t_i, 1.0, 0.0).astype(BF16)
    r_i = lax.broadcasted_iota(jnp.int32, (E, nc, nc), 1)
    c_i = lax.broadcasted_iota(jnp.int32, (E, nc, nc), 2)
    rows_before = jnp.where(c_i < r_i, 1.0, 0.0).astype(BF16)

    def prefix(mask):
        ones = jnp.where(mask, 1.0, 0.0)
        incl = jnp.dot(ones.astype(BF16).reshape(E * nc, L), incl_lane, preferred_element_type=F32).reshape(E, nc, L)
        row_tot = jnp.broadcast_to(incl[:, :, L - 1:L], (E, nc, L)).astype(BF16)
        before = lax.dot_general(rows_before, row_tot, (((2,), (1,)), ((0,), (0,))), preferred_element_type=F32)
        return ones, incl, before

    gt = bits > thr
    eq = bits == thr
    need_eq = cap - count(gt)
    eq_f, eq_incl, eq_before = prefix(eq)
    take_eq = jnp.where(eq_before + eq_incl - eq_f < need_eq, eq_f, 0.0)
    sel = jnp.where(gt, 1.0, take_eq) > 0.5
    _, _, sel_before = prefix(sel)
    tok = (base_tok + lax.broadcasted_iota(jnp.int32, (E, nc, L), 1) * L
           + lax.broadcasted_iota(jnp.int32, (E, nc, L), 2))
    key_ref[...] = jnp.where(sel, tok, NOT_SELECTED)
    rowpref_ref[...] = sel_before.astype(jnp.int32)


def route_select(aff_group_t, cap, base_tok):
    E, n_tok = aff_group_t.shape
    nc = n_tok // ROUTE_LANES
    a3 = aff_group_t.reshape(E, nc, ROUTE_LANES)
    keys, rowpref = pl.pallas_call(
        functools.partial(_route_body, cap=cap, base_tok=base_tok),
        out_shape=[jax.ShapeDtypeStruct((E, nc, ROUTE_LANES), jnp.int32)] * 2,
        compiler_params=pltpu.CompilerParams(vmem_limit_bytes=48 * 1024 * 1024),
        name="route_select",
    )(a3)
    return keys.reshape(E, n_tok), rowpref[:, :, 0]


FFN_SPLITS = 2


def _ffn_body(x_ref, gv_ref, tok_ref, wg_hbm, wu_hbm, wd_hbm, o_ref,
              wg_f32, wu_f32, wd_f32, wg_bf, wu_bf, wd_bf, sem_ref, *, layer, first_expert):
    e = pl.program_id(0)
    n_experts = pl.num_programs(0)

    def weight_copies(local_expert):
        expert = first_expert + local_expert
        return (pltpu.make_async_copy(wg_hbm.at[layer, expert], wg_f32, sem_ref.at[0]),
                pltpu.make_async_copy(wu_hbm.at[layer, expert], wu_f32, sem_ref.at[1]),
                pltpu.make_async_copy(wd_hbm.at[layer, expert], wd_f32, sem_ref.at[2]))

    @pl.when(pl.program_id(1) == 0)
    def _():
        @pl.when(e == 0)
        def _():
            for c in weight_copies(e):
                c.start()

        for c in weight_copies(e):
            c.wait()
        for src, dst in ((wg_f32, wg_bf), (wu_f32, wu_bf), (wd_f32, wd_bf)):
            def cast_rows(r, carry, src=src, dst=dst):
                rows = pl.ds(pl.multiple_of(r * WEIGHT_CAST_ROWS, WEIGHT_CAST_ROWS), WEIGHT_CAST_ROWS)
                dst[rows, :] = src[rows, :].astype(BF16)
                return carry
            lax.fori_loop(0, src.shape[0] // WEIGHT_CAST_ROWS, cast_rows, 0)

        @pl.when(e + 1 < n_experts)
        def _():
            for c in weight_copies(e + 1):
                c.start()

    x = x_ref[0]
    g = jnp.dot(x, wg_bf[...], preferred_element_type=F32)
    u = jnp.dot(x, wu_bf[...], preferred_element_type=F32)
    a = (g * _sigmoid(g) * u).astype(BF16)
    y = jnp.dot(a, wd_bf[...], preferred_element_type=F32) * gv_ref[0]
    o_ref[0, :, :D_MODEL] = y.astype(o_ref.dtype)
    tok = tok_ref[0]
    lane = lax.broadcasted_iota(jnp.int32, (tok.shape[0], META_WIDTH), 1)
    meta = jnp.where(lane == 0, jnp.right_shift(tok, 7), jnp.where(lane == 1, jnp.bitwise_and(tok, 127), 0))
    o_ref[0, :, D_MODEL:] = meta.astype(F32).astype(o_ref.dtype)


def expert_ffn(xe, w_gate, w_up, w_down, layer, first_expert, gate_vals, tok, tm=512):
    E, rows, D = xe.shape
    F = w_gate.shape[-1]
    return pl.pallas_call(
        functools.partial(_ffn_body, layer=layer, first_expert=first_expert),
        grid=(E, rows // tm),
        in_specs=[
            pl.BlockSpec((1, tm, D), lambda e, i: (e, i, 0)),
            pl.BlockSpec((1, tm, 1), lambda e, i: (first_expert + e, i, 0)),
            pl.BlockSpec((1, tm, 1), lambda e, i: (first_expert + e, i, 0)),
            pl.BlockSpec(memory_space=pl.ANY),
            pl.BlockSpec(memory_space=pl.ANY),
            pl.BlockSpec(memory_space=pl.ANY),
        ],
        out_specs=pl.BlockSpec((1, tm, D + META_WIDTH), lambda e, i: (e, i, 0)),
        out_shape=jax.ShapeDtypeStruct((E, rows, D + META_WIDTH), BF16),
        scratch_shapes=[
            pltpu.VMEM((D, F), F32), pltpu.VMEM((D, F), F32), pltpu.VMEM((F, D), F32),
            pltpu.VMEM((D, F), BF16), pltpu.VMEM((D, F), BF16), pltpu.VMEM((F, D), BF16),
            pltpu.SemaphoreType.DMA((3,)),
        ],
        compiler_params=_cparams(2, 56),
        name="expert_ffn",
    )(xe, gate_vals[..., None], tok[..., None], w_gate, w_up, w_down)


def _combine_body(win_ref, minslot_ref, nr_ref, x_ref, gain_ref, *refs, base_tile, n_tiles, n_parts, keep_x):
    ye_win = refs[:N_EXPERTS]
    ye_hbm = refs[N_EXPERTS:N_EXPERTS + n_parts]
    if keep_x:
        o_ref, h_ref, buf_ref, sem_ref = refs[N_EXPERTS + n_parts:]
    else:
        h_ref, o_ref, buf_ref, sem_ref = refs[N_EXPERTS + n_parts:]
    per_part = N_EXPERTS // n_parts
    j = pl.program_id(0)
    t0 = ((j + base_tile) * COMBINE_TM).astype(F32)

    def table_index(e, k):
        return e * (n_tiles * COMBINE_ROUNDS) + j * COMBINE_ROUNDS + k

    def placed(window, k):
        rows, toks = [], []
        for e in range(N_EXPERTS):
            w = window(e)
            meta = w[:, D_MODEL:].astype(F32)
            tok = meta[:, 0:1] * 128.0 + meta[:, 1:2] - t0
            flat = table_index(e, k)
            slot = win_ref[flat] * WIN_BLK + lax.broadcasted_iota(jnp.int32, (WIN_ROWS, 1), 0)
            toks.append(jnp.where(slot >= minslot_ref[flat], tok, -1.0))
            rows.append(w[:, :D_MODEL])
        tok_all = jnp.concatenate(toks, axis=0)
        lane_t = lax.broadcasted_iota(jnp.int32, (N_EXPERTS * WIN_ROWS, COMBINE_TM), 1).astype(F32)
        place = jnp.where(tok_all == lane_t, 1.0, 0.0).astype(BF16)
        return _dot_tn(place, jnp.concatenate(rows, axis=0))

    o_ref[...] = x_ref[...] + placed(lambda e: ye_win[e][...].reshape(WIN_ROWS, D_MODEL + META_WIDTH), 0)

    def window_copy(e, k):
        return pltpu.make_async_copy(
            ye_hbm[e // per_part].at[e % per_part, pl.ds(win_ref[table_index(e, k)], WIN_BLKS)],
            buf_ref.at[e], sem_ref.at[e])

    def extra_round(k, carry):
        for e in range(N_EXPERTS):
            window_copy(e, k).start()
        for e in range(N_EXPERTS):
            window_copy(e, k).wait()
        o_ref[...] += placed(lambda e: buf_ref[e].reshape(WIN_ROWS, D_MODEL + META_WIDTH), k)
        return carry

    lax.fori_loop(1, nr_ref[j], extra_round, 0)
    h_ref[...] = _rms(o_ref[...], gain_ref[...]).astype(h_ref.dtype)


def combine(x, ye_parts, win, minslot, n_rounds, base_tile, gain, h_dtype, keep_x):
    per_part, rows, dext = ye_parts[0].shape
    E = per_part * len(ye_parts)
    n_tiles = n_rounds.shape[0]
    ye4 = [ye.reshape(per_part, rows // WIN_BLK, WIN_BLK, dext) for ye in ye_parts]

    def ye_spec(e):
        return pl.BlockSpec(
            (pl.Element(1), pl.Element(WIN_BLKS), pl.Element(WIN_BLK), pl.Element(dext)),
            lambda j, win_r, ms_r, nr_r: (e % per_part, win_r[e * (n_tiles * COMBINE_ROUNDS) + j * COMBINE_ROUNDS], 0, 0))

    grid_spec = pltpu.PrefetchScalarGridSpec(
        num_scalar_prefetch=3,
        grid=(n_tiles,),
        in_specs=[pl.BlockSpec((COMBINE_TM, D_MODEL), lambda j, *_: (j + base_tile, 0)),
                  pl.BlockSpec((1, D_MODEL), lambda j, *_: (0, 0))]
        + [ye_spec(e) for e in range(E)]
        + [pl.BlockSpec(memory_space=pl.ANY)] * len(ye4),
        out_specs=([pl.BlockSpec((COMBINE_TM, D_MODEL), lambda j, *_: (j + base_tile, 0))] if keep_x else [])
        + [pl.BlockSpec((COMBINE_TM, D_MODEL), lambda j, *_: (j, 0))],
        scratch_shapes=([] if keep_x else [pltpu.VMEM((COMBINE_TM, D_MODEL), F32)]) + [
            pltpu.VMEM((E, WIN_BLKS, WIN_BLK, dext), BF16),
            pltpu.SemaphoreType.DMA((E,)),
        ],
    )
    outs = pl.pallas_call(
        functools.partial(_combine_body, base_tile=base_tile, n_tiles=n_tiles, n_parts=len(ye4), keep_x=keep_x),
        grid_spec=grid_spec,
        out_shape=([jax.ShapeDtypeStruct(x.shape, x.dtype)] if keep_x else [])
        + [jax.ShapeDtypeStruct((n_tiles * COMBINE_TM, D_MODEL), h_dtype)],
        input_output_aliases={3: 0} if keep_x else {},
        compiler_params=_cparams(1, 48),
        name="combine",
    )(win.reshape(-1), minslot.reshape(-1), n_rounds, x, gain.reshape(1, D_MODEL),
      *[ye4[e // per_part] for e in range(E)], *ye4)
    return outs if keep_x else (None, outs[0])


def combine_tables(rowpref, cap, slot_base):
    E = rowpref.shape[0]
    end = slot_base + cap
    starts = slot_base + jnp.concatenate(
        [rowpref[:, ::COMBINE_TM // ROUTE_LANES], jnp.full((E, 1), cap, jnp.int32)], axis=1)
    lo, hi = starts[:, :-1], starts[:, 1:]
    a = (lo // WIN_BLK) * WIN_BLK
    need = jnp.where(hi > lo, -(-(hi - a) // WIN_ROWS), 0)
    k = jnp.arange(COMBINE_ROUNDS, dtype=jnp.int32)[None, None, :]
    kk = jnp.minimum(k, jnp.maximum(need[..., None] - 1, 0))
    win = jnp.minimum(a[..., None] + WIN_ROWS * kk, end - WIN_ROWS) // WIN_BLK
    minslot = jnp.where(k < need[..., None], a[..., None] + WIN_ROWS * k, end)
    return win.astype(jnp.int32), minslot.astype(jnp.int32), jnp.max(need, axis=0).astype(jnp.int32)


def kernel(x_prompt, x_sample, w_in, q_norm, k_norm, lower_bounds, hgrn_norm, w_proj_attn, w_proj_hgrn,
           w_out, norm_mix, norm_ffn, w_router, w_gate, w_up, w_down, norm_final):
    depth = w_in.shape[0]
    groups = (x_prompt, x_sample)
    seq = x_prompt.shape[1]
    group_tokens = [g.shape[0] * g.shape[1] for g in groups]
    xa, xb = (g.reshape(-1, D_MODEL) for g in groups)
    cos, sin = rope_tables(seq)
    w_attn_bf, w_hgrn_bf, w_out_bf = (w.astype(BF16) for w in (w_proj_attn, w_proj_hgrn, w_out))

    h_parts = [rms_norm_rows2(xa, xb, norm_mix[0], BF16)]
    for l in range(depth):
        proj = in_proj(h_parts, w_in, l, F32)
        q, k, v = qk_prep(proj, cos, sin, q_norm[l], k_norm[l], seq)
        attn = attention(q, k, v, seq)
        o_f, o_b = hgrn_scan(proj, hgrn_params(lower_bounds, l), seq)
        merged = merge_branches(attn, o_f, o_b, proj, hgrn_norm[l], w_attn_bf, w_hgrn_bf, l)
        x, h_ffn, aff = out_proj_router(merged, w_out_bf, l, *((xa, xb) if l == 0 else (x, None)),
                                        norm_ffn[l], w_router[l])
        aff_t = aff.T
        idx_parts, gate_parts, tables = [], [], []
        start = slot_base = 0
        for n_tok in group_tokens:
            cap = EC_CAPACITY * n_tok // N_EXPERTS
            aff_g = aff_t[:, start:start + n_tok]
            keys, rowpref = route_select(aff_g, cap, start)
            idx = jnp.sort(keys, axis=1)[:, :cap]
            idx_parts.append(idx)
            gate_parts.append(jnp.take_along_axis(aff_g, idx - start, axis=1))
            tables.append(combine_tables(rowpref, cap, slot_base) + (start // COMBINE_TM,))
            start += n_tok
            slot_base += cap
        idx = jnp.concatenate(idx_parts, axis=1)
        gate_vals = jnp.concatenate(gate_parts, axis=1)
        per = N_EXPERTS // FFN_SPLITS
        ye = [expert_ffn(h_ffn[idx[e0:e0 + per]], w_gate, w_up, w_down, l, e0, gate_vals, idx)
              for e0 in range(0, N_EXPERTS, per)]
        last = l == depth - 1
        h_parts = []
        for win, minslot, n_rounds, base_tile in tables:
            x_next, h_g = combine(x, ye, win, minslot, n_rounds, base_tile,
                                  norm_final if last else norm_mix[l + 1], F32 if last else BF16, not last)
            x = x if last else x_next
            h_parts.append(h_g)

    return tuple(h_g.reshape(g.shape) for h_g, g in zip(h_parts, groups))
```

```python
import functools
import math

import jax
import jax.numpy as jnp
from jax import lax
from jax.experimental import pallas as pl
from jax.experimental.pallas import tpu as pltpu

F32 = jnp.float32
BF16 = jnp.bfloat16

D_MODEL = 2048
SEQ = 4096
GRID_W = 64
HEAD_DIM = 128
N_Q_HEADS = 8
N_KV_HEADS = 2
Q_GROUP = N_Q_HEADS // N_KV_HEADS
ATTN_WIDTH = N_Q_HEADS * HEAD_DIM
KV_WIDTH = N_KV_HEADS * HEAD_DIM
ROPE_THETA = 10000.0
ROPE_AXIS_PAIRS = HEAD_DIM // 4
N_HGRN_HEADS = 8
HGRN_DIM = 128
HGRN_WIDTH = N_HGRN_HEADS * HGRN_DIM
N_EXPERTS = 16
EC_CAPACITY = 2
EXPERT_FF = 1024
NORM_EPS = 1e-6
IN_WIDTH = 10752

QKV_WIDTH = ATTN_WIDTH + 2 * KV_WIDTH
IN_PROJ_TN = QKV_WIDTH
Z_BLOCKS = (1, 2)
B_BLOCKS = (0, 3, 4, 5, 6)
ZCOL_HQ = 0
ZCOL_ZF = ZCOL_HQ + HGRN_WIDTH
ZCOL_ZB = ZCOL_ZF + HGRN_WIDTH
BCOL_QKV = 0
BCOL_HI = BCOL_QKV + QKV_WIDTH
BCOL_HOG = BCOL_HI + HGRN_WIDTH
BCOL_GA = BCOL_HOG + HGRN_WIDTH
BCOL_GB = BCOL_GA + D_MODEL

ATTN_TK = 2048
V_EXT = 2 * HEAD_DIM

HGRN_CHUNK = 64
HGRN_SUB = 16
HGRN_HEADS_PER_STEP = 8
EXP_CLAMP = 80.0

META_WIDTH = 128
COMBINE_TM = 256
WIN_BLK = 16
WIN_BLKS = 4
WIN_ROWS = WIN_BLK * WIN_BLKS
COMBINE_ROUNDS = -(-(WIN_BLK - 1 + COMBINE_TM) // WIN_ROWS)

WEIGHT_CAST_ROWS = 256

V7X_VMEM_BYTES = 64 * 1024 * 1024


def _cparams(n_grid, vmem_mb):
    assert vmem_mb * 1024 * 1024 < V7X_VMEM_BYTES
    return pltpu.CompilerParams(
        dimension_semantics=("arbitrary",) * n_grid,
        vmem_limit_bytes=vmem_mb * 1024 * 1024,
    )


def _sigmoid(x):
    return 1.0 / (1.0 + jnp.exp(-x))


def _rms(x, gain):
    return x * lax.rsqrt(jnp.mean(x * x, axis=-1, keepdims=True) + NORM_EPS) * gain


def _dot_nt(a, b):
    return lax.dot_general(a, b, (((1,), (1,)), ((), ())), preferred_element_type=F32)


def _dot_tn(a, b):
    return lax.dot_general(a, b, (((0,), (0,)), ((), ())), preferred_element_type=F32)


def _split_rows_specs(tm, D, n_first):
    return [pl.BlockSpec((tm, D), lambda i: (jnp.minimum(i, n_first - 1), 0)),
            pl.BlockSpec((tm, D), lambda i: (jnp.maximum(i - n_first, 0), 0))]


def _pick_rows(xa_ref, xb_ref, n_first):
    return jnp.where(pl.program_id(0) < n_first, xa_ref[...], xb_ref[...])


def _norm2_body(xa_ref, xb_ref, g_ref, o_ref, *, n_first):
    o_ref[...] = _rms(_pick_rows(xa_ref, xb_ref, n_first), g_ref[...]).astype(o_ref.dtype)


def rms_norm_rows2(xa, xb, gain, out_dtype, tm=512):
    D = xa.shape[1]
    T = xa.shape[0] + xb.shape[0]
    n_first = xa.shape[0] // tm
    return pl.pallas_call(
        functools.partial(_norm2_body, n_first=n_first),
        grid=(T // tm,),
        in_specs=_split_rows_specs(tm, D, n_first) + [pl.BlockSpec((1, D), lambda i: (0, 0))],
        out_specs=pl.BlockSpec((tm, D), lambda i: (i, 0)),
        out_shape=jax.ShapeDtypeStruct((T, D), out_dtype),
        compiler_params=_cparams(1, 32),
        name="rms_norm_rows2",
    )(xa, xb, gain.reshape(1, D))


def _mm_body(*refs, n_first):
    *a_refs, w_f32, o_ref, w_ref = refs
    i = pl.program_id(1)

    @pl.when(i == 0)
    def _():
        def cast_rows(r, carry):
            rows = pl.ds(pl.multiple_of(r * WEIGHT_CAST_ROWS, WEIGHT_CAST_ROWS), WEIGHT_CAST_ROWS)
            w_ref[rows, :] = w_f32[rows, :].astype(BF16)
            return carry
        lax.fori_loop(0, w_f32.shape[0] // WEIGHT_CAST_ROWS, cast_rows, 0)

    if len(a_refs) == 1:
        o_ref[...] = jnp.dot(a_refs[0][...], w_ref[...], preferred_element_type=F32).astype(o_ref.dtype)
        return

    @pl.when(i < n_first)
    def _():
        o_ref[...] = jnp.dot(a_refs[0][...], w_ref[...], preferred_element_type=F32).astype(o_ref.dtype)

    @pl.when(i >= n_first)
    def _():
        o_ref[...] = jnp.dot(a_refs[1][...], w_ref[...], preferred_element_type=F32).astype(o_ref.dtype)


def in_proj(a_parts, w, layer, col_blocks, out_dtype, tm=512):
    K = a_parts[0].shape[1]
    n_first = a_parts[0].shape[0] // tm
    M = sum(a.shape[0] for a in a_parts)
    tn = IN_PROJ_TN
    N = tn * len(col_blocks)
    assert all(b == col_blocks[1] + k for k, b in enumerate(col_blocks[1:]))

    def src_block(j):
        return jnp.where(j == 0, col_blocks[0], j + (col_blocks[1] - 1))

    if len(a_parts) == 1:
        a_specs = [pl.BlockSpec((tm, K), lambda j, i: (i, 0))]
    else:
        a_specs = [pl.BlockSpec((tm, K), lambda j, i: (jnp.minimum(i, n_first - 1), 0)),
                   pl.BlockSpec((tm, K), lambda j, i: (jnp.maximum(i - n_first, 0), 0))]
    return pl.pallas_call(
        functools.partial(_mm_body, n_first=n_first),
        grid=(N // tn, M // tm),
        in_specs=a_specs + [pl.BlockSpec((None, K, tn), lambda j, i: (layer, 0, src_block(j)))],
        out_specs=pl.BlockSpec((tm, tn), lambda j, i: (i, j)),
        out_shape=jax.ShapeDtypeStruct((M, N), out_dtype),
        scratch_shapes=[pltpu.VMEM((K, tn), BF16)],
        compiler_params=_cparams(2, 56),
        name="in_proj",
    )(*a_parts, w)


def _col_window(rows, width, col, row_block):
    return pl.BlockSpec((pl.Element(rows), pl.Element(width)), lambda *g: (row_block(*g) * rows, col))


def _qkprep_body(p_ref, cos_ref, sin_ref, qg_ref, kg_ref, q_ref, k_ref, v_ref, *, scale):
    cos = cos_ref[...]
    sin = sin_ref[...]
    lane = lax.broadcasted_iota(jnp.int32, cos.shape, 1)
    first = (lane & ROPE_AXIS_PAIRS) == 0
    xs = [p_ref[:, h * HEAD_DIM:(h + 1) * HEAD_DIM].astype(F32) for h in range(N_Q_HEADS + N_KV_HEADS)]
    gains = [qg_ref[...] * scale] * N_Q_HEADS + [kg_ref[...]] * N_KV_HEADS
    ms = [jnp.mean(x * x, axis=-1, keepdims=True) for x in xs]
    ys = [x * lax.rsqrt(m + NORM_EPS) * g for x, m, g in zip(xs, ms, gains)]
    ups = [pltpu.roll(y, HEAD_DIM - ROPE_AXIS_PAIRS, 1) for y in ys]
    downs = [pltpu.roll(y, ROPE_AXIS_PAIRS, 1) for y in ys]
    outs = [y * cos + jnp.where(first, u, d) * sin for y, u, d in zip(ys, ups, downs)]
    for h in range(N_Q_HEADS):
        q_ref[:, h * HEAD_DIM:(h + 1) * HEAD_DIM] = outs[h].astype(q_ref.dtype)
    for h in range(N_KV_HEADS):
        k_ref[:, h * HEAD_DIM:(h + 1) * HEAD_DIM] = outs[N_Q_HEADS + h].astype(k_ref.dtype)
    ones_col = jnp.where(lax.broadcasted_iota(jnp.int32, (cos.shape[0], V_EXT - HEAD_DIM), 1) == 0, 1.0, 0.0)
    for h in range(N_KV_HEADS):
        src = ATTN_WIDTH + KV_WIDTH + h * HEAD_DIM
        v_ref[:, h * V_EXT:h * V_EXT + HEAD_DIM] = p_ref[:, src:src + HEAD_DIM].astype(v_ref.dtype)
        v_ref[:, h * V_EXT + HEAD_DIM:(h + 1) * V_EXT] = ones_col.astype(v_ref.dtype)


def qk_prep(proj, cos, sin, q_gain, k_gain, seq, tm=512):
    T = proj.shape[0]
    nblk = seq // tm
    return pl.pallas_call(
        functools.partial(_qkprep_body, scale=HEAD_DIM ** -0.5 * math.log2(math.e)),
        grid=(T // tm,),
        in_specs=[
            pl.BlockSpec((tm, QKV_WIDTH), lambda i: (i, BCOL_QKV // QKV_WIDTH)),
            pl.BlockSpec((tm, HEAD_DIM), lambda i: (i % nblk, 0)),
            pl.BlockSpec((tm, HEAD_DIM), lambda i: (i % nblk, 0)),
            pl.BlockSpec((1, HEAD_DIM), lambda i: (0, 0)),
            pl.BlockSpec((1, HEAD_DIM), lambda i: (0, 0)),
        ],
        out_specs=[
            pl.BlockSpec((tm, ATTN_WIDTH), lambda i: (i, 0)),
            pl.BlockSpec((tm, KV_WIDTH), lambda i: (i, 0)),
            pl.BlockSpec((tm, N_KV_HEADS * V_EXT), lambda i: (i, 0)),
        ],
        out_shape=[
            jax.ShapeDtypeStruct((T, ATTN_WIDTH), BF16),
            jax.ShapeDtypeStruct((T, KV_WIDTH), BF16),
            jax.ShapeDtypeStruct((T, N_KV_HEADS * V_EXT), BF16),
        ],
        compiler_params=_cparams(1, 32),
        name="qk_prep",
    )(proj, cos, sin, q_gain.reshape(1, HEAD_DIM), k_gain.reshape(1, HEAD_DIM))


def rope_tables(seq):
    rows = seq // GRID_W
    row = jnp.repeat(jnp.arange(rows, dtype=F32), GRID_W)
    col = jnp.tile(jnp.arange(GRID_W, dtype=F32), rows)
    inv_freq = 1.0 / (ROPE_THETA ** (jnp.arange(ROPE_AXIS_PAIRS, dtype=F32) / ROPE_AXIS_PAIRS))
    ang_r = row[:, None] * inv_freq
    ang_c = col[:, None] * inv_freq
    cos = jnp.concatenate([jnp.cos(ang_r), jnp.cos(ang_r), jnp.cos(ang_c), jnp.cos(ang_c)], axis=1)
    sin = jnp.concatenate([-jnp.sin(ang_r), jnp.sin(ang_r), -jnp.sin(ang_c), jnp.sin(ang_c)], axis=1)
    return cos, sin


def _attn_body(q_ref, k_ref, v_ref, o_ref):
    n_chunks = k_ref.shape[0] // ATTN_TK
    sls = [slice(g * HEAD_DIM, (g + 1) * HEAD_DIM) for g in range(Q_GROUP)]
    qs = [q_ref[:, sl] for sl in sls]
    ms = accs = None
    for c in range(n_chunks):
        k = k_ref[c * ATTN_TK:(c + 1) * ATTN_TK, :]
        v = v_ref[c * ATTN_TK:(c + 1) * ATTN_TK, :]
        ss = [_dot_nt(q, k) for q in qs]
        new_ms = [jnp.max(s, axis=-1, keepdims=True) for s in ss]
        if c > 0:
            new_ms = [jnp.maximum(m, cm) for m, cm in zip(ms, new_ms)]
        ps = [jnp.exp2(s - m).astype(BF16) for s, m in zip(ss, new_ms)]
        pvs = [jnp.dot(p, v, preferred_element_type=F32) for p in ps]
        if c == 0:
            accs = pvs
        else:
            alphas = [jnp.exp2(m - nm) for m, nm in zip(ms, new_ms)]
            accs = [acc * a + pv for acc, a, pv in zip(accs, alphas, pvs)]
        ms = new_ms
    for sl, acc in zip(sls, accs):
        o_ref[:, sl] = (acc[:, :HEAD_DIM] / acc[:, HEAD_DIM:HEAD_DIM + 1]).astype(o_ref.dtype)


def attention(q, k, v, seq, tq=512):
    T = q.shape[0]
    n_seq = T // seq
    nq = seq // tq
    gw = Q_GROUP * HEAD_DIM
    return pl.pallas_call(
        _attn_body,
        grid=(n_seq, N_KV_HEADS, nq),
        in_specs=[
            pl.BlockSpec((tq, gw), lambda b, h, i: (b * nq + i, h)),
            pl.BlockSpec((seq, HEAD_DIM), lambda b, h, i: (b, h)),
            pl.BlockSpec((seq, V_EXT), lambda b, h, i: (b, h)),
        ],
        out_specs=pl.BlockSpec((tq, gw), lambda b, h, i: (b * nq + i, h)),
        out_shape=jax.ShapeDtypeStruct((T, ATTN_WIDTH), BF16),
        compiler_params=_cparams(3, 48),
        name="attention",
    )(q, k, v)


def _split3(x):
    hi = x.astype(BF16)
    r1 = x - hi.astype(F32)
    mid = r1.astype(BF16)
    lo = (r1 - mid.astype(F32)).astype(BF16)
    return hi, mid, lo


def _hgrn_gates(z, loglb, log1mlb, omlb):
    e = jnp.exp(-jnp.abs(z))
    r = 1.0 / (1.0 + e)
    kk = omlb * jnp.where(z >= 0, e * r, r)
    cc = log1mlb + (jnp.minimum(z, 0.0) - jnp.log(1.0 + e))
    g = jnp.maximum(loglb, cc) + jnp.log(1.0 + jnp.exp(-jnp.abs(loglb - cc)))
    return g, kk


def _hgrn_body(qf_ref, zf_ref, vf_ref, qb_ref, zb_ref, vb_ref, par_ref, of_ref, ob_ref, stf_ref, stb_ref):
    C, SB, W = HGRN_CHUNK, HGRN_SUB, HGRN_DIM
    NH = HGRN_HEADS_PER_STEP

    @pl.when(pl.program_id(2) == 0)
    def _():
        stf_ref[...] = jnp.zeros(stf_ref.shape, F32)
        stb_ref[...] = jnp.zeros(stb_ref.shape, F32)

    row = lax.broadcasted_iota(jnp.int32, (C, C), 0)
    col = lax.broadcasted_iota(jnp.int32, (C, C), 1)
    dirs = (
        (True, qf_ref, zf_ref, vf_ref, 0, jnp.where(col <= row, 1.0, 0.0).astype(BF16), stf_ref, of_ref),
        (False, qb_ref, zb_ref, vb_ref, 3, jnp.where(col >= row, 1.0, 0.0).astype(BF16), stb_ref, ob_ref),
    )
    gs, kks = [], []
    for fwd, q_ref, z_ref, v_ref, p0, tri, st_ref, o_ref in dirs:
        g, kk = _hgrn_gates(z_ref[...], par_ref[p0:p0 + 1, :], par_ref[p0 + 1:p0 + 2, :], par_ref[p0 + 2:p0 + 3, :])
        gs.append(g)
        kks.append(kk)
    bs = []
    for (fwd, q_ref, z_ref, v_ref, p0, tri, st_ref, o_ref), g in zip(dirs, gs):
        hi, mid, lo = _split3(g)
        bs.append(jnp.dot(tri, hi, preferred_element_type=F32) + jnp.dot(tri, mid, preferred_element_type=F32)
                  + jnp.dot(tri, lo, preferred_element_type=F32))
    work = []
    for (fwd, q_ref, z_ref, v_ref, p0, tri, st_ref, o_ref), b, kk in zip(dirs, bs, kks):
        q = q_ref[...]
        vb = v_ref[...].astype(BF16)
        tot = b[C - 1:C, :] if fwd else b[0:1, :]
        q_in = (q * jnp.exp(b)).astype(BF16)
        k_st = (kk * jnp.exp(tot - b)).astype(BF16)
        dec = jnp.exp(tot)
        subs = []
        for i in range(C // SB):
            r0 = i * SB
            if fwd:
                ref = b[r0 - 1:r0, :] if i > 0 else jnp.zeros((1, NH * W), F32)
                k0, k1 = 0, r0 + SB
            else:
                ref = b[r0 + SB:r0 + SB + 1, :] if r0 + SB < C else jnp.zeros((1, NH * W), F32)
                k0, k1 = r0, C
            qs = (q[r0:r0 + SB] * jnp.exp(b[r0:r0 + SB] - ref)).astype(BF16)
            ks = (kk[k0:k1] * jnp.exp(jnp.minimum(ref - b[k0:k1], EXP_CLAMP))).astype(BF16)
            subs.append((r0, k0, k1, qs, ks))
        work.append((fwd, vb, q_in, k_st, dec, subs, st_ref, o_ref))
    res = []
    for fwd, vb, q_in, k_st, dec, subs, st_ref, o_ref in work:
        for h in range(NH):
            sl = slice(h * W, (h + 1) * W)
            st = st_ref[h]
            o_inter = _dot_nt(q_in[:, sl], st.astype(BF16))
            st_ref[h] = st * dec[:, sl] + _dot_tn(vb[:, sl], k_st[:, sl])
            scores = [_dot_nt(qs[:, sl], ks[:, sl]) for (r0, k0, k1, qs, ks) in subs]
            res.append((fwd, h, o_inter, scores, vb, subs, o_ref))
    masked = []
    for fwd, h, o_inter, scores, vb, subs, o_ref in res:
        ms = []
        for s, (r0, k0, k1, qs, ks) in zip(scores, subs):
            t_idx = r0 + lax.broadcasted_iota(jnp.int32, s.shape, 0)
            s_idx = k0 + lax.broadcasted_iota(jnp.int32, s.shape, 1)
            keep = (s_idx <= t_idx) if fwd else (s_idx >= t_idx)
            ms.append(jnp.where(keep, s, 0.0).astype(BF16))
        masked.append(ms)
    for (fwd, h, o_inter, scores, vb, subs, o_ref), ms in zip(res, masked):
        sl = slice(h * W, (h + 1) * W)
        outs = [jnp.dot(m, vb[k0:k1, sl], preferred_element_type=F32) for m, (r0, k0, k1, qs, ks) in zip(ms, subs)]
        o_ref[:, sl] = (o_inter + jnp.concatenate(outs, axis=0)).astype(o_ref.dtype)


def hgrn_scan(proj_z, proj_b, params, seq):
    T = proj_z.shape[0]
    n_seq = T // seq
    C = HGRN_CHUNK
    nc = seq // C
    w = HGRN_HEADS_PER_STEP * HGRN_DIM
    n_hh = HGRN_WIDTH // w

    assert n_hh == 1

    def fwd_spec(col):
        return _col_window(C, w, col, lambda b, hh, j: b * nc + j)

    def bwd_spec(col):
        return _col_window(C, w, col, lambda b, hh, j: b * nc + nc - 1 - j)

    return pl.pallas_call(
        _hgrn_body,
        grid=(n_seq, n_hh, nc),
        in_specs=[
            fwd_spec(ZCOL_HQ), fwd_spec(ZCOL_ZF), fwd_spec(BCOL_HI),
            bwd_spec(ZCOL_HQ), bwd_spec(ZCOL_ZB), bwd_spec(BCOL_HI),
            pl.BlockSpec((8, w), lambda b, hh, j: (0, hh)),
        ],
        out_specs=[
            pl.BlockSpec((C, w), lambda b, hh, j: (b * nc + j, hh)),
            pl.BlockSpec((C, w), lambda b, hh, j: (b * nc + nc - 1 - j, hh)),
        ],
        out_shape=[jax.ShapeDtypeStruct((T, HGRN_WIDTH), BF16), jax.ShapeDtypeStruct((T, HGRN_WIDTH), BF16)],
        scratch_shapes=[
            pltpu.VMEM((HGRN_HEADS_PER_STEP, HGRN_DIM, HGRN_DIM), F32),
            pltpu.VMEM((HGRN_HEADS_PER_STEP, HGRN_DIM, HGRN_DIM), F32),
        ],
        compiler_params=_cparams(3, 32),
        name="hgrn_scan",
    )(proj_z, proj_z, proj_b, proj_z, proj_z, proj_b, params)


def hgrn_params(lower_bounds, layer):
    lb_all = jnp.cumsum(jax.nn.softmax(lower_bounds.astype(F32), axis=1), axis=1)
    lb_all = lb_all - lb_all[:, :1]
    rows = []
    for d in range(2):
        lb = lb_all[d, layer]
        rows += [jnp.log(lb), jnp.log1p(-lb), 1.0 - lb]
    rows += [jnp.zeros_like(rows[0])] * 2
    return jnp.stack(rows, axis=0)


def _merge_body(attn_ref, of_ref, ob_ref, hog_ref, ga_ref, gb_ref, hg_ref, wa_ref, wh_ref, o_ref):
    ya = jnp.dot(attn_ref[...], wa_ref[...], preferred_element_type=F32)
    hs = []
    for h in range(N_HGRN_HEADS):
        sl = slice(h * HGRN_DIM, (h + 1) * HGRN_DIM)
        o = _rms(of_ref[:, sl].astype(F32) + ob_ref[:, sl].astype(F32), hg_ref[...])
        og = hog_ref[:, sl].astype(F32)
        hs.append((o * (og * _sigmoid(og))).astype(BF16))
    yh = jnp.dot(jnp.concatenate(hs, axis=1), wh_ref[...], preferred_element_type=F32)
    o_ref[...] = (_sigmoid(ga_ref[...].astype(F32)) * ya + _sigmoid(gb_ref[...].astype(F32)) * yh).astype(o_ref.dtype)


def merge_branches(attn, o_f, o_b, proj_b, hgrn_gain, w_attn, w_hgrn, layer, tm=256):
    T = attn.shape[0]
    D = D_MODEL
    return pl.pallas_call(
        _merge_body,
        grid=(T // tm,),
        in_specs=[
            pl.BlockSpec((tm, ATTN_WIDTH), lambda i: (i, 0)),
            pl.BlockSpec((tm, HGRN_WIDTH), lambda i: (i, 0)),
            pl.BlockSpec((tm, HGRN_WIDTH), lambda i: (i, 0)),
            _col_window(tm, HGRN_WIDTH, BCOL_HOG, lambda i: i),
            _col_window(tm, D, BCOL_GA, lambda i: i),
            _col_window(tm, D, BCOL_GB, lambda i: i),
            pl.BlockSpec((1, HGRN_DIM), lambda i: (0, 0)),
            pl.BlockSpec((None, ATTN_WIDTH, D), lambda i: (layer, 0, 0)),
            pl.BlockSpec((None, HGRN_WIDTH, D), lambda i: (layer, 0, 0)),
        ],
        out_specs=pl.BlockSpec((tm, D), lambda i: (i, 0)),
        out_shape=jax.ShapeDtypeStruct((T, D), BF16),
        compiler_params=_cparams(1, 56),
        name="merge_branches",
    )(attn, o_f, o_b, proj_b, proj_b, proj_b, hgrn_gain.reshape(1, HGRN_DIM), w_attn, w_hgrn)


def _outproj_body(m_ref, w_ref, xa_ref, xb_ref, g_ref, wr_ref, xo_ref, h_ref, aff_ref, *, n_first):
    xn = _pick_rows(xa_ref, xb_ref, n_first) + jnp.dot(m_ref[...], w_ref[...], preferred_element_type=F32)
    xo_ref[...] = xn
    h = _rms(xn, g_ref[...])
    h_ref[...] = h.astype(h_ref.dtype)
    hb = h.astype(BF16)
    hl = (h - hb.astype(F32)).astype(BF16)
    both = jnp.dot(hb, wr_ref[...], preferred_element_type=F32)
    logits = (both[:, :N_EXPERTS] + both[:, N_EXPERTS:]
              + jnp.dot(hl, wr_ref[:, :N_EXPERTS], preferred_element_type=F32))
    mx = jnp.max(logits, axis=1, keepdims=True)
    ex = jnp.exp(logits - mx)
    aff_ref[...] = ex / jnp.sum(ex, axis=1, keepdims=True)


def out_proj_router(merged, w_out, layer, xa, xb, ffn_gain, w_router, tm=256):
    D = xa.shape[1]
    n_first = xa.shape[0] // tm
    xb = xa if xb is None else xb
    T = merged.shape[0]
    wr_hi = w_router.astype(BF16)
    wr_lo = (w_router - wr_hi.astype(F32)).astype(BF16)
    wr = jnp.concatenate([wr_hi, wr_lo], axis=1)
    return pl.pallas_call(
        functools.partial(_outproj_body, n_first=n_first),
        grid=(T // tm,),
        in_specs=[
            pl.BlockSpec((tm, D), lambda i: (i, 0)),
            pl.BlockSpec((None, D, D), lambda i: (layer, 0, 0)),
            *_split_rows_specs(tm, D, n_first),
            pl.BlockSpec((1, D), lambda i: (0, 0)),
            pl.BlockSpec((D, 2 * N_EXPERTS), lambda i: (0, 0)),
        ],
        out_specs=[
            pl.BlockSpec((tm, D), lambda i: (i, 0)),
            pl.BlockSpec((tm, D), lambda i: (i, 0)),
            pl.BlockSpec((tm, N_EXPERTS), lambda i: (i, 0)),
        ],
        out_shape=[
            jax.ShapeDtypeStruct((T, D), F32),
            jax.ShapeDtypeStruct((T, D), BF16),
            jax.ShapeDtypeStruct((T, N_EXPERTS), F32),
        ],
        compiler_params=_cparams(1, 48),
        name="out_proj_router",
    )(merged, w_out, xa, xb, ffn_gain.reshape(1, D), wr)


ROUTE_LANES = 128
NOT_SELECTED = 2 ** 30


def _route_body(a_ref, key_ref, rowpref_ref, *, cap, base_tok):
    a = a_ref[...]
    E, nc, L = a.shape
    bits = pltpu.bitcast(a, jnp.int32)

    def count(mask):
        ones = jnp.where(mask, 1.0, 0.0)
        return jnp.sum(jnp.sum(ones, axis=1, keepdims=True), axis=2, keepdims=True)

    def bisect(_, carry):
        lo, hi = carry
        mid = lo + jnp.right_shift(hi - lo, 1)
        ge = count(bits >= mid) >= cap
        return jnp.where(ge, mid, lo), jnp.where(ge, hi, mid)

    lo0 = jnp.zeros((E, 1, 1), jnp.int32)
    hi0 = jnp.full((E, 1, 1), 0x7F800000, jnp.int32)
    thr, _ = lax.fori_loop(0, 31, bisect, (lo0, hi0))

    s_i = lax.broadcasted_iota(jnp.int32, (L, L), 0)
    t_i = lax.broadcasted_iota(jnp.int32, (L, L), 1)
    incl_lane = jnp.where(s_i <= t_i, 1.0, 0.0).astype(BF16)
    r_i = lax.broadcasted_iota(jnp.int32, (E, nc, nc), 1)
    c_i = lax.broadcasted_iota(jnp.int32, (E, nc, nc), 2)
    rows_before = jnp.where(c_i < r_i, 1.0, 0.0).astype(BF16)

    def prefix(mask):
        ones = jnp.where(mask, 1.0, 0.0)
        incl = jnp.dot(ones.astype(BF16).reshape(E * nc, L), incl_lane, preferred_element_type=F32).reshape(E, nc, L)
        row_tot = jnp.broadcast_to(incl[:, :, L - 1:L], (E, nc, L)).astype(BF16)
        before = lax.dot_general(rows_before, row_tot, (((2,), (1,)), ((0,), (0,))), preferred_element_type=F32)
        return ones, incl, before

    gt = bits > thr
    eq = bits == thr
    need_eq = cap - count(gt)
    eq_f, eq_incl, eq_before = prefix(eq)
    take_eq = jnp.where(eq_before + eq_incl - eq_f < need_eq, eq_f, 0.0)
    sel = jnp.where(gt, 1.0, take_eq) > 0.5
    _, _, sel_before = prefix(sel)
    tok = (base_tok + lax.broadcasted_iota(jnp.int32, (E, nc, L), 1) * L
           + lax.broadcasted_iota(jnp.int32, (E, nc, L), 2))
    key_ref[...] = jnp.where(sel, tok, NOT_SELECTED)
    rowpref_ref[...] = sel_before.astype(jnp.int32)


def route_select(aff_group_t, cap, base_tok):
    E, n_tok = aff_group_t.shape
    nc = n_tok // ROUTE_LANES
    a3 = aff_group_t.reshape(E, nc, ROUTE_LANES)
    keys, rowpref = pl.pallas_call(
        functools.partial(_route_body, cap=cap, base_tok=base_tok),
        out_shape=[jax.ShapeDtypeStruct((E, nc, ROUTE_LANES), jnp.int32)] * 2,
        compiler_params=pltpu.CompilerParams(vmem_limit_bytes=48 * 1024 * 1024),
        name="route_select",
    )(a3)
    return keys.reshape(E, n_tok), rowpref[:, :, 0]


FFN_SPLITS = 4


def _ffn_body(x_ref, gv_ref, tok_ref, wg_hbm, wu_hbm, wd_hbm, o_ref,
              wg_f32, wu_f32, wd_f32, wg_bf, wu_bf, wd_bf, sem_ref, *, layer, first_expert):
    e = pl.program_id(0)
    n_experts = pl.num_programs(0)

    def weight_copies(local_expert):
        expert = first_expert + local_expert
        return (pltpu.make_async_copy(wg_hbm.at[layer, expert], wg_f32, sem_ref.at[0]),
                pltpu.make_async_copy(wu_hbm.at[layer, expert], wu_f32, sem_ref.at[1]),
                pltpu.make_async_copy(wd_hbm.at[layer, expert], wd_f32, sem_ref.at[2]))

    @pl.when(pl.program_id(1) == 0)
    def _():
        @pl.when(e == 0)
        def _():
            for c in weight_copies(e):
                c.start()

        for c in weight_copies(e):
            c.wait()
        for src, dst in ((wg_f32, wg_bf), (wu_f32, wu_bf), (wd_f32, wd_bf)):
            def cast_rows(r, carry, src=src, dst=dst):
                rows = pl.ds(pl.multiple_of(r * WEIGHT_CAST_ROWS, WEIGHT_CAST_ROWS), WEIGHT_CAST_ROWS)
                dst[rows, :] = src[rows, :].astype(BF16)
                return carry
            lax.fori_loop(0, src.shape[0] // WEIGHT_CAST_ROWS, cast_rows, 0)

        @pl.when(e + 1 < n_experts)
        def _():
            for c in weight_copies(e + 1):
                c.start()

    x = x_ref[0]
    g = jnp.dot(x, wg_bf[...], preferred_element_type=F32)
    u = jnp.dot(x, wu_bf[...], preferred_element_type=F32)
    a = (g * _sigmoid(g) * u).astype(BF16)
    y = jnp.dot(a, wd_bf[...], preferred_element_type=F32) * gv_ref[0]
    o_ref[0, :, :D_MODEL] = y.astype(o_ref.dtype)
    tok = tok_ref[0]
    lane = lax.broadcasted_iota(jnp.int32, (tok.shape[0], META_WIDTH), 1)
    meta = jnp.where(lane == 0, jnp.right_shift(tok, 7), jnp.where(lane == 1, jnp.bitwise_and(tok, 127), 0))
    o_ref[0, :, D_MODEL:] = meta.astype(F32).astype(o_ref.dtype)


def expert_ffn(xe, w_gate, w_up, w_down, layer, first_expert, gate_vals, tok, tm=512):
    E, rows, D = xe.shape
    F = w_gate.shape[-1]
    return pl.pallas_call(
        functools.partial(_ffn_body, layer=layer, first_expert=first_expert),
        grid=(E, rows // tm),
        in_specs=[
            pl.BlockSpec((1, tm, D), lambda e, i: (e, i, 0)),
            pl.BlockSpec((1, tm, 1), lambda e, i: (first_expert + e, i, 0)),
            pl.BlockSpec((1, tm, 1), lambda e, i: (first_expert + e, i, 0)),
            pl.BlockSpec(memory_space=pl.ANY),
            pl.BlockSpec(memory_space=pl.ANY),
            pl.BlockSpec(memory_space=pl.ANY),
        ],
        out_specs=pl.BlockSpec((1, tm, D + META_WIDTH), lambda e, i: (e, i, 0)),
        out_shape=jax.ShapeDtypeStruct((E, rows, D + META_WIDTH), BF16),
        scratch_shapes=[
            pltpu.VMEM((D, F), F32), pltpu.VMEM((D, F), F32), pltpu.VMEM((F, D), F32),
            pltpu.VMEM((D, F), BF16), pltpu.VMEM((D, F), BF16), pltpu.VMEM((F, D), BF16),
            pltpu.SemaphoreType.DMA((3,)),
        ],
        compiler_params=_cparams(2, 56),
        name="expert_ffn",
    )(xe, gate_vals[..., None], tok[..., None], w_gate, w_up, w_down)


def _combine_body(win_ref, minslot_ref, nr_ref, x_ref, gain_ref, *refs, base_tile, n_tiles, n_parts, keep_x):
    ye_win = refs[:N_EXPERTS]
    ye_hbm = refs[N_EXPERTS:N_EXPERTS + n_parts]
    if keep_x:
        o_ref, h_ref, buf_ref, sem_ref = refs[N_EXPERTS + n_parts:]
    else:
        h_ref, o_ref, buf_ref, sem_ref = refs[N_EXPERTS + n_parts:]
    per_part = N_EXPERTS // n_parts
    j = pl.program_id(0)
    t0 = ((j + base_tile) * COMBINE_TM).astype(F32)

    def table_index(e, k):
        return e * (n_tiles * COMBINE_ROUNDS) + j * COMBINE_ROUNDS + k

    def placed(window, k):
        rows, toks = [], []
        for e in range(N_EXPERTS):
            w = window(e)
            meta = w[:, D_MODEL:].astype(F32)
            tok = meta[:, 0:1] * 128.0 + meta[:, 1:2] - t0
            flat = table_index(e, k)
            slot = win_ref[flat] * WIN_BLK + lax.broadcasted_iota(jnp.int32, (WIN_ROWS, 1), 0)
            toks.append(jnp.where(slot >= minslot_ref[flat], tok, -1.0))
            rows.append(w[:, :D_MODEL])
        tok_all = jnp.concatenate(toks, axis=0)
        lane_t = lax.broadcasted_iota(jnp.int32, (N_EXPERTS * WIN_ROWS, COMBINE_TM), 1).astype(F32)
        place = jnp.where(tok_all == lane_t, 1.0, 0.0).astype(BF16)
        return _dot_tn(place, jnp.concatenate(rows, axis=0))

    o_ref[...] = x_ref[...] + placed(lambda e: ye_win[e][...].reshape(WIN_ROWS, D_MODEL + META_WIDTH), 0)

    def window_copy(e, k):
        return pltpu.make_async_copy(
            ye_hbm[e // per_part].at[e % per_part, pl.ds(win_ref[table_index(e, k)], WIN_BLKS)],
            buf_ref.at[e], sem_ref.at[e])

    def extra_round(k, carry):
        for e in range(N_EXPERTS):
            window_copy(e, k).start()
        for e in range(N_EXPERTS):
            window_copy(e, k).wait()
        o_ref[...] += placed(lambda e: buf_ref[e].reshape(WIN_ROWS, D_MODEL + META_WIDTH), k)
        return carry

    lax.fori_loop(1, nr_ref[j], extra_round, 0)
    h_ref[...] = _rms(o_ref[...], gain_ref[...]).astype(h_ref.dtype)


def combine(x, ye_parts, win, minslot, n_rounds, base_tile, gain, h_dtype, keep_x):
    per_part, rows, dext = ye_parts[0].shape
    E = per_part * len(ye_parts)
    n_tiles = n_rounds.shape[0]
    ye4 = [ye.reshape(per_part, rows // WIN_BLK, WIN_BLK, dext) for ye in ye_parts]

    def ye_spec(e):
        return pl.BlockSpec(
            (pl.Element(1), pl.Element(WIN_BLKS), pl.Element(WIN_BLK), pl.Element(dext)),
            lambda j, win_r, ms_r, nr_r: (e % per_part, win_r[e * (n_tiles * COMBINE_ROUNDS) + j * COMBINE_ROUNDS], 0, 0))

    grid_spec = pltpu.PrefetchScalarGridSpec(
        num_scalar_prefetch=3,
        grid=(n_tiles,),
        in_specs=[pl.BlockSpec((COMBINE_TM, D_MODEL), lambda j, *_: (j + base_tile, 0)),
                  pl.BlockSpec((1, D_MODEL), lambda j, *_: (0, 0))]
        + [ye_spec(e) for e in range(E)]
        + [pl.BlockSpec(memory_space=pl.ANY)] * len(ye4),
        out_specs=([pl.BlockSpec((COMBINE_TM, D_MODEL), lambda j, *_: (j + base_tile, 0))] if keep_x else [])
        + [pl.BlockSpec((COMBINE_TM, D_MODEL), lambda j, *_: (j, 0))],
        scratch_shapes=([] if keep_x else [pltpu.VMEM((COMBINE_TM, D_MODEL), F32)]) + [
            pltpu.VMEM((E, WIN_BLKS, WIN_BLK, dext), BF16),
            pltpu.SemaphoreType.DMA((E,)),
        ],
    )
    outs = pl.pallas_call(
        functools.partial(_combine_body, base_tile=base_tile, n_tiles=n_tiles, n_parts=len(ye4), keep_x=keep_x),
        grid_spec=grid_spec,
        out_shape=([jax.ShapeDtypeStruct(x.shape, x.dtype)] if keep_x else [])
        + [jax.ShapeDtypeStruct((n_tiles * COMBINE_TM, D_MODEL), h_dtype)],
        input_output_aliases={3: 0} if keep_x else {},
        compiler_params=_cparams(1, 48),
        name="combine",
    )(win.reshape(-1), minslot.reshape(-1), n_rounds, x, gain.reshape(1, D_MODEL),
      *[ye4[e // per_part] for e in range(E)], *ye4)
    return outs if keep_x else (None, outs[0])


def combine_tables(rowpref, cap, slot_base):
    E = rowpref.shape[0]
    end = slot_base + cap
    starts = slot_base + jnp.concatenate(
        [rowpref[:, ::COMBINE_TM // ROUTE_LANES], jnp.full((E, 1), cap, jnp.int32)], axis=1)
    lo, hi = starts[:, :-1], starts[:, 1:]
    a = (lo // WIN_BLK) * WIN_BLK
    need = jnp.where(hi > lo, -(-(hi - a) // WIN_ROWS), 0)
    k = jnp.arange(COMBINE_ROUNDS, dtype=jnp.int32)[None, None, :]
    kk = jnp.minimum(k, jnp.maximum(need[..., None] - 1, 0))
    win = jnp.minimum(a[..., None] + WIN_ROWS * kk, end - WIN_ROWS) // WIN_BLK
    minslot = jnp.where(k < need[..., None], a[..., None] + WIN_ROWS * k, end)
    return win.astype(jnp.int32), minslot.astype(jnp.int32), jnp.max(need, axis=0).astype(jnp.int32)


def kernel(x_prompt, x_sample, w_in, q_norm, k_norm, lower_bounds, hgrn_norm, w_proj_attn, w_proj_hgrn,
           w_out, norm_mix, norm_ffn, w_router, w_gate, w_up, w_down, norm_final):
    depth = w_in.shape[0]
    groups = (x_prompt, x_sample)
    seq = x_prompt.shape[1]
    group_tokens = [g.shape[0] * g.shape[1] for g in groups]
    xa, xb = (g.reshape(-1, D_MODEL) for g in groups)
    cos, sin = rope_tables(seq)
    w_attn_bf, w_hgrn_bf, w_out_bf = (w.astype(BF16) for w in (w_proj_attn, w_proj_hgrn, w_out))

    h_parts = [rms_norm_rows2(xa, xb, norm_mix[0], BF16)]
    for l in range(depth):
        proj_z = in_proj(h_parts, w_in, l, Z_BLOCKS, F32)
        proj_b = in_proj(h_parts, w_in, l, B_BLOCKS, BF16)
        q, k, v = qk_prep(proj_b, cos, sin, q_norm[l], k_norm[l], seq)
        attn = attention(q, k, v, seq)
        o_f, o_b = hgrn_scan(proj_z, proj_b, hgrn_params(lower_bounds, l), seq)
        merged = merge_branches(attn, o_f, o_b, proj_b, hgrn_norm[l], w_attn_bf, w_hgrn_bf, l)
        x, h_ffn, aff = out_proj_router(merged, w_out_bf, l, *((xa, xb) if l == 0 else (x, None)),
                                        norm_ffn[l], w_router[l])
        aff_t = aff.T
        idx_parts, gate_parts, tables = [], [], []
        start = slot_base = 0
        for n_tok in group_tokens:
            cap = EC_CAPACITY * n_tok // N_EXPERTS
            aff_g = aff_t[:, start:start + n_tok]
            keys, rowpref = route_select(aff_g, cap, start)
            idx = jnp.sort(keys, axis=1)[:, :cap]
            idx_parts.append(idx)
            gate_parts.append(jnp.take_along_axis(aff_g, idx - start, axis=1))
            tables.append(combine_tables(rowpref, cap, slot_base) + (start // COMBINE_TM,))
            start += n_tok
            slot_base += cap
        idx = jnp.concatenate(idx_parts, axis=1)
        gate_vals = jnp.concatenate(gate_parts, axis=1)
        per = N_EXPERTS // FFN_SPLITS
        ye = [expert_ffn(h_ffn[idx[e0:e0 + per]], w_gate, w_up, w_down, l, e0, gate_vals, idx)
              for e0 in range(0, N_EXPERTS, per)]
        last = l == depth - 1
        h_parts = []
        for win, minslot, n_rounds, base_tile in tables:
            x_next, h_g = combine(x, ye, win, minslot, n_rounds, base_tile,
                                  norm_final if last else norm_mix[l + 1], F32 if last else BF16, not last)
            x = x if last else x_next
            h_parts.append(h_g)

    return tuple(h_g.reshape(g.shape) for h_g, g in zip(h_parts, groups))
```

```python
import functools
import math

import jax
import jax.numpy as jnp
from jax import lax
from jax.experimental import pallas as pl
from jax.experimental.pallas import tpu as pltpu

F32 = jnp.float32
BF16 = jnp.bfloat16

D_MODEL = 2048
SEQ = 4096
GRID_W = 64
HEAD_DIM = 128
N_Q_HEADS = 8
N_KV_HEADS = 2
Q_GROUP = N_Q_HEADS // N_KV_HEADS
ATTN_WIDTH = N_Q_HEADS * HEAD_DIM
KV_WIDTH = N_KV_HEADS * HEAD_DIM
ROPE_THETA = 10000.0
ROPE_AXIS_PAIRS = HEAD_DIM // 4
N_HGRN_HEADS = 8
HGRN_DIM = 128
HGRN_WIDTH = N_HGRN_HEADS * HGRN_DIM
N_EXPERTS = 16
EC_CAPACITY = 2
EXPERT_FF = 1024
NORM_EPS = 1e-6
IN_WIDTH = 10752

QKV_WIDTH = ATTN_WIDTH + 2 * KV_WIDTH
IN_PROJ_TN = QKV_WIDTH
Z_BLOCKS = (1, 2)
B_BLOCKS = (0, 3, 4, 5, 6)
ZCOL_HQ = 0
ZCOL_ZF = ZCOL_HQ + HGRN_WIDTH
ZCOL_ZB = ZCOL_ZF + HGRN_WIDTH
BCOL_QKV = 0
BCOL_HI = BCOL_QKV + QKV_WIDTH
BCOL_HOG = BCOL_HI + HGRN_WIDTH
BCOL_GA = BCOL_HOG + HGRN_WIDTH
BCOL_GB = BCOL_GA + D_MODEL

ATTN_TK = 2048
V_EXT = 2 * HEAD_DIM

HGRN_CHUNK = 64
HGRN_SUB = 16
HGRN_HEADS_PER_STEP = 8
EXP_CLAMP = 80.0
HGRN_CHUNKS_PER_STEP = 4

META_WIDTH = 128
COMBINE_TM = 256
WIN_BLK = 16
WIN_BLKS = 4
WIN_ROWS = WIN_BLK * WIN_BLKS
COMBINE_ROUNDS = -(-(WIN_BLK - 1 + COMBINE_TM) // WIN_ROWS)

WEIGHT_CAST_ROWS = 256

V7X_VMEM_BYTES = 64 * 1024 * 1024


def _cparams(n_grid, vmem_mb):
    assert vmem_mb * 1024 * 1024 < V7X_VMEM_BYTES
    return pltpu.CompilerParams(
        dimension_semantics=("arbitrary",) * n_grid,
        vmem_limit_bytes=vmem_mb * 1024 * 1024,
    )


def _sigmoid(x):
    return 1.0 / (1.0 + jnp.exp(-x))


def _rms(x, gain):
    return x * lax.rsqrt(jnp.mean(x * x, axis=-1, keepdims=True) + NORM_EPS) * gain


def _dot_nt(a, b):
    return lax.dot_general(a, b, (((1,), (1,)), ((), ())), preferred_element_type=F32)


def _dot_tn(a, b):
    return lax.dot_general(a, b, (((0,), (0,)), ((), ())), preferred_element_type=F32)


def _split_rows_specs(tm, D, n_first):
    return [pl.BlockSpec((tm, D), lambda i: (jnp.minimum(i, n_first - 1), 0)),
            pl.BlockSpec((tm, D), lambda i: (jnp.maximum(i - n_first, 0), 0))]


def _pick_rows(xa_ref, xb_ref, n_first):
    return jnp.where(pl.program_id(0) < n_first, xa_ref[...], xb_ref[...])


def _norm2_body(xa_ref, xb_ref, g_ref, o_ref, *, n_first):
    o_ref[...] = _rms(_pick_rows(xa_ref, xb_ref, n_first), g_ref[...]).astype(o_ref.dtype)


def rms_norm_rows2(xa, xb, gain, out_dtype, tm=512):
    D = xa.shape[1]
    T = xa.shape[0] + xb.shape[0]
    n_first = xa.shape[0] // tm
    return pl.pallas_call(
        functools.partial(_norm2_body, n_first=n_first),
        grid=(T // tm,),
        in_specs=_split_rows_specs(tm, D, n_first) + [pl.BlockSpec((1, D), lambda i: (0, 0))],
        out_specs=pl.BlockSpec((tm, D), lambda i: (i, 0)),
        out_shape=jax.ShapeDtypeStruct((T, D), out_dtype),
        compiler_params=_cparams(1, 32),
        name="rms_norm_rows2",
    )(xa, xb, gain.reshape(1, D))


def _mm_body(*refs, n_first):
    *a_refs, w_f32, o_ref, w_ref = refs
    i = pl.program_id(1)

    @pl.when(i == 0)
    def _():
        def cast_rows(r, carry):
            rows = pl.ds(pl.multiple_of(r * WEIGHT_CAST_ROWS, WEIGHT_CAST_ROWS), WEIGHT_CAST_ROWS)
            w_ref[rows, :] = w_f32[rows, :].astype(BF16)
            return carry
        lax.fori_loop(0, w_f32.shape[0] // WEIGHT_CAST_ROWS, cast_rows, 0)

    if len(a_refs) == 1:
        o_ref[...] = jnp.dot(a_refs[0][...], w_ref[...], preferred_element_type=F32).astype(o_ref.dtype)
        return

    @pl.when(i < n_first)
    def _():
        o_ref[...] = jnp.dot(a_refs[0][...], w_ref[...], preferred_element_type=F32).astype(o_ref.dtype)

    @pl.when(i >= n_first)
    def _():
        o_ref[...] = jnp.dot(a_refs[1][...], w_ref[...], preferred_element_type=F32).astype(o_ref.dtype)


def in_proj(a_parts, w, layer, col_blocks, out_dtype, tm=512):
    K = a_parts[0].shape[1]
    n_first = a_parts[0].shape[0] // tm
    M = sum(a.shape[0] for a in a_parts)
    tn = IN_PROJ_TN
    N = tn * len(col_blocks)
    assert all(b == col_blocks[1] + k for k, b in enumerate(col_blocks[1:]))

    def src_block(j):
        return jnp.where(j == 0, col_blocks[0], j + (col_blocks[1] - 1))

    if len(a_parts) == 1:
        a_specs = [pl.BlockSpec((tm, K), lambda j, i: (i, 0))]
    else:
        a_specs = [pl.BlockSpec((tm, K), lambda j, i: (jnp.minimum(i, n_first - 1), 0)),
                   pl.BlockSpec((tm, K), lambda j, i: (jnp.maximum(i - n_first, 0), 0))]
    return pl.pallas_call(
        functools.partial(_mm_body, n_first=n_first),
        grid=(N // tn, M // tm),
        in_specs=a_specs + [pl.BlockSpec((None, K, tn), lambda j, i: (layer, 0, src_block(j)))],
        out_specs=pl.BlockSpec((tm, tn), lambda j, i: (i, j)),
        out_shape=jax.ShapeDtypeStruct((M, N), out_dtype),
        scratch_shapes=[pltpu.VMEM((K, tn), BF16)],
        compiler_params=_cparams(2, 56),
        name="in_proj",
    )(*a_parts, w)


def _col_window(rows, width, col, row_block):
    return pl.BlockSpec((pl.Element(rows), pl.Element(width)), lambda *g: (row_block(*g) * rows, col))


def _qkprep_body(p_ref, cos_ref, sin_ref, qg_ref, kg_ref, q_ref, k_ref, v_ref, *, scale):
    cos = cos_ref[...]
    sin = sin_ref[...]
    lane = lax.broadcasted_iota(jnp.int32, cos.shape, 1)
    first = (lane & ROPE_AXIS_PAIRS) == 0
    xs = [p_ref[:, h * HEAD_DIM:(h + 1) * HEAD_DIM].astype(F32) for h in range(N_Q_HEADS + N_KV_HEADS)]
    gains = [qg_ref[...] * scale] * N_Q_HEADS + [kg_ref[...]] * N_KV_HEADS
    ms = [jnp.mean(x * x, axis=-1, keepdims=True) for x in xs]
    ys = [x * lax.rsqrt(m + NORM_EPS) * g for x, m, g in zip(xs, ms, gains)]
    ups = [pltpu.roll(y, HEAD_DIM - ROPE_AXIS_PAIRS, 1) for y in ys]
    downs = [pltpu.roll(y, ROPE_AXIS_PAIRS, 1) for y in ys]
    outs = [y * cos + jnp.where(first, u, d) * sin for y, u, d in zip(ys, ups, downs)]
    for h in range(N_Q_HEADS):
        q_ref[:, h * HEAD_DIM:(h + 1) * HEAD_DIM] = outs[h].astype(q_ref.dtype)
    for h in range(N_KV_HEADS):
        k_ref[:, h * HEAD_DIM:(h + 1) * HEAD_DIM] = outs[N_Q_HEADS + h].astype(k_ref.dtype)
    ones_col = jnp.where(lax.broadcasted_iota(jnp.int32, (cos.shape[0], V_EXT - HEAD_DIM), 1) == 0, 1.0, 0.0)
    for h in range(N_KV_HEADS):
        src = ATTN_WIDTH + KV_WIDTH + h * HEAD_DIM
        v_ref[:, h * V_EXT:h * V_EXT + HEAD_DIM] = p_ref[:, src:src + HEAD_DIM].astype(v_ref.dtype)
        v_ref[:, h * V_EXT + HEAD_DIM:(h + 1) * V_EXT] = ones_col.astype(v_ref.dtype)


def qk_prep(proj, cos, sin, q_gain, k_gain, seq, tm=512):
    T = proj.shape[0]
    nblk = seq // tm
    return pl.pallas_call(
        functools.partial(_qkprep_body, scale=HEAD_DIM ** -0.5 * math.log2(math.e)),
        grid=(T // tm,),
        in_specs=[
            pl.BlockSpec((tm, QKV_WIDTH), lambda i: (i, BCOL_QKV // QKV_WIDTH)),
            pl.BlockSpec((tm, HEAD_DIM), lambda i: (i % nblk, 0)),
            pl.BlockSpec((tm, HEAD_DIM), lambda i: (i % nblk, 0)),
            pl.BlockSpec((1, HEAD_DIM), lambda i: (0, 0)),
            pl.BlockSpec((1, HEAD_DIM), lambda i: (0, 0)),
        ],
        out_specs=[
            pl.BlockSpec((tm, ATTN_WIDTH), lambda i: (i, 0)),
            pl.BlockSpec((tm, KV_WIDTH), lambda i: (i, 0)),
            pl.BlockSpec((tm, N_KV_HEADS * V_EXT), lambda i: (i, 0)),
        ],
        out_shape=[
            jax.ShapeDtypeStruct((T, ATTN_WIDTH), BF16),
            jax.ShapeDtypeStruct((T, KV_WIDTH), BF16),
            jax.ShapeDtypeStruct((T, N_KV_HEADS * V_EXT), BF16),
        ],
        compiler_params=_cparams(1, 32),
        name="qk_prep",
    )(proj, cos, sin, q_gain.reshape(1, HEAD_DIM), k_gain.reshape(1, HEAD_DIM))


def rope_tables(seq):
    rows = seq // GRID_W
    row = jnp.repeat(jnp.arange(rows, dtype=F32), GRID_W)
    col = jnp.tile(jnp.arange(GRID_W, dtype=F32), rows)
    inv_freq = 1.0 / (ROPE_THETA ** (jnp.arange(ROPE_AXIS_PAIRS, dtype=F32) / ROPE_AXIS_PAIRS))
    ang_r = row[:, None] * inv_freq
    ang_c = col[:, None] * inv_freq
    cos = jnp.concatenate([jnp.cos(ang_r), jnp.cos(ang_r), jnp.cos(ang_c), jnp.cos(ang_c)], axis=1)
    sin = jnp.concatenate([-jnp.sin(ang_r), jnp.sin(ang_r), -jnp.sin(ang_c), jnp.sin(ang_c)], axis=1)
    return cos, sin


def _attn_body(q_ref, k_ref, v_ref, o_ref):
    n_chunks = k_ref.shape[0] // ATTN_TK
    sls = [slice(g * HEAD_DIM, (g + 1) * HEAD_DIM) for g in range(Q_GROUP)]
    qs = [q_ref[:, sl] for sl in sls]
    ms = accs = None
    for c in range(n_chunks):
        k = k_ref[c * ATTN_TK:(c + 1) * ATTN_TK, :]
        v = v_ref[c * ATTN_TK:(c + 1) * ATTN_TK, :]
        ss = [_dot_nt(q, k) for q in qs]
        new_ms = [jnp.max(s, axis=-1, keepdims=True) for s in ss]
        if c > 0:
            new_ms = [jnp.maximum(m, cm) for m, cm in zip(ms, new_ms)]
        ps = [jnp.exp2(s - m).astype(BF16) for s, m in zip(ss, new_ms)]
        pvs = [jnp.dot(p, v, preferred_element_type=F32) for p in ps]
        if c == 0:
            accs = pvs
        else:
            alphas = [jnp.exp2(m - nm) for m, nm in zip(ms, new_ms)]
            accs = [acc * a + pv for acc, a, pv in zip(accs, alphas, pvs)]
        ms = new_ms
    for sl, acc in zip(sls, accs):
        o_ref[:, sl] = (acc[:, :HEAD_DIM] / acc[:, HEAD_DIM:HEAD_DIM + 1]).astype(o_ref.dtype)


def attention(q, k, v, seq, tq=512):
    T = q.shape[0]
    n_seq = T // seq
    nq = seq // tq
    gw = Q_GROUP * HEAD_DIM
    return pl.pallas_call(
        _attn_body,
        grid=(n_seq, N_KV_HEADS, nq),
        in_specs=[
            pl.BlockSpec((tq, gw), lambda b, h, i: (b * nq + i, h)),
            pl.BlockSpec((seq, HEAD_DIM), lambda b, h, i: (b, h)),
            pl.BlockSpec((seq, V_EXT), lambda b, h, i: (b, h)),
        ],
        out_specs=pl.BlockSpec((tq, gw), lambda b, h, i: (b * nq + i, h)),
        out_shape=jax.ShapeDtypeStruct((T, ATTN_WIDTH), BF16),
        compiler_params=_cparams(3, 48),
        name="attention",
    )(q, k, v)


def _split3(x):
    hi = x.astype(BF16)
    r1 = x - hi.astype(F32)
    mid = r1.astype(BF16)
    lo = (r1 - mid.astype(F32)).astype(BF16)
    return hi, mid, lo


def _hgrn_gates(z, loglb, log1mlb, omlb):
    e = jnp.exp(-jnp.abs(z))
    r = 1.0 / (1.0 + e)
    kk = omlb * jnp.where(z >= 0, e * r, r)
    cc = log1mlb + (jnp.minimum(z, 0.0) - jnp.log(1.0 + e))
    g = jnp.maximum(loglb, cc) + jnp.log(1.0 + jnp.exp(-jnp.abs(loglb - cc)))
    return g, kk


def _hgrn_stages(fwd, rows, q_ref, z_ref, v_ref, p0, par_ref, st_ref, o_ref):
    C, SB, W = HGRN_CHUNK, HGRN_SUB, HGRN_DIM
    NH = HGRN_HEADS_PER_STEP
    row = lax.broadcasted_iota(jnp.int32, (C, C), 0)
    col = lax.broadcasted_iota(jnp.int32, (C, C), 1)
    tri = jnp.where((col <= row) if fwd else (col >= row), 1.0, 0.0).astype(BF16)
    g, kk = _hgrn_gates(z_ref[rows, :], par_ref[p0:p0 + 1, :], par_ref[p0 + 1:p0 + 2, :], par_ref[p0 + 2:p0 + 3, :])
    yield
    hi, mid, lo = _split3(g)
    b = (jnp.dot(tri, hi, preferred_element_type=F32) + jnp.dot(tri, mid, preferred_element_type=F32)
         + jnp.dot(tri, lo, preferred_element_type=F32))
    yield
    q = q_ref[rows, :]
    vb = v_ref[rows, :].astype(BF16)
    tot = b[C - 1:C, :] if fwd else b[0:1, :]
    q_in = (q * jnp.exp(b)).astype(BF16)
    k_st = (kk * jnp.exp(tot - b)).astype(BF16)
    dec = jnp.exp(tot)
    subs = []
    for i in range(C // SB):
        r0 = i * SB
        if fwd:
            ref = b[r0 - 1:r0, :] if i > 0 else jnp.zeros((1, NH * W), F32)
            k0, k1 = 0, r0 + SB
        else:
            ref = b[r0 + SB:r0 + SB + 1, :] if r0 + SB < C else jnp.zeros((1, NH * W), F32)
            k0, k1 = r0, C
        qs = (q[r0:r0 + SB] * jnp.exp(b[r0:r0 + SB] - ref)).astype(BF16)
        ks = (kk[k0:k1] * jnp.exp(jnp.minimum(ref - b[k0:k1], EXP_CLAMP))).astype(BF16)
        subs.append((r0, k0, k1, qs, ks))
    yield
    res = []
    for h in range(NH):
        sl = slice(h * W, (h + 1) * W)
        st = st_ref[h]
        o_inter = _dot_nt(q_in[:, sl], st.astype(BF16))
        st_ref[h] = st * dec[:, sl] + _dot_tn(vb[:, sl], k_st[:, sl])
        scores = [_dot_nt(qs[:, sl], ks[:, sl]) for (r0, k0, k1, qs, ks) in subs]
        res.append((o_inter, scores))
    yield
    keeps = []
    for (r0, k0, k1, qs, ks) in subs:
        t_idx = r0 + lax.broadcasted_iota(jnp.int32, (SB, k1 - k0), 0)
        s_idx = k0 + lax.broadcasted_iota(jnp.int32, (SB, k1 - k0), 1)
        keeps.append((s_idx <= t_idx) if fwd else (s_idx >= t_idx))
    masked = [[jnp.where(keep, s, 0.0).astype(BF16) for s, keep in zip(scores, keeps)] for (o_inter, scores) in res]
    yield
    for h, ((o_inter, scores), ms) in enumerate(zip(res, masked)):
        sl = slice(h * W, (h + 1) * W)
        outs = [jnp.dot(m, vb[k0:k1, sl], preferred_element_type=F32) for m, (r0, k0, k1, qs, ks) in zip(ms, subs)]
        o_ref[rows, sl] = (o_inter + jnp.concatenate(outs, axis=0)).astype(o_ref.dtype)
    yield


def _hgrn_body(qf_ref, zf_ref, vf_ref, qb_ref, zb_ref, vb_ref, par_ref, of_ref, ob_ref, stf_ref, stb_ref):
    C = HGRN_CHUNK
    n = HGRN_CHUNKS_PER_STEP

    @pl.when(pl.program_id(2) == 0)
    def _():
        stf_ref[...] = jnp.zeros(stf_ref.shape, F32)
        stb_ref[...] = jnp.zeros(stb_ref.shape, F32)

    gens = []
    for c in range(n):
        gens.append(_hgrn_stages(True, slice(c * C, (c + 1) * C), qf_ref, zf_ref, vf_ref, 0, par_ref, stf_ref, of_ref))
        gens.append(_hgrn_stages(False, slice((n - 1 - c) * C, (n - c) * C), qb_ref, zb_ref, vb_ref, 3, par_ref, stb_ref, ob_ref))
    for _ in range(6):
        for gen in gens:
            next(gen)


def hgrn_scan(proj_z, proj_b, params, seq):
    T = proj_z.shape[0]
    n_seq = T // seq
    C = HGRN_CHUNK * HGRN_CHUNKS_PER_STEP
    nc = seq // C
    w = HGRN_HEADS_PER_STEP * HGRN_DIM
    n_hh = HGRN_WIDTH // w

    assert n_hh == 1

    def fwd_spec(col):
        return _col_window(C, w, col, lambda b, hh, j: b * nc + j)

    def bwd_spec(col):
        return _col_window(C, w, col, lambda b, hh, j: b * nc + nc - 1 - j)

    return pl.pallas_call(
        _hgrn_body,
        grid=(n_seq, n_hh, nc),
        in_specs=[
            fwd_spec(ZCOL_HQ), fwd_spec(ZCOL_ZF), fwd_spec(BCOL_HI),
            bwd_spec(ZCOL_HQ), bwd_spec(ZCOL_ZB), bwd_spec(BCOL_HI),
            pl.BlockSpec((8, w), lambda b, hh, j: (0, hh)),
        ],
        out_specs=[
            pl.BlockSpec((C, w), lambda b, hh, j: (b * nc + j, hh)),
            pl.BlockSpec((C, w), lambda b, hh, j: (b * nc + nc - 1 - j, hh)),
        ],
        out_shape=[jax.ShapeDtypeStruct((T, HGRN_WIDTH), BF16), jax.ShapeDtypeStruct((T, HGRN_WIDTH), BF16)],
        scratch_shapes=[
            pltpu.VMEM((HGRN_HEADS_PER_STEP, HGRN_DIM, HGRN_DIM), F32),
            pltpu.VMEM((HGRN_HEADS_PER_STEP, HGRN_DIM, HGRN_DIM), F32),
        ],
        compiler_params=_cparams(3, 32),
        name="hgrn_scan",
    )(proj_z, proj_z, proj_b, proj_z, proj_z, proj_b, params)


def hgrn_params(lower_bounds, layer):
    lb_all = jnp.cumsum(jax.nn.softmax(lower_bounds.astype(F32), axis=1), axis=1)
    lb_all = lb_all - lb_all[:, :1]
    rows = []
    for d in range(2):
        lb = lb_all[d, layer]
        rows += [jnp.log(lb), jnp.log1p(-lb), 1.0 - lb]
    rows += [jnp.zeros_like(rows[0])] * 2
    return jnp.stack(rows, axis=0)


def _merge_body(attn_ref, of_ref, ob_ref, hog_ref, ga_ref, gb_ref, hg_ref, wa_ref, wh_ref, o_ref):
    ya = jnp.dot(attn_ref[...], wa_ref[...], preferred_element_type=F32)
    hs = []
    for h in range(N_HGRN_HEADS):
        sl = slice(h * HGRN_DIM, (h + 1) * HGRN_DIM)
        o = _rms(of_ref[:, sl].astype(F32) + ob_ref[:, sl].astype(F32), hg_ref[...])
        og = hog_ref[:, sl].astype(F32)
        hs.append((o * (og * _sigmoid(og))).astype(BF16))
    yh = jnp.dot(jnp.concatenate(hs, axis=1), wh_ref[...], preferred_element_type=F32)
    o_ref[...] = (_sigmoid(ga_ref[...].astype(F32)) * ya + _sigmoid(gb_ref[...].astype(F32)) * yh).astype(o_ref.dtype)


def merge_branches(attn, o_f, o_b, proj_b, hgrn_gain, w_attn, w_hgrn, layer, tm=256):
    T = attn.shape[0]
    D = D_MODEL
    return pl.pallas_call(
        _merge_body,
        grid=(T // tm,),
        in_specs=[
            pl.BlockSpec((tm, ATTN_WIDTH), lambda i: (i, 0)),
            pl.BlockSpec((tm, HGRN_WIDTH), lambda i: (i, 0)),
            pl.BlockSpec((tm, HGRN_WIDTH), lambda i: (i, 0)),
            _col_window(tm, HGRN_WIDTH, BCOL_HOG, lambda i: i),
            _col_window(tm, D, BCOL_GA, lambda i: i),
            _col_window(tm, D, BCOL_GB, lambda i: i),
            pl.BlockSpec((1, HGRN_DIM), lambda i: (0, 0)),
            pl.BlockSpec((None, ATTN_WIDTH, D), lambda i: (layer, 0, 0)),
            pl.BlockSpec((None, HGRN_WIDTH, D), lambda i: (layer, 0, 0)),
        ],
        out_specs=pl.BlockSpec((tm, D), lambda i: (i, 0)),
        out_shape=jax.ShapeDtypeStruct((T, D), BF16),
        compiler_params=_cparams(1, 56),
        name="merge_branches",
    )(attn, o_f, o_b, proj_b, proj_b, proj_b, hgrn_gain.reshape(1, HGRN_DIM), w_attn, w_hgrn)


def _outproj_body(m_ref, w_ref, xa_ref, xb_ref, g_ref, wr_ref, xo_ref, h_ref, aff_ref, *, n_first):
    xn = _pick_rows(xa_ref, xb_ref, n_first) + jnp.dot(m_ref[...], w_ref[...], preferred_element_type=F32)
    xo_ref[...] = xn
    h = _rms(xn, g_ref[...])
    h_ref[...] = h.astype(h_ref.dtype)
    hb = h.astype(BF16)
    hl = (h - hb.astype(F32)).astype(BF16)
    both = jnp.dot(hb, wr_ref[...], preferred_element_type=F32)
    logits = (both[:, :N_EXPERTS] + both[:, N_EXPERTS:]
              + jnp.dot(hl, wr_ref[:, :N_EXPERTS], preferred_element_type=F32))
    mx = jnp.max(logits, axis=1, keepdims=True)
    ex = jnp.exp(logits - mx)
    aff_ref[...] = ex / jnp.sum(ex, axis=1, keepdims=True)


def out_proj_router(merged, w_out, layer, xa, xb, ffn_gain, w_router, tm=256):
    D = xa.shape[1]
    n_first = xa.shape[0] // tm
    xb = xa if xb is None else xb
    T = merged.shape[0]
    wr_hi = w_router.astype(BF16)
    wr_lo = (w_router - wr_hi.astype(F32)).astype(BF16)
    wr = jnp.concatenate([wr_hi, wr_lo], axis=1)
    return pl.pallas_call(
        functools.partial(_outproj_body, n_first=n_first),
        grid=(T // tm,),
        in_specs=[
            pl.BlockSpec((tm, D), lambda i: (i, 0)),
            pl.BlockSpec((None, D, D), lambda i: (layer, 0, 0)),
            *_split_rows_specs(tm, D, n_first),
            pl.BlockSpec((1, D), lambda i: (0, 0)),
            pl.BlockSpec((D, 2 * N_EXPERTS), lambda i: (0, 0)),
        ],
        out_specs=[
            pl.BlockSpec((tm, D), lambda i: (i, 0)),
            pl.BlockSpec((tm, D), lambda i: (i, 0)),
            pl.BlockSpec((tm, N_EXPERTS), lambda i: (i, 0)),
        ],
        out_shape=[
            jax.ShapeDtypeStruct((T, D), F32),
            jax.ShapeDtypeStruct((T, D), BF16),
            jax.ShapeDtypeStruct((T, N_EXPERTS), F32),
        ],
        compiler_params=_cparams(1, 48),
        name="out_proj_router",
    )(merged, w_out, xa, xb, ffn_gain.reshape(1, D), wr)


ROUTE_LANES = 128
NOT_SELECTED = 2 ** 30


def _route_body(a_ref, key_ref, rowpref_ref, *, cap, base_tok):
    a = a_ref[...]
    E, nc, L = a.shape
    bits = pltpu.bitcast(a, jnp.int32)

    def count(mask):
        ones = jnp.where(mask, 1.0, 0.0)
        return jnp.sum(jnp.sum(ones, axis=1, keepdims=True), axis=2, keepdims=True)

    def bisect(_, carry):
        lo, hi = carry
        mid = lo + jnp.right_shift(hi - lo, 1)
        ge = count(bits >= mid) >= cap
        return jnp.where(ge, mid, lo), jnp.where(ge, hi, mid)

    lo0 = jnp.zeros((E, 1, 1), jnp.int32)
    hi0 = jnp.full((E, 1, 1), 0x7F800000, jnp.int32)
    thr, _ = lax.fori_loop(0, 31, bisect, (lo0, hi0))

    s_i = lax.broadcasted_iota(jnp.int32, (L, L), 0)
    t_i = lax.broadcasted_iota(jnp.int32, (L, L), 1)
    incl_lane = jnp.where(s_i <= t_i, 1.0, 0.0).astype(BF16)
    r_i = lax.broadcasted_iota(jnp.int32, (E, nc, nc), 1)
    c_i = lax.broadcasted_iota(jnp.int32, (E, nc, nc), 2)
    rows_before = jnp.where(c_i < r_i, 1.0, 0.0).astype(BF16)

    def prefix(mask):
        ones = jnp.where(mask, 1.0, 0.0)
        incl = jnp.dot(ones.astype(BF16).reshape(E * nc, L), incl_lane, preferred_element_type=F32).reshape(E, nc, L)
        row_tot = jnp.broadcast_to(incl[:, :, L - 1:L], (E, nc, L)).astype(BF16)
        before = lax.dot_general(rows_before, row_tot, (((2,), (1,)), ((0,), (0,))), preferred_element_type=F32)
        return ones, incl, before

    gt = bits > thr
    eq = bits == thr
    need_eq = cap - count(gt)
    eq_f, eq_incl, eq_before = prefix(eq)
    take_eq = jnp.where(eq_before + eq_incl - eq_f < need_eq, eq_f, 0.0)
    sel = jnp.where(gt, 1.0, take_eq) > 0.5
    _, _, sel_before = prefix(sel)
    tok = (base_tok + lax.broadcasted_iota(jnp.int32, (E, nc, L), 1) * L
           + lax.broadcasted_iota(jnp.int32, (E, nc, L), 2))
    key_ref[...] = jnp.where(sel, tok, NOT_SELECTED)
    rowpref_ref[...] = sel_before.astype(jnp.int32)


def route_select(aff_group_t, cap, base_tok):
    E, n_tok = aff_group_t.shape
    nc = n_tok // ROUTE_LANES
    a3 = aff_group_t.reshape(E, nc, ROUTE_LANES)
    keys, rowpref = pl.pallas_call(
        functools.partial(_route_body, cap=cap, base_tok=base_tok),
        out_shape=[jax.ShapeDtypeStruct((E, nc, ROUTE_LANES), jnp.int32)] * 2,
        compiler_params=pltpu.CompilerParams(vmem_limit_bytes=48 * 1024 * 1024),
        name="route_select",
    )(a3)
    return keys.reshape(E, n_tok), rowpref[:, :, 0]


FFN_SPLITS = 4


def _ffn_body(x_ref, gv_ref, tok_ref, wg_hbm, wu_hbm, wd_hbm, o_ref,
              wg_f32, wu_f32, wd_f32, wg_bf, wu_bf, wd_bf, sem_ref, *, layer, first_expert):
    e = pl.program_id(0)
    n_experts = pl.num_programs(0)

    def weight_copies(local_expert):
        expert = first_expert + local_expert
        return (pltpu.make_async_copy(wg_hbm.at[layer, expert], wg_f32, sem_ref.at[0]),
                pltpu.make_async_copy(wu_hbm.at[layer, expert], wu_f32, sem_ref.at[1]),
                pltpu.make_async_copy(wd_hbm.at[layer, expert], wd_f32, sem_ref.at[2]))

    @pl.when(pl.program_id(1) == 0)
    def _():
        @pl.when(e == 0)
        def _():
            for c in weight_copies(e):
                c.start()

        for c in weight_copies(e):
            c.wait()
        for src, dst in ((wg_f32, wg_bf), (wu_f32, wu_bf), (wd_f32, wd_bf)):
            def cast_rows(r, carry, src=src, dst=dst):
                rows = pl.ds(pl.multiple_of(r * WEIGHT_CAST_ROWS, WEIGHT_CAST_ROWS), WEIGHT_CAST_ROWS)
                dst[rows, :] = src[rows, :].astype(BF16)
                return carry
            lax.fori_loop(0, src.shape[0] // WEIGHT_CAST_ROWS, cast_rows, 0)

        @pl.when(e + 1 < n_experts)
        def _():
            for c in weight_copies(e + 1):
                c.start()

    x = x_ref[0]
    g = jnp.dot(x, wg_bf[...], preferred_element_type=F32)
    u = jnp.dot(x, wu_bf[...], preferred_element_type=F32)
    a = (g * _sigmoid(g) * u).astype(BF16)
    y = jnp.dot(a, wd_bf[...], preferred_element_type=F32) * gv_ref[0]
    o_ref[0, :, :D_MODEL] = y.astype(o_ref.dtype)
    tok = tok_ref[0]
    lane = lax.broadcasted_iota(jnp.int32, (tok.shape[0], META_WIDTH), 1)
    meta = jnp.where(lane == 0, jnp.right_shift(tok, 7), jnp.where(lane == 1, jnp.bitwise_and(tok, 127), 0))
    o_ref[0, :, D_MODEL:] = meta.astype(F32).astype(o_ref.dtype)


def expert_ffn(xe, w_gate, w_up, w_down, layer, first_expert, gate_vals, tok, tm=512):
    E, rows, D = xe.shape
    F = w_gate.shape[-1]
    return pl.pallas_call(
        functools.partial(_ffn_body, layer=layer, first_expert=first_expert),
        grid=(E, rows // tm),
        in_specs=[
            pl.BlockSpec((1, tm, D), lambda e, i: (e, i, 0)),
            pl.BlockSpec((1, tm, 1), lambda e, i: (first_expert + e, i, 0)),
            pl.BlockSpec((1, tm, 1), lambda e, i: (first_expert + e, i, 0)),
            pl.BlockSpec(memory_space=pl.ANY),
            pl.BlockSpec(memory_space=pl.ANY),
            pl.BlockSpec(memory_space=pl.ANY),
        ],
        out_specs=pl.BlockSpec((1, tm, D + META_WIDTH), lambda e, i: (e, i, 0)),
        out_shape=jax.ShapeDtypeStruct((E, rows, D + META_WIDTH), BF16),
        scratch_shapes=[
            pltpu.VMEM((D, F), F32), pltpu.VMEM((D, F), F32), pltpu.VMEM((F, D), F32),
            pltpu.VMEM((D, F), BF16), pltpu.VMEM((D, F), BF16), pltpu.VMEM((F, D), BF16),
            pltpu.SemaphoreType.DMA((3,)),
        ],
        compiler_params=_cparams(2, 56),
        name="expert_ffn",
    )(xe, gate_vals[..., None], tok[..., None], w_gate, w_up, w_down)


def _combine_body(win_ref, minslot_ref, nr_ref, x_ref, gain_ref, *refs, base_tile, n_tiles, n_parts, keep_x):
    ye_win = refs[:N_EXPERTS]
    ye_hbm = refs[N_EXPERTS:N_EXPERTS + n_parts]
    if keep_x:
        o_ref, h_ref, buf_ref, sem_ref = refs[N_EXPERTS + n_parts:]
    else:
        h_ref, o_ref, buf_ref, sem_ref = refs[N_EXPERTS + n_parts:]
    per_part = N_EXPERTS // n_parts
    j = pl.program_id(0)
    t0 = ((j + base_tile) * COMBINE_TM).astype(F32)

    def table_index(e, k):
        return e * (n_tiles * COMBINE_ROUNDS) + j * COMBINE_ROUNDS + k

    def placed(window, k):
        rows, toks = [], []
        for e in range(N_EXPERTS):
            w = window(e)
            meta = w[:, D_MODEL:].astype(F32)
            tok = meta[:, 0:1] * 128.0 + meta[:, 1:2] - t0
            flat = table_index(e, k)
            slot = win_ref[flat] * WIN_BLK + lax.broadcasted_iota(jnp.int32, (WIN_ROWS, 1), 0)
            toks.append(jnp.where(slot >= minslot_ref[flat], tok, -1.0))
            rows.append(w[:, :D_MODEL])
        tok_all = jnp.concatenate(toks, axis=0)
        lane_t = lax.broadcasted_iota(jnp.int32, (N_EXPERTS * WIN_ROWS, COMBINE_TM), 1).astype(F32)
        place = jnp.where(tok_all == lane_t, 1.0, 0.0).astype(BF16)
        return _dot_tn(place, jnp.concatenate(rows, axis=0))

    o_ref[...] = x_ref[...] + placed(lambda e: ye_win[e][...].reshape(WIN_ROWS, D_MODEL + META_WIDTH), 0)

    def window_copy(e, k):
        return pltpu.make_async_copy(
            ye_hbm[e // per_part].at[e % per_part, pl.ds(win_ref[table_index(e, k)], WIN_BLKS)],
            buf_ref.at[e], sem_ref.at[e])

    def extra_round(k, carry):
        for e in range(N_EXPERTS):
            window_copy(e, k).start()
        for e in range(N_EXPERTS):
            window_copy(e, k).wait()
        o_ref[...] += placed(lambda e: buf_ref[e].reshape(WIN_ROWS, D_MODEL + META_WIDTH), k)
        return carry

    lax.fori_loop(1, nr_ref[j], extra_round, 0)
    h_ref[...] = _rms(o_ref[...], gain_ref[...]).astype(h_ref.dtype)


def combine(x, ye_parts, win, minslot, n_rounds, base_tile, gain, h_dtype, keep_x):
    per_part, rows, dext = ye_parts[0].shape
    E = per_part * len(ye_parts)
    n_tiles = n_rounds.shape[0]
    ye4 = [ye.reshape(per_part, rows // WIN_BLK, WIN_BLK, dext) for ye in ye_parts]

    def ye_spec(e):
        return pl.BlockSpec(
            (pl.Element(1), pl.Element(WIN_BLKS), pl.Element(WIN_BLK), pl.Element(dext)),
            lambda j, win_r, ms_r, nr_r: (e % per_part, win_r[e * (n_tiles * COMBINE_ROUNDS) + j * COMBINE_ROUNDS], 0, 0))

    grid_spec = pltpu.PrefetchScalarGridSpec(
        num_scalar_prefetch=3,
        grid=(n_tiles,),
        in_specs=[pl.BlockSpec((COMBINE_TM, D_MODEL), lambda j, *_: (j + base_tile, 0)),
                  pl.BlockSpec((1, D_MODEL), lambda j, *_: (0, 0))]
        + [ye_spec(e) for e in range(E)]
        + [pl.BlockSpec(memory_space=pl.ANY)] * len(ye4),
        out_specs=([pl.BlockSpec((COMBINE_TM, D_MODEL), lambda j, *_: (j + base_tile, 0))] if keep_x else [])
        + [pl.BlockSpec((COMBINE_TM, D_MODEL), lambda j, *_: (j, 0))],
        scratch_shapes=([] if keep_x else [pltpu.VMEM((COMBINE_TM, D_MODEL), F32)]) + [
            pltpu.VMEM((E, WIN_BLKS, WIN_BLK, dext), BF16),
            pltpu.SemaphoreType.DMA((E,)),
        ],
    )
    outs = pl.pallas_call(
        functools.partial(_combine_body, base_tile=base_tile, n_tiles=n_tiles, n_parts=len(ye4), keep_x=keep_x),
        grid_spec=grid_spec,
        out_shape=([jax.ShapeDtypeStruct(x.shape, x.dtype)] if keep_x else [])
        + [jax.ShapeDtypeStruct((n_tiles * COMBINE_TM, D_MODEL), h_dtype)],
        input_output_aliases={3: 0} if keep_x else {},
        compiler_params=_cparams(1, 48),
        name="combine",
    )(win.reshape(-1), minslot.reshape(-1), n_rounds, x, gain.reshape(1, D_MODEL),
      *[ye4[e // per_part] for e in range(E)], *ye4)
    return outs if keep_x else (None, outs[0])


def combine_tables(rowpref, cap, slot_base):
    E = rowpref.shape[0]
    end = slot_base + cap
    starts = slot_base + jnp.concatenate(
        [rowpref[:, ::COMBINE_TM // ROUTE_LANES], jnp.full((E, 1), cap, jnp.int32)], axis=1)
    lo, hi = starts[:, :-1], starts[:, 1:]
    a = (lo // WIN_BLK) * WIN_BLK
    need = jnp.where(hi > lo, -(-(hi - a) // WIN_ROWS), 0)
    k = jnp.arange(COMBINE_ROUNDS, dtype=jnp.int32)[None, None, :]
    kk = jnp.minimum(k, jnp.maximum(need[..., None] - 1, 0))
    win = jnp.minimum(a[..., None] + WIN_ROWS * kk, end - WIN_ROWS) // WIN_BLK
    minslot = jnp.where(k < need[..., None], a[..., None] + WIN_ROWS * k, end)
    return win.astype(jnp.int32), minslot.astype(jnp.int32), jnp.max(need, axis=0).astype(jnp.int32)


def kernel(x_prompt, x_sample, w_in, q_norm, k_norm, lower_bounds, hgrn_norm, w_proj_attn, w_proj_hgrn,
           w_out, norm_mix, norm_ffn, w_router, w_gate, w_up, w_down, norm_final):
    depth = w_in.shape[0]
    groups = (x_prompt, x_sample)
    seq = x_prompt.shape[1]
    group_tokens = [g.shape[0] * g.shape[1] for g in groups]
    xa, xb = (g.reshape(-1, D_MODEL) for g in groups)
    cos, sin = rope_tables(seq)
    w_attn_bf, w_hgrn_bf, w_out_bf = (w.astype(BF16) for w in (w_proj_attn, w_proj_hgrn, w_out))

    h_parts = [rms_norm_rows2(xa, xb, norm_mix[0], BF16)]
    for l in range(depth):
        proj_z = in_proj(h_parts, w_in, l, Z_BLOCKS, F32)
        proj_b = in_proj(h_parts, w_in, l, B_BLOCKS, BF16)
        q, k, v = qk_prep(proj_b, cos, sin, q_norm[l], k_norm[l], seq)
        attn = attention(q, k, v, seq)
        o_f, o_b = hgrn_scan(proj_z, proj_b, hgrn_params(lower_bounds, l), seq)
        merged = merge_branches(attn, o_f, o_b, proj_b, hgrn_norm[l], w_attn_bf, w_hgrn_bf, l)
        x, h_ffn, aff = out_proj_router(merged, w_out_bf, l, *((xa, xb) if l == 0 else (x, None)),
                                        norm_ffn[l], w_router[l])
        aff_t = aff.T
        idx_parts, gate_parts, tables = [], [], []
        start = slot_base = 0
        for n_tok in group_tokens:
            cap = EC_CAPACITY * n_tok // N_EXPERTS
            aff_g = aff_t[:, start:start + n_tok]
            keys, rowpref = route_select(aff_g, cap, start)
            idx = jnp.sort(keys, axis=1)[:, :cap]
            idx_parts.append(idx)
            gate_parts.append(jnp.take_along_axis(aff_g, idx - start, axis=1))
            tables.append(combine_tables(rowpref, cap, slot_base) + (start // COMBINE_TM,))
            start += n_tok
            slot_base += cap
        idx = jnp.concatenate(idx_parts, axis=1)
        gate_vals = jnp.concatenate(gate_parts, axis=1)
        per = N_EXPERTS // FFN_SPLITS
        ye = [expert_ffn(h_ffn[idx[e0:e0 + per]], w_gate, w_up, w_down, l, e0, gate_vals, idx)
              for e0 in range(0, N_EXPERTS, per)]
        last = l == depth - 1
        h_parts = []
        for win, minslot, n_rounds, base_tile in tables:
            x_next, h_g = combine(x, ye, win, minslot, n_rounds, base_tile,
                                  norm_final if last else norm_mix[l + 1], F32 if last else BF16, not last)
            x = x if last else x_next
            h_parts.append(h_g)

    return tuple(h_g.reshape(g.shape) for h_g, g in zip(h_parts, groups))
```

```python
import functools
import math

import jax
import jax.numpy as jnp
from jax import lax
from jax.experimental import pallas as pl
from jax.experimental.pallas import tpu as pltpu

F32 = jnp.float32
BF16 = jnp.bfloat16

D_MODEL = 2048
SEQ = 4096
GRID_W = 64
HEAD_DIM = 128
N_Q_HEADS = 8
N_KV_HEADS = 2
Q_GROUP = N_Q_HEADS // N_KV_HEADS
ATTN_WIDTH = N_Q_HEADS * HEAD_DIM
KV_WIDTH = N_KV_HEADS * HEAD_DIM
ROPE_THETA = 10000.0
ROPE_AXIS_PAIRS = HEAD_DIM // 4
N_HGRN_HEADS = 8
HGRN_DIM = 128
HGRN_WIDTH = N_HGRN_HEADS * HGRN_DIM
N_EXPERTS = 16
EC_CAPACITY = 2
EXPERT_FF = 1024
NORM_EPS = 1e-6
IN_WIDTH = 10752

QKV_WIDTH = ATTN_WIDTH + 2 * KV_WIDTH
IN_PROJ_TN = QKV_WIDTH
Z_BLOCKS = (1, 2)
B_BLOCKS = (0, 3, 4, 5, 6)
ZCOL_HQ = 0
ZCOL_ZF = ZCOL_HQ + HGRN_WIDTH
ZCOL_ZB = ZCOL_ZF + HGRN_WIDTH
BCOL_QKV = 0
BCOL_HI = BCOL_QKV + QKV_WIDTH
BCOL_HOG = BCOL_HI + HGRN_WIDTH
BCOL_GA = BCOL_HOG + HGRN_WIDTH
BCOL_GB = BCOL_GA + D_MODEL

ATTN_TK = 2048
V_EXT = 2 * HEAD_DIM

HGRN_CHUNK = 64
HGRN_SUB = 16
HGRN_HEADS_PER_STEP = 8
EXP_CLAMP = 80.0
HGRN_CHUNKS_PER_STEP = 8

TOKEN_RADIX = 128
META_WIDTH = 128
COMBINE_TM = 256
WIN_BLK = 16
WIN_BLKS = 4
WIN_ROWS = WIN_BLK * WIN_BLKS
COMBINE_ROUNDS = -(-(WIN_BLK - 1 + COMBINE_TM) // WIN_ROWS)

WEIGHT_CAST_ROWS = 256

V7X_VMEM_BYTES = 64 * 1024 * 1024


def _cparams(n_grid, vmem_mb):
    assert vmem_mb * 1024 * 1024 < V7X_VMEM_BYTES
    return pltpu.CompilerParams(
        dimension_semantics=("arbitrary",) * n_grid,
        vmem_limit_bytes=vmem_mb * 1024 * 1024,
    )


def _sigmoid(x):
    return 1.0 / (1.0 + jnp.exp(-x))


def _rms(x, gain):
    return x * lax.rsqrt(jnp.mean(x * x, axis=-1, keepdims=True) + NORM_EPS) * gain


def _dot_nt(a, b):
    return lax.dot_general(a, b, (((1,), (1,)), ((), ())), preferred_element_type=F32)


def _dot_tn(a, b):
    return lax.dot_general(a, b, (((0,), (0,)), ((), ())), preferred_element_type=F32)


def _split_rows_specs(tm, D, n_first):
    return [pl.BlockSpec((tm, D), lambda i: (jnp.minimum(i, n_first - 1), 0)),
            pl.BlockSpec((tm, D), lambda i: (jnp.maximum(i - n_first, 0), 0))]


def _pick_rows(xa_ref, xb_ref, n_first):
    return jnp.where(pl.program_id(0) < n_first, xa_ref[...], xb_ref[...])


def _norm2_body(xa_ref, xb_ref, g_ref, o_ref, *, n_first):
    o_ref[...] = _rms(_pick_rows(xa_ref, xb_ref, n_first), g_ref[...]).astype(o_ref.dtype)


def rms_norm_rows2(xa, xb, gain, out_dtype, tm=512):
    D = xa.shape[1]
    T = xa.shape[0] + xb.shape[0]
    n_first = xa.shape[0] // tm
    return pl.pallas_call(
        functools.partial(_norm2_body, n_first=n_first),
        grid=(T // tm,),
        in_specs=_split_rows_specs(tm, D, n_first) + [pl.BlockSpec((1, D), lambda i: (0, 0))],
        out_specs=pl.BlockSpec((tm, D), lambda i: (i, 0)),
        out_shape=jax.ShapeDtypeStruct((T, D), out_dtype),
        compiler_params=_cparams(1, 32),
        name="rms_norm_rows2",
    )(xa, xb, gain.reshape(1, D))


def _mm_body(*refs, n_first):
    *a_refs, w_f32, o_ref, w_ref = refs
    i = pl.program_id(1)

    @pl.when(i == 0)
    def _():
        def cast_rows(r, carry):
            rows = pl.ds(pl.multiple_of(r * WEIGHT_CAST_ROWS, WEIGHT_CAST_ROWS), WEIGHT_CAST_ROWS)
            w_ref[rows, :] = w_f32[rows, :].astype(BF16)
            return carry
        lax.fori_loop(0, w_f32.shape[0] // WEIGHT_CAST_ROWS, cast_rows, 0)

    if len(a_refs) == 1:
        o_ref[...] = jnp.dot(a_refs[0][...], w_ref[...], preferred_element_type=F32).astype(o_ref.dtype)
        return

    @pl.when(i < n_first)
    def _():
        o_ref[...] = jnp.dot(a_refs[0][...], w_ref[...], preferred_element_type=F32).astype(o_ref.dtype)

    @pl.when(i >= n_first)
    def _():
        o_ref[...] = jnp.dot(a_refs[1][...], w_ref[...], preferred_element_type=F32).astype(o_ref.dtype)


def in_proj(a_parts, w, layer, col_blocks, out_dtype, tm=512):
    K = a_parts[0].shape[1]
    n_first = a_parts[0].shape[0] // tm
    M = sum(a.shape[0] for a in a_parts)
    tn = IN_PROJ_TN
    N = tn * len(col_blocks)
    assert all(b == col_blocks[1] + k for k, b in enumerate(col_blocks[1:]))

    def src_block(j):
        return jnp.where(j == 0, col_blocks[0], j + (col_blocks[1] - 1))

    if len(a_parts) == 1:
        a_specs = [pl.BlockSpec((tm, K), lambda j, i: (i, 0))]
    else:
        a_specs = [pl.BlockSpec((tm, K), lambda j, i: (jnp.minimum(i, n_first - 1), 0)),
                   pl.BlockSpec((tm, K), lambda j, i: (jnp.maximum(i - n_first, 0), 0))]
    return pl.pallas_call(
        functools.partial(_mm_body, n_first=n_first),
        grid=(N // tn, M // tm),
        in_specs=a_specs + [pl.BlockSpec((None, K, tn), lambda j, i: (layer, 0, src_block(j)))],
        out_specs=pl.BlockSpec((tm, tn), lambda j, i: (i, j)),
        out_shape=jax.ShapeDtypeStruct((M, N), out_dtype),
        scratch_shapes=[pltpu.VMEM((K, tn), BF16)],
        compiler_params=_cparams(2, 56),
        name="in_proj",
    )(*a_parts, w)


def _col_window(rows, width, col, row_block):
    return pl.BlockSpec((pl.Element(rows), pl.Element(width)), lambda *g: (row_block(*g) * rows, col))


def _qkprep_body(p_ref, cos_ref, sin_ref, qg_ref, kg_ref, q_ref, k_ref, v_ref, *, scale):
    cos = cos_ref[...]
    sin = sin_ref[...]
    lane = lax.broadcasted_iota(jnp.int32, cos.shape, 1)
    first = (lane & ROPE_AXIS_PAIRS) == 0
    xs = [p_ref[:, h * HEAD_DIM:(h + 1) * HEAD_DIM].astype(F32) for h in range(N_Q_HEADS + N_KV_HEADS)]
    gains = [qg_ref[...] * scale] * N_Q_HEADS + [kg_ref[...]] * N_KV_HEADS
    ms = [jnp.mean(x * x, axis=-1, keepdims=True) for x in xs]
    ys = [x * lax.rsqrt(m + NORM_EPS) * g for x, m, g in zip(xs, ms, gains)]
    ups = [pltpu.roll(y, HEAD_DIM - ROPE_AXIS_PAIRS, 1) for y in ys]
    downs = [pltpu.roll(y, ROPE_AXIS_PAIRS, 1) for y in ys]
    outs = [y * cos + jnp.where(first, u, d) * sin for y, u, d in zip(ys, ups, downs)]
    for h in range(N_Q_HEADS):
        q_ref[:, h * HEAD_DIM:(h + 1) * HEAD_DIM] = outs[h].astype(q_ref.dtype)
    for h in range(N_KV_HEADS):
        k_ref[:, h * HEAD_DIM:(h + 1) * HEAD_DIM] = outs[N_Q_HEADS + h].astype(k_ref.dtype)
    ones_col = jnp.where(lax.broadcasted_iota(jnp.int32, (cos.shape[0], V_EXT - HEAD_DIM), 1) == 0, 1.0, 0.0)
    for h in range(N_KV_HEADS):
        src = ATTN_WIDTH + KV_WIDTH + h * HEAD_DIM
        v_ref[:, h * V_EXT:h * V_EXT + HEAD_DIM] = p_ref[:, src:src + HEAD_DIM].astype(v_ref.dtype)
        v_ref[:, h * V_EXT + HEAD_DIM:(h + 1) * V_EXT] = ones_col.astype(v_ref.dtype)


def qk_prep(proj, cos, sin, q_gain, k_gain, seq, tm=512):
    T = proj.shape[0]
    nblk = seq // tm
    return pl.pallas_call(
        functools.partial(_qkprep_body, scale=HEAD_DIM ** -0.5 * math.log2(math.e)),
        grid=(T // tm,),
        in_specs=[
            pl.BlockSpec((tm, QKV_WIDTH), lambda i: (i, BCOL_QKV // QKV_WIDTH)),
            pl.BlockSpec((tm, HEAD_DIM), lambda i: (i % nblk, 0)),
            pl.BlockSpec((tm, HEAD_DIM), lambda i: (i % nblk, 0)),
            pl.BlockSpec((1, HEAD_DIM), lambda i: (0, 0)),
            pl.BlockSpec((1, HEAD_DIM), lambda i: (0, 0)),
        ],
        out_specs=[
            pl.BlockSpec((tm, ATTN_WIDTH), lambda i: (i, 0)),
            pl.BlockSpec((tm, KV_WIDTH), lambda i: (i, 0)),
            pl.BlockSpec((tm, N_KV_HEADS * V_EXT), lambda i: (i, 0)),
        ],
        out_shape=[
            jax.ShapeDtypeStruct((T, ATTN_WIDTH), BF16),
            jax.ShapeDtypeStruct((T, KV_WIDTH), BF16),
            jax.ShapeDtypeStruct((T, N_KV_HEADS * V_EXT), BF16),
        ],
        compiler_params=_cparams(1, 32),
        name="qk_prep",
    )(proj, cos, sin, q_gain.reshape(1, HEAD_DIM), k_gain.reshape(1, HEAD_DIM))


def rope_tables(seq):
    rows = seq // GRID_W
    row = jnp.repeat(jnp.arange(rows, dtype=F32), GRID_W)
    col = jnp.tile(jnp.arange(GRID_W, dtype=F32), rows)
    inv_freq = 1.0 / (ROPE_THETA ** (jnp.arange(ROPE_AXIS_PAIRS, dtype=F32) / ROPE_AXIS_PAIRS))
    ang_r = row[:, None] * inv_freq
    ang_c = col[:, None] * inv_freq
    cos = jnp.concatenate([jnp.cos(ang_r), jnp.cos(ang_r), jnp.cos(ang_c), jnp.cos(ang_c)], axis=1)
    sin = jnp.concatenate([-jnp.sin(ang_r), jnp.sin(ang_r), -jnp.sin(ang_c), jnp.sin(ang_c)], axis=1)
    return cos, sin


def _attn_body(q_ref, k_ref, v_ref, o_ref):
    n_chunks = k_ref.shape[0] // ATTN_TK
    sls = [slice(g * HEAD_DIM, (g + 1) * HEAD_DIM) for g in range(Q_GROUP)]
    qs = [q_ref[:, sl] for sl in sls]
    ms = accs = None
    for c in range(n_chunks):
        k = k_ref[c * ATTN_TK:(c + 1) * ATTN_TK, :]
        v = v_ref[c * ATTN_TK:(c + 1) * ATTN_TK, :]
        ss = [_dot_nt(q, k) for q in qs]
        new_ms = [jnp.max(s, axis=-1, keepdims=True) for s in ss]
        if c > 0:
            new_ms = [jnp.maximum(m, cm) for m, cm in zip(ms, new_ms)]
        ps = [jnp.exp2(s - m).astype(BF16) for s, m in zip(ss, new_ms)]
        pvs = [jnp.dot(p, v, preferred_element_type=F32) for p in ps]
        if c == 0:
            accs = pvs
        else:
            alphas = [jnp.exp2(m - nm) for m, nm in zip(ms, new_ms)]
            accs = [acc * a + pv for acc, a, pv in zip(accs, alphas, pvs)]
        ms = new_ms
    for sl, acc in zip(sls, accs):
        o_ref[:, sl] = (acc[:, :HEAD_DIM] / acc[:, HEAD_DIM:HEAD_DIM + 1]).astype(o_ref.dtype)


def attention(q, k, v, seq, tq=512):
    T = q.shape[0]
    n_seq = T // seq
    nq = seq // tq
    gw = Q_GROUP * HEAD_DIM
    return pl.pallas_call(
        _attn_body,
        grid=(n_seq, N_KV_HEADS, nq),
        in_specs=[
            pl.BlockSpec((tq, gw), lambda b, h, i: (b * nq + i, h)),
            pl.BlockSpec((seq, HEAD_DIM), lambda b, h, i: (b, h)),
            pl.BlockSpec((seq, V_EXT), lambda b, h, i: (b, h)),
        ],
        out_specs=pl.BlockSpec((tq, gw), lambda b, h, i: (b * nq + i, h)),
        out_shape=jax.ShapeDtypeStruct((T, ATTN_WIDTH), BF16),
        compiler_params=_cparams(3, 48),
        name="attention",
    )(q, k, v)


def _split3(x):
    hi = x.astype(BF16)
    r1 = x - hi.astype(F32)
    mid = r1.astype(BF16)
    lo = (r1 - mid.astype(F32)).astype(BF16)
    return hi, mid, lo


def _hgrn_gates(z, loglb, log1mlb, omlb):
    e = jnp.exp(-jnp.abs(z))
    r = 1.0 / (1.0 + e)
    kk = omlb * jnp.where(z >= 0, e * r, r)
    cc = log1mlb + (jnp.minimum(z, 0.0) - jnp.log(1.0 + e))
    g = jnp.maximum(loglb, cc) + jnp.log(1.0 + jnp.exp(-jnp.abs(loglb - cc)))
    return g, kk


def _hgrn_stages(fwd, rows, q_ref, z_ref, v_ref, p0, par_ref, st_ref, o_ref):
    C, SB, W = HGRN_CHUNK, HGRN_SUB, HGRN_DIM
    NH = HGRN_HEADS_PER_STEP
    row = lax.broadcasted_iota(jnp.int32, (C, C), 0)
    col = lax.broadcasted_iota(jnp.int32, (C, C), 1)
    tri = jnp.where((col <= row) if fwd else (col >= row), 1.0, 0.0).astype(BF16)
    g, kk = _hgrn_gates(z_ref[rows, :], par_ref[p0:p0 + 1, :], par_ref[p0 + 1:p0 + 2, :], par_ref[p0 + 2:p0 + 3, :])
    yield
    hi, mid, lo = _split3(g)
    b = (jnp.dot(tri, hi, preferred_element_type=F32) + jnp.dot(tri, mid, preferred_element_type=F32)
         + jnp.dot(tri, lo, preferred_element_type=F32))
    yield
    q = q_ref[rows, :]
    vb = v_ref[rows, :].astype(BF16)
    tot = b[C - 1:C, :] if fwd else b[0:1, :]
    q_in = (q * jnp.exp(b)).astype(BF16)
    k_st = (kk * jnp.exp(tot - b)).astype(BF16)
    dec = jnp.exp(tot)
    subs = []
    for i in range(C // SB):
        r0 = i * SB
        if fwd:
            ref = b[r0 - 1:r0, :] if i > 0 else jnp.zeros((1, NH * W), F32)
            k0, k1 = 0, r0 + SB
        else:
            ref = b[r0 + SB:r0 + SB + 1, :] if r0 + SB < C else jnp.zeros((1, NH * W), F32)
            k0, k1 = r0, C
        qs = (q[r0:r0 + SB] * jnp.exp(b[r0:r0 + SB] - ref)).astype(BF16)
        ks = (kk[k0:k1] * jnp.exp(jnp.minimum(ref - b[k0:k1], EXP_CLAMP))).astype(BF16)
        subs.append((r0, k0, k1, qs, ks))
    yield
    res = []
    for h in range(NH):
        sl = slice(h * W, (h + 1) * W)
        st = st_ref[h]
        o_inter = _dot_nt(q_in[:, sl], st.astype(BF16))
        st_ref[h] = st * dec[:, sl] + _dot_tn(vb[:, sl], k_st[:, sl])
        scores = [_dot_nt(qs[:, sl], ks[:, sl]) for (r0, k0, k1, qs, ks) in subs]
        res.append((o_inter, scores))
    yield
    keeps = []
    for (r0, k0, k1, qs, ks) in subs:
        t_idx = r0 + lax.broadcasted_iota(jnp.int32, (SB, k1 - k0), 0)
        s_idx = k0 + lax.broadcasted_iota(jnp.int32, (SB, k1 - k0), 1)
        keeps.append((s_idx <= t_idx) if fwd else (s_idx >= t_idx))
    masked = [[jnp.where(keep, s, 0.0).astype(BF16) for s, keep in zip(scores, keeps)] for (o_inter, scores) in res]
    yield
    for h, ((o_inter, scores), ms) in enumerate(zip(res, masked)):
        sl = slice(h * W, (h + 1) * W)
        outs = [jnp.dot(m, vb[k0:k1, sl], preferred_element_type=F32) for m, (r0, k0, k1, qs, ks) in zip(ms, subs)]
        o_ref[rows, sl] = (o_inter + jnp.concatenate(outs, axis=0)).astype(o_ref.dtype)
    yield


def _hgrn_body(qf_ref, zf_ref, vf_ref, qb_ref, zb_ref, vb_ref, par_ref, of_ref, ob_ref, stf_ref, stb_ref):
    C = HGRN_CHUNK
    n = HGRN_CHUNKS_PER_STEP

    @pl.when(pl.program_id(2) == 0)
    def _():
        stf_ref[...] = jnp.zeros(stf_ref.shape, F32)
        stb_ref[...] = jnp.zeros(stb_ref.shape, F32)

    gens = []
    for c in range(n):
        gens.append(_hgrn_stages(True, slice(c * C, (c + 1) * C), qf_ref, zf_ref, vf_ref, 0, par_ref, stf_ref, of_ref))
        gens.append(_hgrn_stages(False, slice((n - 1 - c) * C, (n - c) * C), qb_ref, zb_ref, vb_ref, 3, par_ref, stb_ref, ob_ref))
    for _ in range(6):
        for gen in gens:
            next(gen)


def hgrn_scan(proj_z, proj_b, params, seq):
    T = proj_z.shape[0]
    n_seq = T // seq
    C = HGRN_CHUNK * HGRN_CHUNKS_PER_STEP
    nc = seq // C
    w = HGRN_HEADS_PER_STEP * HGRN_DIM
    n_hh = HGRN_WIDTH // w

    assert n_hh == 1

    def fwd_spec(col):
        return _col_window(C, w, col, lambda b, hh, j: b * nc + j)

    def bwd_spec(col):
        return _col_window(C, w, col, lambda b, hh, j: b * nc + nc - 1 - j)

    return pl.pallas_call(
        _hgrn_body,
        grid=(n_seq, n_hh, nc),
        in_specs=[
            fwd_spec(ZCOL_HQ), fwd_spec(ZCOL_ZF), fwd_spec(BCOL_HI),
            bwd_spec(ZCOL_HQ), bwd_spec(ZCOL_ZB), bwd_spec(BCOL_HI),
            pl.BlockSpec((8, w), lambda b, hh, j: (0, hh)),
        ],
        out_specs=[
            pl.BlockSpec((C, w), lambda b, hh, j: (b * nc + j, hh)),
            pl.BlockSpec((C, w), lambda b, hh, j: (b * nc + nc - 1 - j, hh)),
        ],
        out_shape=[jax.ShapeDtypeStruct((T, HGRN_WIDTH), BF16), jax.ShapeDtypeStruct((T, HGRN_WIDTH), BF16)],
        scratch_shapes=[
            pltpu.VMEM((HGRN_HEADS_PER_STEP, HGRN_DIM, HGRN_DIM), F32),
            pltpu.VMEM((HGRN_HEADS_PER_STEP, HGRN_DIM, HGRN_DIM), F32),
        ],
        compiler_params=_cparams(3, 56),
        name="hgrn_scan",
    )(proj_z, proj_z, proj_b, proj_z, proj_z, proj_b, params)


def hgrn_params(lower_bounds, layer):
    lb_all = jnp.cumsum(jax.nn.softmax(lower_bounds.astype(F32), axis=1), axis=1)
    lb_all = lb_all - lb_all[:, :1]
    rows = []
    for d in range(2):
        lb = lb_all[d, layer]
        rows += [jnp.log(lb), jnp.log1p(-lb), 1.0 - lb]
    rows += [jnp.zeros_like(rows[0])] * 2
    return jnp.stack(rows, axis=0)


def _merge_body(attn_ref, of_ref, ob_ref, hog_ref, ga_ref, gb_ref, hg_ref, wa_ref, wh_ref, o_ref):
    ya = jnp.dot(attn_ref[...], wa_ref[...], preferred_element_type=F32)
    hs = []
    for h in range(N_HGRN_HEADS):
        sl = slice(h * HGRN_DIM, (h + 1) * HGRN_DIM)
        o = _rms(of_ref[:, sl].astype(F32) + ob_ref[:, sl].astype(F32), hg_ref[...])
        og = hog_ref[:, sl].astype(F32)
        hs.append((o * (og * _sigmoid(og))).astype(BF16))
    yh = jnp.dot(jnp.concatenate(hs, axis=1), wh_ref[...], preferred_element_type=F32)
    o_ref[...] = (_sigmoid(ga_ref[...].astype(F32)) * ya + _sigmoid(gb_ref[...].astype(F32)) * yh).astype(o_ref.dtype)


def merge_branches(attn, o_f, o_b, proj_b, hgrn_gain, w_attn, w_hgrn, layer, tm=256):
    T = attn.shape[0]
    D = D_MODEL
    return pl.pallas_call(
        _merge_body,
        grid=(T // tm,),
        in_specs=[
            pl.BlockSpec((tm, ATTN_WIDTH), lambda i: (i, 0)),
            pl.BlockSpec((tm, HGRN_WIDTH), lambda i: (i, 0)),
            pl.BlockSpec((tm, HGRN_WIDTH), lambda i: (i, 0)),
            _col_window(tm, HGRN_WIDTH, BCOL_HOG, lambda i: i),
            _col_window(tm, D, BCOL_GA, lambda i: i),
            _col_window(tm, D, BCOL_GB, lambda i: i),
            pl.BlockSpec((1, HGRN_DIM), lambda i: (0, 0)),
            pl.BlockSpec((None, ATTN_WIDTH, D), lambda i: (layer, 0, 0)),
            pl.BlockSpec((None, HGRN_WIDTH, D), lambda i: (layer, 0, 0)),
        ],
        out_specs=pl.BlockSpec((tm, D), lambda i: (i, 0)),
        out_shape=jax.ShapeDtypeStruct((T, D), BF16),
        compiler_params=_cparams(1, 56),
        name="merge_branches",
    )(attn, o_f, o_b, proj_b, proj_b, proj_b, hgrn_gain.reshape(1, HGRN_DIM), w_attn, w_hgrn)


def _outproj_body(m_ref, w_ref, xa_ref, xb_ref, g_ref, wr_ref, xo_ref, h_ref, aff_ref, *, n_first):
    xn = _pick_rows(xa_ref, xb_ref, n_first) + jnp.dot(m_ref[...], w_ref[...], preferred_element_type=F32)
    xo_ref[...] = xn
    h = _rms(xn, g_ref[...])
    h_ref[...] = h.astype(h_ref.dtype)
    hb = h.astype(BF16)
    hl = (h - hb.astype(F32)).astype(BF16)
    both = jnp.dot(hb, wr_ref[...], preferred_element_type=F32)
    logits = (both[:, :N_EXPERTS] + both[:, N_EXPERTS:]
              + jnp.dot(hl, wr_ref[:, :N_EXPERTS], preferred_element_type=F32))
    mx = jnp.max(logits, axis=1, keepdims=True)
    ex = jnp.exp(logits - mx)
    aff_ref[...] = ex / jnp.sum(ex, axis=1, keepdims=True)


def out_proj_router(merged, w_out, layer, xa, xb, ffn_gain, w_router, tm=256):
    D = xa.shape[1]
    n_first = xa.shape[0] // tm
    xb = xa if xb is None else xb
    T = merged.shape[0]
    wr_hi = w_router.astype(BF16)
    wr_lo = (w_router - wr_hi.astype(F32)).astype(BF16)
    wr = jnp.concatenate([wr_hi, wr_lo], axis=1)
    return pl.pallas_call(
        functools.partial(_outproj_body, n_first=n_first),
        grid=(T // tm,),
        in_specs=[
            pl.BlockSpec((tm, D), lambda i: (i, 0)),
            pl.BlockSpec((None, D, D), lambda i: (layer, 0, 0)),
            *_split_rows_specs(tm, D, n_first),
            pl.BlockSpec((1, D), lambda i: (0, 0)),
            pl.BlockSpec((D, 2 * N_EXPERTS), lambda i: (0, 0)),
        ],
        out_specs=[
            pl.BlockSpec((tm, D), lambda i: (i, 0)),
            pl.BlockSpec((tm, D), lambda i: (i, 0)),
            pl.BlockSpec((tm, N_EXPERTS), lambda i: (i, 0)),
        ],
        out_shape=[
            jax.ShapeDtypeStruct((T, D), F32),
            jax.ShapeDtypeStruct((T, D), BF16),
            jax.ShapeDtypeStruct((T, N_EXPERTS), F32),
        ],
        compiler_params=_cparams(1, 48),
        name="out_proj_router",
    )(merged, w_out, xa, xb, ffn_gain.reshape(1, D), wr)


ROUTE_LANES = 128
NOT_SELECTED = 2 ** 30


def _route_body(a_ref, key_ref, rowpref_ref, *, cap, base_tok):
    a = a_ref[...]
    E, nc, L = a.shape
    bits = pltpu.bitcast(a, jnp.int32)

    def count(mask):
        ones = jnp.where(mask, 1.0, 0.0)
        return jnp.sum(jnp.sum(ones, axis=1, keepdims=True), axis=2, keepdims=True)

    def bisect(_, carry):
        lo, hi = carry
        mid = lo + jnp.right_shift(hi - lo, 1)
        ge = count(bits >= mid) >= cap
        return jnp.where(ge, mid, lo), jnp.where(ge, hi, mid)

    lo0 = jnp.zeros((E, 1, 1), jnp.int32)
    hi0 = jnp.full((E, 1, 1), 0x7F800000, jnp.int32)
    thr, _ = lax.fori_loop(0, 31, bisect, (lo0, hi0))

    s_i = lax.broadcasted_iota(jnp.int32, (L, L), 0)
    t_i = lax.broadcasted_iota(jnp.int32, (L, L), 1)
    incl_lane = jnp.where(s_i <= t_i, 1.0, 0.0).astype(BF16)
    r_i = lax.broadcasted_iota(jnp.int32, (E, nc, nc), 1)
    c_i = lax.broadcasted_iota(jnp.int32, (E, nc, nc), 2)
    rows_before = jnp.where(c_i < r_i, 1.0, 0.0).astype(BF16)

    def prefix(mask):
        ones = jnp.where(mask, 1.0, 0.0)
        incl = jnp.dot(ones.astype(BF16).reshape(E * nc, L), incl_lane, preferred_element_type=F32).reshape(E, nc, L)
        row_tot = jnp.broadcast_to(incl[:, :, L - 1:L], (E, nc, L)).astype(BF16)
        before = lax.dot_general(rows_before, row_tot, (((2,), (1,)), ((0,), (0,))), preferred_element_type=F32)
        return ones, incl, before

    gt = bits > thr
    eq = bits == thr
    need_eq = cap - count(gt)
    eq_f, eq_incl, eq_before = prefix(eq)
    take_eq = jnp.where(eq_before + eq_incl - eq_f < need_eq, eq_f, 0.0)
    sel = jnp.where(gt, 1.0, take_eq) > 0.5
    _, _, sel_before = prefix(sel)
    tok = (base_tok + lax.broadcasted_iota(jnp.int32, (E, nc, L), 1) * L
           + lax.broadcasted_iota(jnp.int32, (E, nc, L), 2))
    key_ref[...] = jnp.where(sel, tok, NOT_SELECTED)
    rowpref_ref[...] = sel_before.astype(jnp.int32)


def route_select(aff_group_t, cap, base_tok):
    E, n_tok = aff_group_t.shape
    nc = n_tok // ROUTE_LANES
    a3 = aff_group_t.reshape(E, nc, ROUTE_LANES)
    keys, rowpref = pl.pallas_call(
        functools.partial(_route_body, cap=cap, base_tok=base_tok),
        out_shape=[jax.ShapeDtypeStruct((E, nc, ROUTE_LANES), jnp.int32)] * 2,
        compiler_params=pltpu.CompilerParams(vmem_limit_bytes=48 * 1024 * 1024),
        name="route_select",
    )(a3)
    return keys.reshape(E, n_tok), rowpref[:, :, 0]


FFN_SPLITS = 4


def _ffn_body(x_ref, gv_ref, tok_ref, wg_hbm, wu_hbm, wd_hbm, o_ref,
              wg_f32, wu_f32, wd_f32, wg_bf, wu_bf, wd_bf, sem_ref, *, layer, first_expert):
    e = pl.program_id(0)
    n_experts = pl.num_programs(0)

    def weight_copies(local_expert):
        expert = first_expert + local_expert
        return (pltpu.make_async_copy(wg_hbm.at[layer, expert], wg_f32, sem_ref.at[0]),
                pltpu.make_async_copy(wu_hbm.at[layer, expert], wu_f32, sem_ref.at[1]),
                pltpu.make_async_copy(wd_hbm.at[layer, expert], wd_f32, sem_ref.at[2]))

    @pl.when(pl.program_id(1) == 0)
    def _():
        @pl.when(e == 0)
        def _():
            for c in weight_copies(e):
                c.start()

        for c in weight_copies(e):
            c.wait()
        for src, dst in ((wg_f32, wg_bf), (wu_f32, wu_bf), (wd_f32, wd_bf)):
            def cast_rows(r, carry, src=src, dst=dst):
                rows = pl.ds(pl.multiple_of(r * WEIGHT_CAST_ROWS, WEIGHT_CAST_ROWS), WEIGHT_CAST_ROWS)
                dst[rows, :] = src[rows, :].astype(BF16)
                return carry
            lax.fori_loop(0, src.shape[0] // WEIGHT_CAST_ROWS, cast_rows, 0)

        @pl.when(e + 1 < n_experts)
        def _():
            for c in weight_copies(e + 1):
                c.start()

    x = x_ref[0]
    g = jnp.dot(x, wg_bf[...], preferred_element_type=F32)
    u = jnp.dot(x, wu_bf[...], preferred_element_type=F32)
    a = (g * _sigmoid(g) * u).astype(BF16)
    y = jnp.dot(a, wd_bf[...], preferred_element_type=F32) * gv_ref[0]
    o_ref[0, :, :D_MODEL] = y.astype(o_ref.dtype)
    tok = tok_ref[0]
    lane = lax.broadcasted_iota(jnp.int32, (tok.shape[0], META_WIDTH), 1)
    meta = jnp.where(lane == 0, tok // TOKEN_RADIX, jnp.where(lane == 1, tok % TOKEN_RADIX, 0))
    o_ref[0, :, D_MODEL:] = meta.astype(F32).astype(o_ref.dtype)


def expert_ffn(xe, w_gate, w_up, w_down, layer, first_expert, gate_vals, tok, tm=512):
    E, rows, D = xe.shape
    F = w_gate.shape[-1]
    return pl.pallas_call(
        functools.partial(_ffn_body, layer=layer, first_expert=first_expert),
        grid=(E, rows // tm),
        in_specs=[
            pl.BlockSpec((1, tm, D), lambda e, i: (e, i, 0)),
            pl.BlockSpec((1, tm, 1), lambda e, i: (first_expert + e, i, 0)),
            pl.BlockSpec((1, tm, 1), lambda e, i: (first_expert + e, i, 0)),
            pl.BlockSpec(memory_space=pl.ANY),
            pl.BlockSpec(memory_space=pl.ANY),
            pl.BlockSpec(memory_space=pl.ANY),
        ],
        out_specs=pl.BlockSpec((1, tm, D + META_WIDTH), lambda e, i: (e, i, 0)),
        out_shape=jax.ShapeDtypeStruct((E, rows, D + META_WIDTH), BF16),
        scratch_shapes=[
            pltpu.VMEM((D, F), F32), pltpu.VMEM((D, F), F32), pltpu.VMEM((F, D), F32),
            pltpu.VMEM((D, F), BF16), pltpu.VMEM((D, F), BF16), pltpu.VMEM((F, D), BF16),
            pltpu.SemaphoreType.DMA((3,)),
        ],
        compiler_params=_cparams(2, 56),
        name="expert_ffn",
    )(xe, gate_vals[..., None], tok[..., None], w_gate, w_up, w_down)


def _combine_body(win_ref, minslot_ref, nr_ref, x_ref, gain_ref, *refs, base_tile, n_tiles, n_parts, keep_x):
    ye_win = refs[:N_EXPERTS]
    ye_hbm = refs[N_EXPERTS:N_EXPERTS + n_parts]
    if keep_x:
        o_ref, h_ref, buf_ref, sem_ref = refs[N_EXPERTS + n_parts:]
    else:
        h_ref, o_ref, buf_ref, sem_ref = refs[N_EXPERTS + n_parts:]
    per_part = N_EXPERTS // n_parts
    j = pl.program_id(0)
    t0 = ((j + base_tile) * COMBINE_TM).astype(F32)

    def table_index(e, k):
        return e * (n_tiles * COMBINE_ROUNDS) + j * COMBINE_ROUNDS + k

    def placed(window, k):
        rows, toks = [], []
        for e in range(N_EXPERTS):
            w = window(e)
            meta = w[:, D_MODEL:].astype(F32)
            tok = meta[:, 0:1] * float(TOKEN_RADIX) + meta[:, 1:2] - t0
            flat = table_index(e, k)
            slot = win_ref[flat] * WIN_BLK + lax.broadcasted_iota(jnp.int32, (WIN_ROWS, 1), 0)
            toks.append(jnp.where(slot >= minslot_ref[flat], tok, -1.0))
            rows.append(w[:, :D_MODEL])
        tok_all = jnp.concatenate(toks, axis=0)
        lane_t = lax.broadcasted_iota(jnp.int32, (N_EXPERTS * WIN_ROWS, COMBINE_TM), 1).astype(F32)
        place = jnp.where(tok_all == lane_t, 1.0, 0.0).astype(BF16)
        return _dot_tn(place, jnp.concatenate(rows, axis=0))

    o_ref[...] = x_ref[...] + placed(lambda e: ye_win[e][...].reshape(WIN_ROWS, D_MODEL + META_WIDTH), 0)

    def window_copy(e, k):
        return pltpu.make_async_copy(
            ye_hbm[e // per_part].at[e % per_part, pl.ds(win_ref[table_index(e, k)], WIN_BLKS)],
            buf_ref.at[e], sem_ref.at[e])

    def extra_round(k, carry):
        for e in range(N_EXPERTS):
            window_copy(e, k).start()
        for e in range(N_EXPERTS):
            window_copy(e, k).wait()
        o_ref[...] += placed(lambda e: buf_ref[e].reshape(WIN_ROWS, D_MODEL + META_WIDTH), k)
        return carry

    lax.fori_loop(1, nr_ref[j], extra_round, 0)
    h_ref[...] = _rms(o_ref[...], gain_ref[...]).astype(h_ref.dtype)


def combine(x, ye_parts, win, minslot, n_rounds, base_tile, gain, h_dtype, keep_x):
    per_part, rows, dext = ye_parts[0].shape
    E = per_part * len(ye_parts)
    n_tiles = n_rounds.shape[0]
    ye4 = [ye.reshape(per_part, rows // WIN_BLK, WIN_BLK, dext) for ye in ye_parts]

    def ye_spec(e):
        return pl.BlockSpec(
            (pl.Element(1), pl.Element(WIN_BLKS), pl.Element(WIN_BLK), pl.Element(dext)),
            lambda j, win_r, ms_r, nr_r: (e % per_part, win_r[e * (n_tiles * COMBINE_ROUNDS) + j * COMBINE_ROUNDS], 0, 0))

    grid_spec = pltpu.PrefetchScalarGridSpec(
        num_scalar_prefetch=3,
        grid=(n_tiles,),
        in_specs=[pl.BlockSpec((COMBINE_TM, D_MODEL), lambda j, *_: (j + base_tile, 0)),
                  pl.BlockSpec((1, D_MODEL), lambda j, *_: (0, 0))]
        + [ye_spec(e) for e in range(E)]
        + [pl.BlockSpec(memory_space=pl.ANY)] * len(ye4),
        out_specs=([pl.BlockSpec((COMBINE_TM, D_MODEL), lambda j, *_: (j + base_tile, 0))] if keep_x else [])
        + [pl.BlockSpec((COMBINE_TM, D_MODEL), lambda j, *_: (j, 0))],
        scratch_shapes=([] if keep_x else [pltpu.VMEM((COMBINE_TM, D_MODEL), F32)]) + [
            pltpu.VMEM((E, WIN_BLKS, WIN_BLK, dext), BF16),
            pltpu.SemaphoreType.DMA((E,)),
        ],
    )
    outs = pl.pallas_call(
        functools.partial(_combine_body, base_tile=base_tile, n_tiles=n_tiles, n_parts=len(ye4), keep_x=keep_x),
        grid_spec=grid_spec,
        out_shape=([jax.ShapeDtypeStruct(x.shape, x.dtype)] if keep_x else [])
        + [jax.ShapeDtypeStruct((n_tiles * COMBINE_TM, D_MODEL), h_dtype)],
        input_output_aliases={3: 0} if keep_x else {},
        compiler_params=_cparams(1, 48),
        name="combine",
    )(win.reshape(-1), minslot.reshape(-1), n_rounds, x, gain.reshape(1, D_MODEL),
      *[ye4[e // per_part] for e in range(E)], *ye4)
    return outs if keep_x else (None, outs[0])


def combine_tables(rowpref, cap, slot_base):
    E = rowpref.shape[0]
    end = slot_base + cap
    starts = slot_base + jnp.concatenate(
        [rowpref[:, ::COMBINE_TM // ROUTE_LANES], jnp.full((E, 1), cap, jnp.int32)], axis=1)
    lo, hi = starts[:, :-1], starts[:, 1:]
    a = (lo // WIN_BLK) * WIN_BLK
    need = jnp.where(hi > lo, -(-(hi - a) // WIN_ROWS), 0)
    k = jnp.arange(COMBINE_ROUNDS, dtype=jnp.int32)[None, None, :]
    kk = jnp.minimum(k, jnp.maximum(need[..., None] - 1, 0))
    win = jnp.minimum(a[..., None] + WIN_ROWS * kk, end - WIN_ROWS) // WIN_BLK
    minslot = jnp.where(k < need[..., None], a[..., None] + WIN_ROWS * k, end)
    return win.astype(jnp.int32), minslot.astype(jnp.int32), jnp.max(need, axis=0).astype(jnp.int32)


def kernel(x_prompt, x_sample, w_in, q_norm, k_norm, lower_bounds, hgrn_norm, w_proj_attn, w_proj_hgrn,
           w_out, norm_mix, norm_ffn, w_router, w_gate, w_up, w_down, norm_final):
    depth = w_in.shape[0]
    groups = (x_prompt, x_sample)
    seq = x_prompt.shape[1]
    group_tokens = [g.shape[0] * g.shape[1] for g in groups]
    xa, xb = (g.reshape(-1, D_MODEL) for g in groups)
    cos, sin = rope_tables(seq)
    w_attn_bf, w_hgrn_bf, w_out_bf = (w.astype(BF16) for w in (w_proj_attn, w_proj_hgrn, w_out))

    h_parts = [rms_norm_rows2(xa, xb, norm_mix[0], BF16)]
    for l in range(depth):
        proj_z = in_proj(h_parts, w_in, l, Z_BLOCKS, F32)
        proj_b = in_proj(h_parts, w_in, l, B_BLOCKS, BF16)
        q, k, v = qk_prep(proj_b, cos, sin, q_norm[l], k_norm[l], seq)
        attn = attention(q, k, v, seq)
        o_f, o_b = hgrn_scan(proj_z, proj_b, hgrn_params(lower_bounds, l), seq)
        merged = merge_branches(attn, o_f, o_b, proj_b, hgrn_norm[l], w_attn_bf, w_hgrn_bf, l)
        x, h_ffn, aff = out_proj_router(merged, w_out_bf, l, *((xa, xb) if l == 0 else (x, None)),
                                        norm_ffn[l], w_router[l])
        aff_t = aff.T
        idx_parts, gate_parts, tables = [], [], []
        start = slot_base = 0
        for n_tok in group_tokens:
            cap = EC_CAPACITY * n_tok // N_EXPERTS
            aff_g = aff_t[:, start:start + n_tok]
            keys, rowpref = route_select(aff_g, cap, start)
            idx = jnp.sort(keys, axis=1)[:, :cap]
            idx_parts.append(idx)
            gate_parts.append(jnp.take_along_axis(aff_g, idx - start, axis=1))
            tables.append(combine_tables(rowpref, cap, slot_base) + (start // COMBINE_TM,))
            start += n_tok
            slot_base += cap
        idx = jnp.concatenate(idx_parts, axis=1)
        gate_vals = jnp.concatenate(gate_parts, axis=1)
        per = N_EXPERTS // FFN_SPLITS
        ye = [expert_ffn(h_ffn[idx[e0:e0 + per]], w_gate, w_up, w_down, l, e0, gate_vals, idx)
              for e0 in range(0, N_EXPERTS, per)]
        last = l == depth - 1
        h_parts = []
        for win, minslot, n_rounds, base_tile in tables:
            x_next, h_g = combine(x, ye, win, minslot, n_rounds, base_tile,
                                  norm_final if last else norm_mix[l + 1], F32 if last else BF16, not last)
            x = x if last else x_next
            h_parts.append(h_g)

    return tuple(h_g.reshape(g.shape) for h_g, g in zip(h_parts, groups))
```

```python
import functools
import math

import jax
import jax.numpy as jnp
from jax import lax
from jax.experimental import pallas as pl
from jax.experimental.pallas import tpu as pltpu

F32 = jnp.float32
BF16 = jnp.bfloat16

D_MODEL = 2048
SEQ = 4096
GRID_W = 64
HEAD_DIM = 128
N_Q_HEADS = 8
N_KV_HEADS = 2
Q_GROUP = N_Q_HEADS // N_KV_HEADS
ATTN_WIDTH = N_Q_HEADS * HEAD_DIM
KV_WIDTH = N_KV_HEADS * HEAD_DIM
ROPE_THETA = 10000.0
ROPE_AXIS_PAIRS = HEAD_DIM // 4
N_HGRN_HEADS = 8
HGRN_DIM = 128
HGRN_WIDTH = N_HGRN_HEADS * HGRN_DIM
N_EXPERTS = 16
EC_CAPACITY = 2
EXPERT_FF = 1024
NORM_EPS = 1e-6
IN_WIDTH = 10752

QKV_WIDTH = ATTN_WIDTH + 2 * KV_WIDTH
IN_PROJ_TN = QKV_WIDTH
Z_BLOCKS = (1, 2)
B_BLOCKS = (0, 3, 4, 5, 6)
ZCOL_HQ = 0
ZCOL_ZF = ZCOL_HQ + HGRN_WIDTH
ZCOL_ZB = ZCOL_ZF + HGRN_WIDTH
BCOL_QKV = 0
BCOL_HI = BCOL_QKV + QKV_WIDTH
BCOL_HOG = BCOL_HI + HGRN_WIDTH
BCOL_GA = BCOL_HOG + HGRN_WIDTH
BCOL_GB = BCOL_GA + D_MODEL

ATTN_TK = 2048
V_EXT = 2 * HEAD_DIM

HGRN_CHUNK = 64
HGRN_HEADS_PER_STEP = 8
HGRN_CHUNKS_PER_STEP = 8

TOKEN_RADIX = 128
META_WIDTH = 128
COMBINE_TM = 256
WIN_BLK = 16
WIN_BLKS = 4
WIN_ROWS = WIN_BLK * WIN_BLKS
COMBINE_ROUNDS = -(-(WIN_BLK - 1 + COMBINE_TM) // WIN_ROWS)

WEIGHT_CAST_ROWS = 256

V7X_VMEM_BYTES = 64 * 1024 * 1024


def _cparams(n_grid, vmem_mb):
    assert vmem_mb * 1024 * 1024 < V7X_VMEM_BYTES
    return pltpu.CompilerParams(
        dimension_semantics=("arbitrary",) * n_grid,
        vmem_limit_bytes=vmem_mb * 1024 * 1024,
    )


def _sigmoid(x):
    return 1.0 / (1.0 + jnp.exp(-x))


def _rms(x, gain):
    return x * lax.rsqrt(jnp.mean(x * x, axis=-1, keepdims=True) + NORM_EPS) * gain


def _dot_nt(a, b):
    return lax.dot_general(a, b, (((1,), (1,)), ((), ())), preferred_element_type=F32)


def _dot_tn(a, b):
    return lax.dot_general(a, b, (((0,), (0,)), ((), ())), preferred_element_type=F32)


def _split_rows_specs(tm, D, n_first):
    return [pl.BlockSpec((tm, D), lambda i: (jnp.minimum(i, n_first - 1), 0)),
            pl.BlockSpec((tm, D), lambda i: (jnp.maximum(i - n_first, 0), 0))]


def _pick_rows(xa_ref, xb_ref, n_first):
    return jnp.where(pl.program_id(0) < n_first, xa_ref[...], xb_ref[...])


def _norm2_body(xa_ref, xb_ref, g_ref, o_ref, *, n_first):
    o_ref[...] = _rms(_pick_rows(xa_ref, xb_ref, n_first), g_ref[...]).astype(o_ref.dtype)


def rms_norm_rows2(xa, xb, gain, out_dtype, tm=512):
    D = xa.shape[1]
    T = xa.shape[0] + xb.shape[0]
    n_first = xa.shape[0] // tm
    return pl.pallas_call(
        functools.partial(_norm2_body, n_first=n_first),
        grid=(T // tm,),
        in_specs=_split_rows_specs(tm, D, n_first) + [pl.BlockSpec((1, D), lambda i: (0, 0))],
        out_specs=pl.BlockSpec((tm, D), lambda i: (i, 0)),
        out_shape=jax.ShapeDtypeStruct((T, D), out_dtype),
        compiler_params=_cparams(1, 32),
        name="rms_norm_rows2",
    )(xa, xb, gain.reshape(1, D))


def _mm_body(*refs, n_first):
    *a_refs, w_f32, o_ref, w_ref = refs
    i = pl.program_id(1)

    @pl.when(i == 0)
    def _():
        def cast_rows(r, carry):
            rows = pl.ds(pl.multiple_of(r * WEIGHT_CAST_ROWS, WEIGHT_CAST_ROWS), WEIGHT_CAST_ROWS)
            w_ref[rows, :] = w_f32[rows, :].astype(BF16)
            return carry
        lax.fori_loop(0, w_f32.shape[0] // WEIGHT_CAST_ROWS, cast_rows, 0)

    if len(a_refs) == 1:
        o_ref[...] = jnp.dot(a_refs[0][...], w_ref[...], preferred_element_type=F32).astype(o_ref.dtype)
        return

    @pl.when(i < n_first)
    def _():
        o_ref[...] = jnp.dot(a_refs[0][...], w_ref[...], preferred_element_type=F32).astype(o_ref.dtype)

    @pl.when(i >= n_first)
    def _():
        o_ref[...] = jnp.dot(a_refs[1][...], w_ref[...], preferred_element_type=F32).astype(o_ref.dtype)


def in_proj(a_parts, w, layer, col_blocks, out_dtype, tm=512):
    K = a_parts[0].shape[1]
    n_first = a_parts[0].shape[0] // tm
    M = sum(a.shape[0] for a in a_parts)
    tn = IN_PROJ_TN
    N = tn * len(col_blocks)
    assert all(b == col_blocks[1] + k for k, b in enumerate(col_blocks[1:]))

    def src_block(j):
        return jnp.where(j == 0, col_blocks[0], j + (col_blocks[1] - 1))

    if len(a_parts) == 1:
        a_specs = [pl.BlockSpec((tm, K), lambda j, i: (i, 0))]
    else:
        a_specs = [pl.BlockSpec((tm, K), lambda j, i: (jnp.minimum(i, n_first - 1), 0)),
                   pl.BlockSpec((tm, K), lambda j, i: (jnp.maximum(i - n_first, 0), 0))]
    return pl.pallas_call(
        functools.partial(_mm_body, n_first=n_first),
        grid=(N // tn, M // tm),
        in_specs=a_specs + [pl.BlockSpec((None, K, tn), lambda j, i: (layer, 0, src_block(j)))],
        out_specs=pl.BlockSpec((tm, tn), lambda j, i: (i, j)),
        out_shape=jax.ShapeDtypeStruct((M, N), out_dtype),
        scratch_shapes=[pltpu.VMEM((K, tn), BF16)],
        compiler_params=_cparams(2, 56),
        name="in_proj",
    )(*a_parts, w)


def _col_window(rows, width, col, row_block):
    return pl.BlockSpec((pl.Element(rows), pl.Element(width)), lambda *g: (row_block(*g) * rows, col))


def _qkprep_body(p_ref, cos_ref, sin_ref, qg_ref, kg_ref, q_ref, k_ref, v_ref, *, scale):
    cos = cos_ref[...]
    sin = sin_ref[...]
    lane = lax.broadcasted_iota(jnp.int32, cos.shape, 1)
    first = (lane & ROPE_AXIS_PAIRS) == 0
    xs = [p_ref[:, h * HEAD_DIM:(h + 1) * HEAD_DIM].astype(F32) for h in range(N_Q_HEADS + N_KV_HEADS)]
    gains = [qg_ref[...] * scale] * N_Q_HEADS + [kg_ref[...]] * N_KV_HEADS
    ms = [jnp.mean(x * x, axis=-1, keepdims=True) for x in xs]
    ys = [x * lax.rsqrt(m + NORM_EPS) * g for x, m, g in zip(xs, ms, gains)]
    ups = [pltpu.roll(y, HEAD_DIM - ROPE_AXIS_PAIRS, 1) for y in ys]
    downs = [pltpu.roll(y, ROPE_AXIS_PAIRS, 1) for y in ys]
    outs = [y * cos + jnp.where(first, u, d) * sin for y, u, d in zip(ys, ups, downs)]
    for h in range(N_Q_HEADS):
        q_ref[:, h * HEAD_DIM:(h + 1) * HEAD_DIM] = outs[h].astype(q_ref.dtype)
    for h in range(N_KV_HEADS):
        k_ref[:, h * HEAD_DIM:(h + 1) * HEAD_DIM] = outs[N_Q_HEADS + h].astype(k_ref.dtype)
    ones_col = jnp.where(lax.broadcasted_iota(jnp.int32, (cos.shape[0], V_EXT - HEAD_DIM), 1) == 0, 1.0, 0.0)
    for h in range(N_KV_HEADS):
        src = ATTN_WIDTH + KV_WIDTH + h * HEAD_DIM
        v_ref[:, h * V_EXT:h * V_EXT + HEAD_DIM] = p_ref[:, src:src + HEAD_DIM].astype(v_ref.dtype)
        v_ref[:, h * V_EXT + HEAD_DIM:(h + 1) * V_EXT] = ones_col.astype(v_ref.dtype)


def qk_prep(proj, cos, sin, q_gain, k_gain, seq, tm=512):
    T = proj.shape[0]
    nblk = seq // tm
    return pl.pallas_call(
        functools.partial(_qkprep_body, scale=HEAD_DIM ** -0.5 * math.log2(math.e)),
        grid=(T // tm,),
        in_specs=[
            pl.BlockSpec((tm, QKV_WIDTH), lambda i: (i, BCOL_QKV // QKV_WIDTH)),
            pl.BlockSpec((tm, HEAD_DIM), lambda i: (i % nblk, 0)),
            pl.BlockSpec((tm, HEAD_DIM), lambda i: (i % nblk, 0)),
            pl.BlockSpec((1, HEAD_DIM), lambda i: (0, 0)),
            pl.BlockSpec((1, HEAD_DIM), lambda i: (0, 0)),
        ],
        out_specs=[
            pl.BlockSpec((tm, ATTN_WIDTH), lambda i: (i, 0)),
            pl.BlockSpec((tm, KV_WIDTH), lambda i: (i, 0)),
            pl.BlockSpec((tm, N_KV_HEADS * V_EXT), lambda i: (i, 0)),
        ],
        out_shape=[
            jax.ShapeDtypeStruct((T, ATTN_WIDTH), BF16),
            jax.ShapeDtypeStruct((T, KV_WIDTH), BF16),
            jax.ShapeDtypeStruct((T, N_KV_HEADS * V_EXT), BF16),
        ],
        compiler_params=_cparams(1, 32),
        name="qk_prep",
    )(proj, cos, sin, q_gain.reshape(1, HEAD_DIM), k_gain.reshape(1, HEAD_DIM))


def rope_tables(seq):
    rows = seq // GRID_W
    row = jnp.repeat(jnp.arange(rows, dtype=F32), GRID_W)
    col = jnp.tile(jnp.arange(GRID_W, dtype=F32), rows)
    inv_freq = 1.0 / (ROPE_THETA ** (jnp.arange(ROPE_AXIS_PAIRS, dtype=F32) / ROPE_AXIS_PAIRS))
    ang_r = row[:, None] * inv_freq
    ang_c = col[:, None] * inv_freq
    cos = jnp.concatenate([jnp.cos(ang_r), jnp.cos(ang_r), jnp.cos(ang_c), jnp.cos(ang_c)], axis=1)
    sin = jnp.concatenate([-jnp.sin(ang_r), jnp.sin(ang_r), -jnp.sin(ang_c), jnp.sin(ang_c)], axis=1)
    return cos, sin


def _attn_body(q_ref, k_ref, v_ref, o_ref):
    n_chunks = k_ref.shape[0] // ATTN_TK
    sls = [slice(g * HEAD_DIM, (g + 1) * HEAD_DIM) for g in range(Q_GROUP)]
    qs = [q_ref[:, sl] for sl in sls]
    ms = accs = None
    for c in range(n_chunks):
        k = k_ref[c * ATTN_TK:(c + 1) * ATTN_TK, :]
        v = v_ref[c * ATTN_TK:(c + 1) * ATTN_TK, :]
        ss = [_dot_nt(q, k) for q in qs]
        new_ms = [jnp.max(s, axis=-1, keepdims=True) for s in ss]
        if c > 0:
            new_ms = [jnp.maximum(m, cm) for m, cm in zip(ms, new_ms)]
        ps = [jnp.exp2(s - m).astype(BF16) for s, m in zip(ss, new_ms)]
        pvs = [jnp.dot(p, v, preferred_element_type=F32) for p in ps]
        if c == 0:
            accs = pvs
        else:
            alphas = [jnp.exp2(m - nm) for m, nm in zip(ms, new_ms)]
            accs = [acc * a + pv for acc, a, pv in zip(accs, alphas, pvs)]
        ms = new_ms
    for sl, acc in zip(sls, accs):
        o_ref[:, sl] = (acc[:, :HEAD_DIM] / acc[:, HEAD_DIM:HEAD_DIM + 1]).astype(o_ref.dtype)


def attention(q, k, v, seq, tq=512):
    T = q.shape[0]
    n_seq = T // seq
    nq = seq // tq
    gw = Q_GROUP * HEAD_DIM
    return pl.pallas_call(
        _attn_body,
        grid=(n_seq, N_KV_HEADS, nq),
        in_specs=[
            pl.BlockSpec((tq, gw), lambda b, h, i: (b * nq + i, h)),
            pl.BlockSpec((seq, HEAD_DIM), lambda b, h, i: (b, h)),
            pl.BlockSpec((seq, V_EXT), lambda b, h, i: (b, h)),
        ],
        out_specs=pl.BlockSpec((tq, gw), lambda b, h, i: (b * nq + i, h)),
        out_shape=jax.ShapeDtypeStruct((T, ATTN_WIDTH), BF16),
        compiler_params=_cparams(3, 48),
        name="attention",
    )(q, k, v)


def _split3(x):
    hi = x.astype(BF16)
    r1 = x - hi.astype(F32)
    mid = r1.astype(BF16)
    lo = (r1 - mid.astype(F32)).astype(BF16)
    return hi, mid, lo


def _hgrn_gates(z, loglb, log1mlb, omlb):
    e = jnp.exp(-jnp.abs(z))
    r = 1.0 / (1.0 + e)
    kk = omlb * jnp.where(z >= 0, e * r, r)
    cc = log1mlb + (jnp.minimum(z, 0.0) - jnp.log(1.0 + e))
    g = jnp.maximum(loglb, cc) + jnp.log(1.0 + jnp.exp(-jnp.abs(loglb - cc)))
    return g, kk


def _level_refs(b, m, fwd):
    C, NW = b.shape
    if 2 * m >= 8:
        b3 = b.reshape(C // (2 * m), 2 * m, NW)
        r = b3[:, m - 1:m, :] if fwd else b3[:, m:m + 1, :]
        return jnp.broadcast_to(r, b3.shape).reshape(C, NW)
    pos = lax.broadcasted_iota(jnp.int32, (C, NW), 0) % (2 * m)
    target = (m - 1) if fwd else m
    out = b
    for off in range(2 * m):
        if off != target:
            out = jnp.where(pos == off, pltpu.roll(b, (off - target) % C, 0), out)
    return out


def _hgrn_stages(fwd, rows, q_ref, z_ref, v_ref, p0, par_ref, st_ref, o_ref):
    C, W = HGRN_CHUNK, HGRN_DIM
    NH = HGRN_HEADS_PER_STEP
    row = lax.broadcasted_iota(jnp.int32, (C, C), 0)
    col = lax.broadcasted_iota(jnp.int32, (C, C), 1)
    tri = jnp.where((col <= row) if fwd else (col >= row), 1.0, 0.0).astype(BF16)
    g, kk = _hgrn_gates(z_ref[rows, :], par_ref[p0:p0 + 1, :], par_ref[p0 + 1:p0 + 2, :], par_ref[p0 + 2:p0 + 3, :])
    yield
    hi, mid, lo = _split3(g)
    b = (jnp.dot(tri, hi, preferred_element_type=F32) + jnp.dot(tri, mid, preferred_element_type=F32)
         + jnp.dot(tri, lo, preferred_element_type=F32))
    yield
    q = q_ref[rows, :]
    vb = v_ref[rows, :].astype(BF16)
    tot = b[C - 1:C, :] if fwd else b[0:1, :]
    q_in = (q * jnp.exp(b)).astype(BF16)
    k_st = (kk * jnp.exp(tot - b)).astype(BF16)
    dec = jnp.exp(tot)
    levels = [(q.astype(BF16), kk.astype(BF16), (row == col))]
    rows_i = lax.broadcasted_iota(jnp.int32, (C, NH * W), 0)
    m = 1
    while 2 * m <= C:
        if m == 1:
            e = jnp.where((rows_i % 2 == 1) if fwd else (rows_i % 2 == 0), 1.0 - kk, 1.0)
        else:
            e = jnp.exp(-jnp.abs(b - _level_refs(b, m, fwd)))
        t_up = (row % (2 * m)) >= m
        s_up = (col % (2 * m)) >= m
        pair = (row // (2 * m) == col // (2 * m)) & ((t_up & ~s_up) if fwd else (~t_up & s_up))
        levels.append(((q * e).astype(BF16), (kk * e).astype(BF16), pair))
        m *= 2
    yield
    res = []
    for h in range(NH):
        sl = slice(h * W, (h + 1) * W)
        st = st_ref[h]
        o_inter = _dot_nt(q_in[:, sl], st.astype(BF16))
        st_ref[h] = st * dec[:, sl] + _dot_tn(vb[:, sl], k_st[:, sl])
        scores = [_dot_nt(ql[:, sl], kl[:, sl]) for (ql, kl, pair) in levels]
        res.append((o_inter, scores))
    yield
    masked = []
    for (o_inter, scores) in res:
        s_all = jnp.zeros((C, C), F32)
        for s, (ql, kl, pair) in zip(scores, levels):
            s_all = jnp.where(pair, s, s_all)
        masked.append(s_all.astype(BF16))
    yield
    for h, ((o_inter, scores), s_all) in enumerate(zip(res, masked)):
        sl = slice(h * W, (h + 1) * W)
        o_ref[rows, sl] = (o_inter + jnp.dot(s_all, vb[:, sl], preferred_element_type=F32)).astype(o_ref.dtype)
    yield


def _hgrn_body(qf_ref, zf_ref, vf_ref, qb_ref, zb_ref, vb_ref, par_ref, of_ref, ob_ref, stf_ref, stb_ref):
    C = HGRN_CHUNK
    n = HGRN_CHUNKS_PER_STEP

    @pl.when(pl.program_id(2) == 0)
    def _():
        stf_ref[...] = jnp.zeros(stf_ref.shape, F32)
        stb_ref[...] = jnp.zeros(stb_ref.shape, F32)

    gens = []
    for c in range(n):
        gens.append(_hgrn_stages(True, slice(c * C, (c + 1) * C), qf_ref, zf_ref, vf_ref, 0, par_ref, stf_ref, of_ref))
        gens.append(_hgrn_stages(False, slice((n - 1 - c) * C, (n - c) * C), qb_ref, zb_ref, vb_ref, 3, par_ref, stb_ref, ob_ref))
    for _ in range(6):
        for gen in gens:
            next(gen)


def hgrn_scan(proj_z, proj_b, params, seq):
    T = proj_z.shape[0]
    n_seq = T // seq
    C = HGRN_CHUNK * HGRN_CHUNKS_PER_STEP
    nc = seq // C
    w = HGRN_HEADS_PER_STEP * HGRN_DIM
    n_hh = HGRN_WIDTH // w

    assert n_hh == 1

    def fwd_spec(col):
        return _col_window(C, w, col, lambda b, hh, j: b * nc + j)

    def bwd_spec(col):
        return _col_window(C, w, col, lambda b, hh, j: b * nc + nc - 1 - j)

    return pl.pallas_call(
        _hgrn_body,
        grid=(n_seq, n_hh, nc),
        in_specs=[
            fwd_spec(ZCOL_HQ), fwd_spec(ZCOL_ZF), fwd_spec(BCOL_HI),
            bwd_spec(ZCOL_HQ), bwd_spec(ZCOL_ZB), bwd_spec(BCOL_HI),
            pl.BlockSpec((8, w), lambda b, hh, j: (0, hh)),
        ],
        out_specs=[
            pl.BlockSpec((C, w), lambda b, hh, j: (b * nc + j, hh)),
            pl.BlockSpec((C, w), lambda b, hh, j: (b * nc + nc - 1 - j, hh)),
        ],
        out_shape=[jax.ShapeDtypeStruct((T, HGRN_WIDTH), BF16), jax.ShapeDtypeStruct((T, HGRN_WIDTH), BF16)],
        scratch_shapes=[
            pltpu.VMEM((HGRN_HEADS_PER_STEP, HGRN_DIM, HGRN_DIM), F32),
            pltpu.VMEM((HGRN_HEADS_PER_STEP, HGRN_DIM, HGRN_DIM), F32),
        ],
        compiler_params=_cparams(3, 56),
        name="hgrn_scan",
    )(proj_z, proj_z, proj_b, proj_z, proj_z, proj_b, params)


def hgrn_params(lower_bounds, layer):
    lb_all = jnp.cumsum(jax.nn.softmax(lower_bounds.astype(F32), axis=1), axis=1)
    lb_all = lb_all - lb_all[:, :1]
    rows = []
    for d in range(2):
        lb = lb_all[d, layer]
        rows += [jnp.log(lb), jnp.log1p(-lb), 1.0 - lb]
    rows += [jnp.zeros_like(rows[0])] * 2
    return jnp.stack(rows, axis=0)


def _merge_body(attn_ref, of_ref, ob_ref, hog_ref, ga_ref, gb_ref, hg_ref, wa_ref, wh_ref, o_ref):
    ya = jnp.dot(attn_ref[...], wa_ref[...], preferred_element_type=F32)
    hs = []
    for h in range(N_HGRN_HEADS):
        sl = slice(h * HGRN_DIM, (h + 1) * HGRN_DIM)
        o = _rms(of_ref[:, sl].astype(F32) + ob_ref[:, sl].astype(F32), hg_ref[...])
        og = hog_ref[:, sl].astype(F32)
        hs.append((o * (og * _sigmoid(og))).astype(BF16))
    yh = jnp.dot(jnp.concatenate(hs, axis=1), wh_ref[...], preferred_element_type=F32)
    o_ref[...] = (_sigmoid(ga_ref[...].astype(F32)) * ya + _sigmoid(gb_ref[...].astype(F32)) * yh).astype(o_ref.dtype)


def merge_branches(attn, o_f, o_b, proj_b, hgrn_gain, w_attn, w_hgrn, layer, tm=256):
    T = attn.shape[0]
    D = D_MODEL
    return pl.pallas_call(
        _merge_body,
        grid=(T // tm,),
        in_specs=[
            pl.BlockSpec((tm, ATTN_WIDTH), lambda i: (i, 0)),
            pl.BlockSpec((tm, HGRN_WIDTH), lambda i: (i, 0)),
            pl.BlockSpec((tm, HGRN_WIDTH), lambda i: (i, 0)),
            _col_window(tm, HGRN_WIDTH, BCOL_HOG, lambda i: i),
            _col_window(tm, D, BCOL_GA, lambda i: i),
            _col_window(tm, D, BCOL_GB, lambda i: i),
            pl.BlockSpec((1, HGRN_DIM), lambda i: (0, 0)),
            pl.BlockSpec((None, ATTN_WIDTH, D), lambda i: (layer, 0, 0)),
            pl.BlockSpec((None, HGRN_WIDTH, D), lambda i: (layer, 0, 0)),
        ],
        out_specs=pl.BlockSpec((tm, D), lambda i: (i, 0)),
        out_shape=jax.ShapeDtypeStruct((T, D), BF16),
        compiler_params=_cparams(1, 56),
        name="merge_branches",
    )(attn, o_f, o_b, proj_b, proj_b, proj_b, hgrn_gain.reshape(1, HGRN_DIM), w_attn, w_hgrn)


def _outproj_body(m_ref, w_ref, xa_ref, xb_ref, g_ref, wr_ref, xo_ref, h_ref, aff_ref, *, n_first):
    xn = _pick_rows(xa_ref, xb_ref, n_first) + jnp.dot(m_ref[...], w_ref[...], preferred_element_type=F32)
    xo_ref[...] = xn
    h = _rms(xn, g_ref[...])
    h_ref[...] = h.astype(h_ref.dtype)
    hb = h.astype(BF16)
    hl = (h - hb.astype(F32)).astype(BF16)
    both = jnp.dot(hb, wr_ref[...], preferred_element_type=F32)
    logits = (both[:, :N_EXPERTS] + both[:, N_EXPERTS:]
              + jnp.dot(hl, wr_ref[:, :N_EXPERTS], preferred_element_type=F32))
    mx = jnp.max(logits, axis=1, keepdims=True)
    ex = jnp.exp(logits - mx)
    aff_ref[...] = ex / jnp.sum(ex, axis=1, keepdims=True)


def out_proj_router(merged, w_out, layer, xa, xb, ffn_gain, w_router, tm=256):
    D = xa.shape[1]
    n_first = xa.shape[0] // tm
    xb = xa if xb is None else xb
    T = merged.shape[0]
    wr_hi = w_router.astype(BF16)
    wr_lo = (w_router - wr_hi.astype(F32)).astype(BF16)
    wr = jnp.concatenate([wr_hi, wr_lo], axis=1)
    return pl.pallas_call(
        functools.partial(_outproj_body, n_first=n_first),
        grid=(T // tm,),
        in_specs=[
            pl.BlockSpec((tm, D), lambda i: (i, 0)),
            pl.BlockSpec((None, D, D), lambda i: (layer, 0, 0)),
            *_split_rows_specs(tm, D, n_first),
            pl.BlockSpec((1, D), lambda i: (0, 0)),
            pl.BlockSpec((D, 2 * N_EXPERTS), lambda i: (0, 0)),
        ],
        out_specs=[
            pl.BlockSpec((tm, D), lambda i: (i, 0)),
            pl.BlockSpec((tm, D), lambda i: (i, 0)),
            pl.BlockSpec((tm, N_EXPERTS), lambda i: (i, 0)),
        ],
        out_shape=[
            jax.ShapeDtypeStruct((T, D), F32),
            jax.ShapeDtypeStruct((T, D), BF16),
            jax.ShapeDtypeStruct((T, N_EXPERTS), F32),
        ],
        compiler_params=_cparams(1, 48),
        name="out_proj_router",
    )(merged, w_out, xa, xb, ffn_gain.reshape(1, D), wr)


ROUTE_LANES = 128
NOT_SELECTED = 2 ** 30


def _route_body(a_ref, key_ref, rowpref_ref, *, cap, base_tok):
    a = a_ref[...]
    E, nc, L = a.shape
    bits = pltpu.bitcast(a, jnp.int32)

    def count(mask):
        ones = jnp.where(mask, 1.0, 0.0)
        return jnp.sum(jnp.sum(ones, axis=1, keepdims=True), axis=2, keepdims=True)

    def bisect(_, carry):
        lo, hi = carry
        mid = lo + jnp.right_shift(hi - lo, 1)
        ge = count(bits >= mid) >= cap
        return jnp.where(ge, mid, lo), jnp.where(ge, hi, mid)

    lo0 = jnp.zeros((E, 1, 1), jnp.int32)
    hi0 = jnp.full((E, 1, 1), 0x7F800000, jnp.int32)
    thr, _ = lax.fori_loop(0, 31, bisect, (lo0, hi0))

    s_i = lax.broadcasted_iota(jnp.int32, (L, L), 0)
    t_i = lax.broadcasted_iota(jnp.int32, (L, L), 1)
    incl_lane = jnp.where(s_i <= t_i, 1.0, 0.0).astype(BF16)
    r_i = lax.broadcasted_iota(jnp.int32, (E, nc, nc), 1)
    c_i = lax.broadcasted_iota(jnp.int32, (E, nc, nc), 2)
    rows_before = jnp.where(c_i < r_i, 1.0, 0.0).astype(BF16)

    def prefix(mask):
        ones = jnp.where(mask, 1.0, 0.0)
        incl = jnp.dot(ones.astype(BF16).reshape(E * nc, L), incl_lane, preferred_element_type=F32).reshape(E, nc, L)
        row_tot = jnp.broadcast_to(incl[:, :, L - 1:L], (E, nc, L)).astype(BF16)
        before = lax.dot_general(rows_before, row_tot, (((2,), (1,)), ((0,), (0,))), preferred_element_type=F32)
        return ones, incl, before

    gt = bits > thr
    eq = bits == thr
    need_eq = cap - count(gt)
    eq_f, eq_incl, eq_before = prefix(eq)
    take_eq = jnp.where(eq_before + eq_incl - eq_f < need_eq, eq_f, 0.0)
    sel = jnp.where(gt, 1.0, take_eq) > 0.5
    _, _, sel_before = prefix(sel)
    tok = (base_tok + lax.broadcasted_iota(jnp.int32, (E, nc, L), 1) * L
           + lax.broadcasted_iota(jnp.int32, (E, nc, L), 2))
    key_ref[...] = jnp.where(sel, tok, NOT_SELECTED)
    rowpref_ref[...] = sel_before.astype(jnp.int32)


def route_select(aff_group_t, cap, base_tok):
    E, n_tok = aff_group_t.shape
    nc = n_tok // ROUTE_LANES
    a3 = aff_group_t.reshape(E, nc, ROUTE_LANES)
    keys, rowpref = pl.pallas_call(
        functools.partial(_route_body, cap=cap, base_tok=base_tok),
        out_shape=[jax.ShapeDtypeStruct((E, nc, ROUTE_LANES), jnp.int32)] * 2,
        compiler_params=pltpu.CompilerParams(vmem_limit_bytes=48 * 1024 * 1024),
        name="route_select",
    )(a3)
    return keys.reshape(E, n_tok), rowpref[:, :, 0]


FFN_SPLITS = 4


def _ffn_body(x_ref, gv_ref, tok_ref, wg_hbm, wu_hbm, wd_hbm, o_ref,
              wg_f32, wu_f32, wd_f32, wg_bf, wu_bf, wd_bf, sem_ref, *, layer, first_expert):
    e = pl.program_id(0)
    n_experts = pl.num_programs(0)

    def weight_copies(local_expert):
        expert = first_expert + local_expert
        return (pltpu.make_async_copy(wg_hbm.at[layer, expert], wg_f32, sem_ref.at[0]),
                pltpu.make_async_copy(wu_hbm.at[layer, expert], wu_f32, sem_ref.at[1]),
                pltpu.make_async_copy(wd_hbm.at[layer, expert], wd_f32, sem_ref.at[2]))

    @pl.when(pl.program_id(1) == 0)
    def _():
        @pl.when(e == 0)
        def _():
            for c in weight_copies(e):
                c.start()

        for c in weight_copies(e):
            c.wait()
        for src, dst in ((wg_f32, wg_bf), (wu_f32, wu_bf), (wd_f32, wd_bf)):
            def cast_rows(r, carry, src=src, dst=dst):
                rows = pl.ds(pl.multiple_of(r * WEIGHT_CAST_ROWS, WEIGHT_CAST_ROWS), WEIGHT_CAST_ROWS)
                dst[rows, :] = src[rows, :].astype(BF16)
                return carry
            lax.fori_loop(0, src.shape[0] // WEIGHT_CAST_ROWS, cast_rows, 0)

        @pl.when(e + 1 < n_experts)
        def _():
            for c in weight_copies(e + 1):
                c.start()

    x = x_ref[0]
    g = jnp.dot(x, wg_bf[...], preferred_element_type=F32)
    u = jnp.dot(x, wu_bf[...], preferred_element_type=F32)
    a = (g * _sigmoid(g) * u).astype(BF16)
    y = jnp.dot(a, wd_bf[...], preferred_element_type=F32) * gv_ref[0]
    o_ref[0, :, :D_MODEL] = y.astype(o_ref.dtype)
    tok = tok_ref[0]
    lane = lax.broadcasted_iota(jnp.int32, (tok.shape[0], META_WIDTH), 1)
    meta = jnp.where(lane == 0, tok // TOKEN_RADIX, jnp.where(lane == 1, tok % TOKEN_RADIX, 0))
    o_ref[0, :, D_MODEL:] = meta.astype(F32).astype(o_ref.dtype)


def expert_ffn(xe, w_gate, w_up, w_down, layer, first_expert, gate_vals, tok, tm=512):
    E, rows, D = xe.shape
    F = w_gate.shape[-1]
    return pl.pallas_call(
        functools.partial(_ffn_body, layer=layer, first_expert=first_expert),
        grid=(E, rows // tm),
        in_specs=[
            pl.BlockSpec((1, tm, D), lambda e, i: (e, i, 0)),
            pl.BlockSpec((1, tm, 1), lambda e, i: (first_expert + e, i, 0)),
            pl.BlockSpec((1, tm, 1), lambda e, i: (first_expert + e, i, 0)),
            pl.BlockSpec(memory_space=pl.ANY),
            pl.BlockSpec(memory_space=pl.ANY),
            pl.BlockSpec(memory_space=pl.ANY),
        ],
        out_specs=pl.BlockSpec((1, tm, D + META_WIDTH), lambda e, i: (e, i, 0)),
        out_shape=jax.ShapeDtypeStruct((E, rows, D + META_WIDTH), BF16),
        scratch_shapes=[
            pltpu.VMEM((D, F), F32), pltpu.VMEM((D, F), F32), pltpu.VMEM((F, D), F32),
            pltpu.VMEM((D, F), BF16), pltpu.VMEM((D, F), BF16), pltpu.VMEM((F, D), BF16),
            pltpu.SemaphoreType.DMA((3,)),
        ],
        compiler_params=_cparams(2, 56),
        name="expert_ffn",
    )(xe, gate_vals[..., None], tok[..., None], w_gate, w_up, w_down)


def _combine_body(win_ref, minslot_ref, nr_ref, x_ref, gain_ref, *refs, base_tile, n_tiles, n_parts, keep_x):
    ye_win = refs[:N_EXPERTS]
    ye_hbm = refs[N_EXPERTS:N_EXPERTS + n_parts]
    if keep_x:
        o_ref, h_ref, buf_ref, sem_ref = refs[N_EXPERTS + n_parts:]
    else:
        h_ref, o_ref, buf_ref, sem_ref = refs[N_EXPERTS + n_parts:]
    per_part = N_EXPERTS // n_parts
    j = pl.program_id(0)
    t0 = ((j + base_tile) * COMBINE_TM).astype(F32)

    def table_index(e, k):
        return e * (n_tiles * COMBINE_ROUNDS) + j * COMBINE_ROUNDS + k

    def placed(window, k):
        rows, toks = [], []
        for e in range(N_EXPERTS):
            w = window(e)
            meta = w[:, D_MODEL:].astype(F32)
            tok = meta[:, 0:1] * float(TOKEN_RADIX) + meta[:, 1:2] - t0
            flat = table_index(e, k)
            slot = win_ref[flat] * WIN_BLK + lax.broadcasted_iota(jnp.int32, (WIN_ROWS, 1), 0)
            toks.append(jnp.where(slot >= minslot_ref[flat], tok, -1.0))
            rows.append(w[:, :D_MODEL])
        tok_all = jnp.concatenate(toks, axis=0)
        lane_t = lax.broadcasted_iota(jnp.int32, (N_EXPERTS * WIN_ROWS, COMBINE_TM), 1).astype(F32)
        place = jnp.where(tok_all == lane_t, 1.0, 0.0).astype(BF16)
        return _dot_tn(place, jnp.concatenate(rows, axis=0))

    o_ref[...] = x_ref[...] + placed(lambda e: ye_win[e][...].reshape(WIN_ROWS, D_MODEL + META_WIDTH), 0)

    def window_copy(e, k):
        return pltpu.make_async_copy(
            ye_hbm[e // per_part].at[e % per_part, pl.ds(win_ref[table_index(e, k)], WIN_BLKS)],
            buf_ref.at[e], sem_ref.at[e])

    def extra_round(k, carry):
        for e in range(N_EXPERTS):
            window_copy(e, k).start()
        for e in range(N_EXPERTS):
            window_copy(e, k).wait()
        o_ref[...] += placed(lambda e: buf_ref[e].reshape(WIN_ROWS, D_MODEL + META_WIDTH), k)
        return carry

    lax.fori_loop(1, nr_ref[j], extra_round, 0)
    h_ref[...] = _rms(o_ref[...], gain_ref[...]).astype(h_ref.dtype)


def combine(x, ye_parts, win, minslot, n_rounds, base_tile, gain, h_dtype, keep_x):
    per_part, rows, dext = ye_parts[0].shape
    E = per_part * len(ye_parts)
    n_tiles = n_rounds.shape[0]
    ye4 = [ye.reshape(per_part, rows // WIN_BLK, WIN_BLK, dext) for ye in ye_parts]

    def ye_spec(e):
        return pl.BlockSpec(
            (pl.Element(1), pl.Element(WIN_BLKS), pl.Element(WIN_BLK), pl.Element(dext)),
            lambda j, win_r, ms_r, nr_r: (e % per_part, win_r[e * (n_tiles * COMBINE_ROUNDS) + j * COMBINE_ROUNDS], 0, 0))

    grid_spec = pltpu.PrefetchScalarGridSpec(
        num_scalar_prefetch=3,
        grid=(n_tiles,),
        in_specs=[pl.BlockSpec((COMBINE_TM, D_MODEL), lambda j, *_: (j + base_tile, 0)),
                  pl.BlockSpec((1, D_MODEL), lambda j, *_: (0, 0))]
        + [ye_spec(e) for e in range(E)]
        + [pl.BlockSpec(memory_space=pl.ANY)] * len(ye4),
        out_specs=([pl.BlockSpec((COMBINE_TM, D_MODEL), lambda j, *_: (j + base_tile, 0))] if keep_x else [])
        + [pl.BlockSpec((COMBINE_TM, D_MODEL), lambda j, *_: (j, 0))],
        scratch_shapes=([] if keep_x else [pltpu.VMEM((COMBINE_TM, D_MODEL), F32)]) + [
            pltpu.VMEM((E, WIN_BLKS, WIN_BLK, dext), BF16),
            pltpu.SemaphoreType.DMA((E,)),
        ],
    )
    outs = pl.pallas_call(
        functools.partial(_combine_body, base_tile=base_tile, n_tiles=n_tiles, n_parts=len(ye4), keep_x=keep_x),
        grid_spec=grid_spec,
        out_shape=([jax.ShapeDtypeStruct(x.shape, x.dtype)] if keep_x else [])
        + [jax.ShapeDtypeStruct((n_tiles * COMBINE_TM, D_MODEL), h_dtype)],
        input_output_aliases={3: 0} if keep_x else {},
        compiler_params=_cparams(1, 48),
        name="combine",
    )(win.reshape(-1), minslot.reshape(-1), n_rounds, x, gain.reshape(1, D_MODEL),
      *[ye4[e // per_part] for e in range(E)], *ye4)
    return outs if keep_x else (None, outs[0])


def combine_tables(rowpref, cap, slot_base):
    E = rowpref.shape[0]
    end = slot_base + cap
    starts = slot_base + jnp.concatenate(
        [rowpref[:, ::COMBINE_TM // ROUTE_LANES], jnp.full((E, 1), cap, jnp.int32)], axis=1)
    lo, hi = starts[:, :-1], starts[:, 1:]
    a = (lo // WIN_BLK) * WIN_BLK
    need = jnp.where(hi > lo, -(-(hi - a) // WIN_ROWS), 0)
    k = jnp.arange(COMBINE_ROUNDS, dtype=jnp.int32)[None, None, :]
    kk = jnp.minimum(k, jnp.maximum(need[..., None] - 1, 0))
    win = jnp.minimum(a[..., None] + WIN_ROWS * kk, end - WIN_ROWS) // WIN_BLK
    minslot = jnp.where(k < need[..., None], a[..., None] + WIN_ROWS * k, end)
    return win.astype(jnp.int32), minslot.astype(jnp.int32), jnp.max(need, axis=0).astype(jnp.int32)


def kernel(x_prompt, x_sample, w_in, q_norm, k_norm, lower_bounds, hgrn_norm, w_proj_attn, w_proj_hgrn,
           w_out, norm_mix, norm_ffn, w_router, w_gate, w_up, w_down, norm_final):
    depth = w_in.shape[0]
    groups = (x_prompt, x_sample)
    seq = x_prompt.shape[1]
    group_tokens = [g.shape[0] * g.shape[1] for g in groups]
    xa, xb = (g.reshape(-1, D_MODEL) for g in groups)
    cos, sin = rope_tables(seq)
    w_attn_bf, w_hgrn_bf, w_out_bf = (w.astype(BF16) for w in (w_proj_attn, w_proj_hgrn, w_out))

    h_parts = [rms_norm_rows2(xa, xb, norm_mix[0], BF16)]
    for l in range(depth):
        proj_z = in_proj(h_parts, w_in, l, Z_BLOCKS, F32)
        proj_b = in_proj(h_parts, w_in, l, B_BLOCKS, BF16)
        q, k, v = qk_prep(proj_b, cos, sin, q_norm[l], k_norm[l], seq)
        attn = attention(q, k, v, seq)
        o_f, o_b = hgrn_scan(proj_z, proj_b, hgrn_params(lower_bounds, l), seq)
        merged = merge_branches(attn, o_f, o_b, proj_b, hgrn_norm[l], w_attn_bf, w_hgrn_bf, l)
        x, h_ffn, aff = out_proj_router(merged, w_out_bf, l, *((xa, xb) if l == 0 else (x, None)),
                                        norm_ffn[l], w_router[l])
        aff_t = aff.T
        idx_parts, gate_parts, tables = [], [], []
        start = slot_base = 0
        for n_tok in group_tokens:
            cap = EC_CAPACITY * n_tok // N_EXPERTS
            aff_g = aff_t[:, start:start + n_tok]
            keys, rowpref = route_select(aff_g, cap, start)
            idx = jnp.sort(keys, axis=1)[:, :cap]
            idx_parts.append(idx)
            gate_parts.append(jnp.take_along_axis(aff_g, idx - start, axis=1))
            tables.append(combine_tables(rowpref, cap, slot_base) + (start // COMBINE_TM,))
            start += n_tok
            slot_base += cap
        idx = jnp.concatenate(idx_parts, axis=1)
        gate_vals = jnp.concatenate(gate_parts, axis=1)
        per = N_EXPERTS // FFN_SPLITS
        ye = [expert_ffn(h_ffn[idx[e0:e0 + per]], w_gate, w_up, w_down, l, e0, gate_vals, idx)
              for e0 in range(0, N_EXPERTS, per)]
        last = l == depth - 1
        h_parts = []
        for win, minslot, n_rounds, base_tile in tables:
            x_next, h_g = combine(x, ye, win, minslot, n_rounds, base_tile,
                                  norm_final if last else norm_mix[l + 1], F32 if last else BF16, not last)
            x = x if last else x_next
            h_parts.append(h_g)

    return tuple(h_g.reshape(g.shape) for h_g, g in zip(h_parts, groups))
```

```python
import functools
import math

import jax
import jax.numpy as jnp
from jax import lax
from jax.experimental import pallas as pl
from jax.experimental.pallas import tpu as pltpu

F32 = jnp.float32
BF16 = jnp.bfloat16

D_MODEL = 2048
SEQ = 4096
GRID_W = 64
HEAD_DIM = 128
N_Q_HEADS = 8
N_KV_HEADS = 2
Q_GROUP = N_Q_HEADS // N_KV_HEADS
ATTN_WIDTH = N_Q_HEADS * HEAD_DIM
KV_WIDTH = N_KV_HEADS * HEAD_DIM
ROPE_THETA = 10000.0
ROPE_AXIS_PAIRS = HEAD_DIM // 4
N_HGRN_HEADS = 8
HGRN_DIM = 128
HGRN_WIDTH = N_HGRN_HEADS * HGRN_DIM
N_EXPERTS = 16
EC_CAPACITY = 2
EXPERT_FF = 1024
NORM_EPS = 1e-6
IN_WIDTH = 10752

QKV_WIDTH = ATTN_WIDTH + 2 * KV_WIDTH
IN_PROJ_TN = QKV_WIDTH
Z_BLOCKS = (1, 2)
B_BLOCKS = (0, 3, 4, 5, 6)
ZCOL_HQ = 0
ZCOL_ZF = ZCOL_HQ + HGRN_WIDTH
ZCOL_ZB = ZCOL_ZF + HGRN_WIDTH
BCOL_QKV = 0
BCOL_HI = BCOL_QKV + QKV_WIDTH
BCOL_HOG = BCOL_HI + HGRN_WIDTH
BCOL_GA = BCOL_HOG + HGRN_WIDTH
BCOL_GB = BCOL_GA + D_MODEL

ATTN_TK = 2048
V_EXT = 2 * HEAD_DIM

HGRN_CHUNK = 64
HGRN_HEADS_PER_STEP = 8
HGRN_CHUNKS_PER_STEP = 8

TOKEN_RADIX = 128
META_WIDTH = 128
COMBINE_TM = 256
WIN_BLK = 16
WIN_BLKS = 4
WIN_ROWS = WIN_BLK * WIN_BLKS
COMBINE_ROUNDS = -(-(WIN_BLK - 1 + COMBINE_TM) // WIN_ROWS)

WEIGHT_CAST_ROWS = 256

V7X_VMEM_BYTES = 64 * 1024 * 1024


def _cparams(n_grid, vmem_mb):
    assert vmem_mb * 1024 * 1024 < V7X_VMEM_BYTES
    return pltpu.CompilerParams(
        dimension_semantics=("arbitrary",) * n_grid,
        vmem_limit_bytes=vmem_mb * 1024 * 1024,
    )


def _sigmoid(x):
    return 1.0 / (1.0 + jnp.exp(-x))


def _rms(x, gain):
    return x * lax.rsqrt(jnp.mean(x * x, axis=-1, keepdims=True) + NORM_EPS) * gain


def _dot_nt(a, b):
    return lax.dot_general(a, b, (((1,), (1,)), ((), ())), preferred_element_type=F32)


def _dot_tn(a, b):
    return lax.dot_general(a, b, (((0,), (0,)), ((), ())), preferred_element_type=F32)


def _split_rows_specs(tm, D, n_first):
    return [pl.BlockSpec((tm, D), lambda i: (jnp.minimum(i, n_first - 1), 0)),
            pl.BlockSpec((tm, D), lambda i: (jnp.maximum(i - n_first, 0), 0))]


def _pick_rows(xa_ref, xb_ref, n_first):
    return jnp.where(pl.program_id(0) < n_first, xa_ref[...], xb_ref[...])


def _norm2_body(xa_ref, xb_ref, g_ref, o_ref, *, n_first):
    o_ref[...] = _rms(_pick_rows(xa_ref, xb_ref, n_first), g_ref[...]).astype(o_ref.dtype)


def rms_norm_rows2(xa, xb, gain, out_dtype, tm=512):
    D = xa.shape[1]
    T = xa.shape[0] + xb.shape[0]
    n_first = xa.shape[0] // tm
    return pl.pallas_call(
        functools.partial(_norm2_body, n_first=n_first),
        grid=(T // tm,),
        in_specs=_split_rows_specs(tm, D, n_first) + [pl.BlockSpec((1, D), lambda i: (0, 0))],
        out_specs=pl.BlockSpec((tm, D), lambda i: (i, 0)),
        out_shape=jax.ShapeDtypeStruct((T, D), out_dtype),
        compiler_params=_cparams(1, 32),
        name="rms_norm_rows2",
    )(xa, xb, gain.reshape(1, D))


def _mm_body(*refs, n_first):
    *a_refs, w_f32, o_ref, w_ref = refs
    i = pl.program_id(1)

    @pl.when(i == 0)
    def _():
        def cast_rows(r, carry):
            rows = pl.ds(pl.multiple_of(r * WEIGHT_CAST_ROWS, WEIGHT_CAST_ROWS), WEIGHT_CAST_ROWS)
            w_ref[rows, :] = w_f32[rows, :].astype(BF16)
            return carry
        lax.fori_loop(0, w_f32.shape[0] // WEIGHT_CAST_ROWS, cast_rows, 0)

    if len(a_refs) == 1:
        o_ref[...] = jnp.dot(a_refs[0][...], w_ref[...], preferred_element_type=F32).astype(o_ref.dtype)
        return

    @pl.when(i < n_first)
    def _():
        o_ref[...] = jnp.dot(a_refs[0][...], w_ref[...], preferred_element_type=F32).astype(o_ref.dtype)

    @pl.when(i >= n_first)
    def _():
        o_ref[...] = jnp.dot(a_refs[1][...], w_ref[...], preferred_element_type=F32).astype(o_ref.dtype)


def in_proj(a_parts, w, layer, col_blocks, out_dtype, tm=512):
    K = a_parts[0].shape[1]
    n_first = a_parts[0].shape[0] // tm
    M = sum(a.shape[0] for a in a_parts)
    tn = IN_PROJ_TN
    N = tn * len(col_blocks)
    assert all(b == col_blocks[1] + k for k, b in enumerate(col_blocks[1:]))

    def src_block(j):
        return jnp.where(j == 0, col_blocks[0], j + (col_blocks[1] - 1))

    if len(a_parts) == 1:
        a_specs = [pl.BlockSpec((tm, K), lambda j, i: (i, 0))]
    else:
        a_specs = [pl.BlockSpec((tm, K), lambda j, i: (jnp.minimum(i, n_first - 1), 0)),
                   pl.BlockSpec((tm, K), lambda j, i: (jnp.maximum(i - n_first, 0), 0))]
    return pl.pallas_call(
        functools.partial(_mm_body, n_first=n_first),
        grid=(N // tn, M // tm),
        in_specs=a_specs + [pl.BlockSpec((None, K, tn), lambda j, i: (layer, 0, src_block(j)))],
        out_specs=pl.BlockSpec((tm, tn), lambda j, i: (i, j)),
        out_shape=jax.ShapeDtypeStruct((M, N), out_dtype),
        scratch_shapes=[pltpu.VMEM((K, tn), BF16)],
        compiler_params=_cparams(2, 56),
        name="in_proj",
    )(*a_parts, w)


def _col_window(rows, width, col, row_block):
    return pl.BlockSpec((pl.Element(rows), pl.Element(width)), lambda *g: (row_block(*g) * rows, col))


def _qkprep_body(p_ref, cos_ref, sin_ref, qg_ref, kg_ref, q_ref, k_ref, v_ref, *, scale):
    cos = cos_ref[...]
    sin = sin_ref[...]
    lane = lax.broadcasted_iota(jnp.int32, cos.shape, 1)
    first = (lane & ROPE_AXIS_PAIRS) == 0
    xs = [p_ref[:, h * HEAD_DIM:(h + 1) * HEAD_DIM].astype(F32) for h in range(N_Q_HEADS + N_KV_HEADS)]
    gains = [qg_ref[...] * scale] * N_Q_HEADS + [kg_ref[...]] * N_KV_HEADS
    ms = [jnp.mean(x * x, axis=-1, keepdims=True) for x in xs]
    ys = [x * lax.rsqrt(m + NORM_EPS) * g for x, m, g in zip(xs, ms, gains)]
    ups = [pltpu.roll(y, HEAD_DIM - ROPE_AXIS_PAIRS, 1) for y in ys]
    downs = [pltpu.roll(y, ROPE_AXIS_PAIRS, 1) for y in ys]
    outs = [y * cos + jnp.where(first, u, d) * sin for y, u, d in zip(ys, ups, downs)]
    for h in range(N_Q_HEADS):
        q_ref[:, h * HEAD_DIM:(h + 1) * HEAD_DIM] = outs[h].astype(q_ref.dtype)
    for h in range(N_KV_HEADS):
        k_ref[:, h * HEAD_DIM:(h + 1) * HEAD_DIM] = outs[N_Q_HEADS + h].astype(k_ref.dtype)
    ones_col = jnp.where(lax.broadcasted_iota(jnp.int32, (cos.shape[0], V_EXT - HEAD_DIM), 1) == 0, 1.0, 0.0)
    for h in range(N_KV_HEADS):
        src = ATTN_WIDTH + KV_WIDTH + h * HEAD_DIM
        v_ref[:, h * V_EXT:h * V_EXT + HEAD_DIM] = p_ref[:, src:src + HEAD_DIM].astype(v_ref.dtype)
        v_ref[:, h * V_EXT + HEAD_DIM:(h + 1) * V_EXT] = ones_col.astype(v_ref.dtype)


def qk_prep(proj, cos, sin, q_gain, k_gain, seq, tm=512):
    T = proj.shape[0]
    nblk = seq // tm
    return pl.pallas_call(
        functools.partial(_qkprep_body, scale=HEAD_DIM ** -0.5 * math.log2(math.e)),
        grid=(T // tm,),
        in_specs=[
            pl.BlockSpec((tm, QKV_WIDTH), lambda i: (i, BCOL_QKV // QKV_WIDTH)),
            pl.BlockSpec((tm, HEAD_DIM), lambda i: (i % nblk, 0)),
            pl.BlockSpec((tm, HEAD_DIM), lambda i: (i % nblk, 0)),
            pl.BlockSpec((1, HEAD_DIM), lambda i: (0, 0)),
            pl.BlockSpec((1, HEAD_DIM), lambda i: (0, 0)),
        ],
        out_specs=[
            pl.BlockSpec((tm, ATTN_WIDTH), lambda i: (i, 0)),
            pl.BlockSpec((tm, KV_WIDTH), lambda i: (i, 0)),
            pl.BlockSpec((tm, N_KV_HEADS * V_EXT), lambda i: (i, 0)),
        ],
        out_shape=[
            jax.ShapeDtypeStruct((T, ATTN_WIDTH), BF16),
            jax.ShapeDtypeStruct((T, KV_WIDTH), BF16),
            jax.ShapeDtypeStruct((T, N_KV_HEADS * V_EXT), BF16),
        ],
        compiler_params=_cparams(1, 32),
        name="qk_prep",
    )(proj, cos, sin, q_gain.reshape(1, HEAD_DIM), k_gain.reshape(1, HEAD_DIM))


def rope_tables(seq):
    rows = seq // GRID_W
    row = jnp.repeat(jnp.arange(rows, dtype=F32), GRID_W)
    col = jnp.tile(jnp.arange(GRID_W, dtype=F32), rows)
    inv_freq = 1.0 / (ROPE_THETA ** (jnp.arange(ROPE_AXIS_PAIRS, dtype=F32) / ROPE_AXIS_PAIRS))
    ang_r = row[:, None] * inv_freq
    ang_c = col[:, None] * inv_freq
    cos = jnp.concatenate([jnp.cos(ang_r), jnp.cos(ang_r), jnp.cos(ang_c), jnp.cos(ang_c)], axis=1)
    sin = jnp.concatenate([-jnp.sin(ang_r), jnp.sin(ang_r), -jnp.sin(ang_c), jnp.sin(ang_c)], axis=1)
    return cos, sin


def _attn_body(q_ref, k_ref, v_ref, o_ref):
    n_chunks = k_ref.shape[0] // ATTN_TK
    sls = [slice(g * HEAD_DIM, (g + 1) * HEAD_DIM) for g in range(Q_GROUP)]
    qs = [q_ref[:, sl] for sl in sls]
    ms = accs = None
    for c in range(n_chunks):
        k = k_ref[c * ATTN_TK:(c + 1) * ATTN_TK, :]
        v = v_ref[c * ATTN_TK:(c + 1) * ATTN_TK, :]
        ss = [_dot_nt(q, k) for q in qs]
        new_ms = [jnp.max(s, axis=-1, keepdims=True) for s in ss]
        if c > 0:
            new_ms = [jnp.maximum(m, cm) for m, cm in zip(ms, new_ms)]
        ps = [jnp.exp2(s - m).astype(BF16) for s, m in zip(ss, new_ms)]
        pvs = [jnp.dot(p, v, preferred_element_type=F32) for p in ps]
        if c == 0:
            accs = pvs
        else:
            alphas = [jnp.exp2(m - nm) for m, nm in zip(ms, new_ms)]
            accs = [acc * a + pv for acc, a, pv in zip(accs, alphas, pvs)]
        ms = new_ms
    for sl, acc in zip(sls, accs):
        o_ref[:, sl] = (acc[:, :HEAD_DIM] / acc[:, HEAD_DIM:HEAD_DIM + 1]).astype(o_ref.dtype)


def attention(q, k, v, seq, tq=512):
    T = q.shape[0]
    n_seq = T // seq
    nq = seq // tq
    gw = Q_GROUP * HEAD_DIM
    return pl.pallas_call(
        _attn_body,
        grid=(n_seq, N_KV_HEADS, nq),
        in_specs=[
            pl.BlockSpec((tq, gw), lambda b, h, i: (b * nq + i, h)),
            pl.BlockSpec((seq, HEAD_DIM), lambda b, h, i: (b, h)),
            pl.BlockSpec((seq, V_EXT), lambda b, h, i: (b, h)),
        ],
        out_specs=pl.BlockSpec((tq, gw), lambda b, h, i: (b * nq + i, h)),
        out_shape=jax.ShapeDtypeStruct((T, ATTN_WIDTH), BF16),
        compiler_params=_cparams(3, 48),
        name="attention",
    )(q, k, v)


def _split3(x):
    hi = x.astype(BF16)
    r1 = x - hi.astype(F32)
    mid = r1.astype(BF16)
    lo = (r1 - mid.astype(F32)).astype(BF16)
    return hi, mid, lo


def _hgrn_gates(z, loglb, log1mlb, omlb):
    e = jnp.exp(-jnp.abs(z))
    r = 1.0 / (1.0 + e)
    kk = omlb * jnp.where(z >= 0, e * r, r)
    cc = log1mlb + (jnp.minimum(z, 0.0) - jnp.log(1.0 + e))
    g = jnp.maximum(loglb, cc) + jnp.log(1.0 + jnp.exp(-jnp.abs(loglb - cc)))
    return g, kk


def _level_refs(b, m, fwd):
    C, NW = b.shape
    if 2 * m >= 8:
        b3 = b.reshape(C // (2 * m), 2 * m, NW)
        r = b3[:, m - 1:m, :] if fwd else b3[:, m:m + 1, :]
        return jnp.broadcast_to(r, b3.shape).reshape(C, NW)
    pos = lax.broadcasted_iota(jnp.int32, (C, NW), 0) % (2 * m)
    target = (m - 1) if fwd else m
    out = b
    for off in range(2 * m):
        if off != target:
            out = jnp.where(pos == off, pltpu.roll(b, (off - target) % C, 0), out)
    return out


def _hgrn_stages(fwd, rows, q_ref, z_ref, v_ref, p0, par_ref, st_ref, o_ref):
    C, W = HGRN_CHUNK, HGRN_DIM
    NH = HGRN_HEADS_PER_STEP
    row = lax.broadcasted_iota(jnp.int32, (C, C), 0)
    col = lax.broadcasted_iota(jnp.int32, (C, C), 1)
    tri = jnp.where((col <= row) if fwd else (col >= row), 1.0, 0.0).astype(BF16)
    g, kk = _hgrn_gates(z_ref[rows, :], par_ref[p0:p0 + 1, :], par_ref[p0 + 1:p0 + 2, :], par_ref[p0 + 2:p0 + 3, :])
    yield
    hi, mid, lo = _split3(g)
    b = (jnp.dot(tri, hi, preferred_element_type=F32) + jnp.dot(tri, mid, preferred_element_type=F32)
         + jnp.dot(tri, lo, preferred_element_type=F32))
    yield
    q = q_ref[rows, :]
    vb = v_ref[rows, :].astype(BF16)
    tot = b[C - 1:C, :] if fwd else b[0:1, :]
    q_in = (q * jnp.exp(b)).astype(BF16)
    k_st = (kk * jnp.exp(tot - b)).astype(BF16)
    dec = jnp.exp(tot)
    levels = [(q.astype(BF16), kk.astype(BF16), (row == col))]
    rows_i = lax.broadcasted_iota(jnp.int32, (C, NH * W), 0)
    m = 1
    while 2 * m <= C:
        if m == 1:
            e = jnp.where((rows_i % 2 == 1) if fwd else (rows_i % 2 == 0), 1.0 - kk, 1.0)
        else:
            e = jnp.exp(-jnp.abs(b - _level_refs(b, m, fwd)))
        t_up = (row % (2 * m)) >= m
        s_up = (col % (2 * m)) >= m
        pair = (row // (2 * m) == col // (2 * m)) & ((t_up & ~s_up) if fwd else (~t_up & s_up))
        q_side = ((rows_i % (2 * m)) >= m) if fwd else ((rows_i % (2 * m)) < m)
        x = (jnp.where(q_side, q, kk) * e).astype(BF16)
        levels.append((x, x, pair))
        m *= 2
    yield
    res = []
    for h in range(NH):
        sl = slice(h * W, (h + 1) * W)
        st = st_ref[h]
        o_inter = _dot_nt(q_in[:, sl], st.astype(BF16))
        st_ref[h] = st * dec[:, sl] + _dot_tn(vb[:, sl], k_st[:, sl])
        scores = [_dot_nt(ql[:, sl], kl[:, sl]) for (ql, kl, pair) in levels]
        res.append((o_inter, scores))
    yield
    masked = []
    for (o_inter, scores) in res:
        s_all = jnp.zeros((C, C), F32)
        for s, (ql, kl, pair) in zip(scores, levels):
            s_all = jnp.where(pair, s, s_all)
        masked.append(s_all.astype(BF16))
    yield
    for h, ((o_inter, scores), s_all) in enumerate(zip(res, masked)):
        sl = slice(h * W, (h + 1) * W)
        o_ref[rows, sl] = (o_inter + jnp.dot(s_all, vb[:, sl], preferred_element_type=F32)).astype(o_ref.dtype)
    yield


def _hgrn_body(qf_ref, zf_ref, vf_ref, qb_ref, zb_ref, vb_ref, par_ref, of_ref, ob_ref, stf_ref, stb_ref):
    C = HGRN_CHUNK
    n = HGRN_CHUNKS_PER_STEP

    @pl.when(pl.program_id(2) == 0)
    def _():
        stf_ref[...] = jnp.zeros(stf_ref.shape, F32)
        stb_ref[...] = jnp.zeros(stb_ref.shape, F32)

    gens = []
    for c in range(n):
        gens.append(_hgrn_stages(True, slice(c * C, (c + 1) * C), qf_ref, zf_ref, vf_ref, 0, par_ref, stf_ref, of_ref))
        gens.append(_hgrn_stages(False, slice((n - 1 - c) * C, (n - c) * C), qb_ref, zb_ref, vb_ref, 3, par_ref, stb_ref, ob_ref))
    for _ in range(6):
        for gen in gens:
            next(gen)


def hgrn_scan(proj_z, proj_b, params, seq):
    T = proj_z.shape[0]
    n_seq = T // seq
    C = HGRN_CHUNK * HGRN_CHUNKS_PER_STEP
    nc = seq // C
    w = HGRN_HEADS_PER_STEP * HGRN_DIM
    n_hh = HGRN_WIDTH // w

    assert n_hh == 1

    def fwd_spec(col):
        return _col_window(C, w, col, lambda b, hh, j: b * nc + j)

    def bwd_spec(col):
        return _col_window(C, w, col, lambda b, hh, j: b * nc + nc - 1 - j)

    return pl.pallas_call(
        _hgrn_body,
        grid=(n_seq, n_hh, nc),
        in_specs=[
            fwd_spec(ZCOL_HQ), fwd_spec(ZCOL_ZF), fwd_spec(BCOL_HI),
            bwd_spec(ZCOL_HQ), bwd_spec(ZCOL_ZB), bwd_spec(BCOL_HI),
            pl.BlockSpec((8, w), lambda b, hh, j: (0, hh)),
        ],
        out_specs=[
            pl.BlockSpec((C, w), lambda b, hh, j: (b * nc + j, hh)),
            pl.BlockSpec((C, w), lambda b, hh, j: (b * nc + nc - 1 - j, hh)),
        ],
        out_shape=[jax.ShapeDtypeStruct((T, HGRN_WIDTH), BF16), jax.ShapeDtypeStruct((T, HGRN_WIDTH), BF16)],
        scratch_shapes=[
            pltpu.VMEM((HGRN_HEADS_PER_STEP, HGRN_DIM, HGRN_DIM), F32),
            pltpu.VMEM((HGRN_HEADS_PER_STEP, HGRN_DIM, HGRN_DIM), F32),
        ],
        compiler_params=_cparams(3, 56),
        name="hgrn_scan",
    )(proj_z, proj_z, proj_b, proj_z, proj_z, proj_b, params)


def hgrn_params(lower_bounds, layer):
    lb_all = jnp.cumsum(jax.nn.softmax(lower_bounds.astype(F32), axis=1), axis=1)
    lb_all = lb_all - lb_all[:, :1]
    rows = []
    for d in range(2):
        lb = lb_all[d, layer]
        rows += [jnp.log(lb), jnp.log1p(-lb), 1.0 - lb]
    rows += [jnp.zeros_like(rows[0])] * 2
    return jnp.stack(rows, axis=0)


def _merge_body(attn_ref, of_ref, ob_ref, hog_ref, ga_ref, gb_ref, hg_ref, wa_ref, wh_ref, o_ref):
    ya = jnp.dot(attn_ref[...], wa_ref[...], preferred_element_type=F32)
    hs = []
    for h in range(N_HGRN_HEADS):
        sl = slice(h * HGRN_DIM, (h + 1) * HGRN_DIM)
        o = _rms(of_ref[:, sl].astype(F32) + ob_ref[:, sl].astype(F32), hg_ref[...])
        og = hog_ref[:, sl].astype(F32)
        hs.append((o * (og * _sigmoid(og))).astype(BF16))
    yh = jnp.dot(jnp.concatenate(hs, axis=1), wh_ref[...], preferred_element_type=F32)
    o_ref[...] = (_sigmoid(ga_ref[...].astype(F32)) * ya + _sigmoid(gb_ref[...].astype(F32)) * yh).astype(o_ref.dtype)


def merge_branches(attn, o_f, o_b, proj_b, hgrn_gain, w_attn, w_hgrn, layer, tm=256):
    T = attn.shape[0]
    D = D_MODEL
    return pl.pallas_call(
        _merge_body,
        grid=(T // tm,),
        in_specs=[
            pl.BlockSpec((tm, ATTN_WIDTH), lambda i: (i, 0)),
            pl.BlockSpec((tm, HGRN_WIDTH), lambda i: (i, 0)),
            pl.BlockSpec((tm, HGRN_WIDTH), lambda i: (i, 0)),
            _col_window(tm, HGRN_WIDTH, BCOL_HOG, lambda i: i),
            _col_window(tm, D, BCOL_GA, lambda i: i),
            _col_window(tm, D, BCOL_GB, lambda i: i),
            pl.BlockSpec((1, HGRN_DIM), lambda i: (0, 0)),
            pl.BlockSpec((None, ATTN_WIDTH, D), lambda i: (layer, 0, 0)),
            pl.BlockSpec((None, HGRN_WIDTH, D), lambda i: (layer, 0, 0)),
        ],
        out_specs=pl.BlockSpec((tm, D), lambda i: (i, 0)),
        out_shape=jax.ShapeDtypeStruct((T, D), BF16),
        compiler_params=_cparams(1, 56),
        name="merge_branches",
    )(attn, o_f, o_b, proj_b, proj_b, proj_b, hgrn_gain.reshape(1, HGRN_DIM), w_attn, w_hgrn)


def _outproj_body(m_ref, w_ref, xa_ref, xb_ref, g_ref, wr_ref, xo_ref, h_ref, aff_ref, *, n_first):
    xn = _pick_rows(xa_ref, xb_ref, n_first) + jnp.dot(m_ref[...], w_ref[...], preferred_element_type=F32)
    xo_ref[...] = xn
    h = _rms(xn, g_ref[...])
    h_ref[...] = h.astype(h_ref.dtype)
    hb = h.astype(BF16)
    hl = (h - hb.astype(F32)).astype(BF16)
    both = jnp.dot(hb, wr_ref[...], preferred_element_type=F32)
    logits = (both[:, :N_EXPERTS] + both[:, N_EXPERTS:]
              + jnp.dot(hl, wr_ref[:, :N_EXPERTS], preferred_element_type=F32))
    mx = jnp.max(logits, axis=1, keepdims=True)
    ex = jnp.exp(logits - mx)
    aff_ref[...] = ex / jnp.sum(ex, axis=1, keepdims=True)


def out_proj_router(merged, w_out, layer, xa, xb, ffn_gain, w_router, tm=256):
    D = xa.shape[1]
    n_first = xa.shape[0] // tm
    xb = xa if xb is None else xb
    T = merged.shape[0]
    wr_hi = w_router.astype(BF16)
    wr_lo = (w_router - wr_hi.astype(F32)).astype(BF16)
    wr = jnp.concatenate([wr_hi, wr_lo], axis=1)
    return pl.pallas_call(
        functools.partial(_outproj_body, n_first=n_first),
        grid=(T // tm,),
        in_specs=[
            pl.BlockSpec((tm, D), lambda i: (i, 0)),
            pl.BlockSpec((None, D, D), lambda i: (layer, 0, 0)),
            *_split_rows_specs(tm, D, n_first),
            pl.BlockSpec((1, D), lambda i: (0, 0)),
            pl.BlockSpec((D, 2 * N_EXPERTS), lambda i: (0, 0)),
        ],
        out_specs=[
            pl.BlockSpec((tm, D), lambda i: (i, 0)),
            pl.BlockSpec((tm, D), lambda i: (i, 0)),
            pl.BlockSpec((tm, N_EXPERTS), lambda i: (i, 0)),
        ],
        out_shape=[
            jax.ShapeDtypeStruct((T, D), F32),
            jax.ShapeDtypeStruct((T, D), BF16),
            jax.ShapeDtypeStruct((T, N_EXPERTS), F32),
        ],
        compiler_params=_cparams(1, 48),
        name="out_proj_router",
    )(merged, w_out, xa, xb, ffn_gain.reshape(1, D), wr)


ROUTE_LANES = 128
NOT_SELECTED = 2 ** 30


def _route_body(a_ref, key_ref, rowpref_ref, *, cap, base_tok):
    a = a_ref[...]
    E, nc, L = a.shape
    bits = pltpu.bitcast(a, jnp.int32)

    def count(mask):
        ones = jnp.where(mask, 1.0, 0.0)
        return jnp.sum(jnp.sum(ones, axis=1, keepdims=True), axis=2, keepdims=True)

    def bisect(_, carry):
        lo, hi = carry
        mid = lo + jnp.right_shift(hi - lo, 1)
        ge = count(bits >= mid) >= cap
        return jnp.where(ge, mid, lo), jnp.where(ge, hi, mid)

    lo0 = jnp.zeros((E, 1, 1), jnp.int32)
    hi0 = jnp.full((E, 1, 1), 0x7F800000, jnp.int32)
    thr, _ = lax.fori_loop(0, 31, bisect, (lo0, hi0))

    s_i = lax.broadcasted_iota(jnp.int32, (L, L), 0)
    t_i = lax.broadcasted_iota(jnp.int32, (L, L), 1)
    incl_lane = jnp.where(s_i <= t_i, 1.0, 0.0).astype(BF16)
    r_i = lax.broadcasted_iota(jnp.int32, (E, nc, nc), 1)
    c_i = lax.broadcasted_iota(jnp.int32, (E, nc, nc), 2)
    rows_before = jnp.where(c_i < r_i, 1.0, 0.0).astype(BF16)

    def prefix(mask):
        ones = jnp.where(mask, 1.0, 0.0)
        incl = jnp.dot(ones.astype(BF16).reshape(E * nc, L), incl_lane, preferred_element_type=F32).reshape(E, nc, L)
        row_tot = jnp.broadcast_to(incl[:, :, L - 1:L], (E, nc, L)).astype(BF16)
        before = lax.dot_general(rows_before, row_tot, (((2,), (1,)), ((0,), (0,))), preferred_element_type=F32)
        return ones, incl, before

    gt = bits > thr
    eq = bits == thr
    need_eq = cap - count(gt)
    eq_f, eq_incl, eq_before = prefix(eq)
    take_eq = jnp.where(eq_before + eq_incl - eq_f < need_eq, eq_f, 0.0)
    sel = jnp.where(gt, 1.0, take_eq) > 0.5
    _, _, sel_before = prefix(sel)
    tok = (base_tok + lax.broadcasted_iota(jnp.int32, (E, nc, L), 1) * L
           + lax.broadcasted_iota(jnp.int32, (E, nc, L), 2))
    key_ref[...] = jnp.where(sel, tok, NOT_SELECTED)
    rowpref_ref[...] = sel_before.astype(jnp.int32)


def route_select(aff_group_t, cap, base_tok):
    E, n_tok = aff_group_t.shape
    nc = n_tok // ROUTE_LANES
    a3 = aff_group_t.reshape(E, nc, ROUTE_LANES)
    keys, rowpref = pl.pallas_call(
        functools.partial(_route_body, cap=cap, base_tok=base_tok),
        out_shape=[jax.ShapeDtypeStruct((E, nc, ROUTE_LANES), jnp.int32)] * 2,
        compiler_params=pltpu.CompilerParams(vmem_limit_bytes=48 * 1024 * 1024),
        name="route_select",
    )(a3)
    return keys.reshape(E, n_tok), rowpref[:, :, 0]


FFN_SPLITS = 4


def _ffn_body(x_ref, gv_ref, tok_ref, wg_hbm, wu_hbm, wd_hbm, o_ref,
              wg_f32, wu_f32, wd_f32, wg_bf, wu_bf, wd_bf, sem_ref, *, layer, first_expert):
    e = pl.program_id(0)
    n_experts = pl.num_programs(0)

    def weight_copies(local_expert):
        expert = first_expert + local_expert
        return (pltpu.make_async_copy(wg_hbm.at[layer, expert], wg_f32, sem_ref.at[0]),
                pltpu.make_async_copy(wu_hbm.at[layer, expert], wu_f32, sem_ref.at[1]),
                pltpu.make_async_copy(wd_hbm.at[layer, expert], wd_f32, sem_ref.at[2]))

    @pl.when(pl.program_id(1) == 0)
    def _():
        @pl.when(e == 0)
        def _():
            for c in weight_copies(e):
                c.start()

        for c in weight_copies(e):
            c.wait()
        for src, dst in ((wg_f32, wg_bf), (wu_f32, wu_bf), (wd_f32, wd_bf)):
            def cast_rows(r, carry, src=src, dst=dst):
                rows = pl.ds(pl.multiple_of(r * WEIGHT_CAST_ROWS, WEIGHT_CAST_ROWS), WEIGHT_CAST_ROWS)
                dst[rows, :] = src[rows, :].astype(BF16)
                return carry
            lax.fori_loop(0, src.shape[0] // WEIGHT_CAST_ROWS, cast_rows, 0)

        @pl.when(e + 1 < n_experts)
        def _():
            for c in weight_copies(e + 1):
                c.start()

    x = x_ref[0]
    g = jnp.dot(x, wg_bf[...], preferred_element_type=F32)
    u = jnp.dot(x, wu_bf[...], preferred_element_type=F32)
    a = (g * _sigmoid(g) * u).astype(BF16)
    y = jnp.dot(a, wd_bf[...], preferred_element_type=F32) * gv_ref[0]
    o_ref[0, :, :D_MODEL] = y.astype(o_ref.dtype)
    tok = tok_ref[0]
    lane = lax.broadcasted_iota(jnp.int32, (tok.shape[0], META_WIDTH), 1)
    meta = jnp.where(lane == 0, tok // TOKEN_RADIX, jnp.where(lane == 1, tok % TOKEN_RADIX, 0))
    o_ref[0, :, D_MODEL:] = meta.astype(F32).astype(o_ref.dtype)


def expert_ffn(xe, w_gate, w_up, w_down, layer, first_expert, gate_vals, tok, tm=512):
    E, rows, D = xe.shape
    F = w_gate.shape[-1]
    return pl.pallas_call(
        functools.partial(_ffn_body, layer=layer, first_expert=first_expert),
        grid=(E, rows // tm),
        in_specs=[
            pl.BlockSpec((1, tm, D), lambda e, i: (e, i, 0)),
            pl.BlockSpec((1, tm, 1), lambda e, i: (first_expert + e, i, 0)),
            pl.BlockSpec((1, tm, 1), lambda e, i: (first_expert + e, i, 0)),
            pl.BlockSpec(memory_space=pl.ANY),
            pl.BlockSpec(memory_space=pl.ANY),
            pl.BlockSpec(memory_space=pl.ANY),
        ],
        out_specs=pl.BlockSpec((1, tm, D + META_WIDTH), lambda e, i: (e, i, 0)),
        out_shape=jax.ShapeDtypeStruct((E, rows, D + META_WIDTH), BF16),
        scratch_shapes=[
            pltpu.VMEM((D, F), F32), pltpu.VMEM((D, F), F32), pltpu.VMEM((F, D), F32),
            pltpu.VMEM((D, F), BF16), pltpu.VMEM((D, F), BF16), pltpu.VMEM((F, D), BF16),
            pltpu.SemaphoreType.DMA((3,)),
        ],
        compiler_params=_cparams(2, 56),
        name="expert_ffn",
    )(xe, gate_vals[..., None], tok[..., None], w_gate, w_up, w_down)


def _combine_body(win_ref, minslot_ref, nr_ref, x_ref, gain_ref, *refs, base_tile, n_tiles, n_parts, keep_x):
    ye_win = refs[:N_EXPERTS]
    ye_hbm = refs[N_EXPERTS:N_EXPERTS + n_parts]
    if keep_x:
        o_ref, h_ref, buf_ref, sem_ref = refs[N_EXPERTS + n_parts:]
    else:
        h_ref, o_ref, buf_ref, sem_ref = refs[N_EXPERTS + n_parts:]
    per_part = N_EXPERTS // n_parts
    j = pl.program_id(0)
    t0 = ((j + base_tile) * COMBINE_TM).astype(F32)

    def table_index(e, k):
        return e * (n_tiles * COMBINE_ROUNDS) + j * COMBINE_ROUNDS + k

    def placed(window, k):
        rows, toks = [], []
        for e in range(N_EXPERTS):
            w = window(e)
            meta = w[:, D_MODEL:].astype(F32)
            tok = meta[:, 0:1] * float(TOKEN_RADIX) + meta[:, 1:2] - t0
            flat = table_index(e, k)
            slot = win_ref[flat] * WIN_BLK + lax.broadcasted_iota(jnp.int32, (WIN_ROWS, 1), 0)
            toks.append(jnp.where(slot >= minslot_ref[flat], tok, -1.0))
            rows.append(w[:, :D_MODEL])
        tok_all = jnp.concatenate(toks, axis=0)
        lane_t = lax.broadcasted_iota(jnp.int32, (N_EXPERTS * WIN_ROWS, COMBINE_TM), 1).astype(F32)
        place = jnp.where(tok_all == lane_t, 1.0, 0.0).astype(BF16)
        return _dot_tn(place, jnp.concatenate(rows, axis=0))

    o_ref[...] = x_ref[...] + placed(lambda e: ye_win[e][...].reshape(WIN_ROWS, D_MODEL + META_WIDTH), 0)

    def window_copy(e, k):
        return pltpu.make_async_copy(
            ye_hbm[e // per_part].at[e % per_part, pl.ds(win_ref[table_index(e, k)], WIN_BLKS)],
            buf_ref.at[e], sem_ref.at[e])

    def extra_round(k, carry):
        for e in range(N_EXPERTS):
            window_copy(e, k).start()
        for e in range(N_EXPERTS):
            window_copy(e, k).wait()
        o_ref[...] += placed(lambda e: buf_ref[e].reshape(WIN_ROWS, D_MODEL + META_WIDTH), k)
        return carry

    lax.fori_loop(1, nr_ref[j], extra_round, 0)
    h_ref[...] = _rms(o_ref[...], gain_ref[...]).astype(h_ref.dtype)


def combine(x, ye_parts, win, minslot, n_rounds, base_tile, gain, h_dtype, keep_x):
    per_part, rows, dext = ye_parts[0].shape
    E = per_part * len(ye_parts)
    n_tiles = n_rounds.shape[0]
    ye4 = [ye.reshape(per_part, rows // WIN_BLK, WIN_BLK, dext) for ye in ye_parts]

    def ye_spec(e):
        return pl.BlockSpec(
            (pl.Element(1), pl.Element(WIN_BLKS), pl.Element(WIN_BLK), pl.Element(dext)),
            lambda j, win_r, ms_r, nr_r: (e % per_part, win_r[e * (n_tiles * COMBINE_ROUNDS) + j * COMBINE_ROUNDS], 0, 0))

    grid_spec = pltpu.PrefetchScalarGridSpec(
        num_scalar_prefetch=3,
        grid=(n_tiles,),
        in_specs=[pl.BlockSpec((COMBINE_TM, D_MODEL), lambda j, *_: (j + base_tile, 0)),
                  pl.BlockSpec((1, D_MODEL), lambda j, *_: (0, 0))]
        + [ye_spec(e) for e in range(E)]
        + [pl.BlockSpec(memory_space=pl.ANY)] * len(ye4),
        out_specs=([pl.BlockSpec((COMBINE_TM, D_MODEL), lambda j, *_: (j + base_tile, 0))] if keep_x else [])
        + [pl.BlockSpec((COMBINE_TM, D_MODEL), lambda j, *_: (j, 0))],
        scratch_shapes=([] if keep_x else [pltpu.VMEM((COMBINE_TM, D_MODEL), F32)]) + [
            pltpu.VMEM((E, WIN_BLKS, WIN_BLK, dext), BF16),
            pltpu.SemaphoreType.DMA((E,)),
        ],
    )
    outs = pl.pallas_call(
        functools.partial(_combine_body, base_tile=base_tile, n_tiles=n_tiles, n_parts=len(ye4), keep_x=keep_x),
        grid_spec=grid_spec,
        out_shape=([jax.ShapeDtypeStruct(x.shape, x.dtype)] if keep_x else [])
        + [jax.ShapeDtypeStruct((n_tiles * COMBINE_TM, D_MODEL), h_dtype)],
        input_output_aliases={3: 0} if keep_x else {},
        compiler_params=_cparams(1, 48),
        name="combine",
    )(win.reshape(-1), minslot.reshape(-1), n_rounds, x, gain.reshape(1, D_MODEL),
      *[ye4[e // per_part] for e in range(E)], *ye4)
    return outs if keep_x else (None, outs[0])


def combine_tables(rowpref, cap, slot_base):
    E = rowpref.shape[0]
    end = slot_base + cap
    starts = slot_base + jnp.concatenate(
        [rowpref[:, ::COMBINE_TM // ROUTE_LANES], jnp.full((E, 1), cap, jnp.int32)], axis=1)
    lo, hi = starts[:, :-1], starts[:, 1:]
    a = (lo // WIN_BLK) * WIN_BLK
    need = jnp.where(hi > lo, -(-(hi - a) // WIN_ROWS), 0)
    k = jnp.arange(COMBINE_ROUNDS, dtype=jnp.int32)[None, None, :]
    kk = jnp.minimum(k, jnp.maximum(need[..., None] - 1, 0))
    win = jnp.minimum(a[..., None] + WIN_ROWS * kk, end - WIN_ROWS) // WIN_BLK
    minslot = jnp.where(k < need[..., None], a[..., None] + WIN_ROWS * k, end)
    return win.astype(jnp.int32), minslot.astype(jnp.int32), jnp.max(need, axis=0).astype(jnp.int32)


def kernel(x_prompt, x_sample, w_in, q_norm, k_norm, lower_bounds, hgrn_norm, w_proj_attn, w_proj_hgrn,
           w_out, norm_mix, norm_ffn, w_router, w_gate, w_up, w_down, norm_final):
    depth = w_in.shape[0]
    groups = (x_prompt, x_sample)
    seq = x_prompt.shape[1]
    group_tokens = [g.shape[0] * g.shape[1] for g in groups]
    xa, xb = (g.reshape(-1, D_MODEL) for g in groups)
    cos, sin = rope_tables(seq)
    w_attn_bf, w_hgrn_bf, w_out_bf = (w.astype(BF16) for w in (w_proj_attn, w_proj_hgrn, w_out))

    h_parts = [rms_norm_rows2(xa, xb, norm_mix[0], BF16)]
    for l in range(depth):
        proj_z = in_proj(h_parts, w_in, l, Z_BLOCKS, F32)
        proj_b = in_proj(h_parts, w_in, l, B_BLOCKS, BF16)
        q, k, v = qk_prep(proj_b, cos, sin, q_norm[l], k_norm[l], seq)
        attn = attention(q, k, v, seq)
        o_f, o_b = hgrn_scan(proj_z, proj_b, hgrn_params(lower_bounds, l), seq)
        merged = merge_branches(attn, o_f, o_b, proj_b, hgrn_norm[l], w_attn_bf, w_hgrn_bf, l)
        x, h_ffn, aff = out_proj_router(merged, w_out_bf, l, *((xa, xb) if l == 0 else (x, None)),
                                        norm_ffn[l], w_router[l])
        aff_t = aff.T
        idx_parts, gate_parts, tables = [], [], []
        start = slot_base = 0
        for n_tok in group_tokens:
            cap = EC_CAPACITY * n_tok // N_EXPERTS
            aff_g = aff_t[:, start:start + n_tok]
            keys, rowpref = route_select(aff_g, cap, start)
            idx = jnp.sort(keys, axis=1)[:, :cap]
            idx_parts.append(idx)
            gate_parts.append(jnp.take_along_axis(aff_g, idx - start, axis=1))
            tables.append(combine_tables(rowpref, cap, slot_base) + (start // COMBINE_TM,))
            start += n_tok
            slot_base += cap
        idx = jnp.concatenate(idx_parts, axis=1)
        gate_vals = jnp.concatenate(gate_parts, axis=1)
        per = N_EXPERTS // FFN_SPLITS
        ye = [expert_ffn(h_ffn[idx[e0:e0 + per]], w_gate, w_up, w_down, l, e0, gate_vals, idx)
              for e0 in range(0, N_EXPERTS, per)]
        last = l == depth - 1
        h_parts = []
        for win, minslot, n_rounds, base_tile in tables:
            x_next, h_g = combine(x, ye, win, minslot, n_rounds, base_tile,
                                  norm_final if last else norm_mix[l + 1], F32 if last else BF16, not last)
            x = x if last else x_next
            h_parts.append(h_g)

    return tuple(h_g.reshape(g.shape) for h_g, g in zip(h_parts, groups))
```
